```python
import math, functools
import jax, jax.numpy as jnp
from jax import lax
import numpy as np

D_MODEL = 1024
BATCH = 8
SEQ = 2048
DEPTH = 4
DEC_BATCH = 128
DEC_SEQ = 4
PAST_LEN = 16384
PAGE_SIZE = 128

LRU_WIDTH = D_MODEL
LRU_HEADS = 16
LRU_BLOCK = LRU_WIDTH // LRU_HEADS
CONV_WIDTH = 4
LRU_C = 8.0
RWKV_HEAD = 64
RWKV_WIDTH = D_MODEL
RWKV_HEADS = RWKV_WIDTH // RWKV_HEAD
LORA_W = 64
LORA_A = 64
LORA_G = 128
SHIFT_WIDTH = 3 * RWKV_WIDTH + LORA_W + LORA_A + LORA_G
PROJ_WIDTH = 2 * LRU_WIDTH + SHIFT_WIDTH + 2 * D_MODEL
D_FF = 2816
N_EXPERTS = 8
TOP_K = 2
N_DENSE = (DEPTH + 1) // 2
N_MOE = DEPTH // 2
NORM_EPS = 1e-6
GN_EPS = 64e-5

kernel_name = 'hawk_rwkv7_hybrid_step'

F32 = jnp.float32


def rmsnorm(x, g):
    xf = x.astype(F32)
    y = xf * lax.rsqrt(jnp.mean(xf * xf, axis=-1, keepdims=True) + NORM_EPS)
    return (y * g.astype(F32)).astype(x.dtype)


def rglru_branch(xa, ya, conv_state, h0, conv_w, conv_b, ga_w, ga_b, gx_w, gx_b, lam):
    B, T, _ = xa.shape
    xpad = jnp.concatenate([conv_state.astype(xa.dtype), xa], axis=1)
    xc = conv_b + xpad[:, 0:T] * conv_w[0]
    for j in range(1, CONV_WIDTH):
        xc = xc + xpad[:, j:j + T] * conv_w[j]
    new_conv = xpad[:, T:]
    xh = xc.reshape(B, T, LRU_HEADS, LRU_BLOCK)
    r = jax.nn.sigmoid(jnp.einsum('bthi,hij->bthj', xh, ga_w).reshape(B, T, LRU_WIDTH) + ga_b)
    i = jax.nn.sigmoid(jnp.einsum('bthi,hij->bthj', xh, gx_w).reshape(B, T, LRU_WIDTH) + gx_b)
    log_a = (-LRU_C * jax.nn.softplus(-lam.astype(F32))) * r.astype(F32)
    a = jnp.exp(log_a)
    b = jnp.sqrt(-jnp.expm1(2.0 * log_a)) * (i * xc).astype(F32)

    def step(h, ab):
        h = ab[0] * h + ab[1]
        return h, h

    h_last, hs = lax.scan(step, h0.astype(F32), (a.transpose(1, 0, 2), b.transpose(1, 0, 2)))
    hs = hs.transpose(1, 0, 2).astype(xa.dtype)
    return hs * jax.nn.gelu(ya), new_conv, h_last


def wkv7_step(S, inp):
    r, w, k, v, kk, a = inp
    sa = jnp.einsum('bhvk,bhk->bhv', S, -kk)
    S = S * w[:, :, None, :] + sa[..., None] * (kk * a)[:, :, None, :] + v[..., None] * k[:, :, None, :]
    y = jnp.einsum('bhvk,bhk->bhv', S, r)
    return S, y


def rwkv7_branch(ps, shift_state, S0, mu, w0, w_up, a0, a_up, g_up, k_k, k_a, r_k, lnw, lnb):
    B, T, _ = ps.shape
    prev = jnp.concatenate([shift_state[:, None].astype(ps.dtype), ps[:, :-1]], axis=1)
    s = ps + (prev - ps) * mu
    new_shift = ps[:, -1]
    o = RWKV_WIDTH
    r = s[..., :o]
    k = s[..., o:2 * o]
    v = s[..., 2 * o:3 * o]
    dw = s[..., 3 * o:3 * o + LORA_W]
    da = s[..., 3 * o + LORA_W:3 * o + LORA_W + LORA_A]
    dg = s[..., 3 * o + LORA_W + LORA_A:]
    w_log = -jax.nn.softplus(-(w0 + jnp.tanh(dw) @ w_up).astype(F32)) - 0.5
    decay = jnp.exp(-jnp.exp(w_log))
    a = jax.nn.sigmoid(a0 + da @ a_up)
    g = jax.nn.sigmoid(dg) @ g_up

    def heads(t):
        return t.reshape(B, T, RWKV_HEADS, RWKV_HEAD).astype(F32)

    kk = heads(k * k_k)
    kk = kk / jnp.maximum(jnp.linalg.norm(kk, axis=-1, keepdims=True), 1e-12)
    k = k * (1.0 + (a - 1.0) * k_a)
    rh, kh, vh, ah, dh = heads(r), heads(k), heads(v), heads(a), heads(decay)

    def tm(t):
        return t.transpose(1, 0, 2, 3)

    S_last, ys = lax.scan(wkv7_step, S0.astype(F32), (tm(rh), tm(dh), tm(kh), tm(vh), tm(kk), tm(ah)))
    ys = ys.transpose(1, 0, 2, 3)
    mean = jnp.mean(ys, axis=-1, keepdims=True)
    var = jnp.mean(jnp.square(ys - mean), axis=-1, keepdims=True)
    yn = ((ys - mean) * lax.rsqrt(var + GN_EPS)).reshape(B, T, RWKV_WIDTH) * lnw.astype(F32) + lnb.astype(F32)
    bonus = (jnp.sum(rh * kh * r_k.astype(F32), axis=-1, keepdims=True) * vh).reshape(B, T, RWKV_WIDTH)
    out = ((yn + bonus) * g.astype(F32)).astype(ps.dtype)
    return out, new_shift, S_last


def swiglu(h, wg, wu, wd):
    return (jax.nn.silu(h @ wg) * (h @ wu)) @ wd


def moe_ffn(h, w_router, wg, wu, wd):
    probs = jax.nn.softmax((h @ w_router).astype(F32), axis=-1)
    vals, idx = lax.top_k(probs, TOP_K)
    vals = vals / jnp.sum(vals, axis=-1, keepdims=True)
    gates = jnp.sum(jax.nn.one_hot(idx, N_EXPERTS, dtype=F32) * vals[..., None], axis=-2).astype(h.dtype)
    out = jnp.zeros_like(h)
    for e in range(N_EXPERTS):
        out = out + gates[..., e:e + 1] * swiglu(h, wg[e], wu[e], wd[e])
    return out


def trunk(x, st_rwkv, st_lru, st_conv, st_shift, P):
    n_rwkv, n_lru, n_conv, n_shift = [], [], [], []
    s0 = 2 * LRU_WIDTH
    s1 = s0 + SHIFT_WIDTH
    for l in range(DEPTH):
        u = rmsnorm(x, P['norm_mix'][l])
        p = u @ P['w_in'][l]
        xa = p[..., :LRU_WIDTH]
        ya = p[..., LRU_WIDTH:s0]
        ps = p[..., s0:s1]
        gate_a = p[..., s1:s1 + D_MODEL]
        gate_b = p[..., s1 + D_MODEL:]
        y_a, c_new, h_new = rglru_branch(xa, ya, st_conv[l], st_lru[l], P['conv_w'][l], P['conv_b'][l],
                                         P['gate_a_w'][l], P['gate_a_b'][l], P['gate_x_w'][l], P['gate_x_b'][l],
                                         P['lru_lambda'][l])
        y_b, sh_new, S_new = rwkv7_branch(ps, st_shift[l], st_rwkv[l], P['shift_mu'][l], P['w0'][l],
                                          P['w_lora_up'][l], P['a0'][l], P['a_lora_up'][l], P['g_lora_up'][l],
                                          P['k_k'][l], P['k_a'][l], P['r_k'][l], P['ln_x_w'][l], P['ln_x_b'][l])
        merged = jax.nn.sigmoid(gate_a) * (y_a @ P['w_out_a'][l]) + jax.nn.sigmoid(gate_b) * (y_b @ P['w_out_b'][l])
        x = x + merged @ P['w_out'][l]
        h = rmsnorm(x, P['norm_ffn'][l])
        j = l // 2
        if l % 2 == 0:
            x = x + swiglu(h, P['w_ffn_gate'][j], P['w_ffn_up'][j], P['w_ffn_down'][j])
        else:
            x = x + moe_ffn(h, P['w_router'][j], P['w_moe_gate'][j], P['w_moe_up'][j], P['w_moe_down'][j])
        n_rwkv.append(S_new)
        n_lru.append(h_new)
        n_conv.append(c_new)
        n_shift.append(sh_new)
    y = rmsnorm(x, P['norm_final'])
    return y, jnp.stack(n_rwkv), jnp.stack(n_lru), jnp.stack(n_conv), jnp.stack(n_shift)


def setup_inputs(seed: int = 0) -> dict:
    key = jax.random.key(seed)
    ks = iter(jax.random.split(key, 48))

    def nrm(shape, scale):
        return jax.random.normal(next(ks), shape, F32) * scale

    def unif(shape, lo, hi):
        return jax.random.uniform(next(ks), shape, F32, minval=lo, maxval=hi)

    a_target = unif((DEPTH, LRU_WIDTH), 0.9, 0.999)
    sL = a_target ** (1.0 / LRU_C)
    lru_lambda = jnp.log(sL) - jnp.log1p(-sL)
    d = {
        'x_prompt': nrm((BATCH, SEQ, D_MODEL), 1.0),
        'x_sample': nrm((DEC_BATCH, DEC_SEQ, D_MODEL), 1.0),
        'state_rwkv': nrm((DEPTH, DEC_BATCH, RWKV_HEADS, RWKV_HEAD, RWKV_HEAD), 0.1),
        'state_lru': nrm((DEPTH, DEC_BATCH, LRU_WIDTH), 0.5),
        'state_conv': nrm((DEPTH, DEC_BATCH, CONV_WIDTH - 1, LRU_WIDTH), 1.0),
        'state_shift': nrm((DEPTH, DEC_BATCH, SHIFT_WIDTH), 1.0),
        'norm_mix': 1.0 + nrm((DEPTH, D_MODEL), 0.02),
        'w_in': nrm((DEPTH, D_MODEL, PROJ_WIDTH), D_MODEL ** -0.5),
        'conv_w': nrm((DEPTH, CONV_WIDTH, LRU_WIDTH), CONV_WIDTH ** -0.5),
        'conv_b': nrm((DEPTH, LRU_WIDTH), 0.01),
        'gate_a_w': nrm((DEPTH, LRU_HEADS, LRU_BLOCK, LRU_BLOCK), LRU_BLOCK ** -0.5),
        'gate_a_b': nrm((DEPTH, LRU_WIDTH), 0.01),
        'gate_x_w': nrm((DEPTH, LRU_HEADS, LRU_BLOCK, LRU_BLOCK), LRU_BLOCK ** -0.5),
        'gate_x_b': nrm((DEPTH, LRU_WIDTH), 0.01),
        'lru_lambda': lru_lambda,
        'shift_mu': unif((DEPTH, SHIFT_WIDTH), 0.0, 1.0),
        'w0': unif((DEPTH, RWKV_WIDTH), -2.0, 3.0),
        'w_lora_up': nrm((DEPTH, LORA_W, RWKV_WIDTH), 0.1 * LORA_W ** -0.5),
        'a0': nrm((DEPTH, RWKV_WIDTH), 0.1),
        'a_lora_up': nrm((DEPTH, LORA_A, RWKV_WIDTH), 0.1 * LORA_A ** -0.5),
        'g_lora_up': nrm((DEPTH, LORA_G, RWKV_WIDTH), LORA_G ** -0.5),
        'k_k': 0.85 + nrm((DEPTH, RWKV_WIDTH), 0.02),
        'k_a': 1.0 + nrm((DEPTH, RWKV_WIDTH), 0.02),
        'r_k': nrm((DEPTH, RWKV_HEADS, RWKV_HEAD), 0.1),
        'ln_x_w': 1.0 + nrm((DEPTH, RWKV_WIDTH), 0.02),
        'ln_x_b': nrm((DEPTH, RWKV_WIDTH), 0.01),
        'w_out_a': nrm((DEPTH, LRU_WIDTH, D_MODEL), LRU_WIDTH ** -0.5),
        'w_out_b': nrm((DEPTH, RWKV_WIDTH, D_MODEL), RWKV_WIDTH ** -0.5),
        'w_out': nrm((DEPTH, D_MODEL, D_MODEL), 0.5 * D_MODEL ** -0.5),
        'norm_ffn': 1.0 + nrm((DEPTH, D_MODEL), 0.02),
        'w_ffn_gate': nrm((N_DENSE, D_MODEL, D_FF), D_MODEL ** -0.5),
        'w_ffn_up': nrm((N_DENSE, D_MODEL, D_FF), D_MODEL ** -0.5),
        'w_ffn_down': nrm((N_DENSE, D_FF, D_MODEL), 0.5 * D_FF ** -0.5),
        'w_router': nrm((N_MOE, D_MODEL, N_EXPERTS), D_MODEL ** -0.5),
        'w_moe_gate': nrm((N_MOE, N_EXPERTS, D_MODEL, D_FF), D_MODEL ** -0.5),
        'w_moe_up': nrm((N_MOE, N_EXPERTS, D_MODEL, D_FF), D_MODEL ** -0.5),
        'w_moe_down': nrm((N_MOE, N_EXPERTS, D_FF, D_MODEL), 0.5 * D_FF ** -0.5),
        'norm_final': 1.0 + nrm((D_MODEL,), 0.02),
    }
    return d


def reference(x_prompt, x_sample, state_rwkv, state_lru, state_conv, state_shift,
              norm_mix, w_in, conv_w, conv_b, gate_a_w, gate_a_b, gate_x_w, gate_x_b, lru_lambda,
              shift_mu, w0, w_lora_up, a0, a_lora_up, g_lora_up, k_k, k_a, r_k, ln_x_w, ln_x_b,
              w_out_a, w_out_b, w_out, norm_ffn, w_ffn_gate, w_ffn_up, w_ffn_down,
              w_router, w_moe_gate, w_moe_up, w_moe_down, norm_final):
    P = dict(norm_mix=norm_mix, w_in=w_in, conv_w=conv_w, conv_b=conv_b, gate_a_w=gate_a_w, gate_a_b=gate_a_b,
             gate_x_w=gate_x_w, gate_x_b=gate_x_b, lru_lambda=lru_lambda, shift_mu=shift_mu, w0=w0,
             w_lora_up=w_lora_up, a0=a0, a_lora_up=a_lora_up, g_lora_up=g_lora_up, k_k=k_k, k_a=k_a, r_k=r_k,
             ln_x_w=ln_x_w, ln_x_b=ln_x_b, w_out_a=w_out_a, w_out_b=w_out_b, w_out=w_out, norm_ffn=norm_ffn,
             w_ffn_gate=w_ffn_gate, w_ffn_up=w_ffn_up, w_ffn_down=w_ffn_down, w_router=w_router,
             w_moe_gate=w_moe_gate, w_moe_up=w_moe_up, w_moe_down=w_moe_down, norm_final=norm_final)
    dt = x_prompt.dtype
    p_rwkv0 = jnp.zeros((DEPTH, BATCH, RWKV_HEADS, RWKV_HEAD, RWKV_HEAD), F32)
    p_lru0 = jnp.zeros((DEPTH, BATCH, LRU_WIDTH), F32)
    p_conv0 = jnp.zeros((DEPTH, BATCH, CONV_WIDTH - 1, LRU_WIDTH), dt)
    p_shift0 = jnp.zeros((DEPTH, BATCH, SHIFT_WIDTH), dt)
    y_prompt, pr_rwkv, pr_lru, pr_conv, pr_shift = trunk(x_prompt, p_rwkv0, p_lru0, p_conv0, p_shift0, P)
    y_sample, sa_rwkv, sa_lru, sa_conv, sa_shift = trunk(x_sample, state_rwkv, state_lru, state_conv,
                                                         state_shift, P)
    return (y_prompt, y_sample, pr_rwkv, pr_lru, pr_conv, pr_shift, sa_rwkv, sa_lru, sa_conv, sa_shift)
```

```python
import functools

import numpy as np
import jax
import jax.numpy as jnp
from jax import lax
from jax.experimental import pallas as pl
from jax.experimental.pallas import tpu as pltpu

F32 = jnp.float32
BF16 = jnp.bfloat16

D_MODEL = 1024
DEPTH = 4
LRU_WIDTH = D_MODEL
LRU_HEADS = 16
LRU_BLOCK = LRU_WIDTH // LRU_HEADS
CONV_WIDTH = 4
LRU_C = 8.0
RWKV_HEAD = 64
RWKV_WIDTH = D_MODEL
RWKV_HEADS = RWKV_WIDTH // RWKV_HEAD
LORA_W = 64
LORA_A = 64
LORA_G = 128
SHIFT_WIDTH = 3 * RWKV_WIDTH + LORA_W + LORA_A + LORA_G
PROJ_WIDTH = 2 * LRU_WIDTH + SHIFT_WIDTH + 2 * D_MODEL
D_FF = 2816
N_EXPERTS = 8
NORM_EPS = 1e-6
GN_EPS = 64e-5

LANES = 128
SUBLANES = 8
MXU_DIM = 256
VMEM_LIMIT = 56 * 1024 * 1024

CHUNK = 64
GROUP_HEADS = MXU_DIM // CHUNK
GROUP_LANES = GROUP_HEADS * RWKV_HEAD
N_GROUPS = RWKV_HEADS // GROUP_HEADS
FF_CHUNK = 256
SCAN_ROWS = 16
assert CHUNK == RWKV_HEAD


def _cparams(sem):
    return pltpu.CompilerParams(dimension_semantics=sem, vmem_limit_bytes=VMEM_LIMIT)


def _dot(a, b):
    return jnp.dot(a, b, preferred_element_type=F32)


def _dot_tb(a, b):
    return lax.dot_general(a, b, (((1,), (1,)), ((), ())), preferred_element_type=F32)


def _dot_ta(a, b):
    return lax.dot_general(a, b, (((0,), (0,)), ((), ())), preferred_element_type=F32)


def _dot_exact(a, b_bf16):
    hi = a.astype(BF16)
    lo = (a - hi.astype(F32)).astype(BF16)
    return _dot(hi, b_bf16) + _dot(lo, b_bf16)


def _sigmoid(x):
    return jax.nn.sigmoid(x)


def _softplus(z):
    return jnp.maximum(z, 0.0) + jnp.log(1.0 + jnp.exp(-jnp.abs(z)))


def _rms(x, g):
    return x * lax.rsqrt(jnp.mean(x * x, axis=-1, keepdims=True) + NORM_EPS) * g


_PROJ_SPLITS = ((0, 2 * LRU_WIDTH), (2 * LRU_WIDTH, SHIFT_WIDTH), (2 * LRU_WIDTH + SHIFT_WIDTH, 2 * D_MODEL))


def _proj_kernel(x_ref, g_ref, w_ref, xy_ref, ps_ref, gt_ref):
    u = _rms(x_ref[...], g_ref[...]).astype(BF16)
    for out_ref, (c0, width) in zip((xy_ref, ps_ref, gt_ref), _PROJ_SPLITS):
        j = 0
        while j < width:
            w = min(512, width - j)
            out_ref[:, j:j + w] = _dot(u, w_ref[:, c0 + j:c0 + j + w])
            j += w


def _proj(x, g_all, w_all, layer, tm):
    n = x.shape[0]
    return pl.pallas_call(
        _proj_kernel,
        grid=(n // tm,),
        in_specs=[
            pl.BlockSpec((tm, D_MODEL), lambda i: (i, 0)),
            pl.BlockSpec((None, 1, D_MODEL), lambda i: (layer, 0, 0)),
            pl.BlockSpec((None, D_MODEL, PROJ_WIDTH), lambda i: (layer, 0, 0),
                         pipeline_mode=pl.Buffered(1)),
        ],
        out_specs=[
            pl.BlockSpec((tm, 2 * LRU_WIDTH), lambda i: (i, 0)),
            pl.BlockSpec((tm, SHIFT_WIDTH), lambda i: (i, 0)),
            pl.BlockSpec((tm, 2 * D_MODEL), lambda i: (i, 0)),
        ],
        out_shape=[
            jax.ShapeDtypeStruct((n, 2 * LRU_WIDTH), F32),
            jax.ShapeDtypeStruct((n, SHIFT_WIDTH), F32),
            jax.ShapeDtypeStruct((n, 2 * D_MODEL), F32),
        ],
        compiler_params=_cparams(("arbitrary",)),
        name="proj",
    )(x, g_all, w_all)


def _gelu_tanh(x):
    return 0.5 * x * (1.0 + jnp.tanh(np.sqrt(2.0 / np.pi).astype(np.float32) * (x + 0.044715 * (x * x * x))))


def _lru_kernel(xa_ref, ya_ref, cs_ref, h0_ref, cw_ref, cb_ref, gaw_ref, gab_ref, gxw_ref, gxb_ref, lam_ref,
                y_ref, hl_ref, nc_ref, xp_s, h_s, a_s, b_s, *, bb, tt):
    ti = pl.program_id(1)
    last = ti == pl.num_programs(1) - 1
    neg_c = -LRU_C * _softplus(-lam_ref[...])
    cw = cw_ref[...]
    for b in range(bb):
        @pl.when(ti == 0)
        def _():
            xp_s[b, 5:8, :] = cs_ref[b]
            h_s[b] = h0_ref[b]

        xa = xa_ref[b]
        xp_s[b, 8:8 + tt, :] = xa
        xc = cb_ref[...] + xp_s[b, 5:5 + tt, :] * cw[0:1]
        xc = xc + xp_s[b, 6:6 + tt, :] * cw[1:2]
        xc = xc + xp_s[b, 7:7 + tt, :] * cw[2:3]
        xc = xc + xa * cw[3:4]
        tail = xp_s[b, 5 + tt:8 + tt, :]
        xp_s[b, 5:8, :] = tail

        @pl.when(last)
        def _():
            nc_ref[b] = tail

        for c in range(LRU_WIDTH // MXU_DIM):
            sl = slice(c * MXU_DIM, (c + 1) * MXU_DIM)
            xcc = xc[:, sl]
            xcb = xcc.astype(BF16)
            r = _sigmoid(_dot(xcb, gaw_ref[c]) + gab_ref[:, sl])
            i = _sigmoid(_dot(xcb, gxw_ref[c]) + gxb_ref[:, sl])
            log_a = neg_c[:, sl] * r
            a = jnp.exp(log_a)
            a_s[:, sl] = a
            b_s[:, sl] = jnp.sqrt(-jnp.tanh(log_a) * (a * a + 1.0)) * (i * xcc)

        if tt % SCAN_ROWS == 0:
            row = lax.broadcasted_iota(jnp.int32, (SUBLANES, LRU_WIDTH), 0)

            def blk(j, h):
                r0 = pl.multiple_of(j * SCAN_ROWS, SCAN_ROWS)
                hbs = []
                for q in range(SCAN_ROWS // SUBLANES):
                    av = a_s[pl.ds(r0 + q * SUBLANES, SUBLANES), :]
                    bv = b_s[pl.ds(r0 + q * SUBLANES, SUBLANES), :]
                    for d in (1, 2, 4):
                        m = row >= d
                        a_sh = pltpu.roll(av, d, axis=0)
                        b_sh = pltpu.roll(bv, d, axis=0)
                        bv = jnp.where(m, av * b_sh + bv, bv)
                        av = jnp.where(m, av * a_sh, av)
                    hb = av * h + bv
                    h = hb[SUBLANES - 1:SUBLANES, :]
                    hbs.append(hb)
                hs = jnp.concatenate(hbs, axis=0)
                y_ref[b, pl.ds(r0, SCAN_ROWS), :] = (hs * _gelu_tanh(ya_ref[b, pl.ds(r0, SCAN_ROWS), :])).astype(BF16)
                return h

            h = lax.fori_loop(0, tt // SCAN_ROWS, blk, h_s[b])
        else:
            h = h_s[b]
            rows = []
            for t in range(tt):
                h = a_s[t:t + 1, :] * h + b_s[t:t + 1, :]
                rows.append(h)
            hs = jnp.concatenate(rows, axis=0)
            y_ref[b] = (hs * _gelu_tanh(ya_ref[b])).astype(BF16)
        h_s[b] = h

        @pl.when(last)
        def _():
            hl_ref[b] = h


def _lru(xy3, st_conv, st_lru, P, layer, bb, tt):
    B, T, _ = xy3.shape
    W = LRU_WIDTH
    vec = lambda: pl.BlockSpec((None, 1, W), lambda b, t: (layer, 0, 0))
    kern = functools.partial(_lru_kernel, bb=bb, tt=tt)
    return pl.pallas_call(
        kern,
        grid=(B // bb, T // tt),
        in_specs=[
            pl.BlockSpec((bb, tt, W), lambda b, t: (b, t, 0)),
            pl.BlockSpec((bb, tt, W), lambda b, t: (b, t, 1)),
            pl.BlockSpec((bb, CONV_WIDTH - 1, W), lambda b, t: (b, 0, 0)),
            pl.BlockSpec((bb, 1, W), lambda b, t: (b, 0, 0)),
            pl.BlockSpec((None, CONV_WIDTH, W), lambda b, t: (layer, 0, 0)),
            vec(),
            pl.BlockSpec((None, W // MXU_DIM, MXU_DIM, MXU_DIM), lambda b, t: (layer, 0, 0, 0)),
            vec(),
            pl.BlockSpec((None, W // MXU_DIM, MXU_DIM, MXU_DIM), lambda b, t: (layer, 0, 0, 0)),
            vec(),
            vec(),
        ],
        out_specs=[
            pl.BlockSpec((bb, tt, W), lambda b, t: (b, t, 0)),
            pl.BlockSpec((bb, 1, W), lambda b, t: (b, 0, 0)),
            pl.BlockSpec((bb, CONV_WIDTH - 1, W), lambda b, t: (b, 0, 0)),
        ],
        out_shape=[
            jax.ShapeDtypeStruct((B, T, W), BF16),
            jax.ShapeDtypeStruct((B, 1, W), F32),
            jax.ShapeDtypeStruct((B, CONV_WIDTH - 1, W), F32),
        ],
        scratch_shapes=[
            pltpu.VMEM((bb, tt + 8, W), F32),
            pltpu.VMEM((bb, 1, W), F32),
            pltpu.VMEM((tt, W), F32),
            pltpu.VMEM((tt, W), F32),
        ],
        compiler_params=_cparams(("arbitrary", "arbitrary")),
        name="lru",
    )(xy3, xy3, st_conv, st_lru, P['conv_w'], P['conv_b'], P['gaw_bd'], P['gate_a_b'], P['gxw_bd'],
      P['gate_x_b'], P['lru_lambda'])


def _block_diag(x_bf16, mask_ref):
    return jnp.concatenate([x_bf16] * GROUP_HEADS, axis=0) * mask_ref[...]


def _rwkv_kernel(ps_ref, sh_ref, s0_ref, mu_ref, w0_ref, lora_ref, gup_ref, a0_ref, kk_ref, ka_ref, rk_ref,
                 lnw_ref, lnb_ref, tri_ref, ones_ref, mrow_ref, msq_ref,
                 y_ref, nsh_ref, ns_ref, pad_s, car_s, st_s, *, nb, t_valid, zero_state):
    C = CHUNK
    ci = pl.program_id(1)
    last = ci == pl.num_programs(1) - 1
    rows_c = lax.broadcasted_iota(jnp.int32, (C, 1), 0)
    t_idx = lax.broadcasted_iota(jnp.int32, (C, MXU_DIM), 0)
    i_idx = lax.broadcasted_iota(jnp.int32, (C, MXU_DIM), 1) % C
    m_strict = i_idx < t_idx
    m_incl = i_idx <= t_idx
    eye_cat = (i_idx == t_idx).astype(F32)
    ones_bd = ones_ref[...]

    def head_sum(x):
        parts = []
        for g in range(RWKV_WIDTH // MXU_DIM):
            parts.append(_dot_exact(x[:, g * MXU_DIM:(g + 1) * MXU_DIM], ones_bd))
        return jnp.concatenate(parts, axis=1)

    for b in range(nb):
        @pl.when(ci == 0)
        def _():
            car_s[b] = sh_ref[b]
            if zero_state:
                st_s[b] = jnp.zeros(st_s.shape[1:], F32)
            else:
                for g in range(N_GROUPS):
                    rows = [s0_ref[b, g * GROUP_HEADS + hh] for hh in range(GROUP_HEADS)]
                    stacked = jnp.concatenate(rows, axis=0)
                    tiled = jnp.concatenate([stacked] * GROUP_HEADS, axis=1)
                    st_s[b, g] = tiled * msq_ref[...].astype(F32)

        if t_valid == C:
            ps = ps_ref[b]
        else:
            pad_s[...] = jnp.zeros(pad_s.shape, F32)
            pad_s[0:t_valid, :] = ps_ref[b]
            ps = pad_s[...]
        prev = pltpu.roll(ps, 1, axis=0)
        prev = jnp.where(rows_c == 0, car_s[b], prev)
        new_carry = ps[t_valid - 1:t_valid, :]
        car_s[b] = new_carry

        @pl.when(last)
        def _():
            nsh_ref[b] = new_carry

        s = ps + (prev - ps) * mu_ref[...]
        o = RWKV_WIDTH
        r = s[:, :o]
        k = s[:, o:2 * o]
        v = s[:, 2 * o:3 * o]
        dwa = s[:, 3 * o:3 * o + LORA_W + LORA_A]
        dg = s[:, 3 * o + LORA_W + LORA_A:]
        lane = lax.broadcasted_iota(jnp.int32, dwa.shape, 1)
        lora_in = jnp.where(lane < LORA_W, jnp.tanh(dwa), dwa).astype(BF16)
        lora = _dot(lora_in, lora_ref[...])
        w_log = -_softplus(-(w0_ref[...] + lora[:, :o])) - 0.5
        lw = -jnp.exp(w_log)
        a = _sigmoid(a0_ref[...] + lora[:, o:])
        g = _dot(_sigmoid(dg).astype(BF16), gup_ref[...])
        kk = k * kk_ref[...]
        kk = kk / jnp.maximum(jnp.sqrt(head_sum(kk * kk)), 1e-12)
        kmod = k * (1.0 + (a - 1.0) * ka_ref[...])
        if t_valid != C:
            valid = rows_c < t_valid
            lw = jnp.where(valid, lw, 0.0)
            kk = jnp.where(valid, kk, 0.0)
            kmod = jnp.where(valid, kmod, 0.0)
            v = jnp.where(valid, v, 0.0)
        beta = kk * a
        L = jnp.concatenate(
            [_dot_tri(tri_ref[...], lw[:, j * MXU_DIM:(j + 1) * MXU_DIM]) for j in range(o // MXU_DIM)], axis=1)
        e_l = jnp.exp(L)
        e_neg = jnp.exp(-L)
        l_end = L[C - 1:C, :]
        e_end = jnp.exp(l_end - L)
        p_end = jnp.exp(l_end)
        at = (-kk) * jnp.exp(L - lw)
        rt = r * e_l
        bt = beta * e_neg
        kt = kmod * e_neg
        bend = beta * e_end
        kend = kmod * e_end

        ys = []
        for gi in range(N_GROUPS):
            sl = slice(gi * GROUP_LANES, (gi + 1) * GROUP_LANES)
            x2 = jnp.concatenate([at[:, sl], rt[:, sl]], axis=0).astype(BF16)
            wbd = jnp.concatenate([_block_diag(bt[:, sl].astype(BF16), mrow_ref),
                                   _block_diag(kt[:, sl].astype(BF16), mrow_ref)], axis=0)
            res = _dot_tb(x2, wbd)
            cw = GROUP_HEADS * C
            n_cat = jnp.where(m_strict, res[:C, :cw], 0.0)
            a_ak = jnp.where(m_strict, res[:C, cw:], 0.0)
            a_rb = jnp.where(m_incl, res[C:, :cw], 0.0)
            a_rk = jnp.where(m_incl, res[C:, cw:], 0.0)
            x_c = n_cat
            p_c = eye_cat + n_cat
            x_bd = _block_diag(x_c.astype(BF16), msq_ref)
            lvl = 2
            while lvl < C:
                x_c = _dot(x_c.astype(BF16), x_bd)
                x_bd = _block_diag(x_c.astype(BF16), msq_ref)
                p_c = p_c + _dot(p_c.astype(BF16), x_bd)
                lvl *= 2
            st = st_s[b, gi]
            xs = _dot_tb(x2, st.astype(BF16))
            vg = v[:, sl]
            v_bd = _block_diag(vg.astype(BF16), msq_ref)
            rhs = xs[:C] + _dot(a_ak.astype(BF16), v_bd)
            u = _dot(p_c.astype(BF16), _block_diag(rhs.astype(BF16), msq_ref))
            u_bd = _block_diag(u.astype(BF16), msq_ref)
            y = xs[C:] + _dot(jnp.concatenate([a_rb, a_rk], axis=1).astype(BF16),
                              jnp.concatenate([u_bd, v_bd], axis=0))
            ys.append(y)
            uv = jnp.concatenate([u, vg], axis=0).astype(BF16)
            bk = jnp.concatenate([bend[:, sl], kend[:, sl]], axis=0).astype(BF16)
            st_new = st * p_end[:, sl] + _dot_ta(uv, bk) * msq_ref[...].astype(F32)
            st_s[b, gi] = st_new

            @pl.when(last)
            def _():
                for hh in range(GROUP_HEADS):
                    ns_ref[b, gi * GROUP_HEADS + hh] = st_new[hh * RWKV_HEAD:(hh + 1) * RWKV_HEAD,
                                                              hh * RWKV_HEAD:(hh + 1) * RWKV_HEAD]

        yc = jnp.concatenate(ys, axis=1)
        inv_n = 1.0 / RWKV_HEAD
        mean = head_sum(yc) * inv_n
        dlt = yc - mean
        var = head_sum(dlt * dlt) * inv_n
        yn = dlt * lax.rsqrt(var + GN_EPS) * lnw_ref[...] + lnb_ref[...]
        bonus = head_sum(r * kmod * rk_ref[...]) * v
        out = ((yn + bonus) * g).astype(BF16)
        if t_valid == C:
            y_ref[b] = out
        else:
            y_ref[b] = out[0:t_valid, :]


def _dot_tri(tri_bf16, x):
    hi = x.astype(BF16)
    lo = (x - hi.astype(F32)).astype(BF16)
    return _dot(tri_bf16, hi) + _dot(tri_bf16, lo)


def _rwkv_consts():
    C = CHUNK
    gh = GROUP_HEADS
    tri = np.tril(np.ones((C, C), np.float32))
    hl = np.arange(MXU_DIM) // RWKV_HEAD
    ones_bd = (hl[:, None] == hl[None, :]).astype(np.float32)
    rowh = np.arange(gh * C) // C
    colh = np.arange(GROUP_LANES) // RWKV_HEAD
    mrow = (rowh[:, None] == colh[None, :]).astype(np.float32)
    colc = np.arange(gh * C) // C
    msq = (rowh[:, None] == colc[None, :]).astype(np.float32)
    return (jnp.asarray(tri, BF16), jnp.asarray(ones_bd, BF16), jnp.asarray(mrow, BF16), jnp.asarray(msq, BF16))


def _rwkv(ps3, st_shift, st_rwkv, P, layer, nb, zero_state):
    B, T, _ = ps3.shape
    C = CHUNK
    if T % C == 0:
        t_valid, nchunks, tb = C, T // C, C
    else:
        assert T < C
        t_valid, nchunks, tb = T, 1, T
    o = RWKV_WIDTH
    tri, ones_bd, mrow, msq = _rwkv_consts()
    vec = lambda w: pl.BlockSpec((None, 1, w), lambda b, c: (layer, 0, 0))
    const = lambda arr: pl.BlockSpec(arr.shape, lambda b, c: (0,) * arr.ndim)
    kern = functools.partial(_rwkv_kernel, nb=nb, t_valid=t_valid, zero_state=zero_state)
    state_spec = pl.BlockSpec((nb, RWKV_HEADS, RWKV_HEAD, RWKV_HEAD), lambda b, c: (b, 0, 0, 0))
    return pl.pallas_call(
        kern,
        grid=(B // nb, nchunks),
        in_specs=[
            pl.BlockSpec((nb, tb, SHIFT_WIDTH), lambda b, c: (b, c, 0)),
            pl.BlockSpec((nb, 1, SHIFT_WIDTH), lambda b, c: (b, 0, 0)),
            state_spec,
            vec(SHIFT_WIDTH),
            vec(o),
            pl.BlockSpec((None, LORA_W + LORA_A, 2 * o), lambda b, c: (layer, 0, 0)),
            pl.BlockSpec((None, LORA_G, o), lambda b, c: (layer, 0, 0)),
            vec(o), vec(o), vec(o), vec(o), vec(o), vec(o),
            const(tri), const(ones_bd), const(mrow), const(msq),
        ],
        out_specs=[
            pl.BlockSpec((nb, tb, o), lambda b, c: (b, c, 0)),
            pl.BlockSpec((nb, 1, SHIFT_WIDTH), lambda b, c: (b, 0, 0)),
            state_spec,
        ],
        out_shape=[
            jax.ShapeDtypeStruct((B, T, o), BF16),
            jax.ShapeDtypeStruct((B, 1, SHIFT_WIDTH), F32),
            jax.ShapeDtypeStruct((B, RWKV_HEADS, RWKV_HEAD, RWKV_HEAD), F32),
        ],
        scratch_shapes=[
            pltpu.VMEM((C, SHIFT_WIDTH), F32),
            pltpu.VMEM((nb, 1, SHIFT_WIDTH), F32),
            pltpu.VMEM((nb, N_GROUPS, MXU_DIM, MXU_DIM), F32),
        ],
        compiler_params=_cparams(("arbitrary", "arbitrary")),
        name="rwkv",
    )(ps3, st_shift, st_rwkv, P['shift_mu'], P['w0'], P['lora_wa'], P['g_lora_up'], P['a0'], P['k_k'], P['k_a'],
      P['r_k'], P['ln_x_w'], P['ln_x_b'], tri, ones_bd, mrow, msq)


def _merge_kernel(x_ref, ya_ref, yb_ref, gt_ref, wa_ref, wb_ref, wo_ref, nf_ref, *rest, moe):
    if moe:
        wr_ref, xo_ref, h_ref, gates_ref = rest
    else:
        xo_ref, h_ref = rest
    ga = gt_ref[:, :D_MODEL]
    gb = gt_ref[:, D_MODEL:]
    m = _sigmoid(ga) * _dot(ya_ref[...], wa_ref[...]) + _sigmoid(gb) * _dot(yb_ref[...], wb_ref[...])
    x = x_ref[...] + _dot(m.astype(BF16), wo_ref[...])
    xo_ref[...] = x
    h = _rms(x, nf_ref[...])
    h_ref[...] = h.astype(BF16)
    if moe:
        logits = jnp.dot(h, wr_ref[...], preferred_element_type=F32, precision=lax.Precision.HIGHEST)
        lane = lax.broadcasted_iota(jnp.int32, logits.shape, 1)
        real = lane < N_EXPERTS
        logits = jnp.where(real, logits, -jnp.inf)
        e = jnp.exp(logits - jnp.max(logits, axis=-1, keepdims=True))
        p = jnp.where(real, e / jnp.sum(e, axis=-1, keepdims=True), -1.0)
        m1 = jnp.max(p, axis=-1, keepdims=True)
        i1 = jnp.min(jnp.where(p == m1, lane, LANES), axis=-1, keepdims=True)
        oh1 = lane == i1
        p2 = jnp.where(oh1, -1.0, p)
        m2 = jnp.max(p2, axis=-1, keepdims=True)
        i2 = jnp.min(jnp.where(p2 == m2, lane, LANES), axis=-1, keepdims=True)
        oh2 = lane == i2
        tot = m1 + m2
        gates_ref[...] = jnp.where(oh1, m1 / tot, 0.0) + jnp.where(oh2, m2 / tot, 0.0)


def _merge(x, ya, yb, gt, P, layer, tm, moe):
    n = x.shape[0]
    row = lambda w: pl.BlockSpec((tm, w), lambda i: (i, 0))
    wsq = lambda: pl.BlockSpec((None, D_MODEL, D_MODEL), lambda i: (layer, 0, 0))
    in_specs = [row(D_MODEL), row(D_MODEL), row(D_MODEL), row(2 * D_MODEL), wsq(), wsq(), wsq(),
                pl.BlockSpec((None, 1, D_MODEL), lambda i: (layer, 0, 0))]
    args = [x, ya, yb, gt, P['w_out_a'], P['w_out_b'], P['w_out'], P['norm_ffn']]
    out_specs = [row(D_MODEL), row(D_MODEL)]
    out_shape = [jax.ShapeDtypeStruct((n, D_MODEL), F32), jax.ShapeDtypeStruct((n, D_MODEL), BF16)]
    if moe:
        in_specs.append(pl.BlockSpec((None, D_MODEL, LANES), lambda i: (layer // 2, 0, 0)))
        args.append(P['w_router_pad'])
        out_specs.append(row(LANES))
        out_shape.append(jax.ShapeDtypeStruct((n, LANES), F32))
    return pl.pallas_call(
        functools.partial(_merge_kernel, moe=moe),
        grid=(n // tm,),
        in_specs=in_specs,
        out_specs=out_specs,
        out_shape=out_shape,
        compiler_params=_cparams(("arbitrary",)),
        name="merge",
    )(*args)


def _swiglu_acc(h, wg_ref, wu_ref, wd_ref):
    acc = None
    for c in range(D_FF // FF_CHUNK):
        sl = slice(c * FF_CHUNK, (c + 1) * FF_CHUNK)
        gate = _dot(h, wg_ref[:, sl])
        up = _dot(h, wu_ref[:, sl])
        act = (gate * _sigmoid(gate) * up).astype(BF16)
        part = _dot(act, wd_ref[sl, :])
        acc = part if acc is None else acc + part
    return acc


def _ffn_kernel(x_ref, h_ref, wg_ref, wu_ref, wd_ref, o_ref):
    o_ref[...] = x_ref[...] + _swiglu_acc(h_ref[...], wg_ref, wu_ref, wd_ref)


def _ffn(x, h, P, j, tm):
    n = x.shape[0]
    row = lambda: pl.BlockSpec((tm, D_MODEL), lambda i: (i, 0))
    return pl.pallas_call(
        _ffn_kernel,
        grid=(n // tm,),
        in_specs=[row(), row(),
                  pl.BlockSpec((None, D_MODEL, D_FF), lambda i: (j, 0, 0), pipeline_mode=pl.Buffered(1)),
                  pl.BlockSpec((None, D_MODEL, D_FF), lambda i: (j, 0, 0), pipeline_mode=pl.Buffered(1)),
                  pl.BlockSpec((None, D_FF, D_MODEL), lambda i: (j, 0, 0), pipeline_mode=pl.Buffered(1))],
        out_specs=row(),
        out_shape=jax.ShapeDtypeStruct((n, D_MODEL), F32),
        compiler_params=_cparams(("arbitrary",)),
        name="ffn",
    )(x, h, P['w_ffn_gate'], P['w_ffn_up'], P['w_ffn_down'])


def _moe_kernel(x_ref, h_ref, gates_ref, wg_ref, wu_ref, wd_ref, o_ref):
    e = pl.program_id(1)
    gates = gates_ref[...]
    lane = lax.broadcasted_iota(jnp.int32, gates.shape, 1)
    ge = jnp.sum(jnp.where(lane == e, gates, 0.0), axis=-1, keepdims=True)
    term = ge * _swiglu_acc(h_ref[...], wg_ref, wu_ref, wd_ref)

    @pl.when(e == 0)
    def _():
        o_ref[...] = x_ref[...] + term

    @pl.when(e != 0)
    def _():
        o_ref[...] += term


def _moe(x, h, gates, P, j, tm):
    n = x.shape[0]
    row = lambda w: pl.BlockSpec((tm, w), lambda i, e: (i, 0))
    wspec = lambda a, b: pl.BlockSpec((None, None, a, b), lambda i, e: (j, e, 0, 0))
    return pl.pallas_call(
        _moe_kernel,
        grid=(n // tm, N_EXPERTS),
        in_specs=[row(D_MODEL), row(D_MODEL), row(LANES),
                  wspec(D_MODEL, D_FF), wspec(D_MODEL, D_FF), wspec(D_FF, D_MODEL)],
        out_specs=row(D_MODEL),
        out_shape=jax.ShapeDtypeStruct((n, D_MODEL), F32),
        compiler_params=_cparams(("arbitrary", "arbitrary")),
        name="moe",
    )(x, h, gates, P['w_moe_gate'], P['w_moe_up'], P['w_moe_down'])


def _final_norm_kernel(x_ref, g_ref, o_ref):
    o_ref[...] = _rms(x_ref[...], g_ref[...])


def _final_norm(x, g, tm):
    n = x.shape[0]
    return pl.pallas_call(
        _final_norm_kernel,
        grid=(n // tm,),
        in_specs=[pl.BlockSpec((tm, D_MODEL), lambda i: (i, 0)), pl.BlockSpec((1, D_MODEL), lambda i: (0, 0))],
        out_specs=pl.BlockSpec((tm, D_MODEL), lambda i: (i, 0)),
        out_shape=jax.ShapeDtypeStruct((n, D_MODEL), F32),
        compiler_params=_cparams(("arbitrary",)),
        name="final_norm",
    )(x, g)


def _block_diag_weights(w):
    d = w.shape[0]
    per = MXU_DIM // LRU_BLOCK
    w = w.reshape(d, LRU_HEADS // per, per, LRU_BLOCK, LRU_BLOCK)
    eye = jnp.eye(per, dtype=w.dtype)
    bd = jnp.einsum('dcpij,pq->dcpiqj', w, eye)
    return bd.reshape(d, LRU_HEADS // per, MXU_DIM, MXU_DIM).astype(BF16)


def _prep_params(p):
    P = dict(p)
    for name in ('w_in', 'w_out_a', 'w_out_b', 'w_out', 'w_ffn_gate', 'w_ffn_up', 'w_ffn_down',
                 'w_moe_gate', 'w_moe_up', 'w_moe_down', 'g_lora_up'):
        P[name] = p[name].astype(BF16)
    for name in ('norm_mix', 'conv_b', 'gate_a_b', 'gate_x_b', 'lru_lambda', 'shift_mu', 'w0', 'a0', 'k_k', 'k_a',
                 'ln_x_w', 'ln_x_b', 'norm_ffn'):
        P[name] = p[name][:, None, :]
    P['r_k'] = p['r_k'].reshape(DEPTH, 1, RWKV_WIDTH)
    P['norm_final'] = p['norm_final'][None, :]
    P['gaw_bd'] = _block_diag_weights(p['gate_a_w'])
    P['gxw_bd'] = _block_diag_weights(p['gate_x_w'])
    z = jnp.zeros((DEPTH, LORA_W, RWKV_WIDTH), F32)
    P['lora_wa'] = jnp.concatenate([jnp.concatenate([p['w_lora_up'], z], axis=2),
                                    jnp.concatenate([z, p['a_lora_up']], axis=2)], axis=1).astype(BF16)
    P['w_router_pad'] = jnp.pad(p['w_router'], ((0, 0), (0, 0), (0, LANES - N_EXPERTS)))
    return P


def _trunk(x3, st_rwkv, st_lru, st_conv, st_shift, P, *, zero_state, tm, lru_bb, lru_tt, rwkv_nb):
    B, T, _ = x3.shape
    n = B * T
    x = x3.reshape(n, D_MODEL)
    n_rwkv, n_lru, n_conv, n_shift = [], [], [], []
    for l in range(DEPTH):
        xy, ps, gt = _proj(x, P['norm_mix'], P['w_in'], l, min(tm, 256))
        ya, h_new, c_new = _lru(xy.reshape(B, T, 2 * LRU_WIDTH), st_conv[l], st_lru[l][:, None, :], P, l,
                                lru_bb, lru_tt)
        yb, sh_new, s_new = _rwkv(ps.reshape(B, T, SHIFT_WIDTH), st_shift[l][:, None, :], st_rwkv[l], P, l,
                                  rwkv_nb, zero_state)
        moe = l % 2 == 1
        res = _merge(x, ya.reshape(n, D_MODEL), yb.reshape(n, D_MODEL), gt, P, l, tm, moe)
        if moe:
            x, h, gates = res
            x = _moe(x, h, gates, P, l // 2, tm)
        else:
            x, h = res
            x = _ffn(x, h, P, l // 2, tm)
        n_rwkv.append(s_new)
        n_lru.append(h_new[:, 0, :])
        n_conv.append(c_new)
        n_shift.append(sh_new[:, 0, :])
    y = _final_norm(x, P['norm_final'], tm).reshape(B, T, D_MODEL)
    return y, jnp.stack(n_rwkv), jnp.stack(n_lru), jnp.stack(n_conv), jnp.stack(n_shift)


def kernel(x_prompt, x_sample, state_rwkv, state_lru, state_conv, state_shift, norm_mix, w_in, conv_w, conv_b, gate_a_w, gate_a_b, gate_x_w, gate_x_b, lru_lambda, shift_mu, w0, w_lora_up, a0, a_lora_up, g_lora_up, k_k, k_a, r_k, ln_x_w, ln_x_b, w_out_a, w_out_b, w_out, norm_ffn, w_ffn_gate, w_ffn_up, w_ffn_down, w_router, w_moe_gate, w_moe_up, w_moe_down, norm_final):
    P = _prep_params(dict(
        norm_mix=norm_mix, w_in=w_in, conv_w=conv_w, conv_b=conv_b, gate_a_w=gate_a_w, gate_a_b=gate_a_b,
        gate_x_w=gate_x_w, gate_x_b=gate_x_b, lru_lambda=lru_lambda, shift_mu=shift_mu, w0=w0,
        w_lora_up=w_lora_up, a0=a0, a_lora_up=a_lora_up, g_lora_up=g_lora_up, k_k=k_k, k_a=k_a, r_k=r_k,
        ln_x_w=ln_x_w, ln_x_b=ln_x_b, w_out_a=w_out_a, w_out_b=w_out_b, w_out=w_out, norm_ffn=norm_ffn,
        w_ffn_gate=w_ffn_gate, w_ffn_up=w_ffn_up, w_ffn_down=w_ffn_down, w_router=w_router,
        w_moe_gate=w_moe_gate, w_moe_up=w_moe_up, w_moe_down=w_moe_down, norm_final=norm_final))
    bp = x_prompt.shape[0]
    bs = x_sample.shape[0]
    zeros = lambda *shape: jnp.zeros(shape, F32)
    p_out = _trunk(x_prompt,
                   zeros(DEPTH, bp, RWKV_HEADS, RWKV_HEAD, RWKV_HEAD), zeros(DEPTH, bp, LRU_WIDTH),
                   zeros(DEPTH, bp, CONV_WIDTH - 1, LRU_WIDTH), zeros(DEPTH, bp, SHIFT_WIDTH), P,
                   zero_state=True, tm=512, lru_bb=1, lru_tt=256, rwkv_nb=1)
    s_out = _trunk(x_sample, state_rwkv, state_lru, state_conv, state_shift, P,
                   zero_state=False, tm=256, lru_bb=8, lru_tt=x_sample.shape[1], rwkv_nb=1)
    return (p_out[0], s_out[0]) + tuple(p_out[1:]) + tuple(s_out[1:])
```

```python
import functools

import numpy as np
import jax
import jax.numpy as jnp
from jax import lax
from jax.experimental import pallas as pl
from jax.experimental.pallas import tpu as pltpu

F32 = jnp.float32
BF16 = jnp.bfloat16

D_MODEL = 1024
DEPTH = 4
LRU_WIDTH = D_MODEL
LRU_HEADS = 16
LRU_BLOCK = LRU_WIDTH // LRU_HEADS
CONV_WIDTH = 4
LRU_C = 8.0
RWKV_HEAD = 64
RWKV_WIDTH = D_MODEL
RWKV_HEADS = RWKV_WIDTH // RWKV_HEAD
LORA_W = 64
LORA_A = 64
LORA_G = 128
SHIFT_WIDTH = 3 * RWKV_WIDTH + LORA_W + LORA_A + LORA_G
PROJ_WIDTH = 2 * LRU_WIDTH + SHIFT_WIDTH + 2 * D_MODEL
D_FF = 2816
N_EXPERTS = 8
NORM_EPS = 1e-6
GN_EPS = 64e-5

LANES = 128
SUBLANES = 8
MXU_DIM = 256
VMEM_LIMIT = 56 * 1024 * 1024

CHUNK = 64
GROUP_HEADS = MXU_DIM // CHUNK
GROUP_LANES = GROUP_HEADS * RWKV_HEAD
N_GROUPS = RWKV_HEADS // GROUP_HEADS
FF_CHUNK = 256
SCAN_ROWS = 16
assert CHUNK == RWKV_HEAD


def _cparams(sem):
    return pltpu.CompilerParams(dimension_semantics=sem, vmem_limit_bytes=VMEM_LIMIT)


def _dot(a, b):
    return jnp.dot(a, b, preferred_element_type=F32)


def _dot_tb(a, b):
    return lax.dot_general(a, b, (((1,), (1,)), ((), ())), preferred_element_type=F32)


def _dot_ta(a, b):
    return lax.dot_general(a, b, (((0,), (0,)), ((), ())), preferred_element_type=F32)


def _dot_exact(a, b_bf16):
    hi = a.astype(BF16)
    lo = (a - hi.astype(F32)).astype(BF16)
    return _dot(hi, b_bf16) + _dot(lo, b_bf16)


def _sigmoid(x):
    return jax.nn.sigmoid(x)


def _softplus(z):
    return jnp.maximum(z, 0.0) + jnp.log(1.0 + jnp.exp(-jnp.abs(z)))


def _rms(x, g):
    return x * lax.rsqrt(jnp.mean(x * x, axis=-1, keepdims=True) + NORM_EPS) * g


_PROJ_SPLITS = ((0, 2 * LRU_WIDTH), (2 * LRU_WIDTH, SHIFT_WIDTH), (2 * LRU_WIDTH + SHIFT_WIDTH, 2 * D_MODEL))


def _proj_kernel(x_ref, g_ref, w_ref, xy_ref, ps_ref, gt_ref):
    u = _rms(x_ref[...], g_ref[...]).astype(BF16)
    for out_ref, (c0, width) in zip((xy_ref, ps_ref, gt_ref), _PROJ_SPLITS):
        j = 0
        while j < width:
            w = min(512, width - j)
            out_ref[:, j:j + w] = _dot(u, w_ref[:, c0 + j:c0 + j + w])
            j += w


def _proj(x, g_all, w_all, layer, tm):
    n = x.shape[0]
    return pl.pallas_call(
        _proj_kernel,
        grid=(n // tm,),
        in_specs=[
            pl.BlockSpec((tm, D_MODEL), lambda i: (i, 0)),
            pl.BlockSpec((None, 1, D_MODEL), lambda i: (layer, 0, 0)),
            pl.BlockSpec((None, D_MODEL, PROJ_WIDTH), lambda i: (layer, 0, 0),
                         pipeline_mode=pl.Buffered(1)),
        ],
        out_specs=[
            pl.BlockSpec((tm, 2 * LRU_WIDTH), lambda i: (i, 0)),
            pl.BlockSpec((tm, SHIFT_WIDTH), lambda i: (i, 0)),
            pl.BlockSpec((tm, 2 * D_MODEL), lambda i: (i, 0)),
        ],
        out_shape=[
            jax.ShapeDtypeStruct((n, 2 * LRU_WIDTH), F32),
            jax.ShapeDtypeStruct((n, SHIFT_WIDTH), F32),
            jax.ShapeDtypeStruct((n, 2 * D_MODEL), F32),
        ],
        compiler_params=_cparams(("arbitrary",)),
        name="proj",
    )(x, g_all, w_all)


def _gelu_tanh(x):
    return 0.5 * x * (1.0 + jnp.tanh(np.sqrt(2.0 / np.pi).astype(np.float32) * (x + 0.044715 * (x * x * x))))


def _lru_kernel(xa_ref, ya_ref, cs_ref, h0_ref, cw_ref, cb_ref, gaw_ref, gab_ref, gxw_ref, gxb_ref, lam_ref,
                y_ref, hl_ref, nc_ref, xp_s, h_s, a_s, b_s, *, bb, tt):
    ti = pl.program_id(1)
    neg_c = -LRU_C * _softplus(-lam_ref[...])
    cw = cw_ref[...]
    @pl.when(ti == 0)
    def _():
        for b in range(bb):
            xp_s[b, 5:8, :] = cs_ref[b]
            h_s[b] = h0_ref[b]

    for b in range(bb):
        xa = xa_ref[b]
        xp_s[b, 8:8 + tt, :] = xa
        xc = cb_ref[...] + xp_s[b, 5:5 + tt, :] * cw[0:1]
        xc = xc + xp_s[b, 6:6 + tt, :] * cw[1:2]
        xc = xc + xp_s[b, 7:7 + tt, :] * cw[2:3]
        xc = xc + xa * cw[3:4]
        tail = xp_s[b, 5 + tt:8 + tt, :]
        xp_s[b, 5:8, :] = tail
        nc_ref[b] = tail

        for c in range(LRU_WIDTH // MXU_DIM):
            sl = slice(c * MXU_DIM, (c + 1) * MXU_DIM)
            xcc = xc[:, sl]
            xcb = xcc.astype(BF16)
            r = _sigmoid(_dot(xcb, gaw_ref[c]) + gab_ref[:, sl])
            i = _sigmoid(_dot(xcb, gxw_ref[c]) + gxb_ref[:, sl])
            log_a = neg_c[:, sl] * r
            a = jnp.exp(log_a)
            a_s[:, sl] = a
            b_s[:, sl] = jnp.sqrt(-jnp.tanh(log_a) * (a * a + 1.0)) * (i * xcc)

        if tt % SCAN_ROWS == 0:
            row = lax.broadcasted_iota(jnp.int32, (SUBLANES, LRU_WIDTH), 0)

            def blk(j, h):
                r0 = pl.multiple_of(j * SCAN_ROWS, SCAN_ROWS)
                hbs = []
                for q in range(SCAN_ROWS // SUBLANES):
                    av = a_s[pl.ds(r0 + q * SUBLANES, SUBLANES), :]
                    bv = b_s[pl.ds(r0 + q * SUBLANES, SUBLANES), :]
                    for d in (1, 2, 4):
                        m = row >= d
                        a_sh = pltpu.roll(av, d, axis=0)
                        b_sh = pltpu.roll(bv, d, axis=0)
                        bv = jnp.where(m, av * b_sh + bv, bv)
                        av = jnp.where(m, av * a_sh, av)
                    hb = av * h + bv
                    h = hb[SUBLANES - 1:SUBLANES, :]
                    hbs.append(hb)
                hs = jnp.concatenate(hbs, axis=0)
                y_ref[b, pl.ds(r0, SCAN_ROWS), :] = (hs * _gelu_tanh(ya_ref[b, pl.ds(r0, SCAN_ROWS), :])).astype(BF16)
                return h

            h = lax.fori_loop(0, tt // SCAN_ROWS, blk, h_s[b])
        else:
            h = h_s[b]
            rows = []
            for t in range(tt):
                h = a_s[t:t + 1, :] * h + b_s[t:t + 1, :]
                rows.append(h)
            hs = jnp.concatenate(rows, axis=0)
            y_ref[b] = (hs * _gelu_tanh(ya_ref[b])).astype(BF16)
        h_s[b] = h
        hl_ref[b] = h


def _lru(xy3, st_conv, st_lru, P, layer, bb, tt):
    B, T, _ = xy3.shape
    W = LRU_WIDTH
    vec = lambda: pl.BlockSpec((None, 1, W), lambda b, t: (layer, 0, 0))
    kern = functools.partial(_lru_kernel, bb=bb, tt=tt)
    return pl.pallas_call(
        kern,
        grid=(B // bb, T // tt),
        in_specs=[
            pl.BlockSpec((bb, tt, W), lambda b, t: (b, t, 0)),
            pl.BlockSpec((bb, tt, W), lambda b, t: (b, t, 1)),
            pl.BlockSpec((bb, CONV_WIDTH - 1, W), lambda b, t: (b, 0, 0)),
            pl.BlockSpec((bb, 1, W), lambda b, t: (b, 0, 0)),
            pl.BlockSpec((None, CONV_WIDTH, W), lambda b, t: (layer, 0, 0)),
            vec(),
            pl.BlockSpec((None, W // MXU_DIM, MXU_DIM, MXU_DIM), lambda b, t: (layer, 0, 0, 0)),
            vec(),
            pl.BlockSpec((None, W // MXU_DIM, MXU_DIM, MXU_DIM), lambda b, t: (layer, 0, 0, 0)),
            vec(),
            vec(),
        ],
        out_specs=[
            pl.BlockSpec((bb, tt, W), lambda b, t: (b, t, 0)),
            pl.BlockSpec((bb, 1, W), lambda b, t: (b, 0, 0)),
            pl.BlockSpec((bb, CONV_WIDTH - 1, W), lambda b, t: (b, 0, 0)),
        ],
        out_shape=[
            jax.ShapeDtypeStruct((B, T, W), BF16),
            jax.ShapeDtypeStruct((B, 1, W), F32),
            jax.ShapeDtypeStruct((B, CONV_WIDTH - 1, W), F32),
        ],
        scratch_shapes=[
            pltpu.VMEM((bb, tt + 8, W), F32),
            pltpu.VMEM((bb, 1, W), F32),
            pltpu.VMEM((tt, W), F32),
            pltpu.VMEM((tt, W), F32),
        ],
        compiler_params=_cparams(("arbitrary", "arbitrary")),
        name="lru",
    )(xy3, xy3, st_conv, st_lru, P['conv_w'], P['conv_b'], P['gaw_bd'], P['gate_a_b'], P['gxw_bd'],
      P['gate_x_b'], P['lru_lambda'])


def _block_diag(x_bf16, mask_ref):
    return jnp.concatenate([x_bf16] * GROUP_HEADS, axis=0) * mask_ref[...]


def _rwkv_kernel(ps_ref, sh_ref, s0_ref, mu_ref, w0_ref, lora_ref, gup_ref, a0_ref, kk_ref, ka_ref, rk_ref,
                 lnw_ref, lnb_ref, tri_ref, ones_ref, mrow_ref, msq_ref,
                 y_ref, nsh_ref, ns_ref, pad_s, car_s, st_s, *, nb, t_valid, zero_state):
    C = CHUNK
    ci = pl.program_id(1)
    last = ci == pl.num_programs(1) - 1
    rows_c = lax.broadcasted_iota(jnp.int32, (C, 1), 0)
    t_idx = lax.broadcasted_iota(jnp.int32, (C, MXU_DIM), 0)
    i_idx = lax.broadcasted_iota(jnp.int32, (C, MXU_DIM), 1) % C
    m_strict = i_idx < t_idx
    m_incl = i_idx <= t_idx
    eye_cat = (i_idx == t_idx).astype(F32)
    ones_bd = ones_ref[...]

    def head_sum(x):
        parts = []
        for g in range(RWKV_WIDTH // MXU_DIM):
            parts.append(_dot_exact(x[:, g * MXU_DIM:(g + 1) * MXU_DIM], ones_bd))
        return jnp.concatenate(parts, axis=1)

    @pl.when(ci == 0)
    def _():
        for b in range(nb):
            car_s[b] = sh_ref[b]
            if zero_state:
                st_s[b] = jnp.zeros(st_s.shape[1:], F32)
            else:
                for g in range(N_GROUPS):
                    rows = [s0_ref[b, g * GROUP_HEADS + hh] for hh in range(GROUP_HEADS)]
                    stacked = jnp.concatenate(rows, axis=0)
                    tiled = jnp.concatenate([stacked] * GROUP_HEADS, axis=1)
                    st_s[b, g] = tiled * msq_ref[...].astype(F32)

    preps, chains = [], []
    for b in range(nb):
        if t_valid == C:
            ps = ps_ref[b]
        else:
            pad_s[...] = jnp.zeros(pad_s.shape, F32)
            pad_s[0:t_valid, :] = ps_ref[b]
            ps = pad_s[...]
        prev = pltpu.roll(ps, 1, axis=0)
        prev = jnp.where(rows_c == 0, car_s[b], prev)
        new_carry = ps[t_valid - 1:t_valid, :]
        car_s[b] = new_carry
        nsh_ref[b] = new_carry

        s = ps + (prev - ps) * mu_ref[...]
        o = RWKV_WIDTH
        r = s[:, :o]
        k = s[:, o:2 * o]
        v = s[:, 2 * o:3 * o]
        dwa = s[:, 3 * o:3 * o + LORA_W + LORA_A]
        dg = s[:, 3 * o + LORA_W + LORA_A:]
        lane = lax.broadcasted_iota(jnp.int32, dwa.shape, 1)
        lora_in = jnp.where(lane < LORA_W, jnp.tanh(dwa), dwa).astype(BF16)
        lora = _dot(lora_in, lora_ref[...])
        w_log = -_softplus(-(w0_ref[...] + lora[:, :o])) - 0.5
        lw = -jnp.exp(w_log)
        a = _sigmoid(a0_ref[...] + lora[:, o:])
        g = _dot(_sigmoid(dg).astype(BF16), gup_ref[...])
        kk = k * kk_ref[...]
        kk = kk / jnp.maximum(jnp.sqrt(head_sum(kk * kk)), 1e-12)
        kmod = k * (1.0 + (a - 1.0) * ka_ref[...])
        if t_valid != C:
            valid = rows_c < t_valid
            lw = jnp.where(valid, lw, 0.0)
            kk = jnp.where(valid, kk, 0.0)
            kmod = jnp.where(valid, kmod, 0.0)
            v = jnp.where(valid, v, 0.0)
        beta = kk * a
        L = jnp.concatenate(
            [_dot_tri(tri_ref[...], lw[:, j * MXU_DIM:(j + 1) * MXU_DIM]) for j in range(o // MXU_DIM)], axis=1)
        e_l = jnp.exp(L)
        e_neg = jnp.exp(-L)
        l_end = L[C - 1:C, :]
        e_end = jnp.exp(l_end - L)
        p_end = jnp.exp(l_end)
        at = (-kk) * jnp.exp(L - lw)
        rt = r * e_l
        bt = beta * e_neg
        kt = kmod * e_neg
        bend = beta * e_end
        kend = kmod * e_end

        preps.append(dict(r=r, v=v, g=g, kmod=kmod))
        for gi in range(N_GROUPS):
            sl = slice(gi * GROUP_LANES, (gi + 1) * GROUP_LANES)
            chains.append(dict(
                b=b, gi=gi, vg=v[:, sl], p_end=p_end[:, sl],
                x2=jnp.concatenate([at[:, sl], rt[:, sl]], axis=0).astype(BF16),
                wbd=jnp.concatenate([_block_diag(bt[:, sl].astype(BF16), mrow_ref),
                                     _block_diag(kt[:, sl].astype(BF16), mrow_ref)], axis=0),
                bk=jnp.concatenate([bend[:, sl], kend[:, sl]], axis=0).astype(BF16)))

    cw = GROUP_HEADS * C
    for c in chains:
        res = _dot_tb(c['x2'], c['wbd'])
        n_cat = jnp.where(m_strict, res[:C, :cw], 0.0)
        c['a_ak'] = jnp.where(m_strict, res[:C, cw:], 0.0).astype(BF16)
        c['a_r'] = jnp.concatenate([jnp.where(m_incl, res[C:, :cw], 0.0),
                                    jnp.where(m_incl, res[C:, cw:], 0.0)], axis=1).astype(BF16)
        c['x_c'] = n_cat.astype(BF16)
        c['p_c'] = eye_cat + n_cat
        c['x_bd'] = _block_diag(c['x_c'], msq_ref)
    for c in chains:
        st = st_s[c['b'], c['gi']]
        c['xs'] = _dot_tb(c['x2'], st.astype(BF16))
        c['v_bd'] = _block_diag(c['vg'].astype(BF16), msq_ref)
    for c in chains:
        c['rhs'] = c['xs'][:C] + _dot(c['a_ak'], c['v_bd'])
    lvl = 2
    while lvl < C:
        for c in chains:
            c['x_c'] = _dot(c['x_c'], c['x_bd']).astype(BF16)
            c['x_bd'] = _block_diag(c['x_c'], msq_ref)
        for c in chains:
            c['p_c'] = c['p_c'] + _dot(c['p_c'].astype(BF16), c['x_bd'])
        lvl *= 2
    for c in chains:
        c['u'] = _dot(c['p_c'].astype(BF16), _block_diag(c['rhs'].astype(BF16), msq_ref))
    ys = {}
    for c in chains:
        ub = c['u'].astype(BF16)
        ys[(c['b'], c['gi'])] = c['xs'][C:] + _dot(c['a_r'], jnp.concatenate([_block_diag(ub, msq_ref), c['v_bd']],
                                                                             axis=0))
        uv = jnp.concatenate([ub, c['vg'].astype(BF16)], axis=0)
        st = st_s[c['b'], c['gi']]
        st_s[c['b'], c['gi']] = st * c['p_end'] + _dot_ta(uv, c['bk']) * msq_ref[...].astype(F32)

    for b in range(nb):
        r, v, g, kmod = (preps[b][n] for n in ('r', 'v', 'g', 'kmod'))
        yc = jnp.concatenate([ys[(b, gi)] for gi in range(N_GROUPS)], axis=1)
        inv_n = 1.0 / RWKV_HEAD
        mean = head_sum(yc) * inv_n
        dlt = yc - mean
        var = head_sum(dlt * dlt) * inv_n
        yn = dlt * lax.rsqrt(var + GN_EPS) * lnw_ref[...] + lnb_ref[...]
        bonus = head_sum(r * kmod * rk_ref[...]) * v
        out = ((yn + bonus) * g).astype(BF16)
        if t_valid == C:
            y_ref[b] = out
        else:
            y_ref[b] = out[0:t_valid, :]

    @pl.when(last)
    def _():
        for b in range(nb):
            for gi in range(N_GROUPS):
                for hh in range(GROUP_HEADS):
                    ns_ref[b, gi * GROUP_HEADS + hh] = st_s[b, gi, hh * RWKV_HEAD:(hh + 1) * RWKV_HEAD,
                                                            hh * RWKV_HEAD:(hh + 1) * RWKV_HEAD]


def _dot_tri(tri_bf16, x):
    hi = x.astype(BF16)
    lo = (x - hi.astype(F32)).astype(BF16)
    return _dot(tri_bf16, hi) + _dot(tri_bf16, lo)


def _rwkv_consts():
    C = CHUNK
    gh = GROUP_HEADS
    tri = np.tril(np.ones((C, C), np.float32))
    hl = np.arange(MXU_DIM) // RWKV_HEAD
    ones_bd = (hl[:, None] == hl[None, :]).astype(np.float32)
    rowh = np.arange(gh * C) // C
    colh = np.arange(GROUP_LANES) // RWKV_HEAD
    mrow = (rowh[:, None] == colh[None, :]).astype(np.float32)
    colc = np.arange(gh * C) // C
    msq = (rowh[:, None] == colc[None, :]).astype(np.float32)
    return (jnp.asarray(tri, BF16), jnp.asarray(ones_bd, BF16), jnp.asarray(mrow, BF16), jnp.asarray(msq, BF16))


def _rwkv(ps3, st_shift, st_rwkv, P, layer, nb, zero_state):
    B, T, _ = ps3.shape
    C = CHUNK
    if T % C == 0:
        t_valid, nchunks, tb = C, T // C, C
    else:
        assert T < C
        t_valid, nchunks, tb = T, 1, T
    o = RWKV_WIDTH
    tri, ones_bd, mrow, msq = _rwkv_consts()
    vec = lambda w: pl.BlockSpec((None, 1, w), lambda b, c: (layer, 0, 0))
    const = lambda arr: pl.BlockSpec(arr.shape, lambda b, c: (0,) * arr.ndim)
    kern = functools.partial(_rwkv_kernel, nb=nb, t_valid=t_valid, zero_state=zero_state)
    state_spec = pl.BlockSpec((nb, RWKV_HEADS, RWKV_HEAD, RWKV_HEAD), lambda b, c: (b, 0, 0, 0))
    return pl.pallas_call(
        kern,
        grid=(B // nb, nchunks),
        in_specs=[
            pl.BlockSpec((nb, tb, SHIFT_WIDTH), lambda b, c: (b, c, 0)),
            pl.BlockSpec((nb, 1, SHIFT_WIDTH), lambda b, c: (b, 0, 0)),
            state_spec,
            vec(SHIFT_WIDTH),
            vec(o),
            pl.BlockSpec((None, LORA_W + LORA_A, 2 * o), lambda b, c: (layer, 0, 0)),
            pl.BlockSpec((None, LORA_G, o), lambda b, c: (layer, 0, 0)),
            vec(o), vec(o), vec(o), vec(o), vec(o), vec(o),
            const(tri), const(ones_bd), const(mrow), const(msq),
        ],
        out_specs=[
            pl.BlockSpec((nb, tb, o), lambda b, c: (b, c, 0)),
            pl.BlockSpec((nb, 1, SHIFT_WIDTH), lambda b, c: (b, 0, 0)),
            state_spec,
        ],
        out_shape=[
            jax.ShapeDtypeStruct((B, T, o), BF16),
            jax.ShapeDtypeStruct((B, 1, SHIFT_WIDTH), F32),
            jax.ShapeDtypeStruct((B, RWKV_HEADS, RWKV_HEAD, RWKV_HEAD), F32),
        ],
        scratch_shapes=[
            pltpu.VMEM((C, SHIFT_WIDTH), F32),
            pltpu.VMEM((nb, 1, SHIFT_WIDTH), F32),
            pltpu.VMEM((nb, N_GROUPS, MXU_DIM, MXU_DIM), F32),
        ],
        compiler_params=_cparams(("arbitrary", "arbitrary")),
        name="rwkv",
    )(ps3, st_shift, st_rwkv, P['shift_mu'], P['w0'], P['lora_wa'], P['g_lora_up'], P['a0'], P['k_k'], P['k_a'],
      P['r_k'], P['ln_x_w'], P['ln_x_b'], tri, ones_bd, mrow, msq)


def _merge_kernel(x_ref, ya_ref, yb_ref, gt_ref, wa_ref, wb_ref, wo_ref, nf_ref, *rest, moe):
    if moe:
        wr_ref, xo_ref, h_ref, gates_ref = rest
    else:
        xo_ref, h_ref = rest
    ga = gt_ref[:, :D_MODEL]
    gb = gt_ref[:, D_MODEL:]
    m = _sigmoid(ga) * _dot(ya_ref[...], wa_ref[...]) + _sigmoid(gb) * _dot(yb_ref[...], wb_ref[...])
    x = x_ref[...] + _dot(m.astype(BF16), wo_ref[...])
    xo_ref[...] = x
    h = _rms(x, nf_ref[...])
    h_ref[...] = h.astype(BF16)
    if moe:
        logits = jnp.dot(h, wr_ref[...], preferred_element_type=F32, precision=lax.Precision.HIGHEST)
        lane = lax.broadcasted_iota(jnp.int32, logits.shape, 1)
        real = lane < N_EXPERTS
        logits = jnp.where(real, logits, -jnp.inf)
        e = jnp.exp(logits - jnp.max(logits, axis=-1, keepdims=True))
        p = jnp.where(real, e / jnp.sum(e, axis=-1, keepdims=True), -1.0)
        m1 = jnp.max(p, axis=-1, keepdims=True)
        i1 = jnp.min(jnp.where(p == m1, lane, LANES), axis=-1, keepdims=True)
        oh1 = lane == i1
        p2 = jnp.where(oh1, -1.0, p)
        m2 = jnp.max(p2, axis=-1, keepdims=True)
        i2 = jnp.min(jnp.where(p2 == m2, lane, LANES), axis=-1, keepdims=True)
        oh2 = lane == i2
        tot = m1 + m2
        gates_ref[...] = jnp.where(oh1, m1 / tot, 0.0) + jnp.where(oh2, m2 / tot, 0.0)


def _merge(x, ya, yb, gt, P, layer, tm, moe):
    n = x.shape[0]
    row = lambda w: pl.BlockSpec((tm, w), lambda i: (i, 0))
    wsq = lambda: pl.BlockSpec((None, D_MODEL, D_MODEL), lambda i: (layer, 0, 0))
    in_specs = [row(D_MODEL), row(D_MODEL), row(D_MODEL), row(2 * D_MODEL), wsq(), wsq(), wsq(),
                pl.BlockSpec((None, 1, D_MODEL), lambda i: (layer, 0, 0))]
    args = [x, ya, yb, gt, P['w_out_a'], P['w_out_b'], P['w_out'], P['norm_ffn']]
    out_specs = [row(D_MODEL), row(D_MODEL)]
    out_shape = [jax.ShapeDtypeStruct((n, D_MODEL), F32), jax.ShapeDtypeStruct((n, D_MODEL), BF16)]
    if moe:
        in_specs.append(pl.BlockSpec((None, D_MODEL, LANES), lambda i: (layer // 2, 0, 0)))
        args.append(P['w_router_pad'])
        out_specs.append(row(LANES))
        out_shape.append(jax.ShapeDtypeStruct((n, LANES), F32))
    return pl.pallas_call(
        functools.partial(_merge_kernel, moe=moe),
        grid=(n // tm,),
        in_specs=in_specs,
        out_specs=out_specs,
        out_shape=out_shape,
        compiler_params=_cparams(("arbitrary",)),
        name="merge",
    )(*args)


def _swiglu_acc(h, wg_ref, wu_ref, wd_ref):
    acc = None
    for c in range(D_FF // FF_CHUNK):
        sl = slice(c * FF_CHUNK, (c + 1) * FF_CHUNK)
        gate = _dot(h, wg_ref[:, sl])
        up = _dot(h, wu_ref[:, sl])
        act = (gate * _sigmoid(gate) * up).astype(BF16)
        part = _dot(act, wd_ref[sl, :])
        acc = part if acc is None else acc + part
    return acc


def _ffn_kernel(x_ref, h_ref, wg_ref, wu_ref, wd_ref, o_ref):
    o_ref[...] = x_ref[...] + _swiglu_acc(h_ref[...], wg_ref, wu_ref, wd_ref)


def _ffn(x, h, P, j, tm):
    n = x.shape[0]
    row = lambda: pl.BlockSpec((tm, D_MODEL), lambda i: (i, 0))
    return pl.pallas_call(
        _ffn_kernel,
        grid=(n // tm,),
        in_specs=[row(), row(),
                  pl.BlockSpec((None, D_MODEL, D_FF), lambda i: (j, 0, 0), pipeline_mode=pl.Buffered(1)),
                  pl.BlockSpec((None, D_MODEL, D_FF), lambda i: (j, 0, 0), pipeline_mode=pl.Buffered(1)),
                  pl.BlockSpec((None, D_FF, D_MODEL), lambda i: (j, 0, 0), pipeline_mode=pl.Buffered(1))],
        out_specs=row(),
        out_shape=jax.ShapeDtypeStruct((n, D_MODEL), F32),
        compiler_params=_cparams(("arbitrary",)),
        name="ffn",
    )(x, h, P['w_ffn_gate'], P['w_ffn_up'], P['w_ffn_down'])


def _moe_kernel(x_ref, h_ref, gates_ref, wg_ref, wu_ref, wd_ref, o_ref):
    e = pl.program_id(1)
    gates = gates_ref[...]
    lane = lax.broadcasted_iota(jnp.int32, gates.shape, 1)
    ge = jnp.sum(jnp.where(lane == e, gates, 0.0), axis=-1, keepdims=True)
    term = ge * _swiglu_acc(h_ref[...], wg_ref, wu_ref, wd_ref)

    @pl.when(e == 0)
    def _():
        o_ref[...] = x_ref[...] + term

    @pl.when(e != 0)
    def _():
        o_ref[...] += term


def _moe(x, h, gates, P, j, tm):
    n = x.shape[0]
    row = lambda w: pl.BlockSpec((tm, w), lambda i, e: (i, 0))
    wspec = lambda a, b: pl.BlockSpec((None, None, a, b), lambda i, e: (j, e, 0, 0))
    return pl.pallas_call(
        _moe_kernel,
        grid=(n // tm, N_EXPERTS),
        in_specs=[row(D_MODEL), row(D_MODEL), row(LANES),
                  wspec(D_MODEL, D_FF), wspec(D_MODEL, D_FF), wspec(D_FF, D_MODEL)],
        out_specs=row(D_MODEL),
        out_shape=jax.ShapeDtypeStruct((n, D_MODEL), F32),
        compiler_params=_cparams(("arbitrary", "arbitrary")),
        name="moe",
    )(x, h, gates, P['w_moe_gate'], P['w_moe_up'], P['w_moe_down'])


def _final_norm_kernel(x_ref, g_ref, o_ref):
    o_ref[...] = _rms(x_ref[...], g_ref[...])


def _final_norm(x, g, tm):
    n = x.shape[0]
    return pl.pallas_call(
        _final_norm_kernel,
        grid=(n // tm,),
        in_specs=[pl.BlockSpec((tm, D_MODEL), lambda i: (i, 0)), pl.BlockSpec((1, D_MODEL), lambda i: (0, 0))],
        out_specs=pl.BlockSpec((tm, D_MODEL), lambda i: (i, 0)),
        out_shape=jax.ShapeDtypeStruct((n, D_MODEL), F32),
        compiler_params=_cparams(("arbitrary",)),
        name="final_norm",
    )(x, g)


def _block_diag_weights(w):
    d = w.shape[0]
    per = MXU_DIM // LRU_BLOCK
    w = w.reshape(d, LRU_HEADS // per, per, LRU_BLOCK, LRU_BLOCK)
    eye = jnp.eye(per, dtype=w.dtype)
    bd = jnp.einsum('dcpij,pq->dcpiqj', w, eye)
    return bd.reshape(d, LRU_HEADS // per, MXU_DIM, MXU_DIM).astype(BF16)


def _prep_params(p):
    P = dict(p)
    for name in ('w_in', 'w_out_a', 'w_out_b', 'w_out', 'w_ffn_gate', 'w_ffn_up', 'w_ffn_down',
                 'w_moe_gate', 'w_moe_up', 'w_moe_down', 'g_lora_up'):
        P[name] = p[name].astype(BF16)
    for name in ('norm_mix', 'conv_b', 'gate_a_b', 'gate_x_b', 'lru_lambda', 'shift_mu', 'w0', 'a0', 'k_k', 'k_a',
                 'ln_x_w', 'ln_x_b', 'norm_ffn'):
        P[name] = p[name][:, None, :]
    P['r_k'] = p['r_k'].reshape(DEPTH, 1, RWKV_WIDTH)
    P['norm_final'] = p['norm_final'][None, :]
    P['gaw_bd'] = _block_diag_weights(p['gate_a_w'])
    P['gxw_bd'] = _block_diag_weights(p['gate_x_w'])
    z = jnp.zeros((DEPTH, LORA_W, RWKV_WIDTH), F32)
    P['lora_wa'] = jnp.concatenate([jnp.concatenate([p['w_lora_up'], z], axis=2),
                                    jnp.concatenate([z, p['a_lora_up']], axis=2)], axis=1).astype(BF16)
    P['w_router_pad'] = jnp.pad(p['w_router'], ((0, 0), (0, 0), (0, LANES - N_EXPERTS)))
    return P


def _trunk(x3, st_rwkv, st_lru, st_conv, st_shift, P, *, zero_state, tm, lru_bb, lru_tt, rwkv_nb):
    B, T, _ = x3.shape
    n = B * T
    x = x3.reshape(n, D_MODEL)
    n_rwkv, n_lru, n_conv, n_shift = [], [], [], []
    for l in range(DEPTH):
        xy, ps, gt = _proj(x, P['norm_mix'], P['w_in'], l, min(tm, 256))
        ya, h_new, c_new = _lru(xy.reshape(B, T, 2 * LRU_WIDTH), st_conv[l], st_lru[l][:, None, :], P, l,
                                lru_bb, lru_tt)
        yb, sh_new, s_new = _rwkv(ps.reshape(B, T, SHIFT_WIDTH), st_shift[l][:, None, :], st_rwkv[l], P, l,
                                  rwkv_nb, zero_state)
        moe = l % 2 == 1
        res = _merge(x, ya.reshape(n, D_MODEL), yb.reshape(n, D_MODEL), gt, P, l, tm, moe)
        if moe:
            x, h, gates = res
            x = _moe(x, h, gates, P, l // 2, tm)
        else:
            x, h = res
            x = _ffn(x, h, P, l // 2, tm)
        n_rwkv.append(s_new)
        n_lru.append(h_new[:, 0, :])
        n_conv.append(c_new)
        n_shift.append(sh_new[:, 0, :])
    y = _final_norm(x, P['norm_final'], tm).reshape(B, T, D_MODEL)
    return y, jnp.stack(n_rwkv), jnp.stack(n_lru), jnp.stack(n_conv), jnp.stack(n_shift)


def kernel(x_prompt, x_sample, state_rwkv, state_lru, state_conv, state_shift, norm_mix, w_in, conv_w, conv_b, gate_a_w, gate_a_b, gate_x_w, gate_x_b, lru_lambda, shift_mu, w0, w_lora_up, a0, a_lora_up, g_lora_up, k_k, k_a, r_k, ln_x_w, ln_x_b, w_out_a, w_out_b, w_out, norm_ffn, w_ffn_gate, w_ffn_up, w_ffn_down, w_router, w_moe_gate, w_moe_up, w_moe_down, norm_final):
    P = _prep_params(dict(
        norm_mix=norm_mix, w_in=w_in, conv_w=conv_w, conv_b=conv_b, gate_a_w=gate_a_w, gate_a_b=gate_a_b,
        gate_x_w=gate_x_w, gate_x_b=gate_x_b, lru_lambda=lru_lambda, shift_mu=shift_mu, w0=w0,
        w_lora_up=w_lora_up, a0=a0, a_lora_up=a_lora_up, g_lora_up=g_lora_up, k_k=k_k, k_a=k_a, r_k=r_k,
        ln_x_w=ln_x_w, ln_x_b=ln_x_b, w_out_a=w_out_a, w_out_b=w_out_b, w_out=w_out, norm_ffn=norm_ffn,
        w_ffn_gate=w_ffn_gate, w_ffn_up=w_ffn_up, w_ffn_down=w_ffn_down, w_router=w_router,
        w_moe_gate=w_moe_gate, w_moe_up=w_moe_up, w_moe_down=w_moe_down, norm_final=norm_final))
    bp = x_prompt.shape[0]
    bs = x_sample.shape[0]
    zeros = lambda *shape: jnp.zeros(shape, F32)
    p_out = _trunk(x_prompt,
                   zeros(DEPTH, bp, RWKV_HEADS, RWKV_HEAD, RWKV_HEAD), zeros(DEPTH, bp, LRU_WIDTH),
                   zeros(DEPTH, bp, CONV_WIDTH - 1, LRU_WIDTH), zeros(DEPTH, bp, SHIFT_WIDTH), P,
                   zero_state=True, tm=512, lru_bb=1, lru_tt=256, rwkv_nb=2)
    s_out = _trunk(x_sample, state_rwkv, state_lru, state_conv, state_shift, P,
                   zero_state=False, tm=256, lru_bb=8, lru_tt=x_sample.shape[1], rwkv_nb=2)
    return (p_out[0], s_out[0]) + tuple(p_out[1:]) + tuple(s_out[1:])
```

```python
import functools

import numpy as np
import jax
import jax.numpy as jnp
from jax import lax
from jax.experimental import pallas as pl
from jax.experimental.pallas import tpu as pltpu

F32 = jnp.float32
BF16 = jnp.bfloat16

D_MODEL = 1024
DEPTH = 4
LRU_WIDTH = D_MODEL
LRU_HEADS = 16
LRU_BLOCK = LRU_WIDTH // LRU_HEADS
CONV_WIDTH = 4
LRU_C = 8.0
RWKV_HEAD = 64
RWKV_WIDTH = D_MODEL
RWKV_HEADS = RWKV_WIDTH // RWKV_HEAD
LORA_W = 64
LORA_A = 64
LORA_G = 128
SHIFT_WIDTH = 3 * RWKV_WIDTH + LORA_W + LORA_A + LORA_G
PROJ_WIDTH = 2 * LRU_WIDTH + SHIFT_WIDTH + 2 * D_MODEL
D_FF = 2816
N_EXPERTS = 8
NORM_EPS = 1e-6
GN_EPS = 64e-5

LANES = 128
SUBLANES = 8
MXU_DIM = 256
VMEM_LIMIT = 56 * 1024 * 1024

CHUNK = 64
GROUP_HEADS = MXU_DIM // CHUNK
GROUP_LANES = GROUP_HEADS * RWKV_HEAD
N_GROUPS = RWKV_HEADS // GROUP_HEADS
FF_CHUNK = 256
MOE_TILE = 1024
MOE_SUB = 128
MOE_FF_SPLIT = 2
SCAN_ROWS = 16
assert CHUNK == RWKV_HEAD


def _cparams(sem):
    return pltpu.CompilerParams(dimension_semantics=sem, vmem_limit_bytes=VMEM_LIMIT)


def _dot(a, b):
    return jnp.dot(a, b, preferred_element_type=F32)


def _dot_tb(a, b):
    return lax.dot_general(a, b, (((1,), (1,)), ((), ())), preferred_element_type=F32)


def _dot_ta(a, b):
    return lax.dot_general(a, b, (((0,), (0,)), ((), ())), preferred_element_type=F32)


def _dot_exact(a, b_bf16):
    hi = a.astype(BF16)
    lo = (a - hi.astype(F32)).astype(BF16)
    return _dot(hi, b_bf16) + _dot(lo, b_bf16)


def _sigmoid(x):
    return jax.nn.sigmoid(x)


def _softplus(z):
    return jnp.maximum(z, 0.0) + jnp.log(1.0 + jnp.exp(-jnp.abs(z)))


def _rms(x, g):
    return x * lax.rsqrt(jnp.mean(x * x, axis=-1, keepdims=True) + NORM_EPS) * g


_PROJ_SPLITS = ((0, 2 * LRU_WIDTH), (2 * LRU_WIDTH, SHIFT_WIDTH), (2 * LRU_WIDTH + SHIFT_WIDTH, 2 * D_MODEL))


def _proj_kernel(x_ref, g_ref, w_ref, xy_ref, ps_ref, gt_ref):
    u = _rms(x_ref[...], g_ref[...]).astype(BF16)
    for out_ref, (c0, width) in zip((xy_ref, ps_ref, gt_ref), _PROJ_SPLITS):
        j = 0
        while j < width:
            w = min(512, width - j)
            out_ref[:, j:j + w] = _dot(u, w_ref[:, c0 + j:c0 + j + w])
            j += w


def _proj(x, g_all, w_all, layer, tm):
    n = x.shape[0]
    return pl.pallas_call(
        _proj_kernel,
        grid=(n // tm,),
        in_specs=[
            pl.BlockSpec((tm, D_MODEL), lambda i: (i, 0)),
            pl.BlockSpec((None, 1, D_MODEL), lambda i: (layer, 0, 0)),
            pl.BlockSpec((None, D_MODEL, PROJ_WIDTH), lambda i: (layer, 0, 0),
                         pipeline_mode=pl.Buffered(1)),
        ],
        out_specs=[
            pl.BlockSpec((tm, 2 * LRU_WIDTH), lambda i: (i, 0)),
            pl.BlockSpec((tm, SHIFT_WIDTH), lambda i: (i, 0)),
            pl.BlockSpec((tm, 2 * D_MODEL), lambda i: (i, 0)),
        ],
        out_shape=[
            jax.ShapeDtypeStruct((n, 2 * LRU_WIDTH), F32),
            jax.ShapeDtypeStruct((n, SHIFT_WIDTH), F32),
            jax.ShapeDtypeStruct((n, 2 * D_MODEL), F32),
        ],
        compiler_params=_cparams(("arbitrary",)),
        name="proj",
    )(x, g_all, w_all)


def _gelu_tanh(x):
    return 0.5 * x * (1.0 + jnp.tanh(np.sqrt(2.0 / np.pi).astype(np.float32) * (x + 0.044715 * (x * x * x))))


def _lru_kernel(xa_ref, ya_ref, cs_ref, h0_ref, cw_ref, cb_ref, gaw_ref, gab_ref, gxw_ref, gxb_ref, lam_ref,
                y_ref, hl_ref, nc_ref, xp_s, h_s, a_s, b_s, *, bb, tt):
    ti = pl.program_id(1)
    neg_c = -LRU_C * _softplus(-lam_ref[...])
    cw = cw_ref[...]
    @pl.when(ti == 0)
    def _():
        for b in range(bb):
            xp_s[b, 5:8, :] = cs_ref[b]
            h_s[b] = h0_ref[b]

    for b in range(bb):
        xa = xa_ref[b]
        xp_s[b, 8:8 + tt, :] = xa
        xc = cb_ref[...] + xp_s[b, 5:5 + tt, :] * cw[0:1]
        xc = xc + xp_s[b, 6:6 + tt, :] * cw[1:2]
        xc = xc + xp_s[b, 7:7 + tt, :] * cw[2:3]
        xc = xc + xa * cw[3:4]
        tail = xp_s[b, 5 + tt:8 + tt, :]
        xp_s[b, 5:8, :] = tail
        nc_ref[b] = tail

        for c in range(LRU_WIDTH // MXU_DIM):
            sl = slice(c * MXU_DIM, (c + 1) * MXU_DIM)
            xcc = xc[:, sl]
            xcb = xcc.astype(BF16)
            r = _sigmoid(_dot(xcb, gaw_ref[c]) + gab_ref[:, sl])
            i = _sigmoid(_dot(xcb, gxw_ref[c]) + gxb_ref[:, sl])
            log_a = neg_c[:, sl] * r
            a = jnp.exp(log_a)
            a_s[:, sl] = a
            b_s[:, sl] = jnp.sqrt(-jnp.tanh(log_a) * (a * a + 1.0)) * (i * xcc)

        if tt % SCAN_ROWS == 0:
            row = lax.broadcasted_iota(jnp.int32, (SUBLANES, LRU_WIDTH), 0)

            def blk(j, h):
                r0 = pl.multiple_of(j * SCAN_ROWS, SCAN_ROWS)
                hbs = []
                for q in range(SCAN_ROWS // SUBLANES):
                    av = a_s[pl.ds(r0 + q * SUBLANES, SUBLANES), :]
                    bv = b_s[pl.ds(r0 + q * SUBLANES, SUBLANES), :]
                    for d in (1, 2, 4):
                        m = row >= d
                        a_sh = pltpu.roll(av, d, axis=0)
                        b_sh = pltpu.roll(bv, d, axis=0)
                        bv = jnp.where(m, av * b_sh + bv, bv)
                        av = jnp.where(m, av * a_sh, av)
                    hb = av * h + bv
                    h = hb[SUBLANES - 1:SUBLANES, :]
                    hbs.append(hb)
                hs = jnp.concatenate(hbs, axis=0)
                y_ref[b, pl.ds(r0, SCAN_ROWS), :] = (hs * _gelu_tanh(ya_ref[b, pl.ds(r0, SCAN_ROWS), :])).astype(BF16)
                return h

            h = lax.fori_loop(0, tt // SCAN_ROWS, blk, h_s[b])
        else:
            h = h_s[b]
            rows = []
            for t in range(tt):
                h = a_s[t:t + 1, :] * h + b_s[t:t + 1, :]
                rows.append(h)
            hs = jnp.concatenate(rows, axis=0)
            y_ref[b] = (hs * _gelu_tanh(ya_ref[b])).astype(BF16)
        h_s[b] = h
        hl_ref[b] = h


def _lru(xy3, st_conv, st_lru, P, layer, bb, tt):
    B, T, _ = xy3.shape
    W = LRU_WIDTH
    vec = lambda: pl.BlockSpec((None, 1, W), lambda b, t: (layer, 0, 0))
    kern = functools.partial(_lru_kernel, bb=bb, tt=tt)
    return pl.pallas_call(
        kern,
        grid=(B // bb, T // tt),
        in_specs=[
            pl.BlockSpec((bb, tt, W), lambda b, t: (b, t, 0)),
            pl.BlockSpec((bb, tt, W), lambda b, t: (b, t, 1)),
            pl.BlockSpec((bb, CONV_WIDTH - 1, W), lambda b, t: (b, 0, 0)),
            pl.BlockSpec((bb, 1, W), lambda b, t: (b, 0, 0)),
            pl.BlockSpec((None, CONV_WIDTH, W), lambda b, t: (layer, 0, 0)),
            vec(),
            pl.BlockSpec((None, W // MXU_DIM, MXU_DIM, MXU_DIM), lambda b, t: (layer, 0, 0, 0)),
            vec(),
            pl.BlockSpec((None, W // MXU_DIM, MXU_DIM, MXU_DIM), lambda b, t: (layer, 0, 0, 0)),
            vec(),
            vec(),
        ],
        out_specs=[
            pl.BlockSpec((bb, tt, W), lambda b, t: (b, t, 0)),
            pl.BlockSpec((bb, 1, W), lambda b, t: (b, 0, 0)),
            pl.BlockSpec((bb, CONV_WIDTH - 1, W), lambda b, t: (b, 0, 0)),
        ],
        out_shape=[
            jax.ShapeDtypeStruct((B, T, W), BF16),
            jax.ShapeDtypeStruct((B, 1, W), F32),
            jax.ShapeDtypeStruct((B, CONV_WIDTH - 1, W), F32),
        ],
        scratch_shapes=[
            pltpu.VMEM((bb, tt + 8, W), F32),
            pltpu.VMEM((bb, 1, W), F32),
            pltpu.VMEM((tt, W), F32),
            pltpu.VMEM((tt, W), F32),
        ],
        compiler_params=_cparams(("arbitrary", "arbitrary")),
        name="lru",
    )(xy3, xy3, st_conv, st_lru, P['conv_w'], P['conv_b'], P['gaw_bd'], P['gate_a_b'], P['gxw_bd'],
      P['gate_x_b'], P['lru_lambda'])


def _block_diag(x_bf16, mask_ref):
    return jnp.concatenate([x_bf16] * GROUP_HEADS, axis=0) * mask_ref[...]


def _rwkv_kernel(ps_ref, sh_ref, s0_ref, mu_ref, w0_ref, lora_ref, gup_ref, a0_ref, kk_ref, ka_ref, rk_ref,
                 lnw_ref, lnb_ref, tri_ref, ones_ref, mrow_ref, msq_ref,
                 y_ref, nsh_ref, ns_ref, pad_s, car_s, st_s, *, nb, t_valid, zero_state):
    C = CHUNK
    ci = pl.program_id(1)
    last = ci == pl.num_programs(1) - 1
    rows_c = lax.broadcasted_iota(jnp.int32, (C, 1), 0)
    t_idx = lax.broadcasted_iota(jnp.int32, (C, MXU_DIM), 0)
    i_idx = lax.broadcasted_iota(jnp.int32, (C, MXU_DIM), 1) % C
    m_strict = i_idx < t_idx
    m_incl = i_idx <= t_idx
    eye_cat = (i_idx == t_idx).astype(F32)
    ones_bd = ones_ref[...]

    def head_sum(x):
        parts = []
        for g in range(RWKV_WIDTH // MXU_DIM):
            parts.append(_dot_exact(x[:, g * MXU_DIM:(g + 1) * MXU_DIM], ones_bd))
        return jnp.concatenate(parts, axis=1)

    @pl.when(ci == 0)
    def _():
        for b in range(nb):
            car_s[b] = sh_ref[b]
            if zero_state:
                st_s[b] = jnp.zeros(st_s.shape[1:], F32)
            else:
                for g in range(N_GROUPS):
                    rows = [s0_ref[b, g * GROUP_HEADS + hh] for hh in range(GROUP_HEADS)]
                    stacked = jnp.concatenate(rows, axis=0)
                    tiled = jnp.concatenate([stacked] * GROUP_HEADS, axis=1)
                    st_s[b, g] = tiled * msq_ref[...].astype(F32)

    preps, chains = [], []
    for b in range(nb):
        if t_valid == C:
            ps = ps_ref[b]
        else:
            pad_s[...] = jnp.zeros(pad_s.shape, F32)
            pad_s[0:t_valid, :] = ps_ref[b]
            ps = pad_s[...]
        prev = pltpu.roll(ps, 1, axis=0)
        prev = jnp.where(rows_c == 0, car_s[b], prev)
        new_carry = ps[t_valid - 1:t_valid, :]
        car_s[b] = new_carry
        nsh_ref[b] = new_carry

        s = ps + (prev - ps) * mu_ref[...]
        o = RWKV_WIDTH
        r = s[:, :o]
        k = s[:, o:2 * o]
        v = s[:, 2 * o:3 * o]
        dwa = s[:, 3 * o:3 * o + LORA_W + LORA_A]
        dg = s[:, 3 * o + LORA_W + LORA_A:]
        lane = lax.broadcasted_iota(jnp.int32, dwa.shape, 1)
        lora_in = jnp.where(lane < LORA_W, jnp.tanh(dwa), dwa).astype(BF16)
        lora = _dot(lora_in, lora_ref[...])
        w_log = -_softplus(-(w0_ref[...] + lora[:, :o])) - 0.5
        lw = -jnp.exp(w_log)
        a = _sigmoid(a0_ref[...] + lora[:, o:])
        g = _dot(_sigmoid(dg).astype(BF16), gup_ref[...])
        kk = k * kk_ref[...]
        kk = kk / jnp.maximum(jnp.sqrt(head_sum(kk * kk)), 1e-12)
        kmod = k * (1.0 + (a - 1.0) * ka_ref[...])
        if t_valid != C:
            valid = rows_c < t_valid
            lw = jnp.where(valid, lw, 0.0)
            kk = jnp.where(valid, kk, 0.0)
            kmod = jnp.where(valid, kmod, 0.0)
            v = jnp.where(valid, v, 0.0)
        beta = kk * a
        L = jnp.concatenate(
            [_dot_tri(tri_ref[...], lw[:, j * MXU_DIM:(j + 1) * MXU_DIM]) for j in range(o // MXU_DIM)], axis=1)
        e_l = jnp.exp(L)
        e_neg = jnp.exp(-L)
        l_end = L[C - 1:C, :]
        e_end = jnp.exp(l_end - L)
        p_end = jnp.exp(l_end)
        at = (-kk) * jnp.exp(L - lw)
        rt = r * e_l
        bt = beta * e_neg
        kt = kmod * e_neg
        bend = beta * e_end
        kend = kmod * e_end

        preps.append(dict(r=r, v=v, g=g, kmod=kmod))
        for gi in range(N_GROUPS):
            sl = slice(gi * GROUP_LANES, (gi + 1) * GROUP_LANES)
            chains.append(dict(
                b=b, gi=gi, vg=v[:, sl], p_end=p_end[:, sl],
                x2=jnp.concatenate([at[:, sl], rt[:, sl]], axis=0).astype(BF16),
                wbd=jnp.concatenate([_block_diag(bt[:, sl].astype(BF16), mrow_ref),
                                     _block_diag(kt[:, sl].astype(BF16), mrow_ref)], axis=0),
                bk=jnp.concatenate([bend[:, sl], kend[:, sl]], axis=0).astype(BF16)))

    cw = GROUP_HEADS * C
    for c in chains:
        res = _dot_tb(c['x2'], c['wbd'])
        n_cat = jnp.where(m_strict, res[:C, :cw], 0.0)
        c['a_ak'] = jnp.where(m_strict, res[:C, cw:], 0.0).astype(BF16)
        c['a_r'] = jnp.concatenate([jnp.where(m_incl, res[C:, :cw], 0.0),
                                    jnp.where(m_incl, res[C:, cw:], 0.0)], axis=1).astype(BF16)
        c['x_c'] = n_cat.astype(BF16)
        c['p_c'] = eye_cat + n_cat
        c['x_bd'] = _block_diag(c['x_c'], msq_ref)
    for c in chains:
        st = st_s[c['b'], c['gi']]
        c['xs'] = _dot_tb(c['x2'], st.astype(BF16))
        c['v_bd'] = _block_diag(c['vg'].astype(BF16), msq_ref)
    for c in chains:
        c['rhs'] = c['xs'][:C] + _dot(c['a_ak'], c['v_bd'])
    lvl = 2
    while lvl < C:
        for c in chains:
            c['x_c'] = _dot(c['x_c'], c['x_bd']).astype(BF16)
            c['x_bd'] = _block_diag(c['x_c'], msq_ref)
        for c in chains:
            c['p_c'] = c['p_c'] + _dot(c['p_c'].astype(BF16), c['x_bd'])
        lvl *= 2
    for c in chains:
        c['u'] = _dot(c['p_c'].astype(BF16), _block_diag(c['rhs'].astype(BF16), msq_ref))
    ys = {}
    for c in chains:
        ub = c['u'].astype(BF16)
        ys[(c['b'], c['gi'])] = c['xs'][C:] + _dot(c['a_r'], jnp.concatenate([_block_diag(ub, msq_ref), c['v_bd']],
                                                                             axis=0))
        uv = jnp.concatenate([ub, c['vg'].astype(BF16)], axis=0)
        st = st_s[c['b'], c['gi']]
        st_s[c['b'], c['gi']] = st * c['p_end'] + _dot_ta(uv, c['bk']) * msq_ref[...].astype(F32)

    for b in range(nb):
        r, v, g, kmod = (preps[b][n] for n in ('r', 'v', 'g', 'kmod'))
        yc = jnp.concatenate([ys[(b, gi)] for gi in range(N_GROUPS)], axis=1)
        inv_n = 1.0 / RWKV_HEAD
        mean = head_sum(yc) * inv_n
        dlt = yc - mean
        var = head_sum(dlt * dlt) * inv_n
        yn = dlt * lax.rsqrt(var + GN_EPS) * lnw_ref[...] + lnb_ref[...]
        bonus = head_sum(r * kmod * rk_ref[...]) * v
        out = ((yn + bonus) * g).astype(BF16)
        if t_valid == C:
            y_ref[b] = out
        else:
            y_ref[b] = out[0:t_valid, :]

    @pl.when(last)
    def _():
        for b in range(nb):
            for gi in range(N_GROUPS):
                for hh in range(GROUP_HEADS):
                    ns_ref[b, gi * GROUP_HEADS + hh] = st_s[b, gi, hh * RWKV_HEAD:(hh + 1) * RWKV_HEAD,
                                                            hh * RWKV_HEAD:(hh + 1) * RWKV_HEAD]


def _dot_tri(tri_bf16, x):
    hi = x.astype(BF16)
    lo = (x - hi.astype(F32)).astype(BF16)
    return _dot(tri_bf16, hi) + _dot(tri_bf16, lo)


def _rwkv_consts():
    C = CHUNK
    gh = GROUP_HEADS
    tri = np.tril(np.ones((C, C), np.float32))
    hl = np.arange(MXU_DIM) // RWKV_HEAD
    ones_bd = (hl[:, None] == hl[None, :]).astype(np.float32)
    rowh = np.arange(gh * C) // C
    colh = np.arange(GROUP_LANES) // RWKV_HEAD
    mrow = (rowh[:, None] == colh[None, :]).astype(np.float32)
    colc = np.arange(gh * C) // C
    msq = (rowh[:, None] == colc[None, :]).astype(np.float32)
    return (jnp.asarray(tri, BF16), jnp.asarray(ones_bd, BF16), jnp.asarray(mrow, BF16), jnp.asarray(msq, BF16))


def _rwkv(ps3, st_shift, st_rwkv, P, layer, nb, zero_state):
    B, T, _ = ps3.shape
    C = CHUNK
    if T % C == 0:
        t_valid, nchunks, tb = C, T // C, C
    else:
        assert T < C
        t_valid, nchunks, tb = T, 1, T
    o = RWKV_WIDTH
    tri, ones_bd, mrow, msq = _rwkv_consts()
    vec = lambda w: pl.BlockSpec((None, 1, w), lambda b, c: (layer, 0, 0))
    const = lambda arr: pl.BlockSpec(arr.shape, lambda b, c: (0,) * arr.ndim)
    kern = functools.partial(_rwkv_kernel, nb=nb, t_valid=t_valid, zero_state=zero_state)
    state_spec = pl.BlockSpec((nb, RWKV_HEADS, RWKV_HEAD, RWKV_HEAD), lambda b, c: (b, 0, 0, 0))
    return pl.pallas_call(
        kern,
        grid=(B // nb, nchunks),
        in_specs=[
            pl.BlockSpec((nb, tb, SHIFT_WIDTH), lambda b, c: (b, c, 0)),
            pl.BlockSpec((nb, 1, SHIFT_WIDTH), lambda b, c: (b, 0, 0)),
            state_spec,
            vec(SHIFT_WIDTH),
            vec(o),
            pl.BlockSpec((None, LORA_W + LORA_A, 2 * o), lambda b, c: (layer, 0, 0)),
            pl.BlockSpec((None, LORA_G, o), lambda b, c: (layer, 0, 0)),
            vec(o), vec(o), vec(o), vec(o), vec(o), vec(o),
            const(tri), const(ones_bd), const(mrow), const(msq),
        ],
        out_specs=[
            pl.BlockSpec((nb, tb, o), lambda b, c: (b, c, 0)),
            pl.BlockSpec((nb, 1, SHIFT_WIDTH), lambda b, c: (b, 0, 0)),
            state_spec,
        ],
        out_shape=[
            jax.ShapeDtypeStruct((B, T, o), BF16),
            jax.ShapeDtypeStruct((B, 1, SHIFT_WIDTH), F32),
            jax.ShapeDtypeStruct((B, RWKV_HEADS, RWKV_HEAD, RWKV_HEAD), F32),
        ],
        scratch_shapes=[
            pltpu.VMEM((C, SHIFT_WIDTH), F32),
            pltpu.VMEM((nb, 1, SHIFT_WIDTH), F32),
            pltpu.VMEM((nb, N_GROUPS, MXU_DIM, MXU_DIM), F32),
        ],
        compiler_params=_cparams(("arbitrary", "arbitrary")),
        name="rwkv",
    )(ps3, st_shift, st_rwkv, P['shift_mu'], P['w0'], P['lora_wa'], P['g_lora_up'], P['a0'], P['k_k'], P['k_a'],
      P['r_k'], P['ln_x_w'], P['ln_x_b'], tri, ones_bd, mrow, msq)


def _merge_kernel(x_ref, ya_ref, yb_ref, gt_ref, wa_ref, wb_ref, wo_ref, nf_ref, *rest, moe):
    if moe:
        wr_ref, xo_ref, h_ref, gates_ref, cnt_ref = rest
    else:
        xo_ref, h_ref = rest
    ga = gt_ref[:, :D_MODEL]
    gb = gt_ref[:, D_MODEL:]
    m = _sigmoid(ga) * _dot(ya_ref[...], wa_ref[...]) + _sigmoid(gb) * _dot(yb_ref[...], wb_ref[...])
    x = x_ref[...] + _dot(m.astype(BF16), wo_ref[...])
    xo_ref[...] = x
    h = _rms(x, nf_ref[...])
    h_ref[...] = h.astype(BF16)
    if moe:
        logits = jnp.dot(h, wr_ref[...], preferred_element_type=F32, precision=lax.Precision.HIGHEST)
        lane = lax.broadcasted_iota(jnp.int32, logits.shape, 1)
        real = lane < N_EXPERTS
        logits = jnp.where(real, logits, -jnp.inf)
        e = jnp.exp(logits - jnp.max(logits, axis=-1, keepdims=True))
        p = jnp.where(real, e / jnp.sum(e, axis=-1, keepdims=True), -1.0)
        m1 = jnp.max(p, axis=-1, keepdims=True)
        i1 = jnp.min(jnp.where(p == m1, lane, LANES), axis=-1, keepdims=True)
        oh1 = lane == i1
        p2 = jnp.where(oh1, -1.0, p)
        m2 = jnp.max(p2, axis=-1, keepdims=True)
        i2 = jnp.min(jnp.where(p2 == m2, lane, LANES), axis=-1, keepdims=True)
        oh2 = lane == i2
        tot = m1 + m2
        gates = jnp.where(oh1, m1 / tot, 0.0) + jnp.where(oh2, m2 / tot, 0.0)
        gates_ref[...] = gates
        cnt = jnp.sum((gates > 0.0).astype(F32), axis=0, keepdims=True)
        cnt_ref[...] = jnp.broadcast_to(cnt, (SUBLANES, LANES))


def _merge(x, ya, yb, gt, P, layer, tm, moe):
    n = x.shape[0]
    row = lambda w: pl.BlockSpec((tm, w), lambda i: (i, 0))
    wsq = lambda: pl.BlockSpec((None, D_MODEL, D_MODEL), lambda i: (layer, 0, 0))
    in_specs = [row(D_MODEL), row(D_MODEL), row(D_MODEL), row(2 * D_MODEL), wsq(), wsq(), wsq(),
                pl.BlockSpec((None, 1, D_MODEL), lambda i: (layer, 0, 0))]
    args = [x, ya, yb, gt, P['w_out_a'], P['w_out_b'], P['w_out'], P['norm_ffn']]
    out_specs = [row(D_MODEL), row(D_MODEL)]
    out_shape = [jax.ShapeDtypeStruct((n, D_MODEL), F32), jax.ShapeDtypeStruct((n, D_MODEL), BF16)]
    if moe:
        in_specs.append(pl.BlockSpec((None, D_MODEL, LANES), lambda i: (layer // 2, 0, 0)))
        args.append(P['w_router_pad'])
        out_specs.append(row(LANES))
        out_shape.append(jax.ShapeDtypeStruct((n, LANES), F32))
        out_specs.append(pl.BlockSpec((None, SUBLANES, LANES), lambda i: (i, 0, 0)))
        out_shape.append(jax.ShapeDtypeStruct((n // tm, SUBLANES, LANES), F32))
    return pl.pallas_call(
        functools.partial(_merge_kernel, moe=moe),
        grid=(n // tm,),
        in_specs=in_specs,
        out_specs=out_specs,
        out_shape=out_shape,
        compiler_params=_cparams(("arbitrary",)),
        name="merge",
    )(*args)


def _swiglu_acc(h, wg_ref, wu_ref, wd_ref, width=D_FF):
    acc = None
    for c0 in range(0, width, FF_CHUNK):
        sl = slice(c0, min(c0 + FF_CHUNK, width))
        gate = _dot(h, wg_ref[:, sl])
        up = _dot(h, wu_ref[:, sl])
        act = (gate * _sigmoid(gate) * up).astype(BF16)
        part = _dot(act, wd_ref[sl, :])
        acc = part if acc is None else acc + part
    return acc


def _ffn_kernel(x_ref, h_ref, wg_ref, wu_ref, wd_ref, o_ref):
    o_ref[...] = x_ref[...] + _swiglu_acc(h_ref[...], wg_ref, wu_ref, wd_ref)


def _ffn(x, h, P, j, tm):
    n = x.shape[0]
    row = lambda: pl.BlockSpec((tm, D_MODEL), lambda i: (i, 0))
    return pl.pallas_call(
        _ffn_kernel,
        grid=(n // tm,),
        in_specs=[row(), row(),
                  pl.BlockSpec((None, D_MODEL, D_FF), lambda i: (j, 0, 0), pipeline_mode=pl.Buffered(1)),
                  pl.BlockSpec((None, D_MODEL, D_FF), lambda i: (j, 0, 0), pipeline_mode=pl.Buffered(1)),
                  pl.BlockSpec((None, D_FF, D_MODEL), lambda i: (j, 0, 0), pipeline_mode=pl.Buffered(1))],
        out_specs=row(),
        out_shape=jax.ShapeDtypeStruct((n, D_MODEL), F32),
        compiler_params=_cparams(("arbitrary",)),
        name="ffn",
    )(x, h, P['w_ffn_gate'], P['w_ffn_up'], P['w_ffn_down'])


def _moe_kernel(cnt_ref, x_ref, h_ref, gates_ref, wg_ref, wu_ref, wd_ref, o_ref, rank_t_s, rank_n_s, xg_s, y_s,
                *, tw):
    i = pl.program_id(0)
    e = pl.program_id(1)
    f = pl.program_id(2)
    sb = min(MOE_SUB, tw)

    @pl.when((e == 0) & (f == 0))
    def _():
        o_ref[...] = x_ref[...]
        mask = (gates_ref[...] > 0.0).astype(BF16)
        r_idx = lax.broadcasted_iota(jnp.int32, (tw, tw), 0)
        c_idx = lax.broadcasted_iota(jnp.int32, (tw, tw), 1)
        before = (r_idx < c_idx).astype(BF16)
        upto = (r_idx <= c_idx).astype(BF16)
        rank_t_s[:, 0:tw] = _dot_ta(mask, before)
        rank_t_s[:, tw:2 * tw] = _dot_ta(mask, upto)
        rank_n_s[:, 0:LANES] = _dot_ta(before, mask)
        rank_n_s[:, LANES:2 * LANES] = _dot_ta(upto, mask)

    cnt = cnt_ref[i * N_EXPERTS + e]
    nblk = (cnt + sb - 1) // sb
    lane = lax.broadcasted_iota(jnp.int32, (tw, LANES), 1)
    pick = lane == e
    col = lambda a: jnp.sum(jnp.where(pick, a, 0.0), axis=-1, keepdims=True)
    gate_col = col(gates_ref[...])
    excl_col = col(rank_n_s[:, 0:LANES])
    incl_col = col(rank_n_s[:, LANES:2 * LANES])
    excl_row = rank_t_s[pl.ds(e, 1), 0:tw]
    incl_row = rank_t_s[pl.ds(e, 1), tw:2 * tw]

    def block(k, carry):
        r0 = pl.multiple_of(k * sb, sb)
        rr_col = (lax.broadcasted_iota(jnp.int32, (sb, 1), 0) + k * sb).astype(F32)
        rr_row = (lax.broadcasted_iota(jnp.int32, (1, sb), 1) + k * sb).astype(F32)

        @pl.when(f == 0)
        def _():
            gather = ((excl_row == rr_col) & (incl_row == rr_col + 1.0)).astype(BF16)
            xg_s[pl.ds(r0, sb), :] = _dot(gather, h_ref[...]).astype(BF16)
            y_s[pl.ds(r0, sb), :] = _swiglu_acc(xg_s[pl.ds(r0, sb), :], wg_ref, wu_ref, wd_ref, D_FF // MOE_FF_SPLIT)

        @pl.when(f == MOE_FF_SPLIT - 1)
        def _():
            y = y_s[pl.ds(r0, sb), :] + _swiglu_acc(xg_s[pl.ds(r0, sb), :], wg_ref, wu_ref, wd_ref,
                                                    D_FF // MOE_FF_SPLIT)
            hi = y.astype(BF16)
            lo = (y - hi.astype(F32)).astype(BF16)
            scatter = ((excl_col == rr_row) & (incl_col == rr_row + 1.0)).astype(BF16)
            o_ref[...] += gate_col * (_dot(scatter, hi) + _dot(scatter, lo))

        return carry

    lax.fori_loop(0, nblk, block, 0)


def _moe(x, h, gates, counts, P, j, tw):
    n = x.shape[0]
    assert MOE_FF_SPLIT == 2
    ffw = D_FF // MOE_FF_SPLIT
    row = lambda w: pl.BlockSpec((tw, w), lambda i, e, f, c: (i, 0), pipeline_mode=pl.Buffered(1))
    grid_spec = pltpu.PrefetchScalarGridSpec(
        num_scalar_prefetch=1,
        grid=(n // tw, N_EXPERTS, MOE_FF_SPLIT),
        in_specs=[row(D_MODEL), row(D_MODEL), row(LANES),
                  pl.BlockSpec((None, None, D_MODEL, ffw), lambda i, e, f, c: (j, e, 0, f)),
                  pl.BlockSpec((None, None, D_MODEL, ffw), lambda i, e, f, c: (j, e, 0, f)),
                  pl.BlockSpec((None, None, ffw, D_MODEL), lambda i, e, f, c: (j, e, f, 0))],
        out_specs=pl.BlockSpec((tw, D_MODEL), lambda i, e, f, c: (i, 0)),
        scratch_shapes=[pltpu.VMEM((LANES, 2 * tw), F32), pltpu.VMEM((tw, 2 * LANES), F32),
                        pltpu.VMEM((tw, D_MODEL), BF16), pltpu.VMEM((tw, D_MODEL), F32)],
    )
    return pl.pallas_call(
        functools.partial(_moe_kernel, tw=tw),
        grid_spec=grid_spec,
        out_shape=jax.ShapeDtypeStruct((n, D_MODEL), F32),
        compiler_params=_cparams(("arbitrary", "arbitrary", "arbitrary")),
        name="moe",
    )(counts, x, h, gates, P['w_moe_gate'], P['w_moe_up'], P['w_moe_down'])


def _final_norm_kernel(x_ref, g_ref, o_ref):
    o_ref[...] = _rms(x_ref[...], g_ref[...])


def _final_norm(x, g, tm):
    n = x.shape[0]
    return pl.pallas_call(
        _final_norm_kernel,
        grid=(n // tm,),
        in_specs=[pl.BlockSpec((tm, D_MODEL), lambda i: (i, 0)), pl.BlockSpec((1, D_MODEL), lambda i: (0, 0))],
        out_specs=pl.BlockSpec((tm, D_MODEL), lambda i: (i, 0)),
        out_shape=jax.ShapeDtypeStruct((n, D_MODEL), F32),
        compiler_params=_cparams(("arbitrary",)),
        name="final_norm",
    )(x, g)


def _block_diag_weights(w):
    d = w.shape[0]
    per = MXU_DIM // LRU_BLOCK
    w = w.reshape(d, LRU_HEADS // per, per, LRU_BLOCK, LRU_BLOCK)
    eye = jnp.eye(per, dtype=w.dtype)
    bd = jnp.einsum('dcpij,pq->dcpiqj', w, eye)
    return bd.reshape(d, LRU_HEADS // per, MXU_DIM, MXU_DIM).astype(BF16)


def _prep_params(p):
    P = dict(p)
    for name in ('w_in', 'w_out_a', 'w_out_b', 'w_out', 'w_ffn_gate', 'w_ffn_up', 'w_ffn_down',
                 'w_moe_gate', 'w_moe_up', 'w_moe_down', 'g_lora_up'):
        P[name] = p[name].astype(BF16)
    for name in ('norm_mix', 'conv_b', 'gate_a_b', 'gate_x_b', 'lru_lambda', 'shift_mu', 'w0', 'a0', 'k_k', 'k_a',
                 'ln_x_w', 'ln_x_b', 'norm_ffn'):
        P[name] = p[name][:, None, :]
    P['r_k'] = p['r_k'].reshape(DEPTH, 1, RWKV_WIDTH)
    P['norm_final'] = p['norm_final'][None, :]
    P['gaw_bd'] = _block_diag_weights(p['gate_a_w'])
    P['gxw_bd'] = _block_diag_weights(p['gate_x_w'])
    z = jnp.zeros((DEPTH, LORA_W, RWKV_WIDTH), F32)
    P['lora_wa'] = jnp.concatenate([jnp.concatenate([p['w_lora_up'], z], axis=2),
                                    jnp.concatenate([z, p['a_lora_up']], axis=2)], axis=1).astype(BF16)
    P['w_router_pad'] = jnp.pad(p['w_router'], ((0, 0), (0, 0), (0, LANES - N_EXPERTS)))
    return P


def _trunk(x3, st_rwkv, st_lru, st_conv, st_shift, P, *, zero_state, tm, lru_bb, lru_tt, rwkv_nb):
    B, T, _ = x3.shape
    n = B * T
    x = x3.reshape(n, D_MODEL)
    n_rwkv, n_lru, n_conv, n_shift = [], [], [], []
    for l in range(DEPTH):
        xy, ps, gt = _proj(x, P['norm_mix'], P['w_in'], l, min(tm, 256))
        ya, h_new, c_new = _lru(xy.reshape(B, T, 2 * LRU_WIDTH), st_conv[l], st_lru[l][:, None, :], P, l,
                                lru_bb, lru_tt)
        yb, sh_new, s_new = _rwkv(ps.reshape(B, T, SHIFT_WIDTH), st_shift[l][:, None, :], st_rwkv[l], P, l,
                                  rwkv_nb, zero_state)
        moe = l % 2 == 1
        res = _merge(x, ya.reshape(n, D_MODEL), yb.reshape(n, D_MODEL), gt, P, l, tm, moe)
        if moe:
            x, h, gates, cnt = res
            tw = min(n, MOE_TILE)
            counts = cnt[:, 0, :N_EXPERTS].reshape(n // tw, tw // tm, N_EXPERTS).sum(axis=1)
            x = _moe(x, h, gates, counts.astype(jnp.int32).reshape(-1), P, l // 2, tw)
        else:
            x, h = res
            x = _ffn(x, h, P, l // 2, tm)
        n_rwkv.append(s_new)
        n_lru.append(h_new[:, 0, :])
        n_conv.append(c_new)
        n_shift.append(sh_new[:, 0, :])
    y = _final_norm(x, P['norm_final'], tm).reshape(B, T, D_MODEL)
    return y, jnp.stack(n_rwkv), jnp.stack(n_lru), jnp.stack(n_conv), jnp.stack(n_shift)


def kernel(x_prompt, x_sample, state_rwkv, state_lru, state_conv, state_shift, norm_mix, w_in, conv_w, conv_b, gate_a_w, gate_a_b, gate_x_w, gate_x_b, lru_lambda, shift_mu, w0, w_lora_up, a0, a_lora_up, g_lora_up, k_k, k_a, r_k, ln_x_w, ln_x_b, w_out_a, w_out_b, w_out, norm_ffn, w_ffn_gate, w_ffn_up, w_ffn_down, w_router, w_moe_gate, w_moe_up, w_moe_down, norm_final):
    P = _prep_params(dict(
        norm_mix=norm_mix, w_in=w_in, conv_w=conv_w, conv_b=conv_b, gate_a_w=gate_a_w, gate_a_b=gate_a_b,
        gate_x_w=gate_x_w, gate_x_b=gate_x_b, lru_lambda=lru_lambda, shift_mu=shift_mu, w0=w0,
        w_lora_up=w_lora_up, a0=a0, a_lora_up=a_lora_up, g_lora_up=g_lora_up, k_k=k_k, k_a=k_a, r_k=r_k,
        ln_x_w=ln_x_w, ln_x_b=ln_x_b, w_out_a=w_out_a, w_out_b=w_out_b, w_out=w_out, norm_ffn=norm_ffn,
        w_ffn_gate=w_ffn_gate, w_ffn_up=w_ffn_up, w_ffn_down=w_ffn_down, w_router=w_router,
        w_moe_gate=w_moe_gate, w_moe_up=w_moe_up, w_moe_down=w_moe_down, norm_final=norm_final))
    bp = x_prompt.shape[0]
    bs = x_sample.shape[0]
    zeros = lambda *shape: jnp.zeros(shape, F32)
    p_out = _trunk(x_prompt,
                   zeros(DEPTH, bp, RWKV_HEADS, RWKV_HEAD, RWKV_HEAD), zeros(DEPTH, bp, LRU_WIDTH),
                   zeros(DEPTH, bp, CONV_WIDTH - 1, LRU_WIDTH), zeros(DEPTH, bp, SHIFT_WIDTH), P,
                   zero_state=True, tm=512, lru_bb=1, lru_tt=256, rwkv_nb=2)
    s_out = _trunk(x_sample, state_rwkv, state_lru, state_conv, state_shift, P,
                   zero_state=False, tm=256, lru_bb=8, lru_tt=x_sample.shape[1], rwkv_nb=2)
    return (p_out[0], s_out[0]) + tuple(p_out[1:]) + tuple(s_out[1:])
```

```python
import functools

import numpy as np
import jax
import jax.numpy as jnp
from jax import lax
from jax.experimental import pallas as pl
from jax.experimental.pallas import tpu as pltpu

F32 = jnp.float32
BF16 = jnp.bfloat16

D_MODEL = 1024
DEPTH = 4
LRU_WIDTH = D_MODEL
LRU_HEADS = 16
LRU_BLOCK = LRU_WIDTH // LRU_HEADS
CONV_WIDTH = 4
LRU_C = 8.0
RWKV_HEAD = 64
RWKV_WIDTH = D_MODEL
RWKV_HEADS = RWKV_WIDTH // RWKV_HEAD
LORA_W = 64
LORA_A = 64
LORA_G = 128
SHIFT_WIDTH = 3 * RWKV_WIDTH + LORA_W + LORA_A + LORA_G
PROJ_WIDTH = 2 * LRU_WIDTH + SHIFT_WIDTH + 2 * D_MODEL
D_FF = 2816
N_EXPERTS = 8
NORM_EPS = 1e-6
GN_EPS = 64e-5

LANES = 128
SUBLANES = 8
MXU_DIM = 256
VMEM_LIMIT = 56 * 1024 * 1024

CHUNK = 64
GROUP_HEADS = MXU_DIM // CHUNK
GROUP_LANES = GROUP_HEADS * RWKV_HEAD
N_GROUPS = RWKV_HEADS // GROUP_HEADS
FF_CHUNK = 256
MOE_TILE = 1024
MOE_SUB = 128
MOE_FF_SPLIT = 2
SCAN_ROWS = 16
assert CHUNK == RWKV_HEAD


def _cparams(sem):
    return pltpu.CompilerParams(dimension_semantics=sem, vmem_limit_bytes=VMEM_LIMIT)


def _dot(a, b):
    return jnp.dot(a, b, preferred_element_type=F32)


def _dot_tb(a, b):
    return lax.dot_general(a, b, (((1,), (1,)), ((), ())), preferred_element_type=F32)


def _dot_ta(a, b):
    return lax.dot_general(a, b, (((0,), (0,)), ((), ())), preferred_element_type=F32)


def _dot_exact(a, b_bf16):
    hi = a.astype(BF16)
    lo = (a - hi.astype(F32)).astype(BF16)
    return _dot(hi, b_bf16) + _dot(lo, b_bf16)


def _sigmoid(x):
    return jax.nn.sigmoid(x)


def _softplus(z):
    return jnp.maximum(z, 0.0) + jnp.log(1.0 + jnp.exp(-jnp.abs(z)))


def _rms(x, g):
    return x * lax.rsqrt(jnp.mean(x * x, axis=-1, keepdims=True) + NORM_EPS) * g


_PROJ_SPLITS = ((0, 2 * LRU_WIDTH), (2 * LRU_WIDTH, SHIFT_WIDTH), (2 * LRU_WIDTH + SHIFT_WIDTH, 2 * D_MODEL))


def _proj_kernel(x_ref, g_ref, w_ref, xy_ref, ps_ref, gt_ref):
    u = _rms(x_ref[...], g_ref[...]).astype(BF16)
    for out_ref, (c0, width) in zip((xy_ref, ps_ref, gt_ref), _PROJ_SPLITS):
        j = 0
        while j < width:
            w = min(512, width - j)
            out_ref[:, j:j + w] = _dot(u, w_ref[:, c0 + j:c0 + j + w])
            j += w


def _proj(x, g_all, w_all, layer, tm):
    n = x.shape[0]
    return pl.pallas_call(
        _proj_kernel,
        grid=(n // tm,),
        in_specs=[
            pl.BlockSpec((tm, D_MODEL), lambda i: (i, 0)),
            pl.BlockSpec((None, 1, D_MODEL), lambda i: (layer, 0, 0)),
            pl.BlockSpec((None, D_MODEL, PROJ_WIDTH), lambda i: (layer, 0, 0),
                         pipeline_mode=pl.Buffered(1)),
        ],
        out_specs=[
            pl.BlockSpec((tm, 2 * LRU_WIDTH), lambda i: (i, 0)),
            pl.BlockSpec((tm, SHIFT_WIDTH), lambda i: (i, 0)),
            pl.BlockSpec((tm, 2 * D_MODEL), lambda i: (i, 0)),
        ],
        out_shape=[
            jax.ShapeDtypeStruct((n, 2 * LRU_WIDTH), F32),
            jax.ShapeDtypeStruct((n, SHIFT_WIDTH), F32),
            jax.ShapeDtypeStruct((n, 2 * D_MODEL), F32),
        ],
        compiler_params=_cparams(("arbitrary",)),
        name="proj",
    )(x, g_all, w_all)


def _gelu_tanh(x):
    return 0.5 * x * (1.0 + jnp.tanh(np.sqrt(2.0 / np.pi).astype(np.float32) * (x + 0.044715 * (x * x * x))))


def _lru_kernel(xa_ref, ya_ref, cs_ref, h0_ref, cw_ref, cb_ref, gaw_ref, gab_ref, gxw_ref, gxb_ref, lam_ref,
                y_ref, hl_ref, nc_ref, xp_s, h_s, a_s, b_s, *, bb, tt):
    ti = pl.program_id(1)
    neg_c = -LRU_C * _softplus(-lam_ref[...])
    cw = cw_ref[...]
    @pl.when(ti == 0)
    def _():
        for b in range(bb):
            xp_s[b, 5:8, :] = cs_ref[b]
            h_s[b] = h0_ref[b]

    for b in range(bb):
        xa = xa_ref[b]
        xp_s[b, 8:8 + tt, :] = xa
        xc = cb_ref[...] + xp_s[b, 5:5 + tt, :] * cw[0:1]
        xc = xc + xp_s[b, 6:6 + tt, :] * cw[1:2]
        xc = xc + xp_s[b, 7:7 + tt, :] * cw[2:3]
        xc = xc + xa * cw[3:4]
        tail = xp_s[b, 5 + tt:8 + tt, :]
        xp_s[b, 5:8, :] = tail
        nc_ref[b] = tail

        for c in range(LRU_WIDTH // MXU_DIM):
            sl = slice(c * MXU_DIM, (c + 1) * MXU_DIM)
            xcc = xc[:, sl]
            xcb = xcc.astype(BF16)
            r = _sigmoid(_dot(xcb, gaw_ref[c]) + gab_ref[:, sl])
            i = _sigmoid(_dot(xcb, gxw_ref[c]) + gxb_ref[:, sl])
            log_a = neg_c[:, sl] * r
            a = jnp.exp(log_a)
            a_s[:, sl] = a
            b_s[:, sl] = jnp.sqrt(-jnp.tanh(log_a) * (a * a + 1.0)) * (i * xcc)

        if tt % SCAN_ROWS == 0:
            row = lax.broadcasted_iota(jnp.int32, (SUBLANES, LRU_WIDTH), 0)

            def blk(j, h):
                r0 = pl.multiple_of(j * SCAN_ROWS, SCAN_ROWS)
                hbs = []
                for q in range(SCAN_ROWS // SUBLANES):
                    av = a_s[pl.ds(r0 + q * SUBLANES, SUBLANES), :]
                    bv = b_s[pl.ds(r0 + q * SUBLANES, SUBLANES), :]
                    for d in (1, 2, 4):
                        m = row >= d
                        a_sh = pltpu.roll(av, d, axis=0)
                        b_sh = pltpu.roll(bv, d, axis=0)
                        bv = jnp.where(m, av * b_sh + bv, bv)
                        av = jnp.where(m, av * a_sh, av)
                    hb = av * h + bv
                    h = hb[SUBLANES - 1:SUBLANES, :]
                    hbs.append(hb)
                hs = jnp.concatenate(hbs, axis=0)
                y_ref[b, pl.ds(r0, SCAN_ROWS), :] = (hs * _gelu_tanh(ya_ref[b, pl.ds(r0, SCAN_ROWS), :])).astype(BF16)
                return h

            h = lax.fori_loop(0, tt // SCAN_ROWS, blk, h_s[b])
        else:
            h = h_s[b]
            rows = []
            for t in range(tt):
                h = a_s[t:t + 1, :] * h + b_s[t:t + 1, :]
                rows.append(h)
            hs = jnp.concatenate(rows, axis=0)
            y_ref[b] = (hs * _gelu_tanh(ya_ref[b])).astype(BF16)
        h_s[b] = h
        hl_ref[b] = h


def _lru(xy3, st_conv, st_lru, P, layer, bb, tt):
    B, T, _ = xy3.shape
    W = LRU_WIDTH
    vec = lambda: pl.BlockSpec((None, 1, W), lambda b, t: (layer, 0, 0))
    kern = functools.partial(_lru_kernel, bb=bb, tt=tt)
    return pl.pallas_call(
        kern,
        grid=(B // bb, T // tt),
        in_specs=[
            pl.BlockSpec((bb, tt, W), lambda b, t: (b, t, 0)),
            pl.BlockSpec((bb, tt, W), lambda b, t: (b, t, 1)),
            pl.BlockSpec((bb, CONV_WIDTH - 1, W), lambda b, t: (b, 0, 0)),
            pl.BlockSpec((bb, 1, W), lambda b, t: (b, 0, 0)),
            pl.BlockSpec((None, CONV_WIDTH, W), lambda b, t: (layer, 0, 0)),
            vec(),
            pl.BlockSpec((None, W // MXU_DIM, MXU_DIM, MXU_DIM), lambda b, t: (layer, 0, 0, 0)),
            vec(),
            pl.BlockSpec((None, W // MXU_DIM, MXU_DIM, MXU_DIM), lambda b, t: (layer, 0, 0, 0)),
            vec(),
            vec(),
        ],
        out_specs=[
            pl.BlockSpec((bb, tt, W), lambda b, t: (b, t, 0)),
            pl.BlockSpec((bb, 1, W), lambda b, t: (b, 0, 0)),
            pl.BlockSpec((bb, CONV_WIDTH - 1, W), lambda b, t: (b, 0, 0)),
        ],
        out_shape=[
            jax.ShapeDtypeStruct((B, T, W), BF16),
            jax.ShapeDtypeStruct((B, 1, W), F32),
            jax.ShapeDtypeStruct((B, CONV_WIDTH - 1, W), F32),
        ],
        scratch_shapes=[
            pltpu.VMEM((bb, tt + 8, W), F32),
            pltpu.VMEM((bb, 1, W), F32),
            pltpu.VMEM((tt, W), F32),
            pltpu.VMEM((tt, W), F32),
        ],
        compiler_params=_cparams(("arbitrary", "arbitrary")),
        name="lru",
    )(xy3, xy3, st_conv, st_lru, P['conv_w'], P['conv_b'], P['gaw_bd'], P['gate_a_b'], P['gxw_bd'],
      P['gate_x_b'], P['lru_lambda'])


def _block_diag(x_bf16, mask_ref):
    return jnp.concatenate([x_bf16] * GROUP_HEADS, axis=0) * mask_ref[...]


def _rwkv_kernel(ps_ref, sh_ref, s0_ref, mu_ref, w0_ref, lora_ref, gup_ref, a0_ref, kk_ref, ka_ref, rk_ref,
                 lnw_ref, lnb_ref, tri_ref, ones_ref, mrow_ref, msq_ref,
                 y_ref, nsh_ref, ns_ref, pad_s, car_s, st_s, *, nb, t_valid, zero_state):
    C = CHUNK
    ci = pl.program_id(1)
    last = ci == pl.num_programs(1) - 1
    rows_c = lax.broadcasted_iota(jnp.int32, (C, 1), 0)
    t_idx = lax.broadcasted_iota(jnp.int32, (C, MXU_DIM), 0)
    i_idx = lax.broadcasted_iota(jnp.int32, (C, MXU_DIM), 1) % C
    m_strict = i_idx < t_idx
    m_incl = i_idx <= t_idx
    eye_cat = (i_idx == t_idx).astype(F32)
    ones_bd = ones_ref[...]

    def head_sum(x):
        parts = []
        for g in range(RWKV_WIDTH // MXU_DIM):
            parts.append(_dot_exact(x[:, g * MXU_DIM:(g + 1) * MXU_DIM], ones_bd))
        return jnp.concatenate(parts, axis=1)

    @pl.when(ci == 0)
    def _():
        for b in range(nb):
            car_s[b] = sh_ref[b]
            if zero_state:
                st_s[b] = jnp.zeros(st_s.shape[1:], F32)
            else:
                for g in range(N_GROUPS):
                    rows = [s0_ref[b, g * GROUP_HEADS + hh] for hh in range(GROUP_HEADS)]
                    stacked = jnp.concatenate(rows, axis=0)
                    tiled = jnp.concatenate([stacked] * GROUP_HEADS, axis=1)
                    st_s[b, g] = tiled * msq_ref[...].astype(F32)

    preps, chains = [], []
    for b in range(nb):
        if t_valid == C:
            ps = ps_ref[b]
        else:
            pad_s[...] = jnp.zeros(pad_s.shape, F32)
            pad_s[0:t_valid, :] = ps_ref[b]
            ps = pad_s[...]
        prev = pltpu.roll(ps, 1, axis=0)
        prev = jnp.where(rows_c == 0, car_s[b], prev)
        new_carry = ps[t_valid - 1:t_valid, :]
        car_s[b] = new_carry
        nsh_ref[b] = new_carry

        s = ps + (prev - ps) * mu_ref[...]
        o = RWKV_WIDTH
        r = s[:, :o]
        k = s[:, o:2 * o]
        v = s[:, 2 * o:3 * o]
        dwa = s[:, 3 * o:3 * o + LORA_W + LORA_A]
        dg = s[:, 3 * o + LORA_W + LORA_A:]
        lane = lax.broadcasted_iota(jnp.int32, dwa.shape, 1)
        lora_in = jnp.where(lane < LORA_W, jnp.tanh(dwa), dwa).astype(BF16)
        lora = _dot(lora_in, lora_ref[...])
        w_log = -_softplus(-(w0_ref[...] + lora[:, :o])) - 0.5
        lw = -jnp.exp(w_log)
        a = _sigmoid(a0_ref[...] + lora[:, o:])
        g = _dot(_sigmoid(dg).astype(BF16), gup_ref[...])
        kk = k * kk_ref[...]
        kk = kk / jnp.maximum(jnp.sqrt(head_sum(kk * kk)), 1e-12)
        kmod = k * (1.0 + (a - 1.0) * ka_ref[...])
        if t_valid != C:
            valid = rows_c < t_valid
            lw = jnp.where(valid, lw, 0.0)
            kk = jnp.where(valid, kk, 0.0)
            kmod = jnp.where(valid, kmod, 0.0)
            v = jnp.where(valid, v, 0.0)
        beta = kk * a
        L = jnp.concatenate(
            [_dot_tri(tri_ref[...], lw[:, j * MXU_DIM:(j + 1) * MXU_DIM]) for j in range(o // MXU_DIM)], axis=1)
        e_l = jnp.exp(L)
        e_neg = jnp.exp(-L)
        l_end = L[C - 1:C, :]
        e_end = jnp.exp(l_end - L)
        p_end = jnp.exp(l_end)
        at = (-kk) * jnp.exp(L - lw)
        rt = r * e_l
        bt = beta * e_neg
        kt = kmod * e_neg
        bend = beta * e_end
        kend = kmod * e_end

        preps.append(dict(r=r, v=v, g=g, kmod=kmod))
        for gi in range(N_GROUPS):
            sl = slice(gi * GROUP_LANES, (gi + 1) * GROUP_LANES)
            chains.append(dict(
                b=b, gi=gi, vg=v[:, sl], p_end=p_end[:, sl],
                x2=jnp.concatenate([at[:, sl], rt[:, sl]], axis=0).astype(BF16),
                wbd=jnp.concatenate([_block_diag(bt[:, sl].astype(BF16), mrow_ref),
                                     _block_diag(kt[:, sl].astype(BF16), mrow_ref)], axis=0),
                bk=jnp.concatenate([bend[:, sl], kend[:, sl]], axis=0).astype(BF16)))

    cw = GROUP_HEADS * C
    for c in chains:
        res = _dot_tb(c['x2'], c['wbd'])
        n_cat = jnp.where(m_strict, res[:C, :cw], 0.0)
        c['a_ak'] = jnp.where(m_strict, res[:C, cw:], 0.0).astype(BF16)
        c['a_r'] = jnp.concatenate([jnp.where(m_incl, res[C:, :cw], 0.0),
                                    jnp.where(m_incl, res[C:, cw:], 0.0)], axis=1).astype(BF16)
        c['x_c'] = n_cat.astype(BF16)
        c['p_c'] = eye_cat + n_cat
        c['x_bd'] = _block_diag(c['x_c'], msq_ref)
    for c in chains:
        st = st_s[c['b'], c['gi']]
        c['xs'] = _dot_tb(c['x2'], st.astype(BF16))
        c['v_bd'] = _block_diag(c['vg'].astype(BF16), msq_ref)
    for c in chains:
        c['rhs'] = c['xs'][:C] + _dot(c['a_ak'], c['v_bd'])
    lvl = 2
    while lvl < C:
        for c in chains:
            c['x_c'] = _dot(c['x_c'], c['x_bd']).astype(BF16)
            c['x_bd'] = _block_diag(c['x_c'], msq_ref)
        for c in chains:
            c['p_c'] = c['p_c'] + _dot(c['p_c'].astype(BF16), c['x_bd'])
        lvl *= 2
    for c in chains:
        c['u'] = _dot(c['p_c'].astype(BF16), _block_diag(c['rhs'].astype(BF16), msq_ref))
    ys = {}
    for c in chains:
        ub = c['u'].astype(BF16)
        ys[(c['b'], c['gi'])] = c['xs'][C:] + _dot(c['a_r'], jnp.concatenate([_block_diag(ub, msq_ref), c['v_bd']],
                                                                             axis=0))
        uv = jnp.concatenate([ub, c['vg'].astype(BF16)], axis=0)
        st = st_s[c['b'], c['gi']]
        st_s[c['b'], c['gi']] = st * c['p_end'] + _dot_ta(uv, c['bk']) * msq_ref[...].astype(F32)

    for b in range(nb):
        r, v, g, kmod = (preps[b][n] for n in ('r', 'v', 'g', 'kmod'))
        yc = jnp.concatenate([ys[(b, gi)] for gi in range(N_GROUPS)], axis=1)
        inv_n = 1.0 / RWKV_HEAD
        mean = head_sum(yc) * inv_n
        dlt = yc - mean
        var = head_sum(dlt * dlt) * inv_n
        yn = dlt * lax.rsqrt(var + GN_EPS) * lnw_ref[...] + lnb_ref[...]
        bonus = head_sum(r * kmod * rk_ref[...]) * v
        out = ((yn + bonus) * g).astype(BF16)
        if t_valid == C:
            y_ref[b] = out
        else:
            y_ref[b] = out[0:t_valid, :]

    @pl.when(last)
    def _():
        for b in range(nb):
            for gi in range(N_GROUPS):
                for hh in range(GROUP_HEADS):
                    ns_ref[b, gi * GROUP_HEADS + hh] = st_s[b, gi, hh * RWKV_HEAD:(hh + 1) * RWKV_HEAD,
                                                            hh * RWKV_HEAD:(hh + 1) * RWKV_HEAD]


def _dot_tri(tri_bf16, x):
    hi = x.astype(BF16)
    lo = (x - hi.astype(F32)).astype(BF16)
    return _dot(tri_bf16, hi) + _dot(tri_bf16, lo)


def _rwkv_consts():
    C = CHUNK
    gh = GROUP_HEADS
    tri = np.tril(np.ones((C, C), np.float32))
    hl = np.arange(MXU_DIM) // RWKV_HEAD
    ones_bd = (hl[:, None] == hl[None, :]).astype(np.float32)
    rowh = np.arange(gh * C) // C
    colh = np.arange(GROUP_LANES) // RWKV_HEAD
    mrow = (rowh[:, None] == colh[None, :]).astype(np.float32)
    colc = np.arange(gh * C) // C
    msq = (rowh[:, None] == colc[None, :]).astype(np.float32)
    return (jnp.asarray(tri, BF16), jnp.asarray(ones_bd, BF16), jnp.asarray(mrow, BF16), jnp.asarray(msq, BF16))


def _rwkv(ps3, st_shift, st_rwkv, P, layer, nb, zero_state):
    B, T, _ = ps3.shape
    C = CHUNK
    if T % C == 0:
        t_valid, nchunks, tb = C, T // C, C
    else:
        assert T < C
        t_valid, nchunks, tb = T, 1, T
    o = RWKV_WIDTH
    tri, ones_bd, mrow, msq = _rwkv_consts()
    vec = lambda w: pl.BlockSpec((None, 1, w), lambda b, c: (layer, 0, 0))
    const = lambda arr: pl.BlockSpec(arr.shape, lambda b, c: (0,) * arr.ndim)
    kern = functools.partial(_rwkv_kernel, nb=nb, t_valid=t_valid, zero_state=zero_state)
    state_spec = pl.BlockSpec((nb, RWKV_HEADS, RWKV_HEAD, RWKV_HEAD), lambda b, c: (b, 0, 0, 0))
    return pl.pallas_call(
        kern,
        grid=(B // nb, nchunks),
        in_specs=[
            pl.BlockSpec((nb, tb, SHIFT_WIDTH), lambda b, c: (b, c, 0)),
            pl.BlockSpec((nb, 1, SHIFT_WIDTH), lambda b, c: (b, 0, 0)),
            state_spec,
            vec(SHIFT_WIDTH),
            vec(o),
            pl.BlockSpec((None, LORA_W + LORA_A, 2 * o), lambda b, c: (layer, 0, 0)),
            pl.BlockSpec((None, LORA_G, o), lambda b, c: (layer, 0, 0)),
            vec(o), vec(o), vec(o), vec(o), vec(o), vec(o),
            const(tri), const(ones_bd), const(mrow), const(msq),
        ],
        out_specs=[
            pl.BlockSpec((nb, tb, o), lambda b, c: (b, c, 0)),
            pl.BlockSpec((nb, 1, SHIFT_WIDTH), lambda b, c: (b, 0, 0)),
            state_spec,
        ],
        out_shape=[
            jax.ShapeDtypeStruct((B, T, o), BF16),
            jax.ShapeDtypeStruct((B, 1, SHIFT_WIDTH), F32),
            jax.ShapeDtypeStruct((B, RWKV_HEADS, RWKV_HEAD, RWKV_HEAD), F32),
        ],
        scratch_shapes=[
            pltpu.VMEM((C, SHIFT_WIDTH), F32),
            pltpu.VMEM((nb, 1, SHIFT_WIDTH), F32),
            pltpu.VMEM((nb, N_GROUPS, MXU_DIM, MXU_DIM), F32),
        ],
        compiler_params=_cparams(("arbitrary", "arbitrary")),
        name="rwkv",
    )(ps3, st_shift, st_rwkv, P['shift_mu'], P['w0'], P['lora_wa'], P['g_lora_up'], P['a0'], P['k_k'], P['k_a'],
      P['r_k'], P['ln_x_w'], P['ln_x_b'], tri, ones_bd, mrow, msq)


def _rows(x, b, n):
    return x[b * n:(b + 1) * n]


def _dot_split3(sel_bf16, x):
    hi = x.astype(BF16)
    r1 = x - hi.astype(F32)
    mid = r1.astype(BF16)
    lo = (r1 - mid.astype(F32)).astype(BF16)
    return _dot(sel_bf16, hi) + _dot(sel_bf16, mid) + _dot(sel_bf16, lo)


def _wkv_kernel(ps_ref, sh_ref, s0_ref, mu_ref, w0_ref, lora_ref, gup_ref, a0_ref, kk_ref, ka_ref, rk_ref,
                lnw_ref, lnb_ref, tri_ref, endm_ref, ones_ref, mrow_ref, msq_ref,
                y_ref, nsh_ref, ns_ref, car_s, st_s, *, nb, seq, zero_state):
    C = CHUNK
    Q = C // seq
    R = nb * C
    o = RWKV_WIDTH
    ci = pl.program_id(1)
    last = ci == pl.num_programs(1) - 1
    t_idx = lax.broadcasted_iota(jnp.int32, (C, MXU_DIM), 0)
    i_idx = lax.broadcasted_iota(jnp.int32, (C, MXU_DIM), 1) % C
    same = (t_idx // seq) == (i_idx // seq)
    m_strict = same & (i_idx < t_idx)
    m_incl = same & (i_idx <= t_idx)
    eye_cat = (i_idx == t_idx).astype(F32)
    ones_bd = ones_ref[...]
    msq_f = msq_ref[...].astype(F32)
    n_lane_tiles = o // MXU_DIM

    def head_sum(x):
        xs = jnp.concatenate([x[:, g * MXU_DIM:(g + 1) * MXU_DIM] for g in range(n_lane_tiles)], axis=0)
        hi = xs.astype(BF16)
        lo = (xs - hi.astype(F32)).astype(BF16)
        s = _dot(jnp.concatenate([hi, lo], axis=0), ones_bd)
        s = s[:n_lane_tiles * R] + s[n_lane_tiles * R:]
        return jnp.concatenate([_rows(s, g, R) for g in range(n_lane_tiles)], axis=1)

    @pl.when(ci == 0)
    def _():
        for b in range(nb):
            car_s[b] = sh_ref[b]
            for q in range(Q):
                if zero_state:
                    st_s[b, q] = jnp.zeros(st_s.shape[2:], F32)
                else:
                    for g in range(N_GROUPS):
                        heads = [s0_ref[b, q * RWKV_HEADS + g * GROUP_HEADS + hh] for hh in range(GROUP_HEADS)]
                        stacked = jnp.concatenate(heads, axis=0)
                        st_s[b, q, g] = jnp.concatenate([stacked] * GROUP_HEADS, axis=1) * msq_f

    row_c = lax.broadcasted_iota(jnp.int32, (C, 1), 0)
    prevs = []
    for b in range(nb):
        ps_b = ps_ref[b]
        if Q == 1:
            first = car_s[b]
            new_carry = ps_b[C - 1:C, :]
        else:
            put = (lax.broadcasted_iota(jnp.int32, (C, Q), 0)
                   == seq * lax.broadcasted_iota(jnp.int32, (C, Q), 1)).astype(BF16)
            take = (lax.broadcasted_iota(jnp.int32, (Q, C), 1)
                    == seq * lax.broadcasted_iota(jnp.int32, (Q, C), 0) + (seq - 1)).astype(BF16)
            first = _dot_split3(put, car_s[b])
            new_carry = _dot_split3(take, ps_b)
        prevs.append(jnp.where(row_c % seq == 0, first, pltpu.roll(ps_b, 1, axis=0)))
        car_s[b] = new_carry
        nsh_ref[b] = new_carry
    ps = jnp.concatenate([ps_ref[b] for b in range(nb)], axis=0)
    prev = jnp.concatenate(prevs, axis=0)

    s = ps + (prev - ps) * mu_ref[...]
    r = s[:, :o]
    k = s[:, o:2 * o]
    v = s[:, 2 * o:3 * o]
    dwa = s[:, 3 * o:3 * o + LORA_W + LORA_A]
    dg = s[:, 3 * o + LORA_W + LORA_A:]
    lane = lax.broadcasted_iota(jnp.int32, dwa.shape, 1)
    lora_in = jnp.where(lane < LORA_W, jnp.tanh(dwa), dwa).astype(BF16)
    lora = _dot(lora_in, lora_ref[...])
    w_log = -_softplus(-(w0_ref[...] + lora[:, :o])) - 0.5
    lw = -jnp.exp(w_log)
    a = _sigmoid(a0_ref[...] + lora[:, o:])
    g = _dot(_sigmoid(dg).astype(BF16), gup_ref[...])
    kk = k * kk_ref[...]
    kk = kk / jnp.maximum(jnp.sqrt(head_sum(kk * kk)), 1e-12)
    kmod = k * (1.0 + (a - 1.0) * ka_ref[...])
    beta = kk * a
    lw_hi = lw.astype(BF16)
    lw_lo = (lw - lw_hi.astype(F32)).astype(BF16)
    L = _dot(tri_ref[...], lw_hi) + _dot(tri_ref[...], lw_lo)
    l_end = _dot(endm_ref[...], lw_hi) + _dot(endm_ref[...], lw_lo)
    e_neg = jnp.exp(-L)
    e_end = jnp.exp(l_end - L)
    p_end = jnp.exp(l_end)
    at = (-kk) * jnp.exp(L - lw)
    rt = r * jnp.exp(L)
    bt = (beta * e_neg).astype(BF16)
    kt = (kmod * e_neg).astype(BF16)
    bend = beta * e_end
    kend = kmod * e_end

    chains = []
    for b in range(nb):
        for gi in range(N_GROUPS):
            sl = slice(gi * GROUP_LANES, (gi + 1) * GROUP_LANES)
            chains.append(dict(
                b=b, gi=gi, vg=_rows(v, b, C)[:, sl].astype(BF16), p_end=_rows(p_end, b, C)[:, sl],
                x2=jnp.concatenate([_rows(at, b, C)[:, sl], _rows(rt, b, C)[:, sl]], axis=0).astype(BF16),
                wbd=jnp.concatenate([_block_diag(_rows(bt, b, C)[:, sl], mrow_ref),
                                     _block_diag(_rows(kt, b, C)[:, sl], mrow_ref)], axis=0),
                bk=jnp.concatenate([_rows(bend, b, C)[:, sl], _rows(kend, b, C)[:, sl]], axis=0).astype(BF16)))

    cw = GROUP_HEADS * C
    rowseq = (lax.broadcasted_iota(jnp.int32, (2 * C, 1), 0) % C) // seq
    for c in chains:
        res = _dot_tb(c['x2'], c['wbd'])
        n_cat = jnp.where(m_strict, res[:C, :cw], 0.0)
        c['a_ak'] = jnp.where(m_strict, res[:C, cw:], 0.0).astype(BF16)
        c['a_r'] = jnp.concatenate([jnp.where(m_incl, res[C:, :cw], 0.0),
                                    jnp.where(m_incl, res[C:, cw:], 0.0)], axis=1).astype(BF16)
        c['x_c'] = n_cat.astype(BF16)
        c['p_c'] = eye_cat + n_cat
        c['x_bd'] = _block_diag(c['x_c'], msq_ref)
    for c in chains:
        xs = None
        for q in range(Q):
            xq = _dot_tb(c['x2'], st_s[c['b'], q, c['gi']].astype(BF16))
            xs = xq if xs is None else jnp.where(rowseq == q, xq, xs)
        c['xs'] = xs
        c['v_bd'] = _block_diag(c['vg'], msq_ref)
    for c in chains:
        c['rhs'] = c['xs'][:C] + _dot(c['a_ak'], c['v_bd'])
    lvl = 2
    while lvl < seq:
        for c in chains:
            c['x_c'] = _dot(c['x_c'], c['x_bd']).astype(BF16)
            c['x_bd'] = _block_diag(c['x_c'], msq_ref)
        for c in chains:
            c['p_c'] = c['p_c'] + _dot(c['p_c'].astype(BF16), c['x_bd'])
        lvl *= 2
    for c in chains:
        c['u'] = _dot(c['p_c'].astype(BF16), _block_diag(c['rhs'].astype(BF16), msq_ref)).astype(BF16)
    ys = {}
    for c in chains:
        b, gi = c['b'], c['gi']
        ys[(b, gi)] = c['xs'][C:] + _dot(c['a_r'], jnp.concatenate([_block_diag(c['u'], msq_ref), c['v_bd']], axis=0))
        uv = jnp.concatenate([c['u'], c['vg']], axis=0)
        if Q > 1:
            uv = jnp.concatenate([jnp.where(rowseq == q, uv, jnp.zeros_like(uv)) for q in range(Q)], axis=1)
        ds = _dot_ta(uv, c['bk'])
        for q in range(Q):
            st_s[b, q, gi] = (st_s[b, q, gi] * c['p_end'][q * seq:q * seq + 1, :]
                              + _rows(ds, q, GROUP_LANES) * msq_f)

    yc = jnp.concatenate([jnp.concatenate([ys[(b, gi)] for gi in range(N_GROUPS)], axis=1) for b in range(nb)],
                         axis=0)
    inv_n = 1.0 / RWKV_HEAD
    mean = head_sum(yc) * inv_n
    dlt = yc - mean
    var = head_sum(dlt * dlt) * inv_n
    yn = dlt * lax.rsqrt(var + GN_EPS) * lnw_ref[...] + lnb_ref[...]
    bonus = head_sum(r * kmod * rk_ref[...]) * v
    out = ((yn + bonus) * g).astype(BF16)
    for b in range(nb):
        y_ref[b] = _rows(out, b, C)

    @pl.when(last)
    def _():
        for b in range(nb):
            for q in range(Q):
                for gi in range(N_GROUPS):
                    for hh in range(GROUP_HEADS):
                        ns_ref[b, q * RWKV_HEADS + gi * GROUP_HEADS + hh] = st_s[
                            b, q, gi, hh * RWKV_HEAD:(hh + 1) * RWKV_HEAD, hh * RWKV_HEAD:(hh + 1) * RWKV_HEAD]


def _wkv_consts(nb, seq):
    C = CHUNK
    gh = GROUP_HEADS
    rows = np.arange(nb * C)
    same = (rows[:, None] // seq) == (rows[None, :] // seq)
    tri = (same & (rows[None, :] <= rows[:, None])).astype(np.float32)
    endm = same.astype(np.float32)
    hl = np.arange(MXU_DIM) // RWKV_HEAD
    ones_bd = (hl[:, None] == hl[None, :]).astype(np.float32)
    rowh = np.arange(gh * C) // C
    colh = np.arange(GROUP_LANES) // RWKV_HEAD
    mrow = (rowh[:, None] == colh[None, :]).astype(np.float32)
    msq = (rowh[:, None] == rowh[None, :]).astype(np.float32)
    return tuple(jnp.asarray(m, BF16) for m in (tri, endm, ones_bd, mrow, msq))


def _wkv(ps3, st_shift, st_rwkv, P, layer, nb, zero_state):
    B, T, _ = ps3.shape
    C = CHUNK
    seq = C if T % C == 0 else T
    assert C % seq == 0
    Q = C // seq
    G = B // Q
    tg = T * Q
    assert B % Q == 0 and G % nb == 0 and tg % C == 0
    o = RWKV_WIDTH
    consts = _wkv_consts(nb, seq)
    vec = lambda w: pl.BlockSpec((None, 1, w), lambda b, c: (layer, 0, 0))
    const = lambda arr: pl.BlockSpec(arr.shape, lambda b, c: (0,) * arr.ndim)
    kern = functools.partial(_wkv_kernel, nb=nb, seq=seq, zero_state=zero_state)
    shift_spec = pl.BlockSpec((nb, Q, SHIFT_WIDTH), lambda b, c: (b, 0, 0))
    state_spec = pl.BlockSpec((nb, Q * RWKV_HEADS, RWKV_HEAD, RWKV_HEAD), lambda b, c: (b, 0, 0, 0))
    y, nsh, ns = pl.pallas_call(
        kern,
        grid=(G // nb, tg // C),
        in_specs=[
            pl.BlockSpec((nb, C, SHIFT_WIDTH), lambda b, c: (b, c, 0)),
            shift_spec,
            pl.BlockSpec((nb, Q * RWKV_HEADS, RWKV_HEAD, RWKV_HEAD), lambda b, c: (b, 0, 0, 0),
                         pipeline_mode=pl.Buffered(1)),
            vec(SHIFT_WIDTH),
            vec(o),
            pl.BlockSpec((None, LORA_W + LORA_A, 2 * o), lambda b, c: (layer, 0, 0)),
            pl.BlockSpec((None, LORA_G, o), lambda b, c: (layer, 0, 0)),
            vec(o), vec(o), vec(o), vec(o), vec(o), vec(o),
        ] + [const(m) for m in consts],
        out_specs=[
            pl.BlockSpec((nb, C, o), lambda b, c: (b, c, 0)),
            shift_spec,
            state_spec,
        ],
        out_shape=[
            jax.ShapeDtypeStruct((G, tg, o), BF16),
            jax.ShapeDtypeStruct((G, Q, SHIFT_WIDTH), F32),
            jax.ShapeDtypeStruct((G, Q * RWKV_HEADS, RWKV_HEAD, RWKV_HEAD), F32),
        ],
        scratch_shapes=[
            pltpu.VMEM((nb, Q, SHIFT_WIDTH), F32),
            pltpu.VMEM((nb, Q, N_GROUPS, MXU_DIM, MXU_DIM), F32),
        ],
        compiler_params=_cparams(("arbitrary", "arbitrary")),
        name="rwkv",
    )(ps3.reshape(G, tg, SHIFT_WIDTH), st_shift.reshape(G, Q, SHIFT_WIDTH),
      st_rwkv.reshape(G, Q * RWKV_HEADS, RWKV_HEAD, RWKV_HEAD),
      P['shift_mu'], P['w0'], P['lora_wa'], P['g_lora_up'], P['a0'], P['k_k'], P['k_a'],
      P['r_k'], P['ln_x_w'], P['ln_x_b'], *consts)
    return (y.reshape(B, T, o), nsh.reshape(B, SHIFT_WIDTH),
            ns.reshape(B, RWKV_HEADS, RWKV_HEAD, RWKV_HEAD))


def _merge_kernel(x_ref, ya_ref, yb_ref, gt_ref, wa_ref, wb_ref, wo_ref, nf_ref, *rest, moe):
    if moe:
        wr_ref, xo_ref, h_ref, gates_ref, cnt_ref = rest
    else:
        xo_ref, h_ref = rest
    ga = gt_ref[:, :D_MODEL]
    gb = gt_ref[:, D_MODEL:]
    m = _sigmoid(ga) * _dot(ya_ref[...], wa_ref[...]) + _sigmoid(gb) * _dot(yb_ref[...], wb_ref[...])
    x = x_ref[...] + _dot(m.astype(BF16), wo_ref[...])
    xo_ref[...] = x
    h = _rms(x, nf_ref[...])
    h_ref[...] = h.astype(BF16)
    if moe:
        logits = jnp.dot(h, wr_ref[...], preferred_element_type=F32, precision=lax.Precision.HIGHEST)
        lane = lax.broadcasted_iota(jnp.int32, logits.shape, 1)
        real = lane < N_EXPERTS
        logits = jnp.where(real, logits, -jnp.inf)
        e = jnp.exp(logits - jnp.max(logits, axis=-1, keepdims=True))
        p = jnp.where(real, e / jnp.sum(e, axis=-1, keepdims=True), -1.0)
        m1 = jnp.max(p, axis=-1, keepdims=True)
        i1 = jnp.min(jnp.where(p == m1, lane, LANES), axis=-1, keepdims=True)
        oh1 = lane == i1
        p2 = jnp.where(oh1, -1.0, p)
        m2 = jnp.max(p2, axis=-1, keepdims=True)
        i2 = jnp.min(jnp.where(p2 == m2, lane, LANES), axis=-1, keepdims=True)
        oh2 = lane == i2
        tot = m1 + m2
        gates = jnp.where(oh1, m1 / tot, 0.0) + jnp.where(oh2, m2 / tot, 0.0)
        gates_ref[...] = gates
        cnt = jnp.sum((gates > 0.0).astype(F32), axis=0, keepdims=True)
        cnt_ref[...] = jnp.broadcast_to(cnt, (SUBLANES, LANES))


def _merge(x, ya, yb, gt, P, layer, tm, moe):
    n = x.shape[0]
    row = lambda w: pl.BlockSpec((tm, w), lambda i: (i, 0))
    wsq = lambda: pl.BlockSpec((None, D_MODEL, D_MODEL), lambda i: (layer, 0, 0))
    in_specs = [row(D_MODEL), row(D_MODEL), row(D_MODEL), row(2 * D_MODEL), wsq(), wsq(), wsq(),
                pl.BlockSpec((None, 1, D_MODEL), lambda i: (layer, 0, 0))]
    args = [x, ya, yb, gt, P['w_out_a'], P['w_out_b'], P['w_out'], P['norm_ffn']]
    out_specs = [row(D_MODEL), row(D_MODEL)]
    out_shape = [jax.ShapeDtypeStruct((n, D_MODEL), F32), jax.ShapeDtypeStruct((n, D_MODEL), BF16)]
    if moe:
        in_specs.append(pl.BlockSpec((None, D_MODEL, LANES), lambda i: (layer // 2, 0, 0)))
        args.append(P['w_router_pad'])
        out_specs.append(row(LANES))
        out_shape.append(jax.ShapeDtypeStruct((n, LANES), F32))
        out_specs.append(pl.BlockSpec((None, SUBLANES, LANES), lambda i: (i, 0, 0)))
        out_shape.append(jax.ShapeDtypeStruct((n // tm, SUBLANES, LANES), F32))
    return pl.pallas_call(
        functools.partial(_merge_kernel, moe=moe),
        grid=(n // tm,),
        in_specs=in_specs,
        out_specs=out_specs,
        out_shape=out_shape,
        compiler_params=_cparams(("arbitrary",)),
        name="merge",
    )(*args)


def _swiglu_acc(h, wg_ref, wu_ref, wd_ref, width=D_FF):
    acc = None
    for c0 in range(0, width, FF_CHUNK):
        sl = slice(c0, min(c0 + FF_CHUNK, width))
        gate = _dot(h, wg_ref[:, sl])
        up = _dot(h, wu_ref[:, sl])
        act = (gate * _sigmoid(gate) * up).astype(BF16)
        part = _dot(act, wd_ref[sl, :])
        acc = part if acc is None else acc + part
    return acc


def _ffn_kernel(x_ref, h_ref, wg_ref, wu_ref, wd_ref, o_ref):
    o_ref[...] = x_ref[...] + _swiglu_acc(h_ref[...], wg_ref, wu_ref, wd_ref)


def _ffn(x, h, P, j, tm):
    n = x.shape[0]
    row = lambda: pl.BlockSpec((tm, D_MODEL), lambda i: (i, 0))
    return pl.pallas_call(
        _ffn_kernel,
        grid=(n // tm,),
        in_specs=[row(), row(),
                  pl.BlockSpec((None, D_MODEL, D_FF), lambda i: (j, 0, 0), pipeline_mode=pl.Buffered(1)),
                  pl.BlockSpec((None, D_MODEL, D_FF), lambda i: (j, 0, 0), pipeline_mode=pl.Buffered(1)),
                  pl.BlockSpec((None, D_FF, D_MODEL), lambda i: (j, 0, 0), pipeline_mode=pl.Buffered(1))],
        out_specs=row(),
        out_shape=jax.ShapeDtypeStruct((n, D_MODEL), F32),
        compiler_params=_cparams(("arbitrary",)),
        name="ffn",
    )(x, h, P['w_ffn_gate'], P['w_ffn_up'], P['w_ffn_down'])


def _moe_kernel(cnt_ref, x_ref, h_ref, gates_ref, wg_ref, wu_ref, wd_ref, o_ref, rank_t_s, rank_n_s, xg_s, y_s,
                *, tw):
    i = pl.program_id(0)
    e = pl.program_id(1)
    f = pl.program_id(2)
    sb = min(MOE_SUB, tw)

    @pl.when((e == 0) & (f == 0))
    def _():
        o_ref[...] = x_ref[...]
        mask = (gates_ref[...] > 0.0).astype(BF16)
        r_idx = lax.broadcasted_iota(jnp.int32, (tw, tw), 0)
        c_idx = lax.broadcasted_iota(jnp.int32, (tw, tw), 1)
        before = (r_idx < c_idx).astype(BF16)
        upto = (r_idx <= c_idx).astype(BF16)
        rank_t_s[:, 0:tw] = _dot_ta(mask, before)
        rank_t_s[:, tw:2 * tw] = _dot_ta(mask, upto)
        rank_n_s[:, 0:LANES] = _dot_ta(before, mask)
        rank_n_s[:, LANES:2 * LANES] = _dot_ta(upto, mask)

    cnt = cnt_ref[i * N_EXPERTS + e]
    nblk = (cnt + sb - 1) // sb
    lane = lax.broadcasted_iota(jnp.int32, (tw, LANES), 1)
    pick = lane == e
    col = lambda a: jnp.sum(jnp.where(pick, a, 0.0), axis=-1, keepdims=True)
    gate_col = col(gates_ref[...])
    excl_col = col(rank_n_s[:, 0:LANES])
    incl_col = col(rank_n_s[:, LANES:2 * LANES])
    excl_row = rank_t_s[pl.ds(e, 1), 0:tw]
    incl_row = rank_t_s[pl.ds(e, 1), tw:2 * tw]

    def block(k, carry):
        r0 = pl.multiple_of(k * sb, sb)
        rr_col = (lax.broadcasted_iota(jnp.int32, (sb, 1), 0) + k * sb).astype(F32)
        rr_row = (lax.broadcasted_iota(jnp.int32, (1, sb), 1) + k * sb).astype(F32)

        @pl.when(f == 0)
        def _():
            gather = ((excl_row == rr_col) & (incl_row == rr_col + 1.0)).astype(BF16)
            xg_s[pl.ds(r0, sb), :] = _dot(gather, h_ref[...]).astype(BF16)
            y_s[pl.ds(r0, sb), :] = _swiglu_acc(xg_s[pl.ds(r0, sb), :], wg_ref, wu_ref, wd_ref, D_FF // MOE_FF_SPLIT)

        @pl.when(f == MOE_FF_SPLIT - 1)
        def _():
            y = y_s[pl.ds(r0, sb), :] + _swiglu_acc(xg_s[pl.ds(r0, sb), :], wg_ref, wu_ref, wd_ref,
                                                    D_FF // MOE_FF_SPLIT)
            hi = y.astype(BF16)
            lo = (y - hi.astype(F32)).astype(BF16)
            scatter = ((excl_col == rr_row) & (incl_col == rr_row + 1.0)).astype(BF16)
            o_ref[...] += gate_col * (_dot(scatter, hi) + _dot(scatter, lo))

        return carry

    lax.fori_loop(0, nblk, block, 0)


def _moe(x, h, gates, counts, P, j, tw):
    n = x.shape[0]
    assert MOE_FF_SPLIT == 2
    ffw = D_FF // MOE_FF_SPLIT
    row = lambda w: pl.BlockSpec((tw, w), lambda i, e, f, c: (i, 0), pipeline_mode=pl.Buffered(1))
    grid_spec = pltpu.PrefetchScalarGridSpec(
        num_scalar_prefetch=1,
        grid=(n // tw, N_EXPERTS, MOE_FF_SPLIT),
        in_specs=[row(D_MODEL), row(D_MODEL), row(LANES),
                  pl.BlockSpec((None, None, D_MODEL, ffw), lambda i, e, f, c: (j, e, 0, f)),
                  pl.BlockSpec((None, None, D_MODEL, ffw), lambda i, e, f, c: (j, e, 0, f)),
                  pl.BlockSpec((None, None, ffw, D_MODEL), lambda i, e, f, c: (j, e, f, 0))],
        out_specs=pl.BlockSpec((tw, D_MODEL), lambda i, e, f, c: (i, 0)),
        scratch_shapes=[pltpu.VMEM((LANES, 2 * tw), F32), pltpu.VMEM((tw, 2 * LANES), F32),
                        pltpu.VMEM((tw, D_MODEL), BF16), pltpu.VMEM((tw, D_MODEL), F32)],
    )
    return pl.pallas_call(
        functools.partial(_moe_kernel, tw=tw),
        grid_spec=grid_spec,
        out_shape=jax.ShapeDtypeStruct((n, D_MODEL), F32),
        compiler_params=_cparams(("arbitrary", "arbitrary", "arbitrary")),
        name="moe",
    )(counts, x, h, gates, P['w_moe_gate'], P['w_moe_up'], P['w_moe_down'])


def _final_norm_kernel(x_ref, g_ref, o_ref):
    o_ref[...] = _rms(x_ref[...], g_ref[...])


def _final_norm(x, g, tm):
    n = x.shape[0]
    return pl.pallas_call(
        _final_norm_kernel,
        grid=(n // tm,),
        in_specs=[pl.BlockSpec((tm, D_MODEL), lambda i: (i, 0)), pl.BlockSpec((1, D_MODEL), lambda i: (0, 0))],
        out_specs=pl.BlockSpec((tm, D_MODEL), lambda i: (i, 0)),
        out_shape=jax.ShapeDtypeStruct((n, D_MODEL), F32),
        compiler_params=_cparams(("arbitrary",)),
        name="final_norm",
    )(x, g)


def _block_diag_weights(w):
    d = w.shape[0]
    per = MXU_DIM // LRU_BLOCK
    w = w.reshape(d, LRU_HEADS // per, per, LRU_BLOCK, LRU_BLOCK)
    eye = jnp.eye(per, dtype=w.dtype)
    bd = jnp.einsum('dcpij,pq->dcpiqj', w, eye)
    return bd.reshape(d, LRU_HEADS // per, MXU_DIM, MXU_DIM).astype(BF16)


def _prep_params(p):
    P = dict(p)
    for name in ('w_in', 'w_out_a', 'w_out_b', 'w_out', 'w_ffn_gate', 'w_ffn_up', 'w_ffn_down',
                 'w_moe_gate', 'w_moe_up', 'w_moe_down', 'g_lora_up'):
        P[name] = p[name].astype(BF16)
    for name in ('norm_mix', 'conv_b', 'gate_a_b', 'gate_x_b', 'lru_lambda', 'shift_mu', 'w0', 'a0', 'k_k', 'k_a',
                 'ln_x_w', 'ln_x_b', 'norm_ffn'):
        P[name] = p[name][:, None, :]
    P['r_k'] = p['r_k'].reshape(DEPTH, 1, RWKV_WIDTH)
    P['norm_final'] = p['norm_final'][None, :]
    P['gaw_bd'] = _block_diag_weights(p['gate_a_w'])
    P['gxw_bd'] = _block_diag_weights(p['gate_x_w'])
    z = jnp.zeros((DEPTH, LORA_W, RWKV_WIDTH), F32)
    P['lora_wa'] = jnp.concatenate([jnp.concatenate([p['w_lora_up'], z], axis=2),
                                    jnp.concatenate([z, p['a_lora_up']], axis=2)], axis=1).astype(BF16)
    P['w_router_pad'] = jnp.pad(p['w_router'], ((0, 0), (0, 0), (0, LANES - N_EXPERTS)))
    return P


def _trunk(x3, st_rwkv, st_lru, st_conv, st_shift, P, *, zero_state, tm, lru_bb, lru_tt, rwkv_nb):
    B, T, _ = x3.shape
    n = B * T
    x = x3.reshape(n, D_MODEL)
    n_rwkv, n_lru, n_conv, n_shift = [], [], [], []
    for l in range(DEPTH):
        xy, ps, gt = _proj(x, P['norm_mix'], P['w_in'], l, min(tm, 256))
        ya, h_new, c_new = _lru(xy.reshape(B, T, 2 * LRU_WIDTH), st_conv[l], st_lru[l][:, None, :], P, l,
                                lru_bb, lru_tt)
        yb, sh_new, s_new = _wkv(ps.reshape(B, T, SHIFT_WIDTH), st_shift[l], st_rwkv[l], P, l, rwkv_nb, zero_state)
        moe = l % 2 == 1
        res = _merge(x, ya.reshape(n, D_MODEL), yb.reshape(n, D_MODEL), gt, P, l, tm, moe)
        if moe:
            x, h, gates, cnt = res
            tw = min(n, MOE_TILE)
            counts = cnt[:, 0, :N_EXPERTS].reshape(n // tw, tw // tm, N_EXPERTS).sum(axis=1)
            x = _moe(x, h, gates, counts.astype(jnp.int32).reshape(-1), P, l // 2, tw)
        else:
            x, h = res
            x = _ffn(x, h, P, l // 2, tm)
        n_rwkv.append(s_new)
        n_lru.append(h_new[:, 0, :])
        n_conv.append(c_new)
        n_shift.append(sh_new)
    y = _final_norm(x, P['norm_final'], tm).reshape(B, T, D_MODEL)
    return y, jnp.stack(n_rwkv), jnp.stack(n_lru), jnp.stack(n_conv), jnp.stack(n_shift)


def kernel(x_prompt, x_sample, state_rwkv, state_lru, state_conv, state_shift, norm_mix, w_in, conv_w, conv_b, gate_a_w, gate_a_b, gate_x_w, gate_x_b, lru_lambda, shift_mu, w0, w_lora_up, a0, a_lora_up, g_lora_up, k_k, k_a, r_k, ln_x_w, ln_x_b, w_out_a, w_out_b, w_out, norm_ffn, w_ffn_gate, w_ffn_up, w_ffn_down, w_router, w_moe_gate, w_moe_up, w_moe_down, norm_final):
    P = _prep_params(dict(
        norm_mix=norm_mix, w_in=w_in, conv_w=conv_w, conv_b=conv_b, gate_a_w=gate_a_w, gate_a_b=gate_a_b,
        gate_x_w=gate_x_w, gate_x_b=gate_x_b, lru_lambda=lru_lambda, shift_mu=shift_mu, w0=w0,
        w_lora_up=w_lora_up, a0=a0, a_lora_up=a_lora_up, g_lora_up=g_lora_up, k_k=k_k, k_a=k_a, r_k=r_k,
        ln_x_w=ln_x_w, ln_x_b=ln_x_b, w_out_a=w_out_a, w_out_b=w_out_b, w_out=w_out, norm_ffn=norm_ffn,
        w_ffn_gate=w_ffn_gate, w_ffn_up=w_ffn_up, w_ffn_down=w_ffn_down, w_router=w_router,
        w_moe_gate=w_moe_gate, w_moe_up=w_moe_up, w_moe_down=w_moe_down, norm_final=norm_final))
    bp = x_prompt.shape[0]
    bs = x_sample.shape[0]
    zeros = lambda *shape: jnp.zeros(shape, F32)
    p_out = _trunk(x_prompt,
                   zeros(DEPTH, bp, RWKV_HEADS, RWKV_HEAD, RWKV_HEAD), zeros(DEPTH, bp, LRU_WIDTH),
                   zeros(DEPTH, bp, CONV_WIDTH - 1, LRU_WIDTH), zeros(DEPTH, bp, SHIFT_WIDTH), P,
                   zero_state=True, tm=512, lru_bb=1, lru_tt=256, rwkv_nb=2)
    s_out = _trunk(x_sample, state_rwkv, state_lru, state_conv, state_shift, P,
                   zero_state=False, tm=256, lru_bb=8, lru_tt=x_sample.shape[1], rwkv_nb=1)
    return (p_out[0], s_out[0]) + tuple(p_out[1:]) + tuple(s_out[1:])
```

```python
import functools

import numpy as np
import jax
import jax.numpy as jnp
from jax import lax
from jax.experimental import pallas as pl
from jax.experimental.pallas import tpu as pltpu

F32 = jnp.float32
BF16 = jnp.bfloat16

D_MODEL = 1024
DEPTH = 4
LRU_WIDTH = D_MODEL
LRU_HEADS = 16
LRU_BLOCK = LRU_WIDTH // LRU_HEADS
CONV_WIDTH = 4
LRU_C = 8.0
RWKV_HEAD = 64
RWKV_WIDTH = D_MODEL
RWKV_HEADS = RWKV_WIDTH // RWKV_HEAD
LORA_W = 64
LORA_A = 64
LORA_G = 128
SHIFT_WIDTH = 3 * RWKV_WIDTH + LORA_W + LORA_A + LORA_G
PROJ_WIDTH = 2 * LRU_WIDTH + SHIFT_WIDTH + 2 * D_MODEL
D_FF = 2816
N_EXPERTS = 8
NORM_EPS = 1e-6
GN_EPS = 64e-5

LANES = 128
SUBLANES = 8
MXU_DIM = 256
VMEM_LIMIT = 56 * 1024 * 1024

CHUNK = 64
GROUP_HEADS = MXU_DIM // CHUNK
GROUP_LANES = GROUP_HEADS * RWKV_HEAD
N_GROUPS = RWKV_HEADS // GROUP_HEADS
FF_CHUNK = 256
TOP_K = 2
ROUTE_TILE = 512
ROUTE_ROWS = 256
MOE_TILE = 1024
MOE_SUB = 128
MOE_FF_SPLIT = 2
SCAN_ROWS = 16
assert CHUNK == RWKV_HEAD


def _cparams(sem):
    return pltpu.CompilerParams(dimension_semantics=sem, vmem_limit_bytes=VMEM_LIMIT)


def _dot(a, b):
    return jnp.dot(a, b, preferred_element_type=F32)


def _dot_tb(a, b):
    return lax.dot_general(a, b, (((1,), (1,)), ((), ())), preferred_element_type=F32)


def _dot_ta(a, b):
    return lax.dot_general(a, b, (((0,), (0,)), ((), ())), preferred_element_type=F32)


def _dot_exact(a, b_bf16):
    hi = a.astype(BF16)
    lo = (a - hi.astype(F32)).astype(BF16)
    return _dot(hi, b_bf16) + _dot(lo, b_bf16)


def _sigmoid(x):
    return jax.nn.sigmoid(x)


def _softplus(z):
    return jnp.maximum(z, 0.0) + jnp.log(1.0 + jnp.exp(-jnp.abs(z)))


def _rms(x, g):
    return x * lax.rsqrt(jnp.mean(x * x, axis=-1, keepdims=True) + NORM_EPS) * g


_PROJ_SPLITS = ((0, 2 * LRU_WIDTH), (2 * LRU_WIDTH, SHIFT_WIDTH), (2 * LRU_WIDTH + SHIFT_WIDTH, 2 * D_MODEL))


def _proj_kernel(x_ref, g_ref, w_ref, xy_ref, ps_ref, gt_ref):
    u = _rms(x_ref[...], g_ref[...]).astype(BF16)
    for out_ref, (c0, width) in zip((xy_ref, ps_ref, gt_ref), _PROJ_SPLITS):
        j = 0
        while j < width:
            w = min(512, width - j)
            out_ref[:, j:j + w] = _dot(u, w_ref[:, c0 + j:c0 + j + w])
            j += w


def _proj(x, g_all, w_all, layer, tm):
    n = x.shape[0]
    return pl.pallas_call(
        _proj_kernel,
        grid=(n // tm,),
        in_specs=[
            pl.BlockSpec((tm, D_MODEL), lambda i: (i, 0)),
            pl.BlockSpec((None, 1, D_MODEL), lambda i: (layer, 0, 0)),
            pl.BlockSpec((None, D_MODEL, PROJ_WIDTH), lambda i: (layer, 0, 0),
                         pipeline_mode=pl.Buffered(1)),
        ],
        out_specs=[
            pl.BlockSpec((tm, 2 * LRU_WIDTH), lambda i: (i, 0)),
            pl.BlockSpec((tm, SHIFT_WIDTH), lambda i: (i, 0)),
            pl.BlockSpec((tm, 2 * D_MODEL), lambda i: (i, 0)),
        ],
        out_shape=[
            jax.ShapeDtypeStruct((n, 2 * LRU_WIDTH), F32),
            jax.ShapeDtypeStruct((n, SHIFT_WIDTH), F32),
            jax.ShapeDtypeStruct((n, 2 * D_MODEL), F32),
        ],
        compiler_params=_cparams(("arbitrary",)),
        name="proj",
    )(x, g_all, w_all)


def _gelu_tanh(x):
    return 0.5 * x * (1.0 + jnp.tanh(np.sqrt(2.0 / np.pi).astype(np.float32) * (x + 0.044715 * (x * x * x))))


def _lru_kernel(xa_ref, ya_ref, cs_ref, h0_ref, cw_ref, cb_ref, gaw_ref, gab_ref, gxw_ref, gxb_ref, lam_ref,
                y_ref, hl_ref, nc_ref, xp_s, h_s, a_s, b_s, *, bb, tt):
    ti = pl.program_id(1)
    neg_c = -LRU_C * _softplus(-lam_ref[...])
    cw = cw_ref[...]
    @pl.when(ti == 0)
    def _():
        for b in range(bb):
            xp_s[b, 5:8, :] = cs_ref[b]
            h_s[b] = h0_ref[b]

    for b in range(bb):
        xa = xa_ref[b]
        xp_s[b, 8:8 + tt, :] = xa
        xc = cb_ref[...] + xp_s[b, 5:5 + tt, :] * cw[0:1]
        xc = xc + xp_s[b, 6:6 + tt, :] * cw[1:2]
        xc = xc + xp_s[b, 7:7 + tt, :] * cw[2:3]
        xc = xc + xa * cw[3:4]
        tail = xp_s[b, 5 + tt:8 + tt, :]
        xp_s[b, 5:8, :] = tail
        nc_ref[b] = tail

        for c in range(LRU_WIDTH // MXU_DIM):
            sl = slice(c * MXU_DIM, (c + 1) * MXU_DIM)
            xcc = xc[:, sl]
            xcb = xcc.astype(BF16)
            r = _sigmoid(_dot(xcb, gaw_ref[c]) + gab_ref[:, sl])
            i = _sigmoid(_dot(xcb, gxw_ref[c]) + gxb_ref[:, sl])
            log_a = neg_c[:, sl] * r
            a = jnp.exp(log_a)
            a_s[:, sl] = a
            b_s[:, sl] = jnp.sqrt(-jnp.tanh(log_a) * (a * a + 1.0)) * (i * xcc)

        if tt % SCAN_ROWS == 0:
            row = lax.broadcasted_iota(jnp.int32, (SUBLANES, LRU_WIDTH), 0)

            def blk(j, h):
                r0 = pl.multiple_of(j * SCAN_ROWS, SCAN_ROWS)
                hbs = []
                for q in range(SCAN_ROWS // SUBLANES):
                    av = a_s[pl.ds(r0 + q * SUBLANES, SUBLANES), :]
                    bv = b_s[pl.ds(r0 + q * SUBLANES, SUBLANES), :]
                    for d in (1, 2, 4):
                        m = row >= d
                        a_sh = pltpu.roll(av, d, axis=0)
                        b_sh = pltpu.roll(bv, d, axis=0)
                        bv = jnp.where(m, av * b_sh + bv, bv)
                        av = jnp.where(m, av * a_sh, av)
                    hb = av * h + bv
                    h = hb[SUBLANES - 1:SUBLANES, :]
                    hbs.append(hb)
                hs = jnp.concatenate(hbs, axis=0)
                y_ref[b, pl.ds(r0, SCAN_ROWS), :] = (hs * _gelu_tanh(ya_ref[b, pl.ds(r0, SCAN_ROWS), :])).astype(BF16)
                return h

            h = lax.fori_loop(0, tt // SCAN_ROWS, blk, h_s[b])
        else:
            h = h_s[b]
            rows = []
            for t in range(tt):
                h = a_s[t:t + 1, :] * h + b_s[t:t + 1, :]
                rows.append(h)
            hs = jnp.concatenate(rows, axis=0)
            y_ref[b] = (hs * _gelu_tanh(ya_ref[b])).astype(BF16)
        h_s[b] = h
        hl_ref[b] = h


def _lru(xy3, st_conv, st_lru, P, layer, bb, tt):
    B, T, _ = xy3.shape
    W = LRU_WIDTH
    vec = lambda: pl.BlockSpec((None, 1, W), lambda b, t: (layer, 0, 0))
    kern = functools.partial(_lru_kernel, bb=bb, tt=tt)
    return pl.pallas_call(
        kern,
        grid=(B // bb, T // tt),
        in_specs=[
            pl.BlockSpec((bb, tt, W), lambda b, t: (b, t, 0)),
            pl.BlockSpec((bb, tt, W), lambda b, t: (b, t, 1)),
            pl.BlockSpec((bb, CONV_WIDTH - 1, W), lambda b, t: (b, 0, 0)),
            pl.BlockSpec((bb, 1, W), lambda b, t: (b, 0, 0)),
            pl.BlockSpec((None, CONV_WIDTH, W), lambda b, t: (layer, 0, 0)),
            vec(),
            pl.BlockSpec((None, W // MXU_DIM, MXU_DIM, MXU_DIM), lambda b, t: (layer, 0, 0, 0)),
            vec(),
            pl.BlockSpec((None, W // MXU_DIM, MXU_DIM, MXU_DIM), lambda b, t: (layer, 0, 0, 0)),
            vec(),
            vec(),
        ],
        out_specs=[
            pl.BlockSpec((bb, tt, W), lambda b, t: (b, t, 0)),
            pl.BlockSpec((bb, 1, W), lambda b, t: (b, 0, 0)),
            pl.BlockSpec((bb, CONV_WIDTH - 1, W), lambda b, t: (b, 0, 0)),
        ],
        out_shape=[
            jax.ShapeDtypeStruct((B, T, W), BF16),
            jax.ShapeDtypeStruct((B, 1, W), F32),
            jax.ShapeDtypeStruct((B, CONV_WIDTH - 1, W), F32),
        ],
        scratch_shapes=[
            pltpu.VMEM((bb, tt + 8, W), F32),
            pltpu.VMEM((bb, 1, W), F32),
            pltpu.VMEM((tt, W), F32),
            pltpu.VMEM((tt, W), F32),
        ],
        compiler_params=_cparams(("arbitrary", "arbitrary")),
        name="lru",
    )(xy3, xy3, st_conv, st_lru, P['conv_w'], P['conv_b'], P['gaw_bd'], P['gate_a_b'], P['gxw_bd'],
      P['gate_x_b'], P['lru_lambda'])


def _block_diag(x_bf16, mask_ref):
    return jnp.concatenate([x_bf16] * GROUP_HEADS, axis=0) * mask_ref[...]


def _rwkv_kernel(ps_ref, sh_ref, s0_ref, mu_ref, w0_ref, lora_ref, gup_ref, a0_ref, kk_ref, ka_ref, rk_ref,
                 lnw_ref, lnb_ref, tri_ref, ones_ref, mrow_ref, msq_ref,
                 y_ref, nsh_ref, ns_ref, pad_s, car_s, st_s, *, nb, t_valid, zero_state):
    C = CHUNK
    ci = pl.program_id(1)
    last = ci == pl.num_programs(1) - 1
    rows_c = lax.broadcasted_iota(jnp.int32, (C, 1), 0)
    t_idx = lax.broadcasted_iota(jnp.int32, (C, MXU_DIM), 0)
    i_idx = lax.broadcasted_iota(jnp.int32, (C, MXU_DIM), 1) % C
    m_strict = i_idx < t_idx
    m_incl = i_idx <= t_idx
    eye_cat = (i_idx == t_idx).astype(F32)
    ones_bd = ones_ref[...]

    def head_sum(x):
        parts = []
        for g in range(RWKV_WIDTH // MXU_DIM):
            parts.append(_dot_exact(x[:, g * MXU_DIM:(g + 1) * MXU_DIM], ones_bd))
        return jnp.concatenate(parts, axis=1)

    @pl.when(ci == 0)
    def _():
        for b in range(nb):
            car_s[b] = sh_ref[b]
            if zero_state:
                st_s[b] = jnp.zeros(st_s.shape[1:], F32)
            else:
                for g in range(N_GROUPS):
                    rows = [s0_ref[b, g * GROUP_HEADS + hh] for hh in range(GROUP_HEADS)]
                    stacked = jnp.concatenate(rows, axis=0)
                    tiled = jnp.concatenate([stacked] * GROUP_HEADS, axis=1)
                    st_s[b, g] = tiled * msq_ref[...].astype(F32)

    preps, chains = [], []
    for b in range(nb):
        if t_valid == C:
            ps = ps_ref[b]
        else:
            pad_s[...] = jnp.zeros(pad_s.shape, F32)
            pad_s[0:t_valid, :] = ps_ref[b]
            ps = pad_s[...]
        prev = pltpu.roll(ps, 1, axis=0)
        prev = jnp.where(rows_c == 0, car_s[b], prev)
        new_carry = ps[t_valid - 1:t_valid, :]
        car_s[b] = new_carry
        nsh_ref[b] = new_carry

        s = ps + (prev - ps) * mu_ref[...]
        o = RWKV_WIDTH
        r = s[:, :o]
        k = s[:, o:2 * o]
        v = s[:, 2 * o:3 * o]
        dwa = s[:, 3 * o:3 * o + LORA_W + LORA_A]
        dg = s[:, 3 * o + LORA_W + LORA_A:]
        lane = lax.broadcasted_iota(jnp.int32, dwa.shape, 1)
        lora_in = jnp.where(lane < LORA_W, jnp.tanh(dwa), dwa).astype(BF16)
        lora = _dot(lora_in, lora_ref[...])
        w_log = -_softplus(-(w0_ref[...] + lora[:, :o])) - 0.5
        lw = -jnp.exp(w_log)
        a = _sigmoid(a0_ref[...] + lora[:, o:])
        g = _dot(_sigmoid(dg).astype(BF16), gup_ref[...])
        kk = k * kk_ref[...]
        kk = kk / jnp.maximum(jnp.sqrt(head_sum(kk * kk)), 1e-12)
        kmod = k * (1.0 + (a - 1.0) * ka_ref[...])
        if t_valid != C:
            valid = rows_c < t_valid
            lw = jnp.where(valid, lw, 0.0)
            kk = jnp.where(valid, kk, 0.0)
            kmod = jnp.where(valid, kmod, 0.0)
            v = jnp.where(valid, v, 0.0)
        beta = kk * a
        L = jnp.concatenate(
            [_dot_tri(tri_ref[...], lw[:, j * MXU_DIM:(j + 1) * MXU_DIM]) for j in range(o // MXU_DIM)], axis=1)
        e_l = jnp.exp(L)
        e_neg = jnp.exp(-L)
        l_end = L[C - 1:C, :]
        e_end = jnp.exp(l_end - L)
        p_end = jnp.exp(l_end)
        at = (-kk) * jnp.exp(L - lw)
        rt = r * e_l
        bt = beta * e_neg
        kt = kmod * e_neg
        bend = beta * e_end
        kend = kmod * e_end

        preps.append(dict(r=r, v=v, g=g, kmod=kmod))
        for gi in range(N_GROUPS):
            sl = slice(gi * GROUP_LANES, (gi + 1) * GROUP_LANES)
            chains.append(dict(
                b=b, gi=gi, vg=v[:, sl], p_end=p_end[:, sl],
                x2=jnp.concatenate([at[:, sl], rt[:, sl]], axis=0).astype(BF16),
                wbd=jnp.concatenate([_block_diag(bt[:, sl].astype(BF16), mrow_ref),
                                     _block_diag(kt[:, sl].astype(BF16), mrow_ref)], axis=0),
                bk=jnp.concatenate([bend[:, sl], kend[:, sl]], axis=0).astype(BF16)))

    cw = GROUP_HEADS * C
    for c in chains:
        res = _dot_tb(c['x2'], c['wbd'])
        n_cat = jnp.where(m_strict, res[:C, :cw], 0.0)
        c['a_ak'] = jnp.where(m_strict, res[:C, cw:], 0.0).astype(BF16)
        c['a_r'] = jnp.concatenate([jnp.where(m_incl, res[C:, :cw], 0.0),
                                    jnp.where(m_incl, res[C:, cw:], 0.0)], axis=1).astype(BF16)
        c['x_c'] = n_cat.astype(BF16)
        c['p_c'] = eye_cat + n_cat
        c['x_bd'] = _block_diag(c['x_c'], msq_ref)
    for c in chains:
        st = st_s[c['b'], c['gi']]
        c['xs'] = _dot_tb(c['x2'], st.astype(BF16))
        c['v_bd'] = _block_diag(c['vg'].astype(BF16), msq_ref)
    for c in chains:
        c['rhs'] = c['xs'][:C] + _dot(c['a_ak'], c['v_bd'])
    lvl = 2
    while lvl < C:
        for c in chains:
            c['x_c'] = _dot(c['x_c'], c['x_bd']).astype(BF16)
            c['x_bd'] = _block_diag(c['x_c'], msq_ref)
        for c in chains:
            c['p_c'] = c['p_c'] + _dot(c['p_c'].astype(BF16), c['x_bd'])
        lvl *= 2
    for c in chains:
        c['u'] = _dot(c['p_c'].astype(BF16), _block_diag(c['rhs'].astype(BF16), msq_ref))
    ys = {}
    for c in chains:
        ub = c['u'].astype(BF16)
        ys[(c['b'], c['gi'])] = c['xs'][C:] + _dot(c['a_r'], jnp.concatenate([_block_diag(ub, msq_ref), c['v_bd']],
                                                                             axis=0))
        uv = jnp.concatenate([ub, c['vg'].astype(BF16)], axis=0)
        st = st_s[c['b'], c['gi']]
        st_s[c['b'], c['gi']] = st * c['p_end'] + _dot_ta(uv, c['bk']) * msq_ref[...].astype(F32)

    for b in range(nb):
        r, v, g, kmod = (preps[b][n] for n in ('r', 'v', 'g', 'kmod'))
        yc = jnp.concatenate([ys[(b, gi)] for gi in range(N_GROUPS)], axis=1)
        inv_n = 1.0 / RWKV_HEAD
        mean = head_sum(yc) * inv_n
        dlt = yc - mean
        var = head_sum(dlt * dlt) * inv_n
        yn = dlt * lax.rsqrt(var + GN_EPS) * lnw_ref[...] + lnb_ref[...]
        bonus = head_sum(r * kmod * rk_ref[...]) * v
        out = ((yn + bonus) * g).astype(BF16)
        if t_valid == C:
            y_ref[b] = out
        else:
            y_ref[b] = out[0:t_valid, :]

    @pl.when(last)
    def _():
        for b in range(nb):
            for gi in range(N_GROUPS):
                for hh in range(GROUP_HEADS):
                    ns_ref[b, gi * GROUP_HEADS + hh] = st_s[b, gi, hh * RWKV_HEAD:(hh + 1) * RWKV_HEAD,
                                                            hh * RWKV_HEAD:(hh + 1) * RWKV_HEAD]


def _dot_tri(tri_bf16, x):
    hi = x.astype(BF16)
    lo = (x - hi.astype(F32)).astype(BF16)
    return _dot(tri_bf16, hi) + _dot(tri_bf16, lo)


def _rwkv_consts():
    C = CHUNK
    gh = GROUP_HEADS
    tri = np.tril(np.ones((C, C), np.float32))
    hl = np.arange(MXU_DIM) // RWKV_HEAD
    ones_bd = (hl[:, None] == hl[None, :]).astype(np.float32)
    rowh = np.arange(gh * C) // C
    colh = np.arange(GROUP_LANES) // RWKV_HEAD
    mrow = (rowh[:, None] == colh[None, :]).astype(np.float32)
    colc = np.arange(gh * C) // C
    msq = (rowh[:, None] == colc[None, :]).astype(np.float32)
    return (jnp.asarray(tri, BF16), jnp.asarray(ones_bd, BF16), jnp.asarray(mrow, BF16), jnp.asarray(msq, BF16))


def _rwkv(ps3, st_shift, st_rwkv, P, layer, nb, zero_state):
    B, T, _ = ps3.shape
    C = CHUNK
    if T % C == 0:
        t_valid, nchunks, tb = C, T // C, C
    else:
        assert T < C
        t_valid, nchunks, tb = T, 1, T
    o = RWKV_WIDTH
    tri, ones_bd, mrow, msq = _rwkv_consts()
    vec = lambda w: pl.BlockSpec((None, 1, w), lambda b, c: (layer, 0, 0))
    const = lambda arr: pl.BlockSpec(arr.shape, lambda b, c: (0,) * arr.ndim)
    kern = functools.partial(_rwkv_kernel, nb=nb, t_valid=t_valid, zero_state=zero_state)
    state_spec = pl.BlockSpec((nb, RWKV_HEADS, RWKV_HEAD, RWKV_HEAD), lambda b, c: (b, 0, 0, 0))
    return pl.pallas_call(
        kern,
        grid=(B // nb, nchunks),
        in_specs=[
            pl.BlockSpec((nb, tb, SHIFT_WIDTH), lambda b, c: (b, c, 0)),
            pl.BlockSpec((nb, 1, SHIFT_WIDTH), lambda b, c: (b, 0, 0)),
            state_spec,
            vec(SHIFT_WIDTH),
            vec(o),
            pl.BlockSpec((None, LORA_W + LORA_A, 2 * o), lambda b, c: (layer, 0, 0)),
            pl.BlockSpec((None, LORA_G, o), lambda b, c: (layer, 0, 0)),
            vec(o), vec(o), vec(o), vec(o), vec(o), vec(o),
            const(tri), const(ones_bd), const(mrow), const(msq),
        ],
        out_specs=[
            pl.BlockSpec((nb, tb, o), lambda b, c: (b, c, 0)),
            pl.BlockSpec((nb, 1, SHIFT_WIDTH), lambda b, c: (b, 0, 0)),
            state_spec,
        ],
        out_shape=[
            jax.ShapeDtypeStruct((B, T, o), BF16),
            jax.ShapeDtypeStruct((B, 1, SHIFT_WIDTH), F32),
            jax.ShapeDtypeStruct((B, RWKV_HEADS, RWKV_HEAD, RWKV_HEAD), F32),
        ],
        scratch_shapes=[
            pltpu.VMEM((C, SHIFT_WIDTH), F32),
            pltpu.VMEM((nb, 1, SHIFT_WIDTH), F32),
            pltpu.VMEM((nb, N_GROUPS, MXU_DIM, MXU_DIM), F32),
        ],
        compiler_params=_cparams(("arbitrary", "arbitrary")),
        name="rwkv",
    )(ps3, st_shift, st_rwkv, P['shift_mu'], P['w0'], P['lora_wa'], P['g_lora_up'], P['a0'], P['k_k'], P['k_a'],
      P['r_k'], P['ln_x_w'], P['ln_x_b'], tri, ones_bd, mrow, msq)


def _rows(x, b, n):
    return x[b * n:(b + 1) * n]


def _dot_split3(sel_bf16, x):
    hi = x.astype(BF16)
    r1 = x - hi.astype(F32)
    mid = r1.astype(BF16)
    lo = (r1 - mid.astype(F32)).astype(BF16)
    return _dot(sel_bf16, hi) + _dot(sel_bf16, mid) + _dot(sel_bf16, lo)


def _wkv_kernel(ps_ref, sh_ref, s0_ref, mu_ref, w0_ref, lora_ref, gup_ref, a0_ref, kk_ref, ka_ref, rk_ref,
                lnw_ref, lnb_ref, tri_ref, endm_ref, ones_ref, mrow_ref, msq_ref,
                y_ref, nsh_ref, ns_ref, car_s, st_s, *, nb, seq, zero_state):
    C = CHUNK
    Q = C // seq
    R = nb * C
    o = RWKV_WIDTH
    ci = pl.program_id(1)
    last = ci == pl.num_programs(1) - 1
    t_idx = lax.broadcasted_iota(jnp.int32, (C, MXU_DIM), 0)
    i_idx = lax.broadcasted_iota(jnp.int32, (C, MXU_DIM), 1) % C
    same = (t_idx // seq) == (i_idx // seq)
    m_strict = same & (i_idx < t_idx)
    m_incl = same & (i_idx <= t_idx)
    eye_cat = (i_idx == t_idx).astype(F32)
    ones_bd = ones_ref[...]
    msq_f = msq_ref[...].astype(F32)
    n_lane_tiles = o // MXU_DIM

    def head_sum(x):
        xs = jnp.concatenate([x[:, g * MXU_DIM:(g + 1) * MXU_DIM] for g in range(n_lane_tiles)], axis=0)
        hi = xs.astype(BF16)
        lo = (xs - hi.astype(F32)).astype(BF16)
        s = _dot(jnp.concatenate([hi, lo], axis=0), ones_bd)
        s = s[:n_lane_tiles * R] + s[n_lane_tiles * R:]
        return jnp.concatenate([_rows(s, g, R) for g in range(n_lane_tiles)], axis=1)

    @pl.when(ci == 0)
    def _():
        for b in range(nb):
            car_s[b] = sh_ref[b]
            for q in range(Q):
                if zero_state:
                    st_s[b, q] = jnp.zeros(st_s.shape[2:], F32)
                else:
                    for g in range(N_GROUPS):
                        heads = [s0_ref[b, q * RWKV_HEADS + g * GROUP_HEADS + hh] for hh in range(GROUP_HEADS)]
                        stacked = jnp.concatenate(heads, axis=0)
                        st_s[b, q, g] = jnp.concatenate([stacked] * GROUP_HEADS, axis=1) * msq_f

    row_c = lax.broadcasted_iota(jnp.int32, (C, 1), 0)
    prevs = []
    for b in range(nb):
        ps_b = ps_ref[b]
        if Q == 1:
            first = car_s[b]
            new_carry = ps_b[C - 1:C, :]
        else:
            put = (lax.broadcasted_iota(jnp.int32, (C, Q), 0)
                   == seq * lax.broadcasted_iota(jnp.int32, (C, Q), 1)).astype(BF16)
            take = (lax.broadcasted_iota(jnp.int32, (Q, C), 1)
                    == seq * lax.broadcasted_iota(jnp.int32, (Q, C), 0) + (seq - 1)).astype(BF16)
            first = _dot_split3(put, car_s[b])
            new_carry = _dot_split3(take, ps_b)
        prevs.append(jnp.where(row_c % seq == 0, first, pltpu.roll(ps_b, 1, axis=0)))
        car_s[b] = new_carry
        nsh_ref[b] = new_carry
    ps = jnp.concatenate([ps_ref[b] for b in range(nb)], axis=0)
    prev = jnp.concatenate(prevs, axis=0)

    s = ps + (prev - ps) * mu_ref[...]
    r = s[:, :o]
    k = s[:, o:2 * o]
    v = s[:, 2 * o:3 * o]
    dwa = s[:, 3 * o:3 * o + LORA_W + LORA_A]
    dg = s[:, 3 * o + LORA_W + LORA_A:]
    lane = lax.broadcasted_iota(jnp.int32, dwa.shape, 1)
    lora_in = jnp.where(lane < LORA_W, jnp.tanh(dwa), dwa).astype(BF16)
    lora = _dot(lora_in, lora_ref[...])
    w_log = -_softplus(-(w0_ref[...] + lora[:, :o])) - 0.5
    lw = -jnp.exp(w_log)
    a = _sigmoid(a0_ref[...] + lora[:, o:])
    g = _dot(_sigmoid(dg).astype(BF16), gup_ref[...])
    kk = k * kk_ref[...]
    kk = kk / jnp.maximum(jnp.sqrt(head_sum(kk * kk)), 1e-12)
    kmod = k * (1.0 + (a - 1.0) * ka_ref[...])
    beta = kk * a
    lw_hi = lw.astype(BF16)
    lw_lo = (lw - lw_hi.astype(F32)).astype(BF16)
    L = _dot(tri_ref[...], lw_hi) + _dot(tri_ref[...], lw_lo)
    l_end = _dot(endm_ref[...], lw_hi) + _dot(endm_ref[...], lw_lo)
    e_neg = jnp.exp(-L)
    e_end = jnp.exp(l_end - L)
    p_end = jnp.exp(l_end)
    at = (-kk) * jnp.exp(L - lw)
    rt = r * jnp.exp(L)
    bt = (beta * e_neg).astype(BF16)
    kt = (kmod * e_neg).astype(BF16)
    bend = beta * e_end
    kend = kmod * e_end

    chains = []
    for b in range(nb):
        for gi in range(N_GROUPS):
            sl = slice(gi * GROUP_LANES, (gi + 1) * GROUP_LANES)
            chains.append(dict(
                b=b, gi=gi, vg=_rows(v, b, C)[:, sl].astype(BF16), p_end=_rows(p_end, b, C)[:, sl],
                x2=jnp.concatenate([_rows(at, b, C)[:, sl], _rows(rt, b, C)[:, sl]], axis=0).astype(BF16),
                wbd=jnp.concatenate([_block_diag(_rows(bt, b, C)[:, sl], mrow_ref),
                                     _block_diag(_rows(kt, b, C)[:, sl], mrow_ref)], axis=0),
                bk=jnp.concatenate([_rows(bend, b, C)[:, sl], _rows(kend, b, C)[:, sl]], axis=0).astype(BF16)))

    cw = GROUP_HEADS * C
    rowseq = (lax.broadcasted_iota(jnp.int32, (2 * C, 1), 0) % C) // seq
    for c in chains:
        res = _dot_tb(c['x2'], c['wbd'])
        n_cat = jnp.where(m_strict, res[:C, :cw], 0.0)
        c['a_ak'] = jnp.where(m_strict, res[:C, cw:], 0.0).astype(BF16)
        c['a_r'] = jnp.concatenate([jnp.where(m_incl, res[C:, :cw], 0.0),
                                    jnp.where(m_incl, res[C:, cw:], 0.0)], axis=1).astype(BF16)
        c['x_c'] = n_cat.astype(BF16)
        c['p_c'] = eye_cat + n_cat
        c['x_bd'] = _block_diag(c['x_c'], msq_ref)
    for c in chains:
        xs = None
        for q in range(Q):
            xq = _dot_tb(c['x2'], st_s[c['b'], q, c['gi']].astype(BF16))
            xs = xq if xs is None else jnp.where(rowseq == q, xq, xs)
        c['xs'] = xs
        c['v_bd'] = _block_diag(c['vg'], msq_ref)
    for c in chains:
        c['rhs'] = c['xs'][:C] + _dot(c['a_ak'], c['v_bd'])
    lvl = 2
    while lvl < seq:
        for c in chains:
            c['x_c'] = _dot(c['x_c'], c['x_bd']).astype(BF16)
            c['x_bd'] = _block_diag(c['x_c'], msq_ref)
        for c in chains:
            c['p_c'] = c['p_c'] + _dot(c['p_c'].astype(BF16), c['x_bd'])
        lvl *= 2
    for c in chains:
        c['u'] = _dot(c['p_c'].astype(BF16), _block_diag(c['rhs'].astype(BF16), msq_ref)).astype(BF16)
    ys = {}
    for c in chains:
        b, gi = c['b'], c['gi']
        ys[(b, gi)] = c['xs'][C:] + _dot(c['a_r'], jnp.concatenate([_block_diag(c['u'], msq_ref), c['v_bd']], axis=0))
        uv = jnp.concatenate([c['u'], c['vg']], axis=0)
        if Q > 1:
            uv = jnp.concatenate([jnp.where(rowseq == q, uv, jnp.zeros_like(uv)) for q in range(Q)], axis=1)
        ds = _dot_ta(uv, c['bk'])
        for q in range(Q):
            st_s[b, q, gi] = (st_s[b, q, gi] * c['p_end'][q * seq:q * seq + 1, :]
                              + _rows(ds, q, GROUP_LANES) * msq_f)

    yc = jnp.concatenate([jnp.concatenate([ys[(b, gi)] for gi in range(N_GROUPS)], axis=1) for b in range(nb)],
                         axis=0)
    inv_n = 1.0 / RWKV_HEAD
    mean = head_sum(yc) * inv_n
    dlt = yc - mean
    var = head_sum(dlt * dlt) * inv_n
    yn = dlt * lax.rsqrt(var + GN_EPS) * lnw_ref[...] + lnb_ref[...]
    bonus = head_sum(r * kmod * rk_ref[...]) * v
    out = ((yn + bonus) * g).astype(BF16)
    for b in range(nb):
        y_ref[b] = _rows(out, b, C)

    @pl.when(last)
    def _():
        for b in range(nb):
            for q in range(Q):
                for gi in range(N_GROUPS):
                    for hh in range(GROUP_HEADS):
                        ns_ref[b, q * RWKV_HEADS + gi * GROUP_HEADS + hh] = st_s[
                            b, q, gi, hh * RWKV_HEAD:(hh + 1) * RWKV_HEAD, hh * RWKV_HEAD:(hh + 1) * RWKV_HEAD]


def _wkv_consts(nb, seq):
    C = CHUNK
    gh = GROUP_HEADS
    rows = np.arange(nb * C)
    same = (rows[:, None] // seq) == (rows[None, :] // seq)
    tri = (same & (rows[None, :] <= rows[:, None])).astype(np.float32)
    endm = same.astype(np.float32)
    hl = np.arange(MXU_DIM) // RWKV_HEAD
    ones_bd = (hl[:, None] == hl[None, :]).astype(np.float32)
    rowh = np.arange(gh * C) // C
    colh = np.arange(GROUP_LANES) // RWKV_HEAD
    mrow = (rowh[:, None] == colh[None, :]).astype(np.float32)
    msq = (rowh[:, None] == rowh[None, :]).astype(np.float32)
    return tuple(jnp.asarray(m, BF16) for m in (tri, endm, ones_bd, mrow, msq))


def _wkv(ps3, st_shift, st_rwkv, P, layer, nb, zero_state):
    B, T, _ = ps3.shape
    C = CHUNK
    seq = C if T % C == 0 else T
    assert C % seq == 0
    Q = C // seq
    G = B // Q
    tg = T * Q
    assert B % Q == 0 and G % nb == 0 and tg % C == 0
    o = RWKV_WIDTH
    consts = _wkv_consts(nb, seq)
    vec = lambda w: pl.BlockSpec((None, 1, w), lambda b, c: (layer, 0, 0))
    const = lambda arr: pl.BlockSpec(arr.shape, lambda b, c: (0,) * arr.ndim)
    kern = functools.partial(_wkv_kernel, nb=nb, seq=seq, zero_state=zero_state)
    shift_spec = pl.BlockSpec((nb, Q, SHIFT_WIDTH), lambda b, c: (b, 0, 0))
    state_spec = pl.BlockSpec((nb, Q * RWKV_HEADS, RWKV_HEAD, RWKV_HEAD), lambda b, c: (b, 0, 0, 0))
    y, nsh, ns = pl.pallas_call(
        kern,
        grid=(G // nb, tg // C),
        in_specs=[
            pl.BlockSpec((nb, C, SHIFT_WIDTH), lambda b, c: (b, c, 0)),
            shift_spec,
            pl.BlockSpec((nb, Q * RWKV_HEADS, RWKV_HEAD, RWKV_HEAD), lambda b, c: (b, 0, 0, 0),
                         pipeline_mode=pl.Buffered(1)),
            vec(SHIFT_WIDTH),
            vec(o),
            pl.BlockSpec((None, LORA_W + LORA_A, 2 * o), lambda b, c: (layer, 0, 0)),
            pl.BlockSpec((None, LORA_G, o), lambda b, c: (layer, 0, 0)),
            vec(o), vec(o), vec(o), vec(o), vec(o), vec(o),
        ] + [const(m) for m in consts],
        out_specs=[
            pl.BlockSpec((nb, C, o), lambda b, c: (b, c, 0)),
            shift_spec,
            state_spec,
        ],
        out_shape=[
            jax.ShapeDtypeStruct((G, tg, o), BF16),
            jax.ShapeDtypeStruct((G, Q, SHIFT_WIDTH), F32),
            jax.ShapeDtypeStruct((G, Q * RWKV_HEADS, RWKV_HEAD, RWKV_HEAD), F32),
        ],
        scratch_shapes=[
            pltpu.VMEM((nb, Q, SHIFT_WIDTH), F32),
            pltpu.VMEM((nb, Q, N_GROUPS, MXU_DIM, MXU_DIM), F32),
        ],
        compiler_params=_cparams(("arbitrary", "arbitrary")),
        name="rwkv",
    )(ps3.reshape(G, tg, SHIFT_WIDTH), st_shift.reshape(G, Q, SHIFT_WIDTH),
      st_rwkv.reshape(G, Q * RWKV_HEADS, RWKV_HEAD, RWKV_HEAD),
      P['shift_mu'], P['w0'], P['lora_wa'], P['g_lora_up'], P['a0'], P['k_k'], P['k_a'],
      P['r_k'], P['ln_x_w'], P['ln_x_b'], *consts)
    return (y.reshape(B, T, o), nsh.reshape(B, SHIFT_WIDTH),
            ns.reshape(B, RWKV_HEADS, RWKV_HEAD, RWKV_HEAD))


def _merge_kernel(x_ref, ya_ref, yb_ref, gt_ref, wa_ref, wb_ref, wo_ref, nf_ref, *rest, moe):
    if moe:
        wr_ref, xo_ref, h_ref, gates_ref, cnt_ref = rest
    else:
        xo_ref, h_ref = rest
    ga = gt_ref[:, :D_MODEL]
    gb = gt_ref[:, D_MODEL:]
    m = _sigmoid(ga) * _dot(ya_ref[...], wa_ref[...]) + _sigmoid(gb) * _dot(yb_ref[...], wb_ref[...])
    x = x_ref[...] + _dot(m.astype(BF16), wo_ref[...])
    xo_ref[...] = x
    h = _rms(x, nf_ref[...])
    h_ref[...] = h.astype(BF16)
    if moe:
        logits = jnp.dot(h, wr_ref[...], preferred_element_type=F32, precision=lax.Precision.HIGHEST)
        lane = lax.broadcasted_iota(jnp.int32, logits.shape, 1)
        real = lane < N_EXPERTS
        logits = jnp.where(real, logits, -jnp.inf)
        e = jnp.exp(logits - jnp.max(logits, axis=-1, keepdims=True))
        p = jnp.where(real, e / jnp.sum(e, axis=-1, keepdims=True), -1.0)
        m1 = jnp.max(p, axis=-1, keepdims=True)
        i1 = jnp.min(jnp.where(p == m1, lane, LANES), axis=-1, keepdims=True)
        oh1 = lane == i1
        p2 = jnp.where(oh1, -1.0, p)
        m2 = jnp.max(p2, axis=-1, keepdims=True)
        i2 = jnp.min(jnp.where(p2 == m2, lane, LANES), axis=-1, keepdims=True)
        oh2 = lane == i2
        tot = m1 + m2
        gates = jnp.where(oh1, m1 / tot, 0.0) + jnp.where(oh2, m2 / tot, 0.0)
        gates_ref[...] = gates
        cnt = jnp.sum((gates > 0.0).astype(F32), axis=0, keepdims=True)
        cnt_ref[...] = jnp.broadcast_to(cnt, (SUBLANES, LANES))


def _merge(x, ya, yb, gt, P, layer, tm, moe):
    n = x.shape[0]
    row = lambda w: pl.BlockSpec((tm, w), lambda i: (i, 0))
    wsq = lambda: pl.BlockSpec((None, D_MODEL, D_MODEL), lambda i: (layer, 0, 0))
    in_specs = [row(D_MODEL), row(D_MODEL), row(D_MODEL), row(2 * D_MODEL), wsq(), wsq(), wsq(),
                pl.BlockSpec((None, 1, D_MODEL), lambda i: (layer, 0, 0))]
    args = [x, ya, yb, gt, P['w_out_a'], P['w_out_b'], P['w_out'], P['norm_ffn']]
    out_specs = [row(D_MODEL), row(D_MODEL)]
    out_shape = [jax.ShapeDtypeStruct((n, D_MODEL), F32), jax.ShapeDtypeStruct((n, D_MODEL), BF16)]
    if moe:
        in_specs.append(pl.BlockSpec((None, D_MODEL, LANES), lambda i: (layer // 2, 0, 0)))
        args.append(P['w_router_pad'])
        out_specs.append(row(LANES))
        out_shape.append(jax.ShapeDtypeStruct((n, LANES), F32))
        out_specs.append(pl.BlockSpec((None, SUBLANES, LANES), lambda i: (i, 0, 0)))
        out_shape.append(jax.ShapeDtypeStruct((n // tm, SUBLANES, LANES), F32))
    return pl.pallas_call(
        functools.partial(_merge_kernel, moe=moe),
        grid=(n // tm,),
        in_specs=in_specs,
        out_specs=out_specs,
        out_shape=out_shape,
        compiler_params=_cparams(("arbitrary",)),
        name="merge",
    )(*args)


def _swiglu_acc(h, wg_ref, wu_ref, wd_ref, width=D_FF):
    acc = None
    for c0 in range(0, width, FF_CHUNK):
        sl = slice(c0, min(c0 + FF_CHUNK, width))
        gate = _dot(h, wg_ref[:, sl])
        up = _dot(h, wu_ref[:, sl])
        act = (gate * _sigmoid(gate) * up).astype(BF16)
        part = _dot(act, wd_ref[sl, :])
        acc = part if acc is None else acc + part
    return acc


def _ffn_kernel(x_ref, h_ref, wg_ref, wu_ref, wd_ref, o_ref):
    o_ref[...] = x_ref[...] + _swiglu_acc(h_ref[...], wg_ref, wu_ref, wd_ref)


def _ffn(x, h, P, j, tm):
    n = x.shape[0]
    row = lambda: pl.BlockSpec((tm, D_MODEL), lambda i: (i, 0))
    return pl.pallas_call(
        _ffn_kernel,
        grid=(n // tm,),
        in_specs=[row(), row(),
                  pl.BlockSpec((None, D_MODEL, D_FF), lambda i: (j, 0, 0), pipeline_mode=pl.Buffered(1)),
                  pl.BlockSpec((None, D_MODEL, D_FF), lambda i: (j, 0, 0), pipeline_mode=pl.Buffered(1)),
                  pl.BlockSpec((None, D_FF, D_MODEL), lambda i: (j, 0, 0), pipeline_mode=pl.Buffered(1))],
        out_specs=row(),
        out_shape=jax.ShapeDtypeStruct((n, D_MODEL), F32),
        compiler_params=_cparams(("arbitrary",)),
        name="ffn",
    )(x, h, P['w_ffn_gate'], P['w_ffn_up'], P['w_ffn_down'])


def _moe_kernel(cnt_ref, x_ref, h_ref, gates_ref, wg_ref, wu_ref, wd_ref, o_ref, rank_t_s, rank_n_s, xg_s, y_s,
                *, tw):
    i = pl.program_id(0)
    e = pl.program_id(1)
    f = pl.program_id(2)
    sb = min(MOE_SUB, tw)

    @pl.when((e == 0) & (f == 0))
    def _():
        o_ref[...] = x_ref[...]
        mask = (gates_ref[...] > 0.0).astype(BF16)
        r_idx = lax.broadcasted_iota(jnp.int32, (tw, tw), 0)
        c_idx = lax.broadcasted_iota(jnp.int32, (tw, tw), 1)
        before = (r_idx < c_idx).astype(BF16)
        upto = (r_idx <= c_idx).astype(BF16)
        rank_t_s[:, 0:tw] = _dot_ta(mask, before)
        rank_t_s[:, tw:2 * tw] = _dot_ta(mask, upto)
        rank_n_s[:, 0:LANES] = _dot_ta(before, mask)
        rank_n_s[:, LANES:2 * LANES] = _dot_ta(upto, mask)

    cnt = cnt_ref[i * N_EXPERTS + e]
    nblk = (cnt + sb - 1) // sb
    lane = lax.broadcasted_iota(jnp.int32, (tw, LANES), 1)
    pick = lane == e
    col = lambda a: jnp.sum(jnp.where(pick, a, 0.0), axis=-1, keepdims=True)
    gate_col = col(gates_ref[...])
    excl_col = col(rank_n_s[:, 0:LANES])
    incl_col = col(rank_n_s[:, LANES:2 * LANES])
    excl_row = rank_t_s[pl.ds(e, 1), 0:tw]
    incl_row = rank_t_s[pl.ds(e, 1), tw:2 * tw]

    def block(k, carry):
        r0 = pl.multiple_of(k * sb, sb)
        rr_col = (lax.broadcasted_iota(jnp.int32, (sb, 1), 0) + k * sb).astype(F32)
        rr_row = (lax.broadcasted_iota(jnp.int32, (1, sb), 1) + k * sb).astype(F32)

        @pl.when(f == 0)
        def _():
            gather = ((excl_row == rr_col) & (incl_row == rr_col + 1.0)).astype(BF16)
            xg_s[pl.ds(r0, sb), :] = _dot(gather, h_ref[...]).astype(BF16)
            y_s[pl.ds(r0, sb), :] = _swiglu_acc(xg_s[pl.ds(r0, sb), :], wg_ref, wu_ref, wd_ref, D_FF // MOE_FF_SPLIT)

        @pl.when(f == MOE_FF_SPLIT - 1)
        def _():
            y = y_s[pl.ds(r0, sb), :] + _swiglu_acc(xg_s[pl.ds(r0, sb), :], wg_ref, wu_ref, wd_ref,
                                                    D_FF // MOE_FF_SPLIT)
            hi = y.astype(BF16)
            lo = (y - hi.astype(F32)).astype(BF16)
            scatter = ((excl_col == rr_row) & (incl_col == rr_row + 1.0)).astype(BF16)
            o_ref[...] += gate_col * (_dot(scatter, hi) + _dot(scatter, lo))

        return carry

    lax.fori_loop(0, nblk, block, 0)


def _moe(x, h, gates, counts, P, j, tw):
    n = x.shape[0]
    assert MOE_FF_SPLIT == 2
    ffw = D_FF // MOE_FF_SPLIT
    row = lambda w: pl.BlockSpec((tw, w), lambda i, e, f, c: (i, 0), pipeline_mode=pl.Buffered(1))
    grid_spec = pltpu.PrefetchScalarGridSpec(
        num_scalar_prefetch=1,
        grid=(n // tw, N_EXPERTS, MOE_FF_SPLIT),
        in_specs=[row(D_MODEL), row(D_MODEL), row(LANES),
                  pl.BlockSpec((None, None, D_MODEL, ffw), lambda i, e, f, c: (j, e, 0, f)),
                  pl.BlockSpec((None, None, D_MODEL, ffw), lambda i, e, f, c: (j, e, 0, f)),
                  pl.BlockSpec((None, None, ffw, D_MODEL), lambda i, e, f, c: (j, e, f, 0))],
        out_specs=pl.BlockSpec((tw, D_MODEL), lambda i, e, f, c: (i, 0)),
        scratch_shapes=[pltpu.VMEM((LANES, 2 * tw), F32), pltpu.VMEM((tw, 2 * LANES), F32),
                        pltpu.VMEM((tw, D_MODEL), BF16), pltpu.VMEM((tw, D_MODEL), F32)],
    )
    return pl.pallas_call(
        functools.partial(_moe_kernel, tw=tw),
        grid_spec=grid_spec,
        out_shape=jax.ShapeDtypeStruct((n, D_MODEL), F32),
        compiler_params=_cparams(("arbitrary", "arbitrary", "arbitrary")),
        name="moe",
    )(counts, x, h, gates, P['w_moe_gate'], P['w_moe_up'], P['w_moe_down'])


def _route_kernel(gates_ref, base_ref, pos_ref, *, tw, spare):
    mask = (gates_ref[...] > 0.0).astype(BF16)
    r_idx = lax.broadcasted_iota(jnp.int32, (tw, tw), 0)
    c_idx = lax.broadcasted_iota(jnp.int32, (tw, tw), 1)
    excl = _dot_ta(mask, (r_idx < c_idx).astype(BF16))
    sel = _dot_ta(mask, (r_idx == c_idx).astype(BF16)) > 0.0
    pos = excl + base_ref[:, 0:1]
    e_io = lax.broadcasted_iota(jnp.int32, (LANES, tw), 0)
    e_first = jnp.min(jnp.where(sel, e_io, LANES), axis=0, keepdims=True)
    e_last = jnp.max(jnp.where(sel, e_io, -1), axis=0, keepdims=True)
    p_first = jnp.sum(jnp.where(e_io == e_first, pos, 0.0), axis=0, keepdims=True)
    p_last = jnp.sum(jnp.where(e_io == e_last, pos, 0.0), axis=0, keepdims=True)
    p_last = jnp.where(e_last == e_first, float(spare), p_last)
    rows = jnp.concatenate([p_first, p_last, jnp.zeros((SUBLANES - 2, tw), F32)], axis=0)
    pos_ref[...] = rows.astype(jnp.int32)


def _route(gates, base_b, tw, spare):
    n = gates.shape[0]
    return pl.pallas_call(
        functools.partial(_route_kernel, tw=tw, spare=spare),
        grid=(n // tw,),
        in_specs=[pl.BlockSpec((tw, LANES), lambda i: (i, 0)),
                  pl.BlockSpec((None, LANES, LANES), lambda i: (i, 0, 0))],
        out_specs=pl.BlockSpec((None, SUBLANES, tw), lambda i: (i, 0, 0)),
        out_shape=jax.ShapeDtypeStruct((n // tw, SUBLANES, tw), jnp.int32),
        compiler_params=_cparams(("arbitrary",)),
        name="route",
    )(gates, base_b)


def _row_copy(src_hbm, src_row, dst_ref, dst_row, sem):
    return pltpu.make_async_copy(src_hbm.at[pl.ds(src_row, 1), :], dst_ref.at[pl.ds(dst_row, 1), :], sem)


def _disperse_kernel(pos_ref, x_hbm, init_hbm, xs_hbm, sem, *, tw):
    del init_hbm
    i = pl.program_id(0)

    def issue(n, carry):
        _row_copy(x_hbm, i * tw + n, xs_hbm, pos_ref[0, n], sem).start()
        _row_copy(x_hbm, i * tw + n, xs_hbm, pos_ref[1, n], sem).start()
        return carry

    def drain(n, carry):
        _row_copy(x_hbm, 0, xs_hbm, 0, sem).wait()
        _row_copy(x_hbm, 0, xs_hbm, 0, sem).wait()
        return carry

    lax.fori_loop(0, tw, issue, 0)
    lax.fori_loop(0, tw, drain, 0)


def _disperse(pos, x, n_rows, tw):
    n = x.shape[0]
    return pl.pallas_call(
        functools.partial(_disperse_kernel, tw=tw),
        grid=(n // tw,),
        in_specs=[pl.BlockSpec((None, SUBLANES, tw), lambda i: (i, 0, 0), memory_space=pltpu.SMEM),
                  pl.BlockSpec(memory_space=pl.ANY),
                  pl.BlockSpec(memory_space=pl.ANY)],
        out_specs=pl.BlockSpec(memory_space=pl.ANY),
        out_shape=jax.ShapeDtypeStruct((n_rows, D_MODEL), F32),
        scratch_shapes=[pltpu.SemaphoreType.DMA],
        input_output_aliases={2: 0},
        compiler_params=_cparams(("arbitrary",)),
        name="disperse",
    )(pos, x, jnp.zeros((n_rows, D_MODEL), F32))


def _experts_kernel(te_ref, used_ref, xs_ref, nf_ref, wg_ref, wu_ref, wd_ref, ys_ref):
    del te_ref
    j = pl.program_id(0)

    @pl.when(j < used_ref[0])
    def _():
        h = _rms(xs_ref[...], nf_ref[...]).astype(BF16)
        ys_ref[...] = _swiglu_acc(h, wg_ref, wu_ref, wd_ref)

    @pl.when(j >= used_ref[0])
    def _():
        ys_ref[...] = jnp.zeros(ys_ref.shape, F32)


def _experts(tile_expert, used, xs, P, layer, j):
    n_tiles = tile_expert.shape[0]
    tr = ROUTE_ROWS
    wspec = lambda a, b: pl.BlockSpec((None, None, a, b), lambda t, te, u: (j, te[t], 0, 0))
    grid_spec = pltpu.PrefetchScalarGridSpec(
        num_scalar_prefetch=2,
        grid=(n_tiles,),
        in_specs=[pl.BlockSpec((tr, D_MODEL), lambda t, te, u: (t, 0)),
                  pl.BlockSpec((None, 1, D_MODEL), lambda t, te, u: (layer, 0, 0)),
                  wspec(D_MODEL, D_FF), wspec(D_MODEL, D_FF), wspec(D_FF, D_MODEL)],
        out_specs=pl.BlockSpec((tr, D_MODEL), lambda t, te, u: (t, 0)),
    )
    return pl.pallas_call(
        _experts_kernel,
        grid_spec=grid_spec,
        out_shape=jax.ShapeDtypeStruct((n_tiles * tr, D_MODEL), F32),
        compiler_params=_cparams(("arbitrary",)),
        name="experts",
    )(tile_expert, used, xs, P['norm_ffn'], P['w_moe_gate'], P['w_moe_up'], P['w_moe_down'])


def _combine_kernel(pos_ref, x_ref, gates_ref, ys_hbm, o_ref, y1_s, y2_s, sem, *, tw):
    def issue(n, carry):
        _row_copy(ys_hbm, pos_ref[0, n], y1_s, n, sem).start()
        _row_copy(ys_hbm, pos_ref[1, n], y2_s, n, sem).start()
        return carry

    def drain(n, carry):
        _row_copy(ys_hbm, 0, y1_s, 0, sem).wait()
        _row_copy(ys_hbm, 0, y2_s, 0, sem).wait()
        return carry

    lax.fori_loop(0, tw, issue, 0)
    gates = gates_ref[...]
    lane = lax.broadcasted_iota(jnp.int32, gates.shape, 1)
    sel = gates > 0.0
    e_first = jnp.min(jnp.where(sel, lane, LANES), axis=-1, keepdims=True)
    e_last = jnp.max(jnp.where(sel, lane, -1), axis=-1, keepdims=True)
    g_first = jnp.sum(jnp.where(lane == e_first, gates, 0.0), axis=-1, keepdims=True)
    g_last = jnp.sum(jnp.where((lane == e_last) & (e_last != e_first), gates, 0.0), axis=-1, keepdims=True)
    lax.fori_loop(0, tw, drain, 0)
    o_ref[...] = x_ref[...] + g_first * y1_s[...] + g_last * y2_s[...]


def _combine(pos, x, gates, ys, tw):
    n = x.shape[0]
    return pl.pallas_call(
        functools.partial(_combine_kernel, tw=tw),
        grid=(n // tw,),
        in_specs=[pl.BlockSpec((None, SUBLANES, tw), lambda i: (i, 0, 0), memory_space=pltpu.SMEM),
                  pl.BlockSpec((tw, D_MODEL), lambda i: (i, 0)),
                  pl.BlockSpec((tw, LANES), lambda i: (i, 0)),
                  pl.BlockSpec(memory_space=pl.ANY)],
        out_specs=pl.BlockSpec((tw, D_MODEL), lambda i: (i, 0)),
        out_shape=jax.ShapeDtypeStruct((n, D_MODEL), F32),
        scratch_shapes=[pltpu.VMEM((tw, D_MODEL), F32), pltpu.VMEM((tw, D_MODEL), F32), pltpu.SemaphoreType.DMA],
        compiler_params=_cparams(("arbitrary",)),
        name="combine",
    )(pos, x, gates, ys)


def _moe_routed(x, gates, cnt, P, layer, tm):
    n = x.shape[0]
    tw = min(ROUTE_TILE, n)
    tr = ROUTE_ROWS
    n_tiles = (TOP_K * n) // tr + N_EXPERTS
    spare = n_tiles * tr
    c = cnt[:, 0, :].astype(jnp.int32).reshape(n // tw, tw // tm, LANES).sum(axis=1)
    seg_tiles = (c.sum(axis=0) + tr - 1) // tr
    seg_end = jnp.cumsum(seg_tiles)
    base = (seg_end - seg_tiles) * tr + jnp.cumsum(c, axis=0) - c
    base_b = jnp.broadcast_to(base.astype(F32)[:, :, None], (n // tw, LANES, LANES))
    tile_expert = jnp.minimum(jnp.searchsorted(seg_end[:N_EXPERTS], jnp.arange(n_tiles + 1), side='right'),
                              N_EXPERTS - 1).astype(jnp.int32)
    used = seg_end[N_EXPERTS - 1:N_EXPERTS].astype(jnp.int32)
    pos = _route(gates, base_b, tw, spare)
    xs = _disperse(pos, x, spare + tr, tw)
    ys = _experts(tile_expert, used, xs, P, layer, layer // 2)
    return _combine(pos, x, gates, ys, tw)


def _final_norm_kernel(x_ref, g_ref, o_ref):
    o_ref[...] = _rms(x_ref[...], g_ref[...])


def _final_norm(x, g, tm):
    n = x.shape[0]
    return pl.pallas_call(
        _final_norm_kernel,
        grid=(n // tm,),
        in_specs=[pl.BlockSpec((tm, D_MODEL), lambda i: (i, 0)), pl.BlockSpec((1, D_MODEL), lambda i: (0, 0))],
        out_specs=pl.BlockSpec((tm, D_MODEL), lambda i: (i, 0)),
        out_shape=jax.ShapeDtypeStruct((n, D_MODEL), F32),
        compiler_params=_cparams(("arbitrary",)),
        name="final_norm",
    )(x, g)


def _block_diag_weights(w):
    d = w.shape[0]
    per = MXU_DIM // LRU_BLOCK
    w = w.reshape(d, LRU_HEADS // per, per, LRU_BLOCK, LRU_BLOCK)
    eye = jnp.eye(per, dtype=w.dtype)
    bd = jnp.einsum('dcpij,pq->dcpiqj', w, eye)
    return bd.reshape(d, LRU_HEADS // per, MXU_DIM, MXU_DIM).astype(BF16)


def _prep_params(p):
    P = dict(p)
    for name in ('w_in', 'w_out_a', 'w_out_b', 'w_out', 'w_ffn_gate', 'w_ffn_up', 'w_ffn_down',
                 'w_moe_gate', 'w_moe_up', 'w_moe_down', 'g_lora_up'):
        P[name] = p[name].astype(BF16)
    for name in ('norm_mix', 'conv_b', 'gate_a_b', 'gate_x_b', 'lru_lambda', 'shift_mu', 'w0', 'a0', 'k_k', 'k_a',
                 'ln_x_w', 'ln_x_b', 'norm_ffn'):
        P[name] = p[name][:, None, :]
    P['r_k'] = p['r_k'].reshape(DEPTH, 1, RWKV_WIDTH)
    P['norm_final'] = p['norm_final'][None, :]
    P['gaw_bd'] = _block_diag_weights(p['gate_a_w'])
    P['gxw_bd'] = _block_diag_weights(p['gate_x_w'])
    z = jnp.zeros((DEPTH, LORA_W, RWKV_WIDTH), F32)
    P['lora_wa'] = jnp.concatenate([jnp.concatenate([p['w_lora_up'], z], axis=2),
                                    jnp.concatenate([z, p['a_lora_up']], axis=2)], axis=1).astype(BF16)
    P['w_router_pad'] = jnp.pad(p['w_router'], ((0, 0), (0, 0), (0, LANES - N_EXPERTS)))
    return P


def _trunk(x3, st_rwkv, st_lru, st_conv, st_shift, P, *, zero_state, tm, lru_bb, lru_tt, rwkv_nb):
    B, T, _ = x3.shape
    n = B * T
    x = x3.reshape(n, D_MODEL)
    n_rwkv, n_lru, n_conv, n_shift = [], [], [], []
    for l in range(DEPTH):
        xy, ps, gt = _proj(x, P['norm_mix'], P['w_in'], l, min(tm, 256))
        ya, h_new, c_new = _lru(xy.reshape(B, T, 2 * LRU_WIDTH), st_conv[l], st_lru[l][:, None, :], P, l,
                                lru_bb, lru_tt)
        yb, sh_new, s_new = _wkv(ps.reshape(B, T, SHIFT_WIDTH), st_shift[l], st_rwkv[l], P, l, rwkv_nb, zero_state)
        moe = l % 2 == 1
        res = _merge(x, ya.reshape(n, D_MODEL), yb.reshape(n, D_MODEL), gt, P, l, tm, moe)
        if moe:
            x, h, gates, cnt = res
            x = _moe_routed(x, gates, cnt, P, l, tm)
        else:
            x, h = res
            x = _ffn(x, h, P, l // 2, tm)
        n_rwkv.append(s_new)
        n_lru.append(h_new[:, 0, :])
        n_conv.append(c_new)
        n_shift.append(sh_new)
    y = _final_norm(x, P['norm_final'], tm).reshape(B, T, D_MODEL)
    return y, jnp.stack(n_rwkv), jnp.stack(n_lru), jnp.stack(n_conv), jnp.stack(n_shift)


def kernel(x_prompt, x_sample, state_rwkv, state_lru, state_conv, state_shift, norm_mix, w_in, conv_w, conv_b, gate_a_w, gate_a_b, gate_x_w, gate_x_b, lru_lambda, shift_mu, w0, w_lora_up, a0, a_lora_up, g_lora_up, k_k, k_a, r_k, ln_x_w, ln_x_b, w_out_a, w_out_b, w_out, norm_ffn, w_ffn_gate, w_ffn_up, w_ffn_down, w_router, w_moe_gate, w_moe_up, w_moe_down, norm_final):
    P = _prep_params(dict(
        norm_mix=norm_mix, w_in=w_in, conv_w=conv_w, conv_b=conv_b, gate_a_w=gate_a_w, gate_a_b=gate_a_b,
        gate_x_w=gate_x_w, gate_x_b=gate_x_b, lru_lambda=lru_lambda, shift_mu=shift_mu, w0=w0,
        w_lora_up=w_lora_up, a0=a0, a_lora_up=a_lora_up, g_lora_up=g_lora_up, k_k=k_k, k_a=k_a, r_k=r_k,
        ln_x_w=ln_x_w, ln_x_b=ln_x_b, w_out_a=w_out_a, w_out_b=w_out_b, w_out=w_out, norm_ffn=norm_ffn,
        w_ffn_gate=w_ffn_gate, w_ffn_up=w_ffn_up, w_ffn_down=w_ffn_down, w_router=w_router,
        w_moe_gate=w_moe_gate, w_moe_up=w_moe_up, w_moe_down=w_moe_down, norm_final=norm_final))
    bp = x_prompt.shape[0]
    bs = x_sample.shape[0]
    zeros = lambda *shape: jnp.zeros(shape, F32)
    p_out = _trunk(x_prompt,
                   zeros(DEPTH, bp, RWKV_HEADS, RWKV_HEAD, RWKV_HEAD), zeros(DEPTH, bp, LRU_WIDTH),
                   zeros(DEPTH, bp, CONV_WIDTH - 1, LRU_WIDTH), zeros(DEPTH, bp, SHIFT_WIDTH), P,
                   zero_state=True, tm=512, lru_bb=1, lru_tt=256, rwkv_nb=2)
    s_out = _trunk(x_sample, state_rwkv, state_lru, state_conv, state_shift, P,
                   zero_state=False, tm=256, lru_bb=8, lru_tt=x_sample.shape[1], rwkv_nb=1)
    return (p_out[0], s_out[0]) + tuple(p_out[1:]) + tuple(s_out[1:])
```

```python
import functools

import numpy as np
import jax
import jax.numpy as jnp
from jax import lax
from jax.experimental import pallas as pl
from jax.experimental.pallas import tpu as pltpu

F32 = jnp.float32
BF16 = jnp.bfloat16

D_MODEL = 1024
DEPTH = 4
LRU_WIDTH = D_MODEL
LRU_HEADS = 16
LRU_BLOCK = LRU_WIDTH // LRU_HEADS
CONV_WIDTH = 4
LRU_C = 8.0
RWKV_HEAD = 64
RWKV_WIDTH = D_MODEL
RWKV_HEADS = RWKV_WIDTH // RWKV_HEAD
LORA_W = 64
LORA_A = 64
LORA_G = 128
SHIFT_WIDTH = 3 * RWKV_WIDTH + LORA_W + LORA_A + LORA_G
PROJ_WIDTH = 2 * LRU_WIDTH + SHIFT_WIDTH + 2 * D_MODEL
D_FF = 2816
N_EXPERTS = 8
NORM_EPS = 1e-6
GN_EPS = 64e-5

LANES = 128
SUBLANES = 8
MXU_DIM = 256
VMEM_LIMIT = 56 * 1024 * 1024

CHUNK = 64
GROUP_HEADS = MXU_DIM // CHUNK
GROUP_LANES = GROUP_HEADS * RWKV_HEAD
N_GROUPS = RWKV_HEADS // GROUP_HEADS
FF_CHUNK = 256
TOP_K = 2
ROUTE_TILE = 512
ROUTE_ROWS = 256
MOE_TILE = 1024
MOE_SUB = 128
MOE_FF_SPLIT = 2
SCAN_ROWS = 16
assert CHUNK == RWKV_HEAD


def _cparams(sem):
    return pltpu.CompilerParams(dimension_semantics=sem, vmem_limit_bytes=VMEM_LIMIT)


def _dot(a, b):
    return jnp.dot(a, b, preferred_element_type=F32)


def _dot_tb(a, b):
    return lax.dot_general(a, b, (((1,), (1,)), ((), ())), preferred_element_type=F32)


def _dot_ta(a, b):
    return lax.dot_general(a, b, (((0,), (0,)), ((), ())), preferred_element_type=F32)


def _dot_exact(a, b_bf16):
    hi = a.astype(BF16)
    lo = (a - hi.astype(F32)).astype(BF16)
    return _dot(hi, b_bf16) + _dot(lo, b_bf16)


def _sigmoid(x):
    return jax.nn.sigmoid(x)


def _softplus(z):
    return jnp.maximum(z, 0.0) + jnp.log(1.0 + jnp.exp(-jnp.abs(z)))


def _rms(x, g):
    return x * lax.rsqrt(jnp.mean(x * x, axis=-1, keepdims=True) + NORM_EPS) * g


_PROJ_SPLITS = ((0, 2 * LRU_WIDTH), (2 * LRU_WIDTH, SHIFT_WIDTH), (2 * LRU_WIDTH + SHIFT_WIDTH, 2 * D_MODEL))


def _proj_kernel(x_ref, g_ref, w_ref, xy_ref, ps_ref, gt_ref):
    u = _rms(x_ref[...], g_ref[...]).astype(BF16)
    for out_ref, (c0, width) in zip((xy_ref, ps_ref, gt_ref), _PROJ_SPLITS):
        j = 0
        while j < width:
            w = min(512, width - j)
            out_ref[:, j:j + w] = _dot(u, w_ref[:, c0 + j:c0 + j + w])
            j += w


def _proj(x, g_all, w_all, layer, tm):
    n = x.shape[0]
    return pl.pallas_call(
        _proj_kernel,
        grid=(n // tm,),
        in_specs=[
            pl.BlockSpec((tm, D_MODEL), lambda i: (i, 0)),
            pl.BlockSpec((None, 1, D_MODEL), lambda i: (layer, 0, 0)),
            pl.BlockSpec((None, D_MODEL, PROJ_WIDTH), lambda i: (layer, 0, 0),
                         pipeline_mode=pl.Buffered(1)),
        ],
        out_specs=[
            pl.BlockSpec((tm, 2 * LRU_WIDTH), lambda i: (i, 0)),
            pl.BlockSpec((tm, SHIFT_WIDTH), lambda i: (i, 0)),
            pl.BlockSpec((tm, 2 * D_MODEL), lambda i: (i, 0)),
        ],
        out_shape=[
            jax.ShapeDtypeStruct((n, 2 * LRU_WIDTH), F32),
            jax.ShapeDtypeStruct((n, SHIFT_WIDTH), F32),
            jax.ShapeDtypeStruct((n, 2 * D_MODEL), F32),
        ],
        compiler_params=_cparams(("arbitrary",)),
        name="proj",
    )(x, g_all, w_all)


def _gelu_tanh(x):
    return 0.5 * x * (1.0 + jnp.tanh(np.sqrt(2.0 / np.pi).astype(np.float32) * (x + 0.044715 * (x * x * x))))


def _lru_kernel(xa_ref, ya_ref, cs_ref, h0_ref, cw_ref, cb_ref, gaw_ref, gab_ref, gxw_ref, gxb_ref, lam_ref,
                y_ref, hl_ref, nc_ref, xp_s, h_s, a_s, b_s, *, bb, tt):
    ti = pl.program_id(1)
    neg_c = -LRU_C * _softplus(-lam_ref[...])
    cw = cw_ref[...]
    @pl.when(ti == 0)
    def _():
        for b in range(bb):
            xp_s[b, 5:8, :] = cs_ref[b]
            h_s[b] = h0_ref[b]

    for b in range(bb):
        xa = xa_ref[b]
        xp_s[b, 8:8 + tt, :] = xa
        xc = cb_ref[...] + xp_s[b, 5:5 + tt, :] * cw[0:1]
        xc = xc + xp_s[b, 6:6 + tt, :] * cw[1:2]
        xc = xc + xp_s[b, 7:7 + tt, :] * cw[2:3]
        xc = xc + xa * cw[3:4]
        tail = xp_s[b, 5 + tt:8 + tt, :]
        xp_s[b, 5:8, :] = tail
        nc_ref[b] = tail

        for c in range(LRU_WIDTH // MXU_DIM):
            sl = slice(c * MXU_DIM, (c + 1) * MXU_DIM)
            xcc = xc[:, sl]
            xcb = xcc.astype(BF16)
            r = _sigmoid(_dot(xcb, gaw_ref[c]) + gab_ref[:, sl])
            i = _sigmoid(_dot(xcb, gxw_ref[c]) + gxb_ref[:, sl])
            log_a = neg_c[:, sl] * r
            a = jnp.exp(log_a)
            a_s[:, sl] = a
            b_s[:, sl] = jnp.sqrt(-jnp.tanh(log_a) * (a * a + 1.0)) * (i * xcc)

        if tt % SCAN_ROWS == 0:
            row = lax.broadcasted_iota(jnp.int32, (SUBLANES, LRU_WIDTH), 0)

            def blk(j, h):
                r0 = pl.multiple_of(j * SCAN_ROWS, SCAN_ROWS)
                hbs = []
                for q in range(SCAN_ROWS // SUBLANES):
                    av = a_s[pl.ds(r0 + q * SUBLANES, SUBLANES), :]
                    bv = b_s[pl.ds(r0 + q * SUBLANES, SUBLANES), :]
                    for d in (1, 2, 4):
                        m = row >= d
                        a_sh = pltpu.roll(av, d, axis=0)
                        b_sh = pltpu.roll(bv, d, axis=0)
                        bv = jnp.where(m, av * b_sh + bv, bv)
                        av = jnp.where(m, av * a_sh, av)
                    hb = av * h + bv
                    h = hb[SUBLANES - 1:SUBLANES, :]
                    hbs.append(hb)
                hs = jnp.concatenate(hbs, axis=0)
                y_ref[b, pl.ds(r0, SCAN_ROWS), :] = (hs * _gelu_tanh(ya_ref[b, pl.ds(r0, SCAN_ROWS), :])).astype(BF16)
                return h

            h = lax.fori_loop(0, tt // SCAN_ROWS, blk, h_s[b])
        else:
            h = h_s[b]
            rows = []
            for t in range(tt):
                h = a_s[t:t + 1, :] * h + b_s[t:t + 1, :]
                rows.append(h)
            hs = jnp.concatenate(rows, axis=0)
            y_ref[b] = (hs * _gelu_tanh(ya_ref[b])).astype(BF16)
        h_s[b] = h
        hl_ref[b] = h


def _lru(xy3, st_conv, st_lru, P, layer, bb, tt):
    B, T, _ = xy3.shape
    W = LRU_WIDTH
    vec = lambda: pl.BlockSpec((None, 1, W), lambda b, t: (layer, 0, 0))
    kern = functools.partial(_lru_kernel, bb=bb, tt=tt)
    return pl.pallas_call(
        kern,
        grid=(B // bb, T // tt),
        in_specs=[
            pl.BlockSpec((bb, tt, W), lambda b, t: (b, t, 0)),
            pl.BlockSpec((bb, tt, W), lambda b, t: (b, t, 1)),
            pl.BlockSpec((bb, CONV_WIDTH - 1, W), lambda b, t: (b, 0, 0)),
            pl.BlockSpec((bb, 1, W), lambda b, t: (b, 0, 0)),
            pl.BlockSpec((None, CONV_WIDTH, W), lambda b, t: (layer, 0, 0)),
            vec(),
            pl.BlockSpec((None, W // MXU_DIM, MXU_DIM, MXU_DIM), lambda b, t: (layer, 0, 0, 0)),
            vec(),
            pl.BlockSpec((None, W // MXU_DIM, MXU_DIM, MXU_DIM), lambda b, t: (layer, 0, 0, 0)),
            vec(),
            vec(),
        ],
        out_specs=[
            pl.BlockSpec((bb, tt, W), lambda b, t: (b, t, 0)),
            pl.BlockSpec((bb, 1, W), lambda b, t: (b, 0, 0)),
            pl.BlockSpec((bb, CONV_WIDTH - 1, W), lambda b, t: (b, 0, 0)),
        ],
        out_shape=[
            jax.ShapeDtypeStruct((B, T, W), BF16),
            jax.ShapeDtypeStruct((B, 1, W), F32),
            jax.ShapeDtypeStruct((B, CONV_WIDTH - 1, W), F32),
        ],
        scratch_shapes=[
            pltpu.VMEM((bb, tt + 8, W), F32),
            pltpu.VMEM((bb, 1, W), F32),
            pltpu.VMEM((tt, W), F32),
            pltpu.VMEM((tt, W), F32),
        ],
        compiler_params=_cparams(("arbitrary", "arbitrary")),
        name="lru",
    )(xy3, xy3, st_conv, st_lru, P['conv_w'], P['conv_b'], P['gaw_bd'], P['gate_a_b'], P['gxw_bd'],
      P['gate_x_b'], P['lru_lambda'])


def _block_diag(x_bf16, mask_ref):
    return jnp.concatenate([x_bf16] * GROUP_HEADS, axis=0) * mask_ref[...]


def _rwkv_kernel(ps_ref, sh_ref, s0_ref, mu_ref, w0_ref, lora_ref, gup_ref, a0_ref, kk_ref, ka_ref, rk_ref,
                 lnw_ref, lnb_ref, tri_ref, ones_ref, mrow_ref, msq_ref,
                 y_ref, nsh_ref, ns_ref, pad_s, car_s, st_s, *, nb, t_valid, zero_state):
    C = CHUNK
    ci = pl.program_id(1)
    last = ci == pl.num_programs(1) - 1
    rows_c = lax.broadcasted_iota(jnp.int32, (C, 1), 0)
    t_idx = lax.broadcasted_iota(jnp.int32, (C, MXU_DIM), 0)
    i_idx = lax.broadcasted_iota(jnp.int32, (C, MXU_DIM), 1) % C
    m_strict = i_idx < t_idx
    m_incl = i_idx <= t_idx
    eye_cat = (i_idx == t_idx).astype(F32)
    ones_bd = ones_ref[...]

    def head_sum(x):
        parts = []
        for g in range(RWKV_WIDTH // MXU_DIM):
            parts.append(_dot_exact(x[:, g * MXU_DIM:(g + 1) * MXU_DIM], ones_bd))
        return jnp.concatenate(parts, axis=1)

    @pl.when(ci == 0)
    def _():
        for b in range(nb):
            car_s[b] = sh_ref[b]
            if zero_state:
                st_s[b] = jnp.zeros(st_s.shape[1:], F32)
            else:
                for g in range(N_GROUPS):
                    rows = [s0_ref[b, g * GROUP_HEADS + hh] for hh in range(GROUP_HEADS)]
                    stacked = jnp.concatenate(rows, axis=0)
                    tiled = jnp.concatenate([stacked] * GROUP_HEADS, axis=1)
                    st_s[b, g] = tiled * msq_ref[...].astype(F32)

    preps, chains = [], []
    for b in range(nb):
        if t_valid == C:
            ps = ps_ref[b]
        else:
            pad_s[...] = jnp.zeros(pad_s.shape, F32)
            pad_s[0:t_valid, :] = ps_ref[b]
            ps = pad_s[...]
        prev = pltpu.roll(ps, 1, axis=0)
        prev = jnp.where(rows_c == 0, car_s[b], prev)
        new_carry = ps[t_valid - 1:t_valid, :]
        car_s[b] = new_carry
        nsh_ref[b] = new_carry

        s = ps + (prev - ps) * mu_ref[...]
        o = RWKV_WIDTH
        r = s[:, :o]
        k = s[:, o:2 * o]
        v = s[:, 2 * o:3 * o]
        dwa = s[:, 3 * o:3 * o + LORA_W + LORA_A]
        dg = s[:, 3 * o + LORA_W + LORA_A:]
        lane = lax.broadcasted_iota(jnp.int32, dwa.shape, 1)
        lora_in = jnp.where(lane < LORA_W, jnp.tanh(dwa), dwa).astype(BF16)
        lora = _dot(lora_in, lora_ref[...])
        w_log = -_softplus(-(w0_ref[...] + lora[:, :o])) - 0.5
        lw = -jnp.exp(w_log)
        a = _sigmoid(a0_ref[...] + lora[:, o:])
        g = _dot(_sigmoid(dg).astype(BF16), gup_ref[...])
        kk = k * kk_ref[...]
        kk = kk / jnp.maximum(jnp.sqrt(head_sum(kk * kk)), 1e-12)
        kmod = k * (1.0 + (a - 1.0) * ka_ref[...])
        if t_valid != C:
            valid = rows_c < t_valid
            lw = jnp.where(valid, lw, 0.0)
            kk = jnp.where(valid, kk, 0.0)
            kmod = jnp.where(valid, kmod, 0.0)
            v = jnp.where(valid, v, 0.0)
        beta = kk * a
        L = jnp.concatenate(
            [_dot_tri(tri_ref[...], lw[:, j * MXU_DIM:(j + 1) * MXU_DIM]) for j in range(o // MXU_DIM)], axis=1)
        e_l = jnp.exp(L)
        e_neg = jnp.exp(-L)
        l_end = L[C - 1:C, :]
        e_end = jnp.exp(l_end - L)
        p_end = jnp.exp(l_end)
        at = (-kk) * jnp.exp(L - lw)
        rt = r * e_l
        bt = beta * e_neg
        kt = kmod * e_neg
        bend = beta * e_end
        kend = kmod * e_end

        preps.append(dict(r=r, v=v, g=g, kmod=kmod))
        for gi in range(N_GROUPS):
            sl = slice(gi * GROUP_LANES, (gi + 1) * GROUP_LANES)
            chains.append(dict(
                b=b, gi=gi, vg=v[:, sl], p_end=p_end[:, sl],
                x2=jnp.concatenate([at[:, sl], rt[:, sl]], axis=0).astype(BF16),
                wbd=jnp.concatenate([_block_diag(bt[:, sl].astype(BF16), mrow_ref),
                                     _block_diag(kt[:, sl].astype(BF16), mrow_ref)], axis=0),
                bk=jnp.concatenate([bend[:, sl], kend[:, sl]], axis=0).astype(BF16)))

    cw = GROUP_HEADS * C
    for c in chains:
        res = _dot_tb(c['x2'], c['wbd'])
        n_cat = jnp.where(m_strict, res[:C, :cw], 0.0)
        c['a_ak'] = jnp.where(m_strict, res[:C, cw:], 0.0).astype(BF16)
        c['a_r'] = jnp.concatenate([jnp.where(m_incl, res[C:, :cw], 0.0),
                                    jnp.where(m_incl, res[C:, cw:], 0.0)], axis=1).astype(BF16)
        c['x_c'] = n_cat.astype(BF16)
        c['p_c'] = eye_cat + n_cat
        c['x_bd'] = _block_diag(c['x_c'], msq_ref)
    for c in chains:
        st = st_s[c['b'], c['gi']]
        c['xs'] = _dot_tb(c['x2'], st.astype(BF16))
        c['v_bd'] = _block_diag(c['vg'].astype(BF16), msq_ref)
    for c in chains:
        c['rhs'] = c['xs'][:C] + _dot(c['a_ak'], c['v_bd'])
    lvl = 2
    while lvl < C:
        for c in chains:
            c['x_c'] = _dot(c['x_c'], c['x_bd']).astype(BF16)
            c['x_bd'] = _block_diag(c['x_c'], msq_ref)
        for c in chains:
            c['p_c'] = c['p_c'] + _dot(c['p_c'].astype(BF16), c['x_bd'])
        lvl *= 2
    for c in chains:
        c['u'] = _dot(c['p_c'].astype(BF16), _block_diag(c['rhs'].astype(BF16), msq_ref))
    ys = {}
    for c in chains:
        ub = c['u'].astype(BF16)
        ys[(c['b'], c['gi'])] = c['xs'][C:] + _dot(c['a_r'], jnp.concatenate([_block_diag(ub, msq_ref), c['v_bd']],
                                                                             axis=0))
        uv = jnp.concatenate([ub, c['vg'].astype(BF16)], axis=0)
        st = st_s[c['b'], c['gi']]
        st_s[c['b'], c['gi']] = st * c['p_end'] + _dot_ta(uv, c['bk']) * msq_ref[...].astype(F32)

    for b in range(nb):
        r, v, g, kmod = (preps[b][n] for n in ('r', 'v', 'g', 'kmod'))
        yc = jnp.concatenate([ys[(b, gi)] for gi in range(N_GROUPS)], axis=1)
        inv_n = 1.0 / RWKV_HEAD
        mean = head_sum(yc) * inv_n
        dlt = yc - mean
        var = head_sum(dlt * dlt) * inv_n
        yn = dlt * lax.rsqrt(var + GN_EPS) * lnw_ref[...] + lnb_ref[...]
        bonus = head_sum(r * kmod * rk_ref[...]) * v
        out = ((yn + bonus) * g).astype(BF16)
        if t_valid == C:
            y_ref[b] = out
        else:
            y_ref[b] = out[0:t_valid, :]

    @pl.when(last)
    def _():
        for b in range(nb):
            for gi in range(N_GROUPS):
                for hh in range(GROUP_HEADS):
                    ns_ref[b, gi * GROUP_HEADS + hh] = st_s[b, gi, hh * RWKV_HEAD:(hh + 1) * RWKV_HEAD,
                                                            hh * RWKV_HEAD:(hh + 1) * RWKV_HEAD]


def _dot_tri(tri_bf16, x):
    hi = x.astype(BF16)
    lo = (x - hi.astype(F32)).astype(BF16)
    return _dot(tri_bf16, hi) + _dot(tri_bf16, lo)


def _rwkv_consts():
    C = CHUNK
    gh = GROUP_HEADS
    tri = np.tril(np.ones((C, C), np.float32))
    hl = np.arange(MXU_DIM) // RWKV_HEAD
    ones_bd = (hl[:, None] == hl[None, :]).astype(np.float32)
    rowh = np.arange(gh * C) // C
    colh = np.arange(GROUP_LANES) // RWKV_HEAD
    mrow = (rowh[:, None] == colh[None, :]).astype(np.float32)
    colc = np.arange(gh * C) // C
    msq = (rowh[:, None] == colc[None, :]).astype(np.float32)
    return (jnp.asarray(tri, BF16), jnp.asarray(ones_bd, BF16), jnp.asarray(mrow, BF16), jnp.asarray(msq, BF16))


def _rwkv(ps3, st_shift, st_rwkv, P, layer, nb, zero_state):
    B, T, _ = ps3.shape
    C = CHUNK
    if T % C == 0:
        t_valid, nchunks, tb = C, T // C, C
    else:
        assert T < C
        t_valid, nchunks, tb = T, 1, T
    o = RWKV_WIDTH
    tri, ones_bd, mrow, msq = _rwkv_consts()
    vec = lambda w: pl.BlockSpec((None, 1, w), lambda b, c: (layer, 0, 0))
    const = lambda arr: pl.BlockSpec(arr.shape, lambda b, c: (0,) * arr.ndim)
    kern = functools.partial(_rwkv_kernel, nb=nb, t_valid=t_valid, zero_state=zero_state)
    state_spec = pl.BlockSpec((nb, RWKV_HEADS, RWKV_HEAD, RWKV_HEAD), lambda b, c: (b, 0, 0, 0))
    return pl.pallas_call(
        kern,
        grid=(B // nb, nchunks),
        in_specs=[
            pl.BlockSpec((nb, tb, SHIFT_WIDTH), lambda b, c: (b, c, 0)),
            pl.BlockSpec((nb, 1, SHIFT_WIDTH), lambda b, c: (b, 0, 0)),
            state_spec,
            vec(SHIFT_WIDTH),
            vec(o),
            pl.BlockSpec((None, LORA_W + LORA_A, 2 * o), lambda b, c: (layer, 0, 0)),
            pl.BlockSpec((None, LORA_G, o), lambda b, c: (layer, 0, 0)),
            vec(o), vec(o), vec(o), vec(o), vec(o), vec(o),
            const(tri), const(ones_bd), const(mrow), const(msq),
        ],
        out_specs=[
            pl.BlockSpec((nb, tb, o), lambda b, c: (b, c, 0)),
            pl.BlockSpec((nb, 1, SHIFT_WIDTH), lambda b, c: (b, 0, 0)),
            state_spec,
        ],
        out_shape=[
            jax.ShapeDtypeStruct((B, T, o), BF16),
            jax.ShapeDtypeStruct((B, 1, SHIFT_WIDTH), F32),
            jax.ShapeDtypeStruct((B, RWKV_HEADS, RWKV_HEAD, RWKV_HEAD), F32),
        ],
        scratch_shapes=[
            pltpu.VMEM((C, SHIFT_WIDTH), F32),
            pltpu.VMEM((nb, 1, SHIFT_WIDTH), F32),
            pltpu.VMEM((nb, N_GROUPS, MXU_DIM, MXU_DIM), F32),
        ],
        compiler_params=_cparams(("arbitrary", "arbitrary")),
        name="rwkv",
    )(ps3, st_shift, st_rwkv, P['shift_mu'], P['w0'], P['lora_wa'], P['g_lora_up'], P['a0'], P['k_k'], P['k_a'],
      P['r_k'], P['ln_x_w'], P['ln_x_b'], tri, ones_bd, mrow, msq)


def _rows(x, b, n):
    return x[b * n:(b + 1) * n]


def _dot_split3(sel_bf16, x):
    hi = x.astype(BF16)
    r1 = x - hi.astype(F32)
    mid = r1.astype(BF16)
    lo = (r1 - mid.astype(F32)).astype(BF16)
    return _dot(sel_bf16, hi) + _dot(sel_bf16, mid) + _dot(sel_bf16, lo)


def _wkv_kernel(ps_ref, sh_ref, s0_ref, mu_ref, w0_ref, lora_ref, gup_ref, a0_ref, kk_ref, ka_ref, rk_ref,
                lnw_ref, lnb_ref, tri_ref, endm_ref, ones_ref, mrow_ref, msq_ref,
                y_ref, nsh_ref, ns_ref, car_s, st_s, *, nb, seq, zero_state):
    C = CHUNK
    Q = C // seq
    R = nb * C
    o = RWKV_WIDTH
    ci = pl.program_id(1)
    last = ci == pl.num_programs(1) - 1
    t_idx = lax.broadcasted_iota(jnp.int32, (C, MXU_DIM), 0)
    i_idx = lax.broadcasted_iota(jnp.int32, (C, MXU_DIM), 1) % C
    same = (t_idx // seq) == (i_idx // seq)
    m_strict = same & (i_idx < t_idx)
    m_incl = same & (i_idx <= t_idx)
    eye_cat = (i_idx == t_idx).astype(F32)
    ones_bd = ones_ref[...]
    msq_f = msq_ref[...].astype(F32)
    n_lane_tiles = o // MXU_DIM

    def head_sum(x):
        xs = jnp.concatenate([x[:, g * MXU_DIM:(g + 1) * MXU_DIM] for g in range(n_lane_tiles)], axis=0)
        hi = xs.astype(BF16)
        lo = (xs - hi.astype(F32)).astype(BF16)
        s = _dot(jnp.concatenate([hi, lo], axis=0), ones_bd)
        s = s[:n_lane_tiles * R] + s[n_lane_tiles * R:]
        return jnp.concatenate([_rows(s, g, R) for g in range(n_lane_tiles)], axis=1)

    @pl.when(ci == 0)
    def _():
        for b in range(nb):
            car_s[b] = sh_ref[b]
            for q in range(Q):
                if zero_state:
                    st_s[b, q] = jnp.zeros(st_s.shape[2:], F32)
                else:
                    for g in range(N_GROUPS):
                        heads = [s0_ref[b, q * RWKV_HEADS + g * GROUP_HEADS + hh] for hh in range(GROUP_HEADS)]
                        stacked = jnp.concatenate(heads, axis=0)
                        st_s[b, q, g] = jnp.concatenate([stacked] * GROUP_HEADS, axis=1) * msq_f

    row_c = lax.broadcasted_iota(jnp.int32, (C, 1), 0)
    prevs = []
    for b in range(nb):
        ps_b = ps_ref[b]
        if Q == 1:
            first = car_s[b]
            new_carry = ps_b[C - 1:C, :]
        else:
            put = (lax.broadcasted_iota(jnp.int32, (C, Q), 0)
                   == seq * lax.broadcasted_iota(jnp.int32, (C, Q), 1)).astype(BF16)
            take = (lax.broadcasted_iota(jnp.int32, (Q, C), 1)
                    == seq * lax.broadcasted_iota(jnp.int32, (Q, C), 0) + (seq - 1)).astype(BF16)
            first = _dot_split3(put, car_s[b])
            new_carry = _dot_split3(take, ps_b)
        prevs.append(jnp.where(row_c % seq == 0, first, pltpu.roll(ps_b, 1, axis=0)))
        car_s[b] = new_carry
        nsh_ref[b] = new_carry
    ps = jnp.concatenate([ps_ref[b] for b in range(nb)], axis=0)
    prev = jnp.concatenate(prevs, axis=0)

    s = ps + (prev - ps) * mu_ref[...]
    r = s[:, :o]
    k = s[:, o:2 * o]
    v = s[:, 2 * o:3 * o]
    dwa = s[:, 3 * o:3 * o + LORA_W + LORA_A]
    dg = s[:, 3 * o + LORA_W + LORA_A:]
    lane = lax.broadcasted_iota(jnp.int32, dwa.shape, 1)
    lora_in = jnp.where(lane < LORA_W, jnp.tanh(dwa), dwa).astype(BF16)
    lora = _dot(lora_in, lora_ref[...])
    w_log = -_softplus(-(w0_ref[...] + lora[:, :o])) - 0.5
    lw = -jnp.exp(w_log)
    a = _sigmoid(a0_ref[...] + lora[:, o:])
    g = _dot(_sigmoid(dg).astype(BF16), gup_ref[...])
    kk = k * kk_ref[...]
    kk = kk / jnp.maximum(jnp.sqrt(head_sum(kk * kk)), 1e-12)
    kmod = k * (1.0 + (a - 1.0) * ka_ref[...])
    beta = kk * a
    lw_hi = lw.astype(BF16)
    lw_lo = (lw - lw_hi.astype(F32)).astype(BF16)
    L = _dot(tri_ref[...], lw_hi) + _dot(tri_ref[...], lw_lo)
    l_end = _dot(endm_ref[...], lw_hi) + _dot(endm_ref[...], lw_lo)
    e_neg = jnp.exp(-L)
    e_end = jnp.exp(l_end - L)
    p_end = jnp.exp(l_end)
    at = (-kk) * jnp.exp(L - lw)
    rt = r * jnp.exp(L)
    bt = (beta * e_neg).astype(BF16)
    kt = (kmod * e_neg).astype(BF16)
    bend = beta * e_end
    kend = kmod * e_end

    chains = []
    for b in range(nb):
        for gi in range(N_GROUPS):
            sl = slice(gi * GROUP_LANES, (gi + 1) * GROUP_LANES)
            chains.append(dict(
                b=b, gi=gi, vg=_rows(v, b, C)[:, sl].astype(BF16), p_end=_rows(p_end, b, C)[:, sl],
                x2=jnp.concatenate([_rows(at, b, C)[:, sl], _rows(rt, b, C)[:, sl]], axis=0).astype(BF16),
                wbd=jnp.concatenate([_block_diag(_rows(bt, b, C)[:, sl], mrow_ref),
                                     _block_diag(_rows(kt, b, C)[:, sl], mrow_ref)], axis=0),
                bk=jnp.concatenate([_rows(bend, b, C)[:, sl], _rows(kend, b, C)[:, sl]], axis=0).astype(BF16)))

    cw = GROUP_HEADS * C
    rowseq = (lax.broadcasted_iota(jnp.int32, (2 * C, 1), 0) % C) // seq
    for c in chains:
        res = _dot_tb(c['x2'], c['wbd'])
        n_cat = jnp.where(m_strict, res[:C, :cw], 0.0)
        c['a_ak'] = jnp.where(m_strict, res[:C, cw:], 0.0).astype(BF16)
        c['a_r'] = jnp.concatenate([jnp.where(m_incl, res[C:, :cw], 0.0),
                                    jnp.where(m_incl, res[C:, cw:], 0.0)], axis=1).astype(BF16)
        c['x_c'] = n_cat.astype(BF16)
        c['p_c'] = eye_cat + n_cat
        c['x_bd'] = _block_diag(c['x_c'], msq_ref)
    for c in chains:
        xs = None
        for q in range(Q):
            xq = _dot_tb(c['x2'], st_s[c['b'], q, c['gi']].astype(BF16))
            xs = xq if xs is None else jnp.where(rowseq == q, xq, xs)
        c['xs'] = xs
        c['v_bd'] = _block_diag(c['vg'], msq_ref)
    for c in chains:
        c['rhs'] = c['xs'][:C] + _dot(c['a_ak'], c['v_bd'])
    lvl = 2
    while lvl < seq:
        for c in chains:
            c['x_c'] = _dot(c['x_c'], c['x_bd']).astype(BF16)
            c['x_bd'] = _block_diag(c['x_c'], msq_ref)
        for c in chains:
            c['p_c'] = c['p_c'] + _dot(c['p_c'].astype(BF16), c['x_bd'])
        lvl *= 2
    for c in chains:
        c['u'] = _dot(c['p_c'].astype(BF16), _block_diag(c['rhs'].astype(BF16), msq_ref)).astype(BF16)
    ys = {}
    for c in chains:
        b, gi = c['b'], c['gi']
        ys[(b, gi)] = c['xs'][C:] + _dot(c['a_r'], jnp.concatenate([_block_diag(c['u'], msq_ref), c['v_bd']], axis=0))
        uv = jnp.concatenate([c['u'], c['vg']], axis=0)
        if Q > 1:
            uv = jnp.concatenate([jnp.where(rowseq == q, uv, jnp.zeros_like(uv)) for q in range(Q)], axis=1)
        ds = _dot_ta(uv, c['bk'])
        for q in range(Q):
            st_s[b, q, gi] = (st_s[b, q, gi] * c['p_end'][q * seq:q * seq + 1, :]
                              + _rows(ds, q, GROUP_LANES) * msq_f)

    yc = jnp.concatenate([jnp.concatenate([ys[(b, gi)] for gi in range(N_GROUPS)], axis=1) for b in range(nb)],
                         axis=0)
    inv_n = 1.0 / RWKV_HEAD
    mean = head_sum(yc) * inv_n
    dlt = yc - mean
    var = head_sum(dlt * dlt) * inv_n
    yn = dlt * lax.rsqrt(var + GN_EPS) * lnw_ref[...] + lnb_ref[...]
    bonus = head_sum(r * kmod * rk_ref[...]) * v
    out = ((yn + bonus) * g).astype(BF16)
    for b in range(nb):
        y_ref[b] = _rows(out, b, C)

    @pl.when(last)
    def _():
        for b in range(nb):
            for q in range(Q):
                for gi in range(N_GROUPS):
                    for hh in range(GROUP_HEADS):
                        ns_ref[b, q * RWKV_HEADS + gi * GROUP_HEADS + hh] = st_s[
                            b, q, gi, hh * RWKV_HEAD:(hh + 1) * RWKV_HEAD, hh * RWKV_HEAD:(hh + 1) * RWKV_HEAD]


def _wkv_consts(nb, seq):
    C = CHUNK
    gh = GROUP_HEADS
    rows = np.arange(nb * C)
    same = (rows[:, None] // seq) == (rows[None, :] // seq)
    tri = (same & (rows[None, :] <= rows[:, None])).astype(np.float32)
    endm = same.astype(np.float32)
    hl = np.arange(MXU_DIM) // RWKV_HEAD
    ones_bd = (hl[:, None] == hl[None, :]).astype(np.float32)
    rowh = np.arange(gh * C) // C
    colh = np.arange(GROUP_LANES) // RWKV_HEAD
    mrow = (rowh[:, None] == colh[None, :]).astype(np.float32)
    msq = (rowh[:, None] == rowh[None, :]).astype(np.float32)
    return tuple(jnp.asarray(m, BF16) for m in (tri, endm, ones_bd, mrow, msq))


def _wkv(ps3, st_shift, st_rwkv, P, layer, nb, zero_state):
    B, T, _ = ps3.shape
    C = CHUNK
    seq = C if T % C == 0 else T
    assert C % seq == 0
    Q = C // seq
    G = B // Q
    tg = T * Q
    assert B % Q == 0 and G % nb == 0 and tg % C == 0
    o = RWKV_WIDTH
    consts = _wkv_consts(nb, seq)
    vec = lambda w: pl.BlockSpec((None, 1, w), lambda b, c: (layer, 0, 0))
    const = lambda arr: pl.BlockSpec(arr.shape, lambda b, c: (0,) * arr.ndim)
    kern = functools.partial(_wkv_kernel, nb=nb, seq=seq, zero_state=zero_state)
    shift_spec = pl.BlockSpec((nb, Q, SHIFT_WIDTH), lambda b, c: (b, 0, 0))
    state_spec = pl.BlockSpec((nb, Q * RWKV_HEADS, RWKV_HEAD, RWKV_HEAD), lambda b, c: (b, 0, 0, 0))
    y, nsh, ns = pl.pallas_call(
        kern,
        grid=(G // nb, tg // C),
        in_specs=[
            pl.BlockSpec((nb, C, SHIFT_WIDTH), lambda b, c: (b, c, 0)),
            shift_spec,
            pl.BlockSpec((nb, Q * RWKV_HEADS, RWKV_HEAD, RWKV_HEAD), lambda b, c: (b, 0, 0, 0),
                         pipeline_mode=pl.Buffered(1)),
            vec(SHIFT_WIDTH),
            vec(o),
            pl.BlockSpec((None, LORA_W + LORA_A, 2 * o), lambda b, c: (layer, 0, 0)),
            pl.BlockSpec((None, LORA_G, o), lambda b, c: (layer, 0, 0)),
            vec(o), vec(o), vec(o), vec(o), vec(o), vec(o),
        ] + [const(m) for m in consts],
        out_specs=[
            pl.BlockSpec((nb, C, o), lambda b, c: (b, c, 0)),
            shift_spec,
            state_spec,
        ],
        out_shape=[
            jax.ShapeDtypeStruct((G, tg, o), BF16),
            jax.ShapeDtypeStruct((G, Q, SHIFT_WIDTH), F32),
            jax.ShapeDtypeStruct((G, Q * RWKV_HEADS, RWKV_HEAD, RWKV_HEAD), F32),
        ],
        scratch_shapes=[
            pltpu.VMEM((nb, Q, SHIFT_WIDTH), F32),
            pltpu.VMEM((nb, Q, N_GROUPS, MXU_DIM, MXU_DIM), F32),
        ],
        compiler_params=_cparams(("arbitrary", "arbitrary")),
        name="rwkv",
    )(ps3.reshape(G, tg, SHIFT_WIDTH), st_shift.reshape(G, Q, SHIFT_WIDTH),
      st_rwkv.reshape(G, Q * RWKV_HEADS, RWKV_HEAD, RWKV_HEAD),
      P['shift_mu'], P['w0'], P['lora_wa'], P['g_lora_up'], P['a0'], P['k_k'], P['k_a'],
      P['r_k'], P['ln_x_w'], P['ln_x_b'], *consts)
    return (y.reshape(B, T, o), nsh.reshape(B, SHIFT_WIDTH),
            ns.reshape(B, RWKV_HEADS, RWKV_HEAD, RWKV_HEAD))


def _merge_kernel(x_ref, ya_ref, yb_ref, gt_ref, wa_ref, wb_ref, wo_ref, nf_ref, *rest, moe):
    if moe:
        wr_ref, xo_ref, h_ref, gates_ref, cnt_ref = rest
    else:
        xo_ref, h_ref = rest
    ga = gt_ref[:, :D_MODEL]
    gb = gt_ref[:, D_MODEL:]
    m = _sigmoid(ga) * _dot(ya_ref[...], wa_ref[...]) + _sigmoid(gb) * _dot(yb_ref[...], wb_ref[...])
    x = x_ref[...] + _dot(m.astype(BF16), wo_ref[...])
    xo_ref[...] = x
    h = _rms(x, nf_ref[...])
    h_ref[...] = h.astype(BF16)
    if moe:
        logits = jnp.dot(h, wr_ref[...], preferred_element_type=F32, precision=lax.Precision.HIGHEST)
        lane = lax.broadcasted_iota(jnp.int32, logits.shape, 1)
        real = lane < N_EXPERTS
        logits = jnp.where(real, logits, -jnp.inf)
        e = jnp.exp(logits - jnp.max(logits, axis=-1, keepdims=True))
        p = jnp.where(real, e / jnp.sum(e, axis=-1, keepdims=True), -1.0)
        m1 = jnp.max(p, axis=-1, keepdims=True)
        i1 = jnp.min(jnp.where(p == m1, lane, LANES), axis=-1, keepdims=True)
        oh1 = lane == i1
        p2 = jnp.where(oh1, -1.0, p)
        m2 = jnp.max(p2, axis=-1, keepdims=True)
        i2 = jnp.min(jnp.where(p2 == m2, lane, LANES), axis=-1, keepdims=True)
        oh2 = lane == i2
        tot = m1 + m2
        gates = jnp.where(oh1, m1 / tot, 0.0) + jnp.where(oh2, m2 / tot, 0.0)
        gates_ref[...] = gates
        cnt = jnp.sum((gates > 0.0).astype(F32), axis=0, keepdims=True)
        cnt_ref[...] = jnp.broadcast_to(cnt, (SUBLANES, LANES))


def _merge(x, ya, yb, gt, P, layer, tm, moe):
    n = x.shape[0]
    row = lambda w: pl.BlockSpec((tm, w), lambda i: (i, 0))
    wsq = lambda: pl.BlockSpec((None, D_MODEL, D_MODEL), lambda i: (layer, 0, 0))
    in_specs = [row(D_MODEL), row(D_MODEL), row(D_MODEL), row(2 * D_MODEL), wsq(), wsq(), wsq(),
                pl.BlockSpec((None, 1, D_MODEL), lambda i: (layer, 0, 0))]
    args = [x, ya, yb, gt, P['w_out_a'], P['w_out_b'], P['w_out'], P['norm_ffn']]
    out_specs = [row(D_MODEL), row(D_MODEL)]
    out_shape = [jax.ShapeDtypeStruct((n, D_MODEL), F32), jax.ShapeDtypeStruct((n, D_MODEL), BF16)]
    if moe:
        in_specs.append(pl.BlockSpec((None, D_MODEL, LANES), lambda i: (layer // 2, 0, 0)))
        args.append(P['w_router_pad'])
        out_specs.append(row(LANES))
        out_shape.append(jax.ShapeDtypeStruct((n, LANES), F32))
        out_specs.append(pl.BlockSpec((None, SUBLANES, LANES), lambda i: (i, 0, 0)))
        out_shape.append(jax.ShapeDtypeStruct((n // tm, SUBLANES, LANES), F32))
    return pl.pallas_call(
        functools.partial(_merge_kernel, moe=moe),
        grid=(n // tm,),
        in_specs=in_specs,
        out_specs=out_specs,
        out_shape=out_shape,
        compiler_params=_cparams(("arbitrary",)),
        name="merge",
    )(*args)


def _swiglu_acc(h, wg_ref, wu_ref, wd_ref, width=D_FF):
    acc = None
    for c0 in range(0, width, FF_CHUNK):
        sl = slice(c0, min(c0 + FF_CHUNK, width))
        gate = _dot(h, wg_ref[:, sl])
        up = _dot(h, wu_ref[:, sl])
        act = (gate * _sigmoid(gate) * up).astype(BF16)
        part = _dot(act, wd_ref[sl, :])
        acc = part if acc is None else acc + part
    return acc


def _ffn_kernel(x_ref, h_ref, wg_ref, wu_ref, wd_ref, o_ref):
    o_ref[...] = x_ref[...] + _swiglu_acc(h_ref[...], wg_ref, wu_ref, wd_ref)


def _ffn(x, h, P, j, tm):
    n = x.shape[0]
    row = lambda: pl.BlockSpec((tm, D_MODEL), lambda i: (i, 0))
    return pl.pallas_call(
        _ffn_kernel,
        grid=(n // tm,),
        in_specs=[row(), row(),
                  pl.BlockSpec((None, D_MODEL, D_FF), lambda i: (j, 0, 0), pipeline_mode=pl.Buffered(1)),
                  pl.BlockSpec((None, D_MODEL, D_FF), lambda i: (j, 0, 0), pipeline_mode=pl.Buffered(1)),
                  pl.BlockSpec((None, D_FF, D_MODEL), lambda i: (j, 0, 0), pipeline_mode=pl.Buffered(1))],
        out_specs=row(),
        out_shape=jax.ShapeDtypeStruct((n, D_MODEL), F32),
        compiler_params=_cparams(("arbitrary",)),
        name="ffn",
    )(x, h, P['w_ffn_gate'], P['w_ffn_up'], P['w_ffn_down'])


def _moe_kernel(cnt_ref, x_ref, h_ref, gates_ref, wg_ref, wu_ref, wd_ref, o_ref, rank_t_s, rank_n_s, xg_s, y_s,
                *, tw):
    i = pl.program_id(0)
    e = pl.program_id(1)
    f = pl.program_id(2)
    sb = min(MOE_SUB, tw)

    @pl.when((e == 0) & (f == 0))
    def _():
        o_ref[...] = x_ref[...]
        mask = (gates_ref[...] > 0.0).astype(BF16)
        r_idx = lax.broadcasted_iota(jnp.int32, (tw, tw), 0)
        c_idx = lax.broadcasted_iota(jnp.int32, (tw, tw), 1)
        before = (r_idx < c_idx).astype(BF16)
        upto = (r_idx <= c_idx).astype(BF16)
        rank_t_s[:, 0:tw] = _dot_ta(mask, before)
        rank_t_s[:, tw:2 * tw] = _dot_ta(mask, upto)
        rank_n_s[:, 0:LANES] = _dot_ta(before, mask)
        rank_n_s[:, LANES:2 * LANES] = _dot_ta(upto, mask)

    cnt = cnt_ref[i * N_EXPERTS + e]
    nblk = (cnt + sb - 1) // sb
    lane = lax.broadcasted_iota(jnp.int32, (tw, LANES), 1)
    pick = lane == e
    col = lambda a: jnp.sum(jnp.where(pick, a, 0.0), axis=-1, keepdims=True)
    gate_col = col(gates_ref[...])
    excl_col = col(rank_n_s[:, 0:LANES])
    incl_col = col(rank_n_s[:, LANES:2 * LANES])
    excl_row = rank_t_s[pl.ds(e, 1), 0:tw]
    incl_row = rank_t_s[pl.ds(e, 1), tw:2 * tw]

    def block(k, carry):
        r0 = pl.multiple_of(k * sb, sb)
        rr_col = (lax.broadcasted_iota(jnp.int32, (sb, 1), 0) + k * sb).astype(F32)
        rr_row = (lax.broadcasted_iota(jnp.int32, (1, sb), 1) + k * sb).astype(F32)

        @pl.when(f == 0)
        def _():
            gather = ((excl_row == rr_col) & (incl_row == rr_col + 1.0)).astype(BF16)
            xg_s[pl.ds(r0, sb), :] = _dot(gather, h_ref[...]).astype(BF16)
            y_s[pl.ds(r0, sb), :] = _swiglu_acc(xg_s[pl.ds(r0, sb), :], wg_ref, wu_ref, wd_ref, D_FF // MOE_FF_SPLIT)

        @pl.when(f == MOE_FF_SPLIT - 1)
        def _():
            y = y_s[pl.ds(r0, sb), :] + _swiglu_acc(xg_s[pl.ds(r0, sb), :], wg_ref, wu_ref, wd_ref,
                                                    D_FF // MOE_FF_SPLIT)
            hi = y.astype(BF16)
            lo = (y - hi.astype(F32)).astype(BF16)
            scatter = ((excl_col == rr_row) & (incl_col == rr_row + 1.0)).astype(BF16)
            o_ref[...] += gate_col * (_dot(scatter, hi) + _dot(scatter, lo))

        return carry

    lax.fori_loop(0, nblk, block, 0)


def _moe(x, h, gates, counts, P, j, tw):
    n = x.shape[0]
    assert MOE_FF_SPLIT == 2
    ffw = D_FF // MOE_FF_SPLIT
    row = lambda w: pl.BlockSpec((tw, w), lambda i, e, f, c: (i, 0), pipeline_mode=pl.Buffered(1))
    grid_spec = pltpu.PrefetchScalarGridSpec(
        num_scalar_prefetch=1,
        grid=(n // tw, N_EXPERTS, MOE_FF_SPLIT),
        in_specs=[row(D_MODEL), row(D_MODEL), row(LANES),
                  pl.BlockSpec((None, None, D_MODEL, ffw), lambda i, e, f, c: (j, e, 0, f)),
                  pl.BlockSpec((None, None, D_MODEL, ffw), lambda i, e, f, c: (j, e, 0, f)),
                  pl.BlockSpec((None, None, ffw, D_MODEL), lambda i, e, f, c: (j, e, f, 0))],
        out_specs=pl.BlockSpec((tw, D_MODEL), lambda i, e, f, c: (i, 0)),
        scratch_shapes=[pltpu.VMEM((LANES, 2 * tw), F32), pltpu.VMEM((tw, 2 * LANES), F32),
                        pltpu.VMEM((tw, D_MODEL), BF16), pltpu.VMEM((tw, D_MODEL), F32)],
    )
    return pl.pallas_call(
        functools.partial(_moe_kernel, tw=tw),
        grid_spec=grid_spec,
        out_shape=jax.ShapeDtypeStruct((n, D_MODEL), F32),
        compiler_params=_cparams(("arbitrary", "arbitrary", "arbitrary")),
        name="moe",
    )(counts, x, h, gates, P['w_moe_gate'], P['w_moe_up'], P['w_moe_down'])


def _route_kernel(gates_ref, base_ref, pos_ref, *, tw, spare):
    mask = (gates_ref[...] > 0.0).astype(BF16)
    r_idx = lax.broadcasted_iota(jnp.int32, (tw, tw), 0)
    c_idx = lax.broadcasted_iota(jnp.int32, (tw, tw), 1)
    excl = _dot_ta(mask, (r_idx < c_idx).astype(BF16))
    sel = _dot_ta(mask, (r_idx == c_idx).astype(BF16)) > 0.0
    pos = excl + base_ref[:, 0:1]
    e_io = lax.broadcasted_iota(jnp.int32, (LANES, tw), 0)
    e_first = jnp.min(jnp.where(sel, e_io, LANES), axis=0, keepdims=True)
    e_last = jnp.max(jnp.where(sel, e_io, -1), axis=0, keepdims=True)
    p_first = jnp.sum(jnp.where(e_io == e_first, pos, 0.0), axis=0, keepdims=True)
    p_last = jnp.sum(jnp.where(e_io == e_last, pos, 0.0), axis=0, keepdims=True)
    p_last = jnp.where(e_last == e_first, float(spare), p_last)
    rows = jnp.concatenate([p_first, p_last, jnp.zeros((SUBLANES - 2, tw), F32)], axis=0)
    pos_ref[...] = rows.astype(jnp.int32)


def _route(gates, base_b, tw, spare):
    n = gates.shape[0]
    return pl.pallas_call(
        functools.partial(_route_kernel, tw=tw, spare=spare),
        grid=(n // tw,),
        in_specs=[pl.BlockSpec((tw, LANES), lambda i: (i, 0)),
                  pl.BlockSpec((None, LANES, LANES), lambda i: (i, 0, 0))],
        out_specs=pl.BlockSpec((None, SUBLANES, tw), lambda i: (i, 0, 0)),
        out_shape=jax.ShapeDtypeStruct((n // tw, SUBLANES, tw), jnp.int32),
        compiler_params=_cparams(("arbitrary",)),
        name="route",
    )(gates, base_b)


def _row_copy(src_hbm, src_row, dst_ref, dst_row, sem):
    return pltpu.make_async_copy(src_hbm.at[pl.ds(src_row, 1), :], dst_ref.at[pl.ds(dst_row, 1), :], sem)


def _disperse_kernel(pos_ref, x_ref, init_hbm, xs_hbm, sem, *, tw):
    del init_hbm

    def issue(n, carry):
        _row_copy(x_ref, n, xs_hbm, pos_ref[0, n], sem).start()
        _row_copy(x_ref, n, xs_hbm, pos_ref[1, n], sem).start()
        return carry

    def drain(n, carry):
        _row_copy(x_ref, 0, xs_hbm, 0, sem).wait()
        _row_copy(x_ref, 0, xs_hbm, 0, sem).wait()
        return carry

    lax.fori_loop(0, tw, issue, 0)
    lax.fori_loop(0, tw, drain, 0)


def _disperse(pos, x, n_rows, tw):
    n = x.shape[0]
    return pl.pallas_call(
        functools.partial(_disperse_kernel, tw=tw),
        grid=(n // tw,),
        in_specs=[pl.BlockSpec((None, SUBLANES, tw), lambda i: (i, 0, 0), memory_space=pltpu.SMEM),
                  pl.BlockSpec((tw, D_MODEL), lambda i: (i, 0)),
                  pl.BlockSpec(memory_space=pl.ANY)],
        out_specs=pl.BlockSpec(memory_space=pl.ANY),
        out_shape=jax.ShapeDtypeStruct((n_rows, D_MODEL), F32),
        scratch_shapes=[pltpu.SemaphoreType.DMA],
        input_output_aliases={2: 0},
        compiler_params=_cparams(("arbitrary",)),
        name="disperse",
    )(pos, x, jnp.zeros((n_rows, D_MODEL), F32))


def _experts_kernel(te_ref, used_ref, xs_ref, nf_ref, wg_ref, wu_ref, wd_ref, ys_ref):
    del te_ref
    j = pl.program_id(0)

    @pl.when(j < used_ref[0])
    def _():
        h = _rms(xs_ref[...], nf_ref[...]).astype(BF16)
        ys_ref[...] = _swiglu_acc(h, wg_ref, wu_ref, wd_ref)

    @pl.when(j >= used_ref[0])
    def _():
        ys_ref[...] = jnp.zeros(ys_ref.shape, F32)


def _experts(tile_expert, used, xs, P, layer, j):
    n_tiles = tile_expert.shape[0]
    tr = ROUTE_ROWS
    wspec = lambda a, b: pl.BlockSpec((None, None, a, b), lambda t, te, u: (j, te[t], 0, 0))
    grid_spec = pltpu.PrefetchScalarGridSpec(
        num_scalar_prefetch=2,
        grid=(n_tiles,),
        in_specs=[pl.BlockSpec((tr, D_MODEL), lambda t, te, u: (t, 0)),
                  pl.BlockSpec((None, 1, D_MODEL), lambda t, te, u: (layer, 0, 0)),
                  wspec(D_MODEL, D_FF), wspec(D_MODEL, D_FF), wspec(D_FF, D_MODEL)],
        out_specs=pl.BlockSpec((tr, D_MODEL), lambda t, te, u: (t, 0)),
    )
    return pl.pallas_call(
        _experts_kernel,
        grid_spec=grid_spec,
        out_shape=jax.ShapeDtypeStruct((n_tiles * tr, D_MODEL), F32),
        compiler_params=_cparams(("arbitrary",)),
        name="experts",
    )(tile_expert, used, xs, P['norm_ffn'], P['w_moe_gate'], P['w_moe_up'], P['w_moe_down'])


def _combine_kernel(pos_ref, x_ref, gates_ref, ys_hbm, o_ref, y1_s, y2_s, sem, *, tw):
    def issue(n, carry):
        _row_copy(ys_hbm, pos_ref[0, n], y1_s, n, sem).start()
        _row_copy(ys_hbm, pos_ref[1, n], y2_s, n, sem).start()
        return carry

    def drain(n, carry):
        _row_copy(ys_hbm, 0, y1_s, 0, sem).wait()
        _row_copy(ys_hbm, 0, y2_s, 0, sem).wait()
        return carry

    lax.fori_loop(0, tw, issue, 0)
    gates = gates_ref[...]
    lane = lax.broadcasted_iota(jnp.int32, gates.shape, 1)
    sel = gates > 0.0
    e_first = jnp.min(jnp.where(sel, lane, LANES), axis=-1, keepdims=True)
    e_last = jnp.max(jnp.where(sel, lane, -1), axis=-1, keepdims=True)
    g_first = jnp.sum(jnp.where(lane == e_first, gates, 0.0), axis=-1, keepdims=True)
    g_last = jnp.sum(jnp.where((lane == e_last) & (e_last != e_first), gates, 0.0), axis=-1, keepdims=True)
    lax.fori_loop(0, tw, drain, 0)
    o_ref[...] = x_ref[...] + g_first * y1_s[...] + g_last * y2_s[...]


def _combine(pos, x, gates, ys, tw):
    n = x.shape[0]
    return pl.pallas_call(
        functools.partial(_combine_kernel, tw=tw),
        grid=(n // tw,),
        in_specs=[pl.BlockSpec((None, SUBLANES, tw), lambda i: (i, 0, 0), memory_space=pltpu.SMEM),
                  pl.BlockSpec((tw, D_MODEL), lambda i: (i, 0)),
                  pl.BlockSpec((tw, LANES), lambda i: (i, 0)),
                  pl.BlockSpec(memory_space=pl.ANY)],
        out_specs=pl.BlockSpec((tw, D_MODEL), lambda i: (i, 0)),
        out_shape=jax.ShapeDtypeStruct((n, D_MODEL), F32),
        scratch_shapes=[pltpu.VMEM((tw, D_MODEL), F32), pltpu.VMEM((tw, D_MODEL), F32), pltpu.SemaphoreType.DMA],
        compiler_params=_cparams(("arbitrary",)),
        name="combine",
    )(pos, x, gates, ys)


def _moe_routed(x, gates, cnt, P, layer, tm):
    n = x.shape[0]
    tw = min(ROUTE_TILE, n)
    tr = ROUTE_ROWS
    n_tiles = (TOP_K * n) // tr + N_EXPERTS
    spare = n_tiles * tr
    c = cnt[:, 0, :].astype(jnp.int32).reshape(n // tw, tw // tm, LANES).sum(axis=1)
    seg_tiles = (c.sum(axis=0) + tr - 1) // tr
    seg_end = jnp.cumsum(seg_tiles)
    base = (seg_end - seg_tiles) * tr + jnp.cumsum(c, axis=0) - c
    base_b = jnp.broadcast_to(base.astype(F32)[:, :, None], (n // tw, LANES, LANES))
    tile_expert = jnp.minimum((jnp.arange(n_tiles + 1)[:, None] >= seg_end[None, :N_EXPERTS]).sum(axis=1),
                              N_EXPERTS - 1).astype(jnp.int32)
    used = seg_end[N_EXPERTS - 1:N_EXPERTS].astype(jnp.int32)
    pos = _route(gates, base_b, tw, spare)
    xs = _disperse(pos, x, spare + tr, tw)
    ys = _experts(tile_expert, used, xs, P, layer, layer // 2)
    return _combine(pos, x, gates, ys, tw)


def _final_norm_kernel(x_ref, g_ref, o_ref):
    o_ref[...] = _rms(x_ref[...], g_ref[...])


def _final_norm(x, g, tm):
    n = x.shape[0]
    return pl.pallas_call(
        _final_norm_kernel,
        grid=(n // tm,),
        in_specs=[pl.BlockSpec((tm, D_MODEL), lambda i: (i, 0)), pl.BlockSpec((1, D_MODEL), lambda i: (0, 0))],
        out_specs=pl.BlockSpec((tm, D_MODEL), lambda i: (i, 0)),
        out_shape=jax.ShapeDtypeStruct((n, D_MODEL), F32),
        compiler_params=_cparams(("arbitrary",)),
        name="final_norm",
    )(x, g)


def _block_diag_weights(w):
    d = w.shape[0]
    per = MXU_DIM // LRU_BLOCK
    w = w.reshape(d, LRU_HEADS // per, per, LRU_BLOCK, LRU_BLOCK)
    eye = jnp.eye(per, dtype=w.dtype)
    bd = jnp.einsum('dcpij,pq->dcpiqj', w, eye)
    return bd.reshape(d, LRU_HEADS // per, MXU_DIM, MXU_DIM).astype(BF16)


def _prep_params(p):
    P = dict(p)
    for name in ('w_in', 'w_out_a', 'w_out_b', 'w_out', 'w_ffn_gate', 'w_ffn_up', 'w_ffn_down',
                 'w_moe_gate', 'w_moe_up', 'w_moe_down', 'g_lora_up'):
        P[name] = p[name].astype(BF16)
    for name in ('norm_mix', 'conv_b', 'gate_a_b', 'gate_x_b', 'lru_lambda', 'shift_mu', 'w0', 'a0', 'k_k', 'k_a',
                 'ln_x_w', 'ln_x_b', 'norm_ffn'):
        P[name] = p[name][:, None, :]
    P['r_k'] = p['r_k'].reshape(DEPTH, 1, RWKV_WIDTH)
    P['norm_final'] = p['norm_final'][None, :]
    P['gaw_bd'] = _block_diag_weights(p['gate_a_w'])
    P['gxw_bd'] = _block_diag_weights(p['gate_x_w'])
    z = jnp.zeros((DEPTH, LORA_W, RWKV_WIDTH), F32)
    P['lora_wa'] = jnp.concatenate([jnp.concatenate([p['w_lora_up'], z], axis=2),
                                    jnp.concatenate([z, p['a_lora_up']], axis=2)], axis=1).astype(BF16)
    P['w_router_pad'] = jnp.pad(p['w_router'], ((0, 0), (0, 0), (0, LANES - N_EXPERTS)))
    return P


def _trunk(x3, st_rwkv, st_lru, st_conv, st_shift, P, *, zero_state, tm, lru_bb, lru_tt, rwkv_nb):
    B, T, _ = x3.shape
    n = B * T
    x = x3.reshape(n, D_MODEL)
    n_rwkv, n_lru, n_conv, n_shift = [], [], [], []
    for l in range(DEPTH):
        xy, ps, gt = _proj(x, P['norm_mix'], P['w_in'], l, min(tm, 256))
        ya, h_new, c_new = _lru(xy.reshape(B, T, 2 * LRU_WIDTH), st_conv[l], st_lru[l][:, None, :], P, l,
                                lru_bb, lru_tt)
        yb, sh_new, s_new = _wkv(ps.reshape(B, T, SHIFT_WIDTH), st_shift[l], st_rwkv[l], P, l, rwkv_nb, zero_state)
        moe = l % 2 == 1
        res = _merge(x, ya.reshape(n, D_MODEL), yb.reshape(n, D_MODEL), gt, P, l, tm, moe)
        if moe:
            x, h, gates, cnt = res
            x = _moe_routed(x, gates, cnt, P, l, tm)
        else:
            x, h = res
            x = _ffn(x, h, P, l // 2, tm)
        n_rwkv.append(s_new)
        n_lru.append(h_new[:, 0, :])
        n_conv.append(c_new)
        n_shift.append(sh_new)
    y = _final_norm(x, P['norm_final'], tm).reshape(B, T, D_MODEL)
    return y, jnp.stack(n_rwkv), jnp.stack(n_lru), jnp.stack(n_conv), jnp.stack(n_shift)


def kernel(x_prompt, x_sample, state_rwkv, state_lru, state_conv, state_shift, norm_mix, w_in, conv_w, conv_b, gate_a_w, gate_a_b, gate_x_w, gate_x_b, lru_lambda, shift_mu, w0, w_lora_up, a0, a_lora_up, g_lora_up, k_k, k_a, r_k, ln_x_w, ln_x_b, w_out_a, w_out_b, w_out, norm_ffn, w_ffn_gate, w_ffn_up, w_ffn_down, w_router, w_moe_gate, w_moe_up, w_moe_down, norm_final):
    P = _prep_params(dict(
        norm_mix=norm_mix, w_in=w_in, conv_w=conv_w, conv_b=conv_b, gate_a_w=gate_a_w, gate_a_b=gate_a_b,
        gate_x_w=gate_x_w, gate_x_b=gate_x_b, lru_lambda=lru_lambda, shift_mu=shift_mu, w0=w0,
        w_lora_up=w_lora_up, a0=a0, a_lora_up=a_lora_up, g_lora_up=g_lora_up, k_k=k_k, k_a=k_a, r_k=r_k,
        ln_x_w=ln_x_w, ln_x_b=ln_x_b, w_out_a=w_out_a, w_out_b=w_out_b, w_out=w_out, norm_ffn=norm_ffn,
        w_ffn_gate=w_ffn_gate, w_ffn_up=w_ffn_up, w_ffn_down=w_ffn_down, w_router=w_router,
        w_moe_gate=w_moe_gate, w_moe_up=w_moe_up, w_moe_down=w_moe_down, norm_final=norm_final))
    bp = x_prompt.shape[0]
    bs = x_sample.shape[0]
    zeros = lambda *shape: jnp.zeros(shape, F32)
    p_out = _trunk(x_prompt,
                   zeros(DEPTH, bp, RWKV_HEADS, RWKV_HEAD, RWKV_HEAD), zeros(DEPTH, bp, LRU_WIDTH),
                   zeros(DEPTH, bp, CONV_WIDTH - 1, LRU_WIDTH), zeros(DEPTH, bp, SHIFT_WIDTH), P,
                   zero_state=True, tm=512, lru_bb=1, lru_tt=256, rwkv_nb=2)
    s_out = _trunk(x_sample, state_rwkv, state_lru, state_conv, state_shift, P,
                   zero_state=False, tm=256, lru_bb=8, lru_tt=x_sample.shape[1], rwkv_nb=1)
    return (p_out[0], s_out[0]) + tuple(p_out[1:]) + tuple(s_out[1:])
```

```python
import functools

import numpy as np
import jax
import jax.numpy as jnp
from jax import lax
from jax.experimental import pallas as pl
from jax.experimental.pallas import tpu as pltpu

F32 = jnp.float32
BF16 = jnp.bfloat16

D_MODEL = 1024
DEPTH = 4
LRU_WIDTH = D_MODEL
LRU_HEADS = 16
LRU_BLOCK = LRU_WIDTH // LRU_HEADS
CONV_WIDTH = 4
LRU_C = 8.0
RWKV_HEAD = 64
RWKV_WIDTH = D_MODEL
RWKV_HEADS = RWKV_WIDTH // RWKV_HEAD
LORA_W = 64
LORA_A = 64
LORA_G = 128
SHIFT_WIDTH = 3 * RWKV_WIDTH + LORA_W + LORA_A + LORA_G
PROJ_WIDTH = 2 * LRU_WIDTH + SHIFT_WIDTH + 2 * D_MODEL
D_FF = 2816
N_EXPERTS = 8
NORM_EPS = 1e-6
GN_EPS = 64e-5

LANES = 128
SUBLANES = 8
MXU_DIM = 256
VMEM_LIMIT = 56 * 1024 * 1024

CHUNK = 64
GROUP_HEADS = MXU_DIM // CHUNK
GROUP_LANES = GROUP_HEADS * RWKV_HEAD
N_GROUPS = RWKV_HEADS // GROUP_HEADS
FF_CHUNK = 256
TOP_K = 2
ROUTE_TILE = 512
ROUTE_ROWS = 512
SCAN_ROWS = 16
assert CHUNK == RWKV_HEAD


def _cparams(sem):
    return pltpu.CompilerParams(dimension_semantics=sem, vmem_limit_bytes=VMEM_LIMIT)


def _dot(a, b):
    return jnp.dot(a, b, preferred_element_type=F32)


def _dot_tb(a, b):
    return lax.dot_general(a, b, (((1,), (1,)), ((), ())), preferred_element_type=F32)


def _dot_ta(a, b):
    return lax.dot_general(a, b, (((0,), (0,)), ((), ())), preferred_element_type=F32)


def _sigmoid(x):
    return jax.nn.sigmoid(x)


def _softplus(z):
    return jnp.maximum(z, 0.0) + jnp.log(1.0 + jnp.exp(-jnp.abs(z)))


def _rms(x, g):
    return x * lax.rsqrt(jnp.mean(x * x, axis=-1, keepdims=True) + NORM_EPS) * g


_PROJ_SPLITS = ((0, 2 * LRU_WIDTH), (2 * LRU_WIDTH, SHIFT_WIDTH), (2 * LRU_WIDTH + SHIFT_WIDTH, 2 * D_MODEL))


def _proj_kernel(x_ref, g_ref, w_ref, xy_ref, ps_ref, gt_ref):
    u = _rms(x_ref[...], g_ref[...]).astype(BF16)
    for out_ref, (c0, width) in zip((xy_ref, ps_ref, gt_ref), _PROJ_SPLITS):
        j = 0
        while j < width:
            w = min(512, width - j)
            out_ref[:, j:j + w] = _dot(u, w_ref[:, c0 + j:c0 + j + w])
            j += w


def _proj(x, g_all, w_all, layer, tm):
    n = x.shape[0]
    return pl.pallas_call(
        _proj_kernel,
        grid=(n // tm,),
        in_specs=[
            pl.BlockSpec((tm, D_MODEL), lambda i: (i, 0)),
            pl.BlockSpec((None, 1, D_MODEL), lambda i: (layer, 0, 0)),
            pl.BlockSpec((None, D_MODEL, PROJ_WIDTH), lambda i: (layer, 0, 0),
                         pipeline_mode=pl.Buffered(1)),
        ],
        out_specs=[
            pl.BlockSpec((tm, 2 * LRU_WIDTH), lambda i: (i, 0)),
            pl.BlockSpec((tm, SHIFT_WIDTH), lambda i: (i, 0)),
            pl.BlockSpec((tm, 2 * D_MODEL), lambda i: (i, 0)),
        ],
        out_shape=[
            jax.ShapeDtypeStruct((n, 2 * LRU_WIDTH), F32),
            jax.ShapeDtypeStruct((n, SHIFT_WIDTH), F32),
            jax.ShapeDtypeStruct((n, 2 * D_MODEL), F32),
        ],
        compiler_params=_cparams(("arbitrary",)),
        name="proj",
    )(x, g_all, w_all)


def _gelu_tanh(x):
    return 0.5 * x * (1.0 + jnp.tanh(np.sqrt(2.0 / np.pi).astype(np.float32) * (x + 0.044715 * (x * x * x))))


def _lru_kernel(xa_ref, ya_ref, cs_ref, h0_ref, cw_ref, cb_ref, gaw_ref, gab_ref, gxw_ref, gxb_ref, lam_ref,
                y_ref, hl_ref, nc_ref, xp_s, h_s, a_s, b_s, *, bb, tt):
    ti = pl.program_id(1)
    neg_c = -LRU_C * _softplus(-lam_ref[...])
    cw = cw_ref[...]
    @pl.when(ti == 0)
    def _():
        for b in range(bb):
            xp_s[b, 5:8, :] = cs_ref[b]
            h_s[b] = h0_ref[b]

    for b in range(bb):
        xa = xa_ref[b]
        xp_s[b, 8:8 + tt, :] = xa
        xc = cb_ref[...] + xp_s[b, 5:5 + tt, :] * cw[0:1]
        xc = xc + xp_s[b, 6:6 + tt, :] * cw[1:2]
        xc = xc + xp_s[b, 7:7 + tt, :] * cw[2:3]
        xc = xc + xa * cw[3:4]
        tail = xp_s[b, 5 + tt:8 + tt, :]
        xp_s[b, 5:8, :] = tail
        nc_ref[b] = tail

        for c in range(LRU_WIDTH // MXU_DIM):
            sl = slice(c * MXU_DIM, (c + 1) * MXU_DIM)
            xcc = xc[:, sl]
            xcb = xcc.astype(BF16)
            r = _sigmoid(_dot(xcb, gaw_ref[c]) + gab_ref[:, sl])
            i = _sigmoid(_dot(xcb, gxw_ref[c]) + gxb_ref[:, sl])
            log_a = neg_c[:, sl] * r
            a = jnp.exp(log_a)
            a_s[:, sl] = a
            b_s[:, sl] = jnp.sqrt(-jnp.tanh(log_a) * (a * a + 1.0)) * (i * xcc)

        if tt % SCAN_ROWS == 0:
            row = lax.broadcasted_iota(jnp.int32, (SUBLANES, LRU_WIDTH), 0)

            def blk(j, h):
                r0 = pl.multiple_of(j * SCAN_ROWS, SCAN_ROWS)
                hbs = []
                for q in range(SCAN_ROWS // SUBLANES):
                    av = a_s[pl.ds(r0 + q * SUBLANES, SUBLANES), :]
                    bv = b_s[pl.ds(r0 + q * SUBLANES, SUBLANES), :]
                    for d in (1, 2, 4):
                        m = row >= d
                        a_sh = pltpu.roll(av, d, axis=0)
                        b_sh = pltpu.roll(bv, d, axis=0)
                        bv = jnp.where(m, av * b_sh + bv, bv)
                        av = jnp.where(m, av * a_sh, av)
                    hb = av * h + bv
                    h = hb[SUBLANES - 1:SUBLANES, :]
                    hbs.append(hb)
                hs = jnp.concatenate(hbs, axis=0)
                y_ref[b, pl.ds(r0, SCAN_ROWS), :] = (hs * _gelu_tanh(ya_ref[b, pl.ds(r0, SCAN_ROWS), :])).astype(BF16)
                return h

            h = lax.fori_loop(0, tt // SCAN_ROWS, blk, h_s[b])
        else:
            h = h_s[b]
            rows = []
            for t in range(tt):
                h = a_s[t:t + 1, :] * h + b_s[t:t + 1, :]
                rows.append(h)
            hs = jnp.concatenate(rows, axis=0)
            y_ref[b] = (hs * _gelu_tanh(ya_ref[b])).astype(BF16)
        h_s[b] = h
        hl_ref[b] = h


def _lru(xy3, st_conv, st_lru, P, layer, bb, tt):
    B, T, _ = xy3.shape
    W = LRU_WIDTH
    vec = lambda: pl.BlockSpec((None, 1, W), lambda b, t: (layer, 0, 0))
    kern = functools.partial(_lru_kernel, bb=bb, tt=tt)
    return pl.pallas_call(
        kern,
        grid=(B // bb, T // tt),
        in_specs=[
            pl.BlockSpec((bb, tt, W), lambda b, t: (b, t, 0)),
            pl.BlockSpec((bb, tt, W), lambda b, t: (b, t, 1)),
            pl.BlockSpec((bb, CONV_WIDTH - 1, W), lambda b, t: (b, 0, 0)),
            pl.BlockSpec((bb, 1, W), lambda b, t: (b, 0, 0)),
            pl.BlockSpec((None, CONV_WIDTH, W), lambda b, t: (layer, 0, 0)),
            vec(),
            pl.BlockSpec((None, W // MXU_DIM, MXU_DIM, MXU_DIM), lambda b, t: (layer, 0, 0, 0)),
            vec(),
            pl.BlockSpec((None, W // MXU_DIM, MXU_DIM, MXU_DIM), lambda b, t: (layer, 0, 0, 0)),
            vec(),
            vec(),
        ],
        out_specs=[
            pl.BlockSpec((bb, tt, W), lambda b, t: (b, t, 0)),
            pl.BlockSpec((bb, 1, W), lambda b, t: (b, 0, 0)),
            pl.BlockSpec((bb, CONV_WIDTH - 1, W), lambda b, t: (b, 0, 0)),
        ],
        out_shape=[
            jax.ShapeDtypeStruct((B, T, W), BF16),
            jax.ShapeDtypeStruct((B, 1, W), F32),
            jax.ShapeDtypeStruct((B, CONV_WIDTH - 1, W), F32),
        ],
        scratch_shapes=[
            pltpu.VMEM((bb, tt + 8, W), F32),
            pltpu.VMEM((bb, 1, W), F32),
            pltpu.VMEM((tt, W), F32),
            pltpu.VMEM((tt, W), F32),
        ],
        compiler_params=_cparams(("arbitrary", "arbitrary")),
        name="lru",
    )(xy3, xy3, st_conv, st_lru, P['conv_w'], P['conv_b'], P['gaw_bd'], P['gate_a_b'], P['gxw_bd'],
      P['gate_x_b'], P['lru_lambda'])


def _block_diag(x_bf16, mask_ref):
    return jnp.concatenate([x_bf16] * GROUP_HEADS, axis=0) * mask_ref[...]


def _rows(x, b, n):
    return x[b * n:(b + 1) * n]


def _dot_split3(sel_bf16, x):
    hi = x.astype(BF16)
    r1 = x - hi.astype(F32)
    mid = r1.astype(BF16)
    lo = (r1 - mid.astype(F32)).astype(BF16)
    return _dot(sel_bf16, hi) + _dot(sel_bf16, mid) + _dot(sel_bf16, lo)


def _wkv_kernel(ps_ref, sh_ref, s0_ref, mu_ref, w0_ref, lora_ref, gup_ref, a0_ref, kk_ref, ka_ref, rk_ref,
                lnw_ref, lnb_ref, tri_ref, endm_ref, ones_ref, mrow_ref, msq_ref,
                y_ref, nsh_ref, ns_ref, car_s, st_s, *, nb, seq, zero_state):
    C = CHUNK
    Q = C // seq
    R = nb * C
    o = RWKV_WIDTH
    ci = pl.program_id(1)
    last = ci == pl.num_programs(1) - 1
    t_idx = lax.broadcasted_iota(jnp.int32, (C, MXU_DIM), 0)
    i_idx = lax.broadcasted_iota(jnp.int32, (C, MXU_DIM), 1) % C
    same = (t_idx // seq) == (i_idx // seq)
    m_strict = same & (i_idx < t_idx)
    m_incl = same & (i_idx <= t_idx)
    eye_cat = (i_idx == t_idx).astype(F32)
    ones_bd = ones_ref[...]
    msq_f = msq_ref[...].astype(F32)
    n_lane_tiles = o // MXU_DIM

    def head_sum(x):
        xs = jnp.concatenate([x[:, g * MXU_DIM:(g + 1) * MXU_DIM] for g in range(n_lane_tiles)], axis=0)
        hi = xs.astype(BF16)
        lo = (xs - hi.astype(F32)).astype(BF16)
        s = _dot(jnp.concatenate([hi, lo], axis=0), ones_bd)
        s = s[:n_lane_tiles * R] + s[n_lane_tiles * R:]
        return jnp.concatenate([_rows(s, g, R) for g in range(n_lane_tiles)], axis=1)

    @pl.when(ci == 0)
    def _():
        for b in range(nb):
            car_s[b] = sh_ref[b]
            for q in range(Q):
                if zero_state:
                    st_s[b, q] = jnp.zeros(st_s.shape[2:], F32)
                else:
                    for g in range(N_GROUPS):
                        heads = [s0_ref[b, q * RWKV_HEADS + g * GROUP_HEADS + hh] for hh in range(GROUP_HEADS)]
                        stacked = jnp.concatenate(heads, axis=0)
                        st_s[b, q, g] = jnp.concatenate([stacked] * GROUP_HEADS, axis=1) * msq_f

    row_c = lax.broadcasted_iota(jnp.int32, (C, 1), 0)
    prevs = []
    for b in range(nb):
        ps_b = ps_ref[b]
        if Q == 1:
            first = car_s[b]
            new_carry = ps_b[C - 1:C, :]
        else:
            put = (lax.broadcasted_iota(jnp.int32, (C, Q), 0)
                   == seq * lax.broadcasted_iota(jnp.int32, (C, Q), 1)).astype(BF16)
            take = (lax.broadcasted_iota(jnp.int32, (Q, C), 1)
                    == seq * lax.broadcasted_iota(jnp.int32, (Q, C), 0) + (seq - 1)).astype(BF16)
            first = _dot_split3(put, car_s[b])
            new_carry = _dot_split3(take, ps_b)
        prevs.append(jnp.where(row_c % seq == 0, first, pltpu.roll(ps_b, 1, axis=0)))
        car_s[b] = new_carry
        nsh_ref[b] = new_carry
    ps = jnp.concatenate([ps_ref[b] for b in range(nb)], axis=0)
    prev = jnp.concatenate(prevs, axis=0)

    s = ps + (prev - ps) * mu_ref[...]
    r = s[:, :o]
    k = s[:, o:2 * o]
    v = s[:, 2 * o:3 * o]
    dwa = s[:, 3 * o:3 * o + LORA_W + LORA_A]
    dg = s[:, 3 * o + LORA_W + LORA_A:]
    lane = lax.broadcasted_iota(jnp.int32, dwa.shape, 1)
    lora_in = jnp.where(lane < LORA_W, jnp.tanh(dwa), dwa).astype(BF16)
    lora = _dot(lora_in, lora_ref[...])
    w_log = -_softplus(-(w0_ref[...] + lora[:, :o])) - 0.5
    lw = -jnp.exp(w_log)
    a = _sigmoid(a0_ref[...] + lora[:, o:])
    g = _dot(_sigmoid(dg).astype(BF16), gup_ref[...])
    kk = k * kk_ref[...]
    kk = kk / jnp.maximum(jnp.sqrt(head_sum(kk * kk)), 1e-12)
    kmod = k * (1.0 + (a - 1.0) * ka_ref[...])
    beta = kk * a
    lw_hi = lw.astype(BF16)
    lw_lo = (lw - lw_hi.astype(F32)).astype(BF16)
    L = _dot(tri_ref[...], lw_hi) + _dot(tri_ref[...], lw_lo)
    l_end = _dot(endm_ref[...], lw_hi) + _dot(endm_ref[...], lw_lo)
    e_neg = jnp.exp(-L)
    e_end = jnp.exp(l_end - L)
    p_end = jnp.exp(l_end)
    at = (-kk) * jnp.exp(L - lw)
    rt = r * jnp.exp(L)
    bt = (beta * e_neg).astype(BF16)
    kt = (kmod * e_neg).astype(BF16)
    bend = beta * e_end
    kend = kmod * e_end

    chains = []
    for b in range(nb):
        for gi in range(N_GROUPS):
            sl = slice(gi * GROUP_LANES, (gi + 1) * GROUP_LANES)
            chains.append(dict(
                b=b, gi=gi, vg=_rows(v, b, C)[:, sl].astype(BF16), p_end=_rows(p_end, b, C)[:, sl],
                x2=jnp.concatenate([_rows(at, b, C)[:, sl], _rows(rt, b, C)[:, sl]], axis=0).astype(BF16),
                wbd=jnp.concatenate([_block_diag(_rows(bt, b, C)[:, sl], mrow_ref),
                                     _block_diag(_rows(kt, b, C)[:, sl], mrow_ref)], axis=0),
                bk=jnp.concatenate([_rows(bend, b, C)[:, sl], _rows(kend, b, C)[:, sl]], axis=0).astype(BF16)))

    cw = GROUP_HEADS * C
    rowseq = (lax.broadcasted_iota(jnp.int32, (2 * C, 1), 0) % C) // seq
    for c in chains:
        res = _dot_tb(c['x2'], c['wbd'])
        n_cat = jnp.where(m_strict, res[:C, :cw], 0.0)
        c['a_ak'] = jnp.where(m_strict, res[:C, cw:], 0.0).astype(BF16)
        c['a_r'] = jnp.concatenate([jnp.where(m_incl, res[C:, :cw], 0.0),
                                    jnp.where(m_incl, res[C:, cw:], 0.0)], axis=1).astype(BF16)
        c['x_c'] = n_cat.astype(BF16)
        c['p_c'] = eye_cat + n_cat
        c['x_bd'] = _block_diag(c['x_c'], msq_ref)
    for c in chains:
        xs = None
        for q in range(Q):
            xq = _dot_tb(c['x2'], st_s[c['b'], q, c['gi']].astype(BF16))
            xs = xq if xs is None else jnp.where(rowseq == q, xq, xs)
        c['xs'] = xs
        c['v_bd'] = _block_diag(c['vg'], msq_ref)
    for c in chains:
        c['rhs'] = c['xs'][:C] + _dot(c['a_ak'], c['v_bd'])
    lvl = 2
    while lvl < seq:
        for c in chains:
            c['x_c'] = _dot(c['x_c'], c['x_bd']).astype(BF16)
            c['x_bd'] = _block_diag(c['x_c'], msq_ref)
        for c in chains:
            c['p_c'] = c['p_c'] + _dot(c['p_c'].astype(BF16), c['x_bd'])
        lvl *= 2
    for c in chains:
        c['u'] = _dot(c['p_c'].astype(BF16), _block_diag(c['rhs'].astype(BF16), msq_ref)).astype(BF16)
    ys = {}
    for c in chains:
        b, gi = c['b'], c['gi']
        ys[(b, gi)] = c['xs'][C:] + _dot(c['a_r'], jnp.concatenate([_block_diag(c['u'], msq_ref), c['v_bd']], axis=0))
        uv = jnp.concatenate([c['u'], c['vg']], axis=0)
        if Q > 1:
            uv = jnp.concatenate([jnp.where(rowseq == q, uv, jnp.zeros_like(uv)) for q in range(Q)], axis=1)
        ds = _dot_ta(uv, c['bk'])
        for q in range(Q):
            st_s[b, q, gi] = (st_s[b, q, gi] * c['p_end'][q * seq:q * seq + 1, :]
                              + _rows(ds, q, GROUP_LANES) * msq_f)

    yc = jnp.concatenate([jnp.concatenate([ys[(b, gi)] for gi in range(N_GROUPS)], axis=1) for b in range(nb)],
                         axis=0)
    inv_n = 1.0 / RWKV_HEAD
    mean = head_sum(yc) * inv_n
    dlt = yc - mean
    var = head_sum(dlt * dlt) * inv_n
    yn = dlt * lax.rsqrt(var + GN_EPS) * lnw_ref[...] + lnb_ref[...]
    bonus = head_sum(r * kmod * rk_ref[...]) * v
    out = ((yn + bonus) * g).astype(BF16)
    for b in range(nb):
        y_ref[b] = _rows(out, b, C)

    @pl.when(last)
    def _():
        for b in range(nb):
            for q in range(Q):
                for gi in range(N_GROUPS):
                    for hh in range(GROUP_HEADS):
                        ns_ref[b, q * RWKV_HEADS + gi * GROUP_HEADS + hh] = st_s[
                            b, q, gi, hh * RWKV_HEAD:(hh + 1) * RWKV_HEAD, hh * RWKV_HEAD:(hh + 1) * RWKV_HEAD]


def _wkv_consts(nb, seq):
    C = CHUNK
    gh = GROUP_HEADS
    rows = np.arange(nb * C)
    same = (rows[:, None] // seq) == (rows[None, :] // seq)
    tri = (same & (rows[None, :] <= rows[:, None])).astype(np.float32)
    endm = same.astype(np.float32)
    hl = np.arange(MXU_DIM) // RWKV_HEAD
    ones_bd = (hl[:, None] == hl[None, :]).astype(np.float32)
    rowh = np.arange(gh * C) // C
    colh = np.arange(GROUP_LANES) // RWKV_HEAD
    mrow = (rowh[:, None] == colh[None, :]).astype(np.float32)
    msq = (rowh[:, None] == rowh[None, :]).astype(np.float32)
    return tuple(jnp.asarray(m, BF16) for m in (tri, endm, ones_bd, mrow, msq))


def _wkv(ps3, st_shift, st_rwkv, P, layer, nb, zero_state):
    B, T, _ = ps3.shape
    C = CHUNK
    seq = C if T % C == 0 else T
    assert C % seq == 0
    Q = C // seq
    G = B // Q
    tg = T * Q
    assert B % Q == 0 and G % nb == 0 and tg % C == 0
    o = RWKV_WIDTH
    consts = _wkv_consts(nb, seq)
    vec = lambda w: pl.BlockSpec((None, 1, w), lambda b, c: (layer, 0, 0))
    const = lambda arr: pl.BlockSpec(arr.shape, lambda b, c: (0,) * arr.ndim)
    kern = functools.partial(_wkv_kernel, nb=nb, seq=seq, zero_state=zero_state)
    shift_spec = pl.BlockSpec((nb, Q, SHIFT_WIDTH), lambda b, c: (b, 0, 0))
    state_spec = pl.BlockSpec((nb, Q * RWKV_HEADS, RWKV_HEAD, RWKV_HEAD), lambda b, c: (b, 0, 0, 0))
    y, nsh, ns = pl.pallas_call(
        kern,
        grid=(G // nb, tg // C),
        in_specs=[
            pl.BlockSpec((nb, C, SHIFT_WIDTH), lambda b, c: (b, c, 0)),
            shift_spec,
            pl.BlockSpec((nb, Q * RWKV_HEADS, RWKV_HEAD, RWKV_HEAD), lambda b, c: (b, 0, 0, 0),
                         pipeline_mode=pl.Buffered(1)),
            vec(SHIFT_WIDTH),
            vec(o),
            pl.BlockSpec((None, LORA_W + LORA_A, 2 * o), lambda b, c: (layer, 0, 0)),
            pl.BlockSpec((None, LORA_G, o), lambda b, c: (layer, 0, 0)),
            vec(o), vec(o), vec(o), vec(o), vec(o), vec(o),
        ] + [const(m) for m in consts],
        out_specs=[
            pl.BlockSpec((nb, C, o), lambda b, c: (b, c, 0)),
            shift_spec,
            state_spec,
        ],
        out_shape=[
            jax.ShapeDtypeStruct((G, tg, o), BF16),
            jax.ShapeDtypeStruct((G, Q, SHIFT_WIDTH), F32),
            jax.ShapeDtypeStruct((G, Q * RWKV_HEADS, RWKV_HEAD, RWKV_HEAD), F32),
        ],
        scratch_shapes=[
            pltpu.VMEM((nb, Q, SHIFT_WIDTH), F32),
            pltpu.VMEM((nb, Q, N_GROUPS, MXU_DIM, MXU_DIM), F32),
        ],
        compiler_params=_cparams(("arbitrary", "arbitrary")),
        name="rwkv",
    )(ps3.reshape(G, tg, SHIFT_WIDTH), st_shift.reshape(G, Q, SHIFT_WIDTH),
      st_rwkv.reshape(G, Q * RWKV_HEADS, RWKV_HEAD, RWKV_HEAD),
      P['shift_mu'], P['w0'], P['lora_wa'], P['g_lora_up'], P['a0'], P['k_k'], P['k_a'],
      P['r_k'], P['ln_x_w'], P['ln_x_b'], *consts)
    return (y.reshape(B, T, o), nsh.reshape(B, SHIFT_WIDTH),
            ns.reshape(B, RWKV_HEADS, RWKV_HEAD, RWKV_HEAD))


def _merge_kernel(x_ref, ya_ref, yb_ref, gt_ref, wa_ref, wb_ref, wo_ref, nf_ref, *rest, moe):
    if moe:
        wr_ref, xo_ref, gates_ref, cnt_ref = rest
    else:
        xo_ref, h_ref = rest
    ga = gt_ref[:, :D_MODEL]
    gb = gt_ref[:, D_MODEL:]
    m = _sigmoid(ga) * _dot(ya_ref[...], wa_ref[...]) + _sigmoid(gb) * _dot(yb_ref[...], wb_ref[...])
    x = x_ref[...] + _dot(m.astype(BF16), wo_ref[...])
    xo_ref[...] = x
    h = _rms(x, nf_ref[...])
    h_hi = h.astype(BF16)
    if not moe:
        h_ref[...] = h_hi
    else:
        w = wr_ref[...]
        w_hi = w.astype(BF16)
        w_lo = (w - w_hi.astype(F32)).astype(BF16)
        h_lo = (h - h_hi.astype(F32)).astype(BF16)
        logits = _dot(h_hi, w_hi) + _dot(h_lo, w_hi) + _dot(h_hi, w_lo)
        lane = lax.broadcasted_iota(jnp.int32, logits.shape, 1)
        real = lane < N_EXPERTS
        logits = jnp.where(real, logits, -jnp.inf)
        e = jnp.exp(logits - jnp.max(logits, axis=-1, keepdims=True))
        p = jnp.where(real, e / jnp.sum(e, axis=-1, keepdims=True), -1.0)
        m1 = jnp.max(p, axis=-1, keepdims=True)
        i1 = jnp.min(jnp.where(p == m1, lane, LANES), axis=-1, keepdims=True)
        oh1 = lane == i1
        p2 = jnp.where(oh1, -1.0, p)
        m2 = jnp.max(p2, axis=-1, keepdims=True)
        i2 = jnp.min(jnp.where(p2 == m2, lane, LANES), axis=-1, keepdims=True)
        oh2 = lane == i2
        tot = m1 + m2
        gates = jnp.where(oh1, m1 / tot, 0.0) + jnp.where(oh2, m2 / tot, 0.0)
        gates_ref[...] = gates
        cnt = jnp.sum((gates > 0.0).astype(F32), axis=0, keepdims=True)
        cnt_ref[...] = jnp.broadcast_to(cnt, (SUBLANES, LANES))


def _merge(x, ya, yb, gt, P, layer, tm, moe):
    n = x.shape[0]
    row = lambda w: pl.BlockSpec((tm, w), lambda i: (i, 0))
    wsq = lambda: pl.BlockSpec((None, D_MODEL, D_MODEL), lambda i: (layer, 0, 0))
    in_specs = [row(D_MODEL), row(D_MODEL), row(D_MODEL), row(2 * D_MODEL), wsq(), wsq(), wsq(),
                pl.BlockSpec((None, 1, D_MODEL), lambda i: (layer, 0, 0))]
    args = [x, ya, yb, gt, P['w_out_a'], P['w_out_b'], P['w_out'], P['norm_ffn']]
    out_specs = [row(D_MODEL)]
    out_shape = [jax.ShapeDtypeStruct((n, D_MODEL), F32)]
    if moe:
        in_specs.append(pl.BlockSpec((None, D_MODEL, LANES), lambda i: (layer // 2, 0, 0)))
        args.append(P['w_router_pad'])
        out_specs.append(row(LANES))
        out_shape.append(jax.ShapeDtypeStruct((n, LANES), F32))
        out_specs.append(pl.BlockSpec((None, SUBLANES, LANES), lambda i: (i, 0, 0)))
        out_shape.append(jax.ShapeDtypeStruct((n // tm, SUBLANES, LANES), F32))
    else:
        out_specs.append(row(D_MODEL))
        out_shape.append(jax.ShapeDtypeStruct((n, D_MODEL), BF16))
    return pl.pallas_call(
        functools.partial(_merge_kernel, moe=moe),
        grid=(n // tm,),
        in_specs=in_specs,
        out_specs=out_specs,
        out_shape=out_shape,
        compiler_params=_cparams(("arbitrary",)),
        name="merge",
    )(*args)


def _swiglu_acc(h, wg_ref, wu_ref, wd_ref):
    acc = None
    for c0 in range(0, D_FF, FF_CHUNK):
        sl = slice(c0, c0 + FF_CHUNK)
        gate = _dot(h, wg_ref[:, sl])
        up = _dot(h, wu_ref[:, sl])
        act = (gate * _sigmoid(gate) * up).astype(BF16)
        part = _dot(act, wd_ref[sl, :])
        acc = part if acc is None else acc + part
    return acc


def _ffn_kernel(x_ref, h_ref, wg_ref, wu_ref, wd_ref, o_ref):
    o_ref[...] = x_ref[...] + _swiglu_acc(h_ref[...], wg_ref, wu_ref, wd_ref)


def _ffn(x, h, P, j, tm):
    n = x.shape[0]
    row = lambda: pl.BlockSpec((tm, D_MODEL), lambda i: (i, 0))
    return pl.pallas_call(
        _ffn_kernel,
        grid=(n // tm,),
        in_specs=[row(), row(),
                  pl.BlockSpec((None, D_MODEL, D_FF), lambda i: (j, 0, 0), pipeline_mode=pl.Buffered(1)),
                  pl.BlockSpec((None, D_MODEL, D_FF), lambda i: (j, 0, 0), pipeline_mode=pl.Buffered(1)),
                  pl.BlockSpec((None, D_FF, D_MODEL), lambda i: (j, 0, 0), pipeline_mode=pl.Buffered(1))],
        out_specs=row(),
        out_shape=jax.ShapeDtypeStruct((n, D_MODEL), F32),
        compiler_params=_cparams(("arbitrary",)),
        name="ffn",
    )(x, h, P['w_ffn_gate'], P['w_ffn_up'], P['w_ffn_down'])


def _route_kernel(gates_ref, base_ref, pos_ref, *, tw, spare):
    mask = (gates_ref[...] > 0.0).astype(BF16)
    r_idx = lax.broadcasted_iota(jnp.int32, (tw, tw), 0)
    c_idx = lax.broadcasted_iota(jnp.int32, (tw, tw), 1)
    excl = _dot_ta(mask, (r_idx < c_idx).astype(BF16))
    sel = _dot_ta(mask, (r_idx == c_idx).astype(BF16)) > 0.0
    pos = excl + base_ref[:, 0:1]
    e_io = lax.broadcasted_iota(jnp.int32, (LANES, tw), 0)
    e_first = jnp.min(jnp.where(sel, e_io, LANES), axis=0, keepdims=True)
    e_last = jnp.max(jnp.where(sel, e_io, -1), axis=0, keepdims=True)
    p_first = jnp.sum(jnp.where(e_io == e_first, pos, 0.0), axis=0, keepdims=True)
    p_last = jnp.sum(jnp.where(e_io == e_last, pos, 0.0), axis=0, keepdims=True)
    p_last = jnp.where(e_last == e_first, float(spare), p_last)
    rows = jnp.concatenate([p_first, p_last, jnp.zeros((SUBLANES - 2, tw), F32)], axis=0)
    pos_ref[...] = rows.astype(jnp.int32)


def _route(gates, base_b, tw, spare):
    n = gates.shape[0]
    return pl.pallas_call(
        functools.partial(_route_kernel, tw=tw, spare=spare),
        grid=(n // tw,),
        in_specs=[pl.BlockSpec((tw, LANES), lambda i: (i, 0)),
                  pl.BlockSpec((None, LANES, LANES), lambda i: (i, 0, 0))],
        out_specs=pl.BlockSpec((None, SUBLANES, tw), lambda i: (i, 0, 0)),
        out_shape=jax.ShapeDtypeStruct((n // tw, SUBLANES, tw), jnp.int32),
        compiler_params=_cparams(("arbitrary",)),
        name="route",
    )(gates, base_b)


def _row_copy(src_hbm, src_row, dst_ref, dst_row, sem):
    return pltpu.make_async_copy(src_hbm.at[pl.ds(src_row, 1), :], dst_ref.at[pl.ds(dst_row, 1), :], sem)


def _disperse_kernel(pos_ref, x_ref, init_hbm, xs_hbm, sem, *, tw):
    del init_hbm

    def issue(n, carry):
        _row_copy(x_ref, n, xs_hbm, pos_ref[0, n], sem).start()
        _row_copy(x_ref, n, xs_hbm, pos_ref[1, n], sem).start()
        return carry

    def drain(n, carry):
        _row_copy(x_ref, 0, xs_hbm, 0, sem).wait()
        _row_copy(x_ref, 0, xs_hbm, 0, sem).wait()
        return carry

    lax.fori_loop(0, tw, issue, 0, unroll=8)
    lax.fori_loop(0, tw, drain, 0, unroll=8)


def _disperse(pos, x, n_rows, tw):
    n = x.shape[0]
    return pl.pallas_call(
        functools.partial(_disperse_kernel, tw=tw),
        grid=(n // tw,),
        in_specs=[pl.BlockSpec((None, SUBLANES, tw), lambda i: (i, 0, 0), memory_space=pltpu.SMEM),
                  pl.BlockSpec((tw, D_MODEL), lambda i: (i, 0)),
                  pl.BlockSpec(memory_space=pl.ANY)],
        out_specs=pl.BlockSpec(memory_space=pl.ANY),
        out_shape=jax.ShapeDtypeStruct((n_rows, D_MODEL), F32),
        scratch_shapes=[pltpu.SemaphoreType.DMA],
        input_output_aliases={2: 0},
        compiler_params=_cparams(("arbitrary",)),
        name="disperse",
    )(pos, x, jnp.zeros((n_rows, D_MODEL), F32))


def _experts_kernel(te_ref, used_ref, xs_ref, nf_ref, wg_ref, wu_ref, wd_ref, ys_ref):
    del te_ref
    j = pl.program_id(0)

    @pl.when(j < used_ref[0])
    def _():
        h = _rms(xs_ref[...], nf_ref[...]).astype(BF16)
        ys_ref[...] = _swiglu_acc(h, wg_ref, wu_ref, wd_ref)

    @pl.when(j >= used_ref[0])
    def _():
        ys_ref[...] = jnp.zeros(ys_ref.shape, F32)


def _experts(tile_expert, used, xs, P, layer, j):
    n_tiles = tile_expert.shape[0]
    tr = ROUTE_ROWS
    wspec = lambda a, b: pl.BlockSpec((None, None, a, b), lambda t, te, u: (j, te[t], 0, 0))
    grid_spec = pltpu.PrefetchScalarGridSpec(
        num_scalar_prefetch=2,
        grid=(n_tiles,),
        in_specs=[pl.BlockSpec((tr, D_MODEL), lambda t, te, u: (t, 0)),
                  pl.BlockSpec((None, 1, D_MODEL), lambda t, te, u: (layer, 0, 0)),
                  wspec(D_MODEL, D_FF), wspec(D_MODEL, D_FF), wspec(D_FF, D_MODEL)],
        out_specs=pl.BlockSpec((tr, D_MODEL), lambda t, te, u: (t, 0)),
    )
    return pl.pallas_call(
        _experts_kernel,
        grid_spec=grid_spec,
        out_shape=jax.ShapeDtypeStruct((n_tiles * tr, D_MODEL), F32),
        compiler_params=_cparams(("arbitrary",)),
        name="experts",
    )(tile_expert, used, xs, P['norm_ffn'], P['w_moe_gate'], P['w_moe_up'], P['w_moe_down'])


def _combine_kernel(pos_ref, x_ref, gates_ref, ys_hbm, o_ref, y1_s, y2_s, sem, *, tw):
    def issue(n, carry):
        _row_copy(ys_hbm, pos_ref[0, n], y1_s, n, sem).start()
        _row_copy(ys_hbm, pos_ref[1, n], y2_s, n, sem).start()
        return carry

    def drain(n, carry):
        _row_copy(ys_hbm, 0, y1_s, 0, sem).wait()
        _row_copy(ys_hbm, 0, y2_s, 0, sem).wait()
        return carry

    lax.fori_loop(0, tw, issue, 0, unroll=8)
    gates = gates_ref[...]
    lane = lax.broadcasted_iota(jnp.int32, gates.shape, 1)
    sel = gates > 0.0
    e_first = jnp.min(jnp.where(sel, lane, LANES), axis=-1, keepdims=True)
    e_last = jnp.max(jnp.where(sel, lane, -1), axis=-1, keepdims=True)
    g_first = jnp.sum(jnp.where(lane == e_first, gates, 0.0), axis=-1, keepdims=True)
    g_last = jnp.sum(jnp.where((lane == e_last) & (e_last != e_first), gates, 0.0), axis=-1, keepdims=True)
    lax.fori_loop(0, tw, drain, 0, unroll=8)
    o_ref[...] = x_ref[...] + g_first * y1_s[...] + g_last * y2_s[...]


def _combine(pos, x, gates, ys, tw):
    n = x.shape[0]
    return pl.pallas_call(
        functools.partial(_combine_kernel, tw=tw),
        grid=(n // tw,),
        in_specs=[pl.BlockSpec((None, SUBLANES, tw), lambda i: (i, 0, 0), memory_space=pltpu.SMEM),
                  pl.BlockSpec((tw, D_MODEL), lambda i: (i, 0)),
                  pl.BlockSpec((tw, LANES), lambda i: (i, 0)),
                  pl.BlockSpec(memory_space=pl.ANY)],
        out_specs=pl.BlockSpec((tw, D_MODEL), lambda i: (i, 0)),
        out_shape=jax.ShapeDtypeStruct((n, D_MODEL), F32),
        scratch_shapes=[pltpu.VMEM((tw, D_MODEL), F32), pltpu.VMEM((tw, D_MODEL), F32), pltpu.SemaphoreType.DMA],
        compiler_params=_cparams(("arbitrary",)),
        name="combine",
    )(pos, x, gates, ys)


def _moe_routed(x, gates, cnt, P, layer, tm):
    n = x.shape[0]
    tw = min(ROUTE_TILE, n)
    tr = ROUTE_ROWS
    n_tiles = (TOP_K * n) // tr + N_EXPERTS
    spare = n_tiles * tr
    c = cnt[:, 0, :].astype(jnp.int32).reshape(n // tw, tw // tm, LANES).sum(axis=1)
    seg_tiles = (c.sum(axis=0) + tr - 1) // tr
    seg_end = jnp.cumsum(seg_tiles)
    base = (seg_end - seg_tiles) * tr + jnp.cumsum(c, axis=0) - c
    base_b = jnp.broadcast_to(base.astype(F32)[:, :, None], (n // tw, LANES, LANES))
    tile_expert = jnp.minimum((jnp.arange(n_tiles + 1)[:, None] >= seg_end[None, :N_EXPERTS]).sum(axis=1),
                              N_EXPERTS - 1).astype(jnp.int32)
    used = seg_end[N_EXPERTS - 1:N_EXPERTS].astype(jnp.int32)
    pos = _route(gates, base_b, tw, spare)
    xs = _disperse(pos, x, spare + tr, tw)
    ys = _experts(tile_expert, used, xs, P, layer, layer // 2)
    return _combine(pos, x, gates, ys, tw)


def _final_norm_kernel(x_ref, g_ref, o_ref):
    o_ref[...] = _rms(x_ref[...], g_ref[...])


def _final_norm(x, g, tm):
    n = x.shape[0]
    return pl.pallas_call(
        _final_norm_kernel,
        grid=(n // tm,),
        in_specs=[pl.BlockSpec((tm, D_MODEL), lambda i: (i, 0)), pl.BlockSpec((1, D_MODEL), lambda i: (0, 0))],
        out_specs=pl.BlockSpec((tm, D_MODEL), lambda i: (i, 0)),
        out_shape=jax.ShapeDtypeStruct((n, D_MODEL), F32),
        compiler_params=_cparams(("arbitrary",)),
        name="final_norm",
    )(x, g)


def _block_diag_weights(w):
    d = w.shape[0]
    per = MXU_DIM // LRU_BLOCK
    w = w.reshape(d, LRU_HEADS // per, per, LRU_BLOCK, LRU_BLOCK)
    eye = jnp.eye(per, dtype=w.dtype)
    bd = jnp.einsum('dcpij,pq->dcpiqj', w, eye)
    return bd.reshape(d, LRU_HEADS // per, MXU_DIM, MXU_DIM).astype(BF16)


def _prep_params(p):
    P = dict(p)
    for name in ('w_in', 'w_out_a', 'w_out_b', 'w_out', 'w_ffn_gate', 'w_ffn_up', 'w_ffn_down',
                 'w_moe_gate', 'w_moe_up', 'w_moe_down', 'g_lora_up'):
        P[name] = p[name].astype(BF16)
    for name in ('norm_mix', 'conv_b', 'gate_a_b', 'gate_x_b', 'lru_lambda', 'shift_mu', 'w0', 'a0', 'k_k', 'k_a',
                 'ln_x_w', 'ln_x_b', 'norm_ffn'):
        P[name] = p[name][:, None, :]
    P['r_k'] = p['r_k'].reshape(DEPTH, 1, RWKV_WIDTH)
    P['norm_final'] = p['norm_final'][None, :]
    P['gaw_bd'] = _block_diag_weights(p['gate_a_w'])
    P['gxw_bd'] = _block_diag_weights(p['gate_x_w'])
    z = jnp.zeros((DEPTH, LORA_W, RWKV_WIDTH), F32)
    P['lora_wa'] = jnp.concatenate([jnp.concatenate([p['w_lora_up'], z], axis=2),
                                    jnp.concatenate([z, p['a_lora_up']], axis=2)], axis=1).astype(BF16)
    P['w_router_pad'] = jnp.pad(p['w_router'], ((0, 0), (0, 0), (0, LANES - N_EXPERTS)))
    return P


def _trunk(x3, st_rwkv, st_lru, st_conv, st_shift, P, *, zero_state, tm, lru_bb, lru_tt, rwkv_nb):
    B, T, _ = x3.shape
    n = B * T
    x = x3.reshape(n, D_MODEL)
    n_rwkv, n_lru, n_conv, n_shift = [], [], [], []
    for l in range(DEPTH):
        xy, ps, gt = _proj(x, P['norm_mix'], P['w_in'], l, min(tm, 256))
        ya, h_new, c_new = _lru(xy.reshape(B, T, 2 * LRU_WIDTH), st_conv[l], st_lru[l][:, None, :], P, l,
                                lru_bb, lru_tt)
        yb, sh_new, s_new = _wkv(ps.reshape(B, T, SHIFT_WIDTH), st_shift[l], st_rwkv[l], P, l, rwkv_nb, zero_state)
        moe = l % 2 == 1
        res = _merge(x, ya.reshape(n, D_MODEL), yb.reshape(n, D_MODEL), gt, P, l, tm, moe)
        if moe:
            x, gates, cnt = res
            x = _moe_routed(x, gates, cnt, P, l, tm)
        else:
            x, h = res
            x = _ffn(x, h, P, l // 2, tm)
        n_rwkv.append(s_new)
        n_lru.append(h_new[:, 0, :])
        n_conv.append(c_new)
        n_shift.append(sh_new)
    y = _final_norm(x, P['norm_final'], tm).reshape(B, T, D_MODEL)
    return y, jnp.stack(n_rwkv), jnp.stack(n_lru), jnp.stack(n_conv), jnp.stack(n_shift)


def kernel(x_prompt, x_sample, state_rwkv, state_lru, state_conv, state_shift, norm_mix, w_in, conv_w, conv_b, gate_a_w, gate_a_b, gate_x_w, gate_x_b, lru_lambda, shift_mu, w0, w_lora_up, a0, a_lora_up, g_lora_up, k_k, k_a, r_k, ln_x_w, ln_x_b, w_out_a, w_out_b, w_out, norm_ffn, w_ffn_gate, w_ffn_up, w_ffn_down, w_router, w_moe_gate, w_moe_up, w_moe_down, norm_final):
    P = _prep_params(dict(
        norm_mix=norm_mix, w_in=w_in, conv_w=conv_w, conv_b=conv_b, gate_a_w=gate_a_w, gate_a_b=gate_a_b,
        gate_x_w=gate_x_w, gate_x_b=gate_x_b, lru_lambda=lru_lambda, shift_mu=shift_mu, w0=w0,
        w_lora_up=w_lora_up, a0=a0, a_lora_up=a_lora_up, g_lora_up=g_lora_up, k_k=k_k, k_a=k_a, r_k=r_k,
        ln_x_w=ln_x_w, ln_x_b=ln_x_b, w_out_a=w_out_a, w_out_b=w_out_b, w_out=w_out, norm_ffn=norm_ffn,
        w_ffn_gate=w_ffn_gate, w_ffn_up=w_ffn_up, w_ffn_down=w_ffn_down, w_router=w_router,
        w_moe_gate=w_moe_gate, w_moe_up=w_moe_up, w_moe_down=w_moe_down, norm_final=norm_final))
    bp = x_prompt.shape[0]
    bs = x_sample.shape[0]
    zeros = lambda *shape: jnp.zeros(shape, F32)
    p_out = _trunk(x_prompt,
                   zeros(DEPTH, bp, RWKV_HEADS, RWKV_HEAD, RWKV_HEAD), zeros(DEPTH, bp, LRU_WIDTH),
                   zeros(DEPTH, bp, CONV_WIDTH - 1, LRU_WIDTH), zeros(DEPTH, bp, SHIFT_WIDTH), P,
                   zero_state=True, tm=512, lru_bb=1, lru_tt=256, rwkv_nb=4)
    s_out = _trunk(x_sample, state_rwkv, state_lru, state_conv, state_shift, P,
                   zero_state=False, tm=256, lru_bb=8, lru_tt=x_sample.shape[1], rwkv_nb=1)
    return (p_out[0], s_out[0]) + tuple(p_out[1:]) + tuple(s_out[1:])
```

```python
import functools

import numpy as np
import jax
import jax.numpy as jnp
from jax import lax
from jax.experimental import pallas as pl
from jax.experimental.pallas import tpu as pltpu

F32 = jnp.float32
BF16 = jnp.bfloat16

D_MODEL = 1024
DEPTH = 4
LRU_WIDTH = D_MODEL
LRU_HEADS = 16
LRU_BLOCK = LRU_WIDTH // LRU_HEADS
CONV_WIDTH = 4
LRU_C = 8.0
RWKV_HEAD = 64
RWKV_WIDTH = D_MODEL
RWKV_HEADS = RWKV_WIDTH // RWKV_HEAD
LORA_W = 64
LORA_A = 64
LORA_G = 128
SHIFT_WIDTH = 3 * RWKV_WIDTH + LORA_W + LORA_A + LORA_G
PROJ_WIDTH = 2 * LRU_WIDTH + SHIFT_WIDTH + 2 * D_MODEL
D_FF = 2816
N_EXPERTS = 8
NORM_EPS = 1e-6
GN_EPS = 64e-5

LANES = 128
SUBLANES = 8
MXU_DIM = 256
VMEM_LIMIT = 56 * 1024 * 1024

CHUNK = 64
GROUP_HEADS = MXU_DIM // CHUNK
GROUP_LANES = GROUP_HEADS * RWKV_HEAD
N_GROUPS = RWKV_HEADS // GROUP_HEADS
FF_CHUNK = 256
TOP_K = 2
ROUTE_TILE = 512
ROUTE_ROWS = 512
ROUTE_ROWS_SMALL = 128
SCAN_ROWS = 16
assert CHUNK == RWKV_HEAD


def _cparams(sem):
    return pltpu.CompilerParams(dimension_semantics=sem, vmem_limit_bytes=VMEM_LIMIT)


def _dot(a, b):
    return jnp.dot(a, b, preferred_element_type=F32)


def _dot_tb(a, b):
    return lax.dot_general(a, b, (((1,), (1,)), ((), ())), preferred_element_type=F32)


def _dot_ta(a, b):
    return lax.dot_general(a, b, (((0,), (0,)), ((), ())), preferred_element_type=F32)


def _sigmoid(x):
    return jax.nn.sigmoid(x)


def _softplus(z):
    return jnp.maximum(z, 0.0) + jnp.log(1.0 + jnp.exp(-jnp.abs(z)))


def _rms(x, g):
    return x * lax.rsqrt(jnp.mean(x * x, axis=-1, keepdims=True) + NORM_EPS) * g


_PROJ_SPLITS = ((0, 2 * LRU_WIDTH), (2 * LRU_WIDTH, SHIFT_WIDTH), (2 * LRU_WIDTH + SHIFT_WIDTH, 2 * D_MODEL))


def _proj_kernel(x_ref, g_ref, w_ref, xy_ref, ps_ref, gt_ref):
    u = _rms(x_ref[...], g_ref[...]).astype(BF16)
    for out_ref, (c0, width) in zip((xy_ref, ps_ref, gt_ref), _PROJ_SPLITS):
        j = 0
        while j < width:
            w = min(512, width - j)
            out_ref[:, j:j + w] = _dot(u, w_ref[:, c0 + j:c0 + j + w])
            j += w


def _proj(x, g_all, w_all, layer, tm):
    n = x.shape[0]
    return pl.pallas_call(
        _proj_kernel,
        grid=(n // tm,),
        in_specs=[
            pl.BlockSpec((tm, D_MODEL), lambda i: (i, 0)),
            pl.BlockSpec((None, 1, D_MODEL), lambda i: (layer, 0, 0)),
            pl.BlockSpec((None, D_MODEL, PROJ_WIDTH), lambda i: (layer, 0, 0),
                         pipeline_mode=pl.Buffered(1)),
        ],
        out_specs=[
            pl.BlockSpec((tm, 2 * LRU_WIDTH), lambda i: (i, 0)),
            pl.BlockSpec((tm, SHIFT_WIDTH), lambda i: (i, 0)),
            pl.BlockSpec((tm, 2 * D_MODEL), lambda i: (i, 0)),
        ],
        out_shape=[
            jax.ShapeDtypeStruct((n, 2 * LRU_WIDTH), F32),
            jax.ShapeDtypeStruct((n, SHIFT_WIDTH), F32),
            jax.ShapeDtypeStruct((n, 2 * D_MODEL), F32),
        ],
        compiler_params=_cparams(("arbitrary",)),
        name="proj",
    )(x, g_all, w_all)


def _gelu_tanh(x):
    return 0.5 * x * (1.0 + jnp.tanh(np.sqrt(2.0 / np.pi).astype(np.float32) * (x + 0.044715 * (x * x * x))))


def _lru_kernel(xa_ref, ya_ref, cs_ref, h0_ref, cw_ref, cb_ref, gaw_ref, gab_ref, gxw_ref, gxb_ref, lam_ref,
                y_ref, hl_ref, nc_ref, xp_s, h_s, a_s, b_s, *, bb, tt):
    ti = pl.program_id(1)
    neg_c = -LRU_C * _softplus(-lam_ref[...])
    cw = cw_ref[...]
    @pl.when(ti == 0)
    def _():
        for b in range(bb):
            xp_s[b, 5:8, :] = cs_ref[b]
            h_s[b] = h0_ref[b]

    for b in range(bb):
        xa = xa_ref[b]
        xp_s[b, 8:8 + tt, :] = xa
        xc = cb_ref[...] + xp_s[b, 5:5 + tt, :] * cw[0:1]
        xc = xc + xp_s[b, 6:6 + tt, :] * cw[1:2]
        xc = xc + xp_s[b, 7:7 + tt, :] * cw[2:3]
        xc = xc + xa * cw[3:4]
        tail = xp_s[b, 5 + tt:8 + tt, :]
        xp_s[b, 5:8, :] = tail
        nc_ref[b] = tail

        for c in range(LRU_WIDTH // MXU_DIM):
            sl = slice(c * MXU_DIM, (c + 1) * MXU_DIM)
            xcc = xc[:, sl]
            xcb = xcc.astype(BF16)
            r = _sigmoid(_dot(xcb, gaw_ref[c]) + gab_ref[:, sl])
            i = _sigmoid(_dot(xcb, gxw_ref[c]) + gxb_ref[:, sl])
            log_a = neg_c[:, sl] * r
            a = jnp.exp(log_a)
            a_s[:, sl] = a
            b_s[:, sl] = jnp.sqrt(-jnp.tanh(log_a) * (a * a + 1.0)) * (i * xcc)

        if tt % SCAN_ROWS == 0:
            row = lax.broadcasted_iota(jnp.int32, (SUBLANES, LRU_WIDTH), 0)

            def blk(j, h):
                r0 = pl.multiple_of(j * SCAN_ROWS, SCAN_ROWS)
                hbs = []
                for q in range(SCAN_ROWS // SUBLANES):
                    av = a_s[pl.ds(r0 + q * SUBLANES, SUBLANES), :]
                    bv = b_s[pl.ds(r0 + q * SUBLANES, SUBLANES), :]
                    for d in (1, 2, 4):
                        m = row >= d
                        a_sh = pltpu.roll(av, d, axis=0)
                        b_sh = pltpu.roll(bv, d, axis=0)
                        bv = jnp.where(m, av * b_sh + bv, bv)
                        av = jnp.where(m, av * a_sh, av)
                    hb = av * h + bv
                    h = hb[SUBLANES - 1:SUBLANES, :]
                    hbs.append(hb)
                hs = jnp.concatenate(hbs, axis=0)
                y_ref[b, pl.ds(r0, SCAN_ROWS), :] = (hs * _gelu_tanh(ya_ref[b, pl.ds(r0, SCAN_ROWS), :])).astype(BF16)
                return h

            h = lax.fori_loop(0, tt // SCAN_ROWS, blk, h_s[b])
        else:
            h = h_s[b]
            rows = []
            for t in range(tt):
                h = a_s[t:t + 1, :] * h + b_s[t:t + 1, :]
                rows.append(h)
            hs = jnp.concatenate(rows, axis=0)
            y_ref[b] = (hs * _gelu_tanh(ya_ref[b])).astype(BF16)
        h_s[b] = h
        hl_ref[b] = h


def _lru(xy3, st_conv, st_lru, P, layer, bb, tt):
    B, T, _ = xy3.shape
    W = LRU_WIDTH
    vec = lambda: pl.BlockSpec((None, 1, W), lambda b, t: (layer, 0, 0))
    kern = functools.partial(_lru_kernel, bb=bb, tt=tt)
    return pl.pallas_call(
        kern,
        grid=(B // bb, T // tt),
        in_specs=[
            pl.BlockSpec((bb, tt, W), lambda b, t: (b, t, 0)),
            pl.BlockSpec((bb, tt, W), lambda b, t: (b, t, 1)),
            pl.BlockSpec((bb, CONV_WIDTH - 1, W), lambda b, t: (b, 0, 0)),
            pl.BlockSpec((bb, 1, W), lambda b, t: (b, 0, 0)),
            pl.BlockSpec((None, CONV_WIDTH, W), lambda b, t: (layer, 0, 0)),
            vec(),
            pl.BlockSpec((None, W // MXU_DIM, MXU_DIM, MXU_DIM), lambda b, t: (layer, 0, 0, 0)),
            vec(),
            pl.BlockSpec((None, W // MXU_DIM, MXU_DIM, MXU_DIM), lambda b, t: (layer, 0, 0, 0)),
            vec(),
            vec(),
        ],
        out_specs=[
            pl.BlockSpec((bb, tt, W), lambda b, t: (b, t, 0)),
            pl.BlockSpec((bb, 1, W), lambda b, t: (b, 0, 0)),
            pl.BlockSpec((bb, CONV_WIDTH - 1, W), lambda b, t: (b, 0, 0)),
        ],
        out_shape=[
            jax.ShapeDtypeStruct((B, T, W), BF16),
            jax.ShapeDtypeStruct((B, 1, W), F32),
            jax.ShapeDtypeStruct((B, CONV_WIDTH - 1, W), F32),
        ],
        scratch_shapes=[
            pltpu.VMEM((bb, tt + 8, W), F32),
            pltpu.VMEM((bb, 1, W), F32),
            pltpu.VMEM((tt, W), F32),
            pltpu.VMEM((tt, W), F32),
        ],
        compiler_params=_cparams(("arbitrary", "arbitrary")),
        name="lru",
    )(xy3, xy3, st_conv, st_lru, P['conv_w'], P['conv_b'], P['gaw_bd'], P['gate_a_b'], P['gxw_bd'],
      P['gate_x_b'], P['lru_lambda'])


def _block_diag(x_bf16, mask_ref):
    return jnp.concatenate([x_bf16] * GROUP_HEADS, axis=0) * mask_ref[...]


def _rows(x, b, n):
    return x[b * n:(b + 1) * n]


def _dot_split3(sel_bf16, x):
    hi = x.astype(BF16)
    r1 = x - hi.astype(F32)
    mid = r1.astype(BF16)
    lo = (r1 - mid.astype(F32)).astype(BF16)
    return _dot(sel_bf16, hi) + _dot(sel_bf16, mid) + _dot(sel_bf16, lo)


def _wkv_kernel(ps_ref, sh_ref, s0_ref, mu_ref, w0_ref, lora_ref, gup_ref, a0_ref, kk_ref, ka_ref, rk_ref,
                lnw_ref, lnb_ref, tri_ref, endm_ref, ones_ref, mrow_ref, msq_ref, *rest, nb, seq, zero_state, chained):
    y_ref, nsh_ref, ns_ref, car_s, st_s = rest[1:] if chained else rest
    C = CHUNK
    Q = C // seq
    R = nb * C
    o = RWKV_WIDTH
    ci = pl.program_id(1)
    last = ci == pl.num_programs(1) - 1
    t_idx = lax.broadcasted_iota(jnp.int32, (C, MXU_DIM), 0)
    i_idx = lax.broadcasted_iota(jnp.int32, (C, MXU_DIM), 1) % C
    same = (t_idx // seq) == (i_idx // seq)
    m_strict = same & (i_idx < t_idx)
    m_incl = same & (i_idx <= t_idx)
    eye_cat = (i_idx == t_idx).astype(F32)
    ones_bd = ones_ref[...]
    msq_f = msq_ref[...].astype(F32)
    n_lane_tiles = o // MXU_DIM

    def head_sum(x):
        xs = jnp.concatenate([x[:, g * MXU_DIM:(g + 1) * MXU_DIM] for g in range(n_lane_tiles)], axis=0)
        hi = xs.astype(BF16)
        lo = (xs - hi.astype(F32)).astype(BF16)
        s = _dot(jnp.concatenate([hi, lo], axis=0), ones_bd)
        s = s[:n_lane_tiles * R] + s[n_lane_tiles * R:]
        return jnp.concatenate([_rows(s, g, R) for g in range(n_lane_tiles)], axis=1)

    @pl.when(ci == 0)
    def _():
        for b in range(nb):
            car_s[b] = sh_ref[b]
            for q in range(Q):
                if zero_state:
                    st_s[b, q] = jnp.zeros(st_s.shape[2:], F32)
                else:
                    for g in range(N_GROUPS):
                        heads = [s0_ref[b, q * RWKV_HEADS + g * GROUP_HEADS + hh] for hh in range(GROUP_HEADS)]
                        stacked = jnp.concatenate(heads, axis=0)
                        st_s[b, q, g] = jnp.concatenate([stacked] * GROUP_HEADS, axis=1) * msq_f

    row_c = lax.broadcasted_iota(jnp.int32, (C, 1), 0)
    prevs = []
    for b in range(nb):
        ps_b = ps_ref[b]
        if Q == 1:
            first = car_s[b]
            new_carry = ps_b[C - 1:C, :]
        else:
            put = (lax.broadcasted_iota(jnp.int32, (C, Q), 0)
                   == seq * lax.broadcasted_iota(jnp.int32, (C, Q), 1)).astype(BF16)
            take = (lax.broadcasted_iota(jnp.int32, (Q, C), 1)
                    == seq * lax.broadcasted_iota(jnp.int32, (Q, C), 0) + (seq - 1)).astype(BF16)
            first = _dot_split3(put, car_s[b])
            new_carry = _dot_split3(take, ps_b)
        prevs.append(jnp.where(row_c % seq == 0, first, pltpu.roll(ps_b, 1, axis=0)))
        car_s[b] = new_carry
        nsh_ref[b] = new_carry
    ps = jnp.concatenate([ps_ref[b] for b in range(nb)], axis=0)
    prev = jnp.concatenate(prevs, axis=0)

    s = ps + (prev - ps) * mu_ref[...]
    r = s[:, :o]
    k = s[:, o:2 * o]
    v = s[:, 2 * o:3 * o]
    dwa = s[:, 3 * o:3 * o + LORA_W + LORA_A]
    dg = s[:, 3 * o + LORA_W + LORA_A:]
    lane = lax.broadcasted_iota(jnp.int32, dwa.shape, 1)
    lora_in = jnp.where(lane < LORA_W, jnp.tanh(dwa), dwa).astype(BF16)
    lora = _dot(lora_in, lora_ref[...])
    w_log = -_softplus(-(w0_ref[...] + lora[:, :o])) - 0.5
    lw = -jnp.exp(w_log)
    a = _sigmoid(a0_ref[...] + lora[:, o:])
    g = _dot(_sigmoid(dg).astype(BF16), gup_ref[...])
    kk = k * kk_ref[...]
    kk = kk / jnp.maximum(jnp.sqrt(head_sum(kk * kk)), 1e-12)
    kmod = k * (1.0 + (a - 1.0) * ka_ref[...])
    beta = kk * a
    lw_hi = lw.astype(BF16)
    lw_lo = (lw - lw_hi.astype(F32)).astype(BF16)
    L = _dot(tri_ref[...], lw_hi) + _dot(tri_ref[...], lw_lo)
    l_end = _dot(endm_ref[...], lw_hi) + _dot(endm_ref[...], lw_lo)
    e_neg = jnp.exp(-L)
    e_end = jnp.exp(l_end - L)
    p_end = jnp.exp(l_end)
    at = (-kk) * jnp.exp(L - lw)
    rt = r * jnp.exp(L)
    bt = (beta * e_neg).astype(BF16)
    kt = (kmod * e_neg).astype(BF16)
    bend = beta * e_end
    kend = kmod * e_end

    chains = []
    for b in range(nb):
        for gi in range(N_GROUPS):
            sl = slice(gi * GROUP_LANES, (gi + 1) * GROUP_LANES)
            chains.append(dict(
                b=b, gi=gi, vg=_rows(v, b, C)[:, sl].astype(BF16), p_end=_rows(p_end, b, C)[:, sl],
                x2=jnp.concatenate([_rows(at, b, C)[:, sl], _rows(rt, b, C)[:, sl]], axis=0).astype(BF16),
                wbd=jnp.concatenate([_block_diag(_rows(bt, b, C)[:, sl], mrow_ref),
                                     _block_diag(_rows(kt, b, C)[:, sl], mrow_ref)], axis=0),
                bk=jnp.concatenate([_rows(bend, b, C)[:, sl], _rows(kend, b, C)[:, sl]], axis=0).astype(BF16)))

    cw = GROUP_HEADS * C
    rowseq = (lax.broadcasted_iota(jnp.int32, (2 * C, 1), 0) % C) // seq
    for c in chains:
        res = _dot_tb(c['x2'], c['wbd'])
        n_cat = jnp.where(m_strict, res[:C, :cw], 0.0)
        c['a_ak'] = jnp.where(m_strict, res[:C, cw:], 0.0).astype(BF16)
        c['a_r'] = jnp.concatenate([jnp.where(m_incl, res[C:, :cw], 0.0),
                                    jnp.where(m_incl, res[C:, cw:], 0.0)], axis=1).astype(BF16)
        c['x_c'] = n_cat.astype(BF16)
        c['p_c'] = eye_cat + n_cat
        c['x_bd'] = _block_diag(c['x_c'], msq_ref)
    for c in chains:
        xs = None
        for q in range(Q):
            xq = _dot_tb(c['x2'], st_s[c['b'], q, c['gi']].astype(BF16))
            xs = xq if xs is None else jnp.where(rowseq == q, xq, xs)
        c['xs'] = xs
        c['v_bd'] = _block_diag(c['vg'], msq_ref)
    for c in chains:
        c['rhs'] = c['xs'][:C] + _dot(c['a_ak'], c['v_bd'])
    lvl = 2
    while lvl < seq:
        for c in chains:
            c['x_c'] = _dot(c['x_c'], c['x_bd']).astype(BF16)
            c['x_bd'] = _block_diag(c['x_c'], msq_ref)
        for c in chains:
            c['p_c'] = c['p_c'] + _dot(c['p_c'].astype(BF16), c['x_bd'])
        lvl *= 2
    for c in chains:
        c['u'] = _dot(c['p_c'].astype(BF16), _block_diag(c['rhs'].astype(BF16), msq_ref)).astype(BF16)
    ys = {}
    for c in chains:
        b, gi = c['b'], c['gi']
        ys[(b, gi)] = c['xs'][C:] + _dot(c['a_r'], jnp.concatenate([_block_diag(c['u'], msq_ref), c['v_bd']], axis=0))
        uv = jnp.concatenate([c['u'], c['vg']], axis=0)
        if Q > 1:
            uv = jnp.concatenate([jnp.where(rowseq == q, uv, jnp.zeros_like(uv)) for q in range(Q)], axis=1)
        ds = _dot_ta(uv, c['bk'])
        for q in range(Q):
            st_s[b, q, gi] = (st_s[b, q, gi] * c['p_end'][q * seq:q * seq + 1, :]
                              + _rows(ds, q, GROUP_LANES) * msq_f)

    yc = jnp.concatenate([jnp.concatenate([ys[(b, gi)] for gi in range(N_GROUPS)], axis=1) for b in range(nb)],
                         axis=0)
    inv_n = 1.0 / RWKV_HEAD
    mean = head_sum(yc) * inv_n
    dlt = yc - mean
    var = head_sum(dlt * dlt) * inv_n
    yn = dlt * lax.rsqrt(var + GN_EPS) * lnw_ref[...] + lnb_ref[...]
    bonus = head_sum(r * kmod * rk_ref[...]) * v
    out = ((yn + bonus) * g).astype(BF16)
    for b in range(nb):
        y_ref[b] = _rows(out, b, C)

    @pl.when(last)
    def _():
        for b in range(nb):
            for q in range(Q):
                for gi in range(N_GROUPS):
                    for hh in range(GROUP_HEADS):
                        ns_ref[b, q * RWKV_HEADS + gi * GROUP_HEADS + hh] = st_s[
                            b, q, gi, hh * RWKV_HEAD:(hh + 1) * RWKV_HEAD, hh * RWKV_HEAD:(hh + 1) * RWKV_HEAD]


def _wkv_consts(nb, seq):
    C = CHUNK
    gh = GROUP_HEADS
    rows = np.arange(nb * C)
    same = (rows[:, None] // seq) == (rows[None, :] // seq)
    tri = (same & (rows[None, :] <= rows[:, None])).astype(np.float32)
    endm = same.astype(np.float32)
    hl = np.arange(MXU_DIM) // RWKV_HEAD
    ones_bd = (hl[:, None] == hl[None, :]).astype(np.float32)
    rowh = np.arange(gh * C) // C
    colh = np.arange(GROUP_LANES) // RWKV_HEAD
    mrow = (rowh[:, None] == colh[None, :]).astype(np.float32)
    msq = (rowh[:, None] == rowh[None, :]).astype(np.float32)
    return tuple(jnp.asarray(m, BF16) for m in (tri, endm, ones_bd, mrow, msq))


def _wkv(ps3, st_shift, st_rwkv_all, new_rwkv_all, P, layer, nb, zero_state):
    B, T, _ = ps3.shape
    C = CHUNK
    seq = C if T % C == 0 else T
    assert C % seq == 0
    Q = C // seq
    G = B // Q
    tg = T * Q
    assert B % Q == 0 and G % nb == 0 and tg % C == 0
    o = RWKV_WIDTH
    consts = _wkv_consts(nb, seq)
    vec = lambda w: pl.BlockSpec((None, 1, w), lambda b, c: (layer, 0, 0))
    const = lambda arr: pl.BlockSpec(arr.shape, lambda b, c: (0,) * arr.ndim)
    chained = new_rwkv_all is not None
    kern = functools.partial(_wkv_kernel, nb=nb, seq=seq, zero_state=zero_state, chained=chained)
    shift_spec = pl.BlockSpec((nb, Q, SHIFT_WIDTH), lambda b, c: (b, 0, 0))
    state_shape = (DEPTH, G, Q * RWKV_HEADS, RWKV_HEAD, RWKV_HEAD)
    state_block = (None, nb, Q * RWKV_HEADS, RWKV_HEAD, RWKV_HEAD)
    state_index = lambda b, c: (layer, b, 0, 0, 0)
    in_specs = [
        pl.BlockSpec((nb, C, SHIFT_WIDTH), lambda b, c: (b, c, 0)),
        shift_spec,
        pl.BlockSpec(state_block, state_index, pipeline_mode=pl.Buffered(1)),
        vec(SHIFT_WIDTH),
        vec(o),
        pl.BlockSpec((None, LORA_W + LORA_A, 2 * o), lambda b, c: (layer, 0, 0)),
        pl.BlockSpec((None, LORA_G, o), lambda b, c: (layer, 0, 0)),
        vec(o), vec(o), vec(o), vec(o), vec(o), vec(o),
    ] + [const(m) for m in consts]
    args = [ps3.reshape(G, tg, SHIFT_WIDTH), st_shift.reshape(G, Q, SHIFT_WIDTH), st_rwkv_all.reshape(state_shape),
            P['shift_mu'], P['w0'], P['lora_wa'], P['g_lora_up'], P['a0'], P['k_k'], P['k_a'],
            P['r_k'], P['ln_x_w'], P['ln_x_b'], *consts]
    aliases = {}
    if chained:
        aliases = {len(args): 2}
        in_specs.append(pl.BlockSpec(memory_space=pl.ANY))
        args.append(new_rwkv_all.reshape(state_shape))
    y, nsh, ns = pl.pallas_call(
        kern,
        grid=(G // nb, tg // C),
        in_specs=in_specs,
        out_specs=[
            pl.BlockSpec((nb, C, o), lambda b, c: (b, c, 0)),
            shift_spec,
            pl.BlockSpec(state_block, state_index),
        ],
        out_shape=[
            jax.ShapeDtypeStruct((G, tg, o), BF16),
            jax.ShapeDtypeStruct((G, Q, SHIFT_WIDTH), F32),
            jax.ShapeDtypeStruct(state_shape, F32),
        ],
        scratch_shapes=[
            pltpu.VMEM((nb, Q, SHIFT_WIDTH), F32),
            pltpu.VMEM((nb, Q, N_GROUPS, MXU_DIM, MXU_DIM), F32),
        ],
        input_output_aliases=aliases,
        compiler_params=_cparams(("arbitrary", "arbitrary")),
        name="rwkv",
    )(*args)
    return (y.reshape(B, T, o), nsh.reshape(B, SHIFT_WIDTH),
            ns.reshape(DEPTH, B, RWKV_HEADS, RWKV_HEAD, RWKV_HEAD))


def _merge_kernel(x_ref, ya_ref, yb_ref, gt_ref, wa_ref, wb_ref, wo_ref, nf_ref, *rest, moe):
    if moe:
        wr_ref, xo_ref, gates_ref, cnt_ref = rest
    else:
        xo_ref, h_ref = rest
    ga = gt_ref[:, :D_MODEL]
    gb = gt_ref[:, D_MODEL:]
    m = _sigmoid(ga) * _dot(ya_ref[...], wa_ref[...]) + _sigmoid(gb) * _dot(yb_ref[...], wb_ref[...])
    x = x_ref[...] + _dot(m.astype(BF16), wo_ref[...])
    xo_ref[...] = x
    h = _rms(x, nf_ref[...])
    h_hi = h.astype(BF16)
    if not moe:
        h_ref[...] = h_hi
    else:
        w = wr_ref[...]
        w_hi = w.astype(BF16)
        w_lo = (w - w_hi.astype(F32)).astype(BF16)
        h_lo = (h - h_hi.astype(F32)).astype(BF16)
        logits = _dot(h_hi, w_hi) + _dot(h_lo, w_hi) + _dot(h_hi, w_lo)
        lane = lax.broadcasted_iota(jnp.int32, logits.shape, 1)
        real = lane < N_EXPERTS
        logits = jnp.where(real, logits, -jnp.inf)
        e = jnp.exp(logits - jnp.max(logits, axis=-1, keepdims=True))
        p = jnp.where(real, e / jnp.sum(e, axis=-1, keepdims=True), -1.0)
        m1 = jnp.max(p, axis=-1, keepdims=True)
        i1 = jnp.min(jnp.where(p == m1, lane, LANES), axis=-1, keepdims=True)
        oh1 = lane == i1
        p2 = jnp.where(oh1, -1.0, p)
        m2 = jnp.max(p2, axis=-1, keepdims=True)
        i2 = jnp.min(jnp.where(p2 == m2, lane, LANES), axis=-1, keepdims=True)
        oh2 = lane == i2
        tot = m1 + m2
        gates = jnp.where(oh1, m1 / tot, 0.0) + jnp.where(oh2, m2 / tot, 0.0)
        gates_ref[...] = gates
        cnt = jnp.sum((gates > 0.0).astype(F32), axis=0, keepdims=True)
        cnt_ref[...] = jnp.broadcast_to(cnt, (SUBLANES, LANES))


def _merge(x, ya, yb, gt, P, layer, tm, moe):
    n = x.shape[0]
    row = lambda w: pl.BlockSpec((tm, w), lambda i: (i, 0))
    wsq = lambda: pl.BlockSpec((None, D_MODEL, D_MODEL), lambda i: (layer, 0, 0))
    in_specs = [row(D_MODEL), row(D_MODEL), row(D_MODEL), row(2 * D_MODEL), wsq(), wsq(), wsq(),
                pl.BlockSpec((None, 1, D_MODEL), lambda i: (layer, 0, 0))]
    args = [x, ya, yb, gt, P['w_out_a'], P['w_out_b'], P['w_out'], P['norm_ffn']]
    out_specs = [row(D_MODEL)]
    out_shape = [jax.ShapeDtypeStruct((n, D_MODEL), F32)]
    if moe:
        in_specs.append(pl.BlockSpec((None, D_MODEL, LANES), lambda i: (layer // 2, 0, 0)))
        args.append(P['w_router_pad'])
        out_specs.append(row(LANES))
        out_shape.append(jax.ShapeDtypeStruct((n, LANES), F32))
        out_specs.append(pl.BlockSpec((None, SUBLANES, LANES), lambda i: (i, 0, 0)))
        out_shape.append(jax.ShapeDtypeStruct((n // tm, SUBLANES, LANES), F32))
    else:
        out_specs.append(row(D_MODEL))
        out_shape.append(jax.ShapeDtypeStruct((n, D_MODEL), BF16))
    return pl.pallas_call(
        functools.partial(_merge_kernel, moe=moe),
        grid=(n // tm,),
        in_specs=in_specs,
        out_specs=out_specs,
        out_shape=out_shape,
        compiler_params=_cparams(("arbitrary",)),
        name="merge",
    )(*args)


def _swiglu_acc(h, wg_ref, wu_ref, wd_ref):
    acc = None
    for c0 in range(0, D_FF, FF_CHUNK):
        sl = slice(c0, c0 + FF_CHUNK)
        gate = _dot(h, wg_ref[:, sl])
        up = _dot(h, wu_ref[:, sl])
        act = (gate * _sigmoid(gate) * up).astype(BF16)
        part = _dot(act, wd_ref[sl, :])
        acc = part if acc is None else acc + part
    return acc


def _ffn_kernel(x_ref, h_ref, wg_ref, wu_ref, wd_ref, o_ref):
    o_ref[...] = x_ref[...] + _swiglu_acc(h_ref[...], wg_ref, wu_ref, wd_ref)


def _ffn(x, h, P, j, tm):
    n = x.shape[0]
    row = lambda: pl.BlockSpec((tm, D_MODEL), lambda i: (i, 0))
    return pl.pallas_call(
        _ffn_kernel,
        grid=(n // tm,),
        in_specs=[row(), row(),
                  pl.BlockSpec((None, D_MODEL, D_FF), lambda i: (j, 0, 0), pipeline_mode=pl.Buffered(1)),
                  pl.BlockSpec((None, D_MODEL, D_FF), lambda i: (j, 0, 0), pipeline_mode=pl.Buffered(1)),
                  pl.BlockSpec((None, D_FF, D_MODEL), lambda i: (j, 0, 0), pipeline_mode=pl.Buffered(1))],
        out_specs=row(),
        out_shape=jax.ShapeDtypeStruct((n, D_MODEL), F32),
        compiler_params=_cparams(("arbitrary",)),
        name="ffn",
    )(x, h, P['w_ffn_gate'], P['w_ffn_up'], P['w_ffn_down'])


def _route_kernel(gates_ref, base_ref, pos_ref, *, tw, spare):
    mask = (gates_ref[...] > 0.0).astype(BF16)
    r_idx = lax.broadcasted_iota(jnp.int32, (tw, tw), 0)
    c_idx = lax.broadcasted_iota(jnp.int32, (tw, tw), 1)
    excl = _dot_ta(mask, (r_idx < c_idx).astype(BF16))
    sel = _dot_ta(mask, (r_idx == c_idx).astype(BF16)) > 0.0
    pos = excl + base_ref[:, 0:1]
    e_io = lax.broadcasted_iota(jnp.int32, (LANES, tw), 0)
    e_first = jnp.min(jnp.where(sel, e_io, LANES), axis=0, keepdims=True)
    e_last = jnp.max(jnp.where(sel, e_io, -1), axis=0, keepdims=True)
    p_first = jnp.sum(jnp.where(e_io == e_first, pos, 0.0), axis=0, keepdims=True)
    p_last = jnp.sum(jnp.where(e_io == e_last, pos, 0.0), axis=0, keepdims=True)
    p_last = jnp.where(e_last == e_first, float(spare), p_last)
    rows = jnp.concatenate([p_first, p_last, jnp.zeros((SUBLANES - 2, tw), F32)], axis=0)
    pos_ref[...] = rows.astype(jnp.int32)


def _route(gates, base_b, tw, spare):
    n = gates.shape[0]
    return pl.pallas_call(
        functools.partial(_route_kernel, tw=tw, spare=spare),
        grid=(n // tw,),
        in_specs=[pl.BlockSpec((tw, LANES), lambda i: (i, 0)),
                  pl.BlockSpec((None, LANES, LANES), lambda i: (i, 0, 0))],
        out_specs=pl.BlockSpec((None, SUBLANES, tw), lambda i: (i, 0, 0)),
        out_shape=jax.ShapeDtypeStruct((n // tw, SUBLANES, tw), jnp.int32),
        compiler_params=_cparams(("arbitrary",)),
        name="route",
    )(gates, base_b)


def _row_copy(src_hbm, src_row, dst_ref, dst_row, sem):
    return pltpu.make_async_copy(src_hbm.at[pl.ds(src_row, 1), :], dst_ref.at[pl.ds(dst_row, 1), :], sem)


def _disperse_kernel(pos_ref, x_ref, init_hbm, xs_hbm, sem, *, tw):
    del init_hbm

    def issue(n, carry):
        _row_copy(x_ref, n, xs_hbm, pos_ref[0, n], sem).start()
        _row_copy(x_ref, n, xs_hbm, pos_ref[1, n], sem).start()
        return carry

    def drain(n, carry):
        _row_copy(x_ref, 0, xs_hbm, 0, sem).wait()
        _row_copy(x_ref, 0, xs_hbm, 0, sem).wait()
        return carry

    lax.fori_loop(0, tw, issue, 0, unroll=8)
    lax.fori_loop(0, tw, drain, 0, unroll=8)


def _disperse(pos, x, n_rows, tw):
    n = x.shape[0]
    return pl.pallas_call(
        functools.partial(_disperse_kernel, tw=tw),
        grid=(n // tw,),
        in_specs=[pl.BlockSpec((None, SUBLANES, tw), lambda i: (i, 0, 0), memory_space=pltpu.SMEM),
                  pl.BlockSpec((tw, D_MODEL), lambda i: (i, 0)),
                  pl.BlockSpec(memory_space=pl.ANY)],
        out_specs=pl.BlockSpec(memory_space=pl.ANY),
        out_shape=jax.ShapeDtypeStruct((n_rows, D_MODEL), F32),
        scratch_shapes=[pltpu.SemaphoreType.DMA],
        input_output_aliases={2: 0},
        compiler_params=_cparams(("arbitrary",)),
        name="disperse",
    )(pos, x, jnp.zeros((n_rows, D_MODEL), F32))


def _experts_kernel(te_ref, used_ref, xs_ref, nf_ref, wg_ref, wu_ref, wd_ref, ys_ref):
    del te_ref
    j = pl.program_id(0)

    @pl.when(j < used_ref[0])
    def _():
        h = _rms(xs_ref[...], nf_ref[...]).astype(BF16)
        ys_ref[...] = _swiglu_acc(h, wg_ref, wu_ref, wd_ref)

    @pl.when(j >= used_ref[0])
    def _():
        ys_ref[...] = jnp.zeros(ys_ref.shape, F32)


def _experts(tile_expert, used, xs, P, layer, j):
    n_tiles = tile_expert.shape[0]
    tr = xs.shape[0] // n_tiles
    wspec = lambda a, b: pl.BlockSpec((None, None, a, b), lambda t, te, u: (j, te[t], 0, 0))
    grid_spec = pltpu.PrefetchScalarGridSpec(
        num_scalar_prefetch=2,
        grid=(n_tiles,),
        in_specs=[pl.BlockSpec((tr, D_MODEL), lambda t, te, u: (t, 0)),
                  pl.BlockSpec((None, 1, D_MODEL), lambda t, te, u: (layer, 0, 0)),
                  wspec(D_MODEL, D_FF), wspec(D_MODEL, D_FF), wspec(D_FF, D_MODEL)],
        out_specs=pl.BlockSpec((tr, D_MODEL), lambda t, te, u: (t, 0)),
    )
    return pl.pallas_call(
        _experts_kernel,
        grid_spec=grid_spec,
        out_shape=jax.ShapeDtypeStruct((n_tiles * tr, D_MODEL), F32),
        compiler_params=_cparams(("arbitrary",)),
        name="experts",
    )(tile_expert, used, xs, P['norm_ffn'], P['w_moe_gate'], P['w_moe_up'], P['w_moe_down'])


def _combine_kernel(pos_ref, x_ref, gates_ref, ys_hbm, o_ref, y1_s, y2_s, sem, *, tw):
    def issue(n, carry):
        _row_copy(ys_hbm, pos_ref[0, n], y1_s, n, sem).start()
        _row_copy(ys_hbm, pos_ref[1, n], y2_s, n, sem).start()
        return carry

    def drain(n, carry):
        _row_copy(ys_hbm, 0, y1_s, 0, sem).wait()
        _row_copy(ys_hbm, 0, y2_s, 0, sem).wait()
        return carry

    lax.fori_loop(0, tw, issue, 0, unroll=8)
    gates = gates_ref[...]
    lane = lax.broadcasted_iota(jnp.int32, gates.shape, 1)
    sel = gates > 0.0
    e_first = jnp.min(jnp.where(sel, lane, LANES), axis=-1, keepdims=True)
    e_last = jnp.max(jnp.where(sel, lane, -1), axis=-1, keepdims=True)
    g_first = jnp.sum(jnp.where(lane == e_first, gates, 0.0), axis=-1, keepdims=True)
    g_last = jnp.sum(jnp.where((lane == e_last) & (e_last != e_first), gates, 0.0), axis=-1, keepdims=True)
    lax.fori_loop(0, tw, drain, 0, unroll=8)
    o_ref[...] = x_ref[...] + g_first * y1_s[...] + g_last * y2_s[...]


def _combine(pos, x, gates, ys, tw):
    n = x.shape[0]
    return pl.pallas_call(
        functools.partial(_combine_kernel, tw=tw),
        grid=(n // tw,),
        in_specs=[pl.BlockSpec((None, SUBLANES, tw), lambda i: (i, 0, 0), memory_space=pltpu.SMEM),
                  pl.BlockSpec((tw, D_MODEL), lambda i: (i, 0)),
                  pl.BlockSpec((tw, LANES), lambda i: (i, 0)),
                  pl.BlockSpec(memory_space=pl.ANY)],
        out_specs=pl.BlockSpec((tw, D_MODEL), lambda i: (i, 0)),
        out_shape=jax.ShapeDtypeStruct((n, D_MODEL), F32),
        scratch_shapes=[pltpu.VMEM((tw, D_MODEL), F32), pltpu.VMEM((tw, D_MODEL), F32), pltpu.SemaphoreType.DMA],
        compiler_params=_cparams(("arbitrary",)),
        name="combine",
    )(pos, x, gates, ys)


def _moe_routed(x, gates, cnt, P, layer, tm):
    n = x.shape[0]
    tw = min(ROUTE_TILE, n)
    tr = ROUTE_ROWS if TOP_K * n >= 4 * N_EXPERTS * ROUTE_ROWS else ROUTE_ROWS_SMALL
    n_tiles = (TOP_K * n) // tr + N_EXPERTS
    spare = n_tiles * tr
    c = cnt[:, 0, :].astype(jnp.int32).reshape(n // tw, tw // tm, LANES).sum(axis=1)
    seg_tiles = (c.sum(axis=0) + tr - 1) // tr
    seg_end = jnp.cumsum(seg_tiles)
    base = (seg_end - seg_tiles) * tr + jnp.cumsum(c, axis=0) - c
    base_b = jnp.broadcast_to(base.astype(F32)[:, :, None], (n // tw, LANES, LANES))
    tile_expert = jnp.minimum((jnp.arange(n_tiles + 1)[:, None] >= seg_end[None, :N_EXPERTS]).sum(axis=1),
                              N_EXPERTS - 1).astype(jnp.int32)
    used = seg_end[N_EXPERTS - 1:N_EXPERTS].astype(jnp.int32)
    pos = _route(gates, base_b, tw, spare)
    xs = _disperse(pos, x, spare + tr, tw)
    ys = _experts(tile_expert, used, xs, P, layer, layer // 2)
    return _combine(pos, x, gates, ys, tw)


def _final_norm_kernel(x_ref, g_ref, o_ref):
    o_ref[...] = _rms(x_ref[...], g_ref[...])


def _final_norm(x, g, tm):
    n = x.shape[0]
    return pl.pallas_call(
        _final_norm_kernel,
        grid=(n // tm,),
        in_specs=[pl.BlockSpec((tm, D_MODEL), lambda i: (i, 0)), pl.BlockSpec((1, D_MODEL), lambda i: (0, 0))],
        out_specs=pl.BlockSpec((tm, D_MODEL), lambda i: (i, 0)),
        out_shape=jax.ShapeDtypeStruct((n, D_MODEL), F32),
        compiler_params=_cparams(("arbitrary",)),
        name="final_norm",
    )(x, g)


def _block_diag_weights(w):
    d = w.shape[0]
    per = MXU_DIM // LRU_BLOCK
    w = w.reshape(d, LRU_HEADS // per, per, LRU_BLOCK, LRU_BLOCK)
    eye = jnp.eye(per, dtype=w.dtype)
    bd = jnp.einsum('dcpij,pq->dcpiqj', w, eye)
    return bd.reshape(d, LRU_HEADS // per, MXU_DIM, MXU_DIM).astype(BF16)


def _prep_params(p):
    P = dict(p)
    for name in ('w_in', 'w_out_a', 'w_out_b', 'w_out', 'w_ffn_gate', 'w_ffn_up', 'w_ffn_down',
                 'w_moe_gate', 'w_moe_up', 'w_moe_down', 'g_lora_up'):
        P[name] = p[name].astype(BF16)
    for name in ('norm_mix', 'conv_b', 'gate_a_b', 'gate_x_b', 'lru_lambda', 'shift_mu', 'w0', 'a0', 'k_k', 'k_a',
                 'ln_x_w', 'ln_x_b', 'norm_ffn'):
        P[name] = p[name][:, None, :]
    P['r_k'] = p['r_k'].reshape(DEPTH, 1, RWKV_WIDTH)
    P['norm_final'] = p['norm_final'][None, :]
    P['gaw_bd'] = _block_diag_weights(p['gate_a_w'])
    P['gxw_bd'] = _block_diag_weights(p['gate_x_w'])
    z = jnp.zeros((DEPTH, LORA_W, RWKV_WIDTH), F32)
    P['lora_wa'] = jnp.concatenate([jnp.concatenate([p['w_lora_up'], z], axis=2),
                                    jnp.concatenate([z, p['a_lora_up']], axis=2)], axis=1).astype(BF16)
    P['w_router_pad'] = jnp.pad(p['w_router'], ((0, 0), (0, 0), (0, LANES - N_EXPERTS)))
    return P


def _trunk(x3, st_rwkv, st_lru, st_conv, st_shift, P, *, zero_state, tm, lru_bb, lru_tt, rwkv_nb):
    B, T, _ = x3.shape
    n = B * T
    x = x3.reshape(n, D_MODEL)
    new_rwkv, n_lru, n_conv, n_shift = jnp.zeros(st_rwkv.shape, F32), [], [], []
    for l in range(DEPTH):
        xy, ps, gt = _proj(x, P['norm_mix'], P['w_in'], l, min(tm, 256))
        ya, h_new, c_new = _lru(xy.reshape(B, T, 2 * LRU_WIDTH), st_conv[l], st_lru[l][:, None, :], P, l,
                                lru_bb, lru_tt)
        yb, sh_new, new_rwkv = _wkv(ps.reshape(B, T, SHIFT_WIDTH), st_shift[l], st_rwkv, new_rwkv, P, l, rwkv_nb,
                                    zero_state)
        moe = l % 2 == 1
        res = _merge(x, ya.reshape(n, D_MODEL), yb.reshape(n, D_MODEL), gt, P, l, tm, moe)
        if moe:
            x, gates, cnt = res
            x = _moe_routed(x, gates, cnt, P, l, tm)
        else:
            x, h = res
            x = _ffn(x, h, P, l // 2, tm)
        n_lru.append(h_new[:, 0, :])
        n_conv.append(c_new)
        n_shift.append(sh_new)
    y = _final_norm(x, P['norm_final'], tm).reshape(B, T, D_MODEL)
    return y, new_rwkv, jnp.stack(n_lru), jnp.stack(n_conv), jnp.stack(n_shift)


def kernel(x_prompt, x_sample, state_rwkv, state_lru, state_conv, state_shift, norm_mix, w_in, conv_w, conv_b, gate_a_w, gate_a_b, gate_x_w, gate_x_b, lru_lambda, shift_mu, w0, w_lora_up, a0, a_lora_up, g_lora_up, k_k, k_a, r_k, ln_x_w, ln_x_b, w_out_a, w_out_b, w_out, norm_ffn, w_ffn_gate, w_ffn_up, w_ffn_down, w_router, w_moe_gate, w_moe_up, w_moe_down, norm_final):
    P = _prep_params(dict(
        norm_mix=norm_mix, w_in=w_in, conv_w=conv_w, conv_b=conv_b, gate_a_w=gate_a_w, gate_a_b=gate_a_b,
        gate_x_w=gate_x_w, gate_x_b=gate_x_b, lru_lambda=lru_lambda, shift_mu=shift_mu, w0=w0,
        w_lora_up=w_lora_up, a0=a0, a_lora_up=a_lora_up, g_lora_up=g_lora_up, k_k=k_k, k_a=k_a, r_k=r_k,
        ln_x_w=ln_x_w, ln_x_b=ln_x_b, w_out_a=w_out_a, w_out_b=w_out_b, w_out=w_out, norm_ffn=norm_ffn,
        w_ffn_gate=w_ffn_gate, w_ffn_up=w_ffn_up, w_ffn_down=w_ffn_down, w_router=w_router,
        w_moe_gate=w_moe_gate, w_moe_up=w_moe_up, w_moe_down=w_moe_down, norm_final=norm_final))
    bp = x_prompt.shape[0]
    bs = x_sample.shape[0]
    zeros = lambda *shape: jnp.zeros(shape, F32)
    p_out = _trunk(x_prompt,
                   zeros(DEPTH, bp, RWKV_HEADS, RWKV_HEAD, RWKV_HEAD), zeros(DEPTH, bp, LRU_WIDTH),
                   zeros(DEPTH, bp, CONV_WIDTH - 1, LRU_WIDTH), zeros(DEPTH, bp, SHIFT_WIDTH), P,
                   zero_state=True, tm=512, lru_bb=1, lru_tt=256, rwkv_nb=4)
    s_out = _trunk(x_sample, state_rwkv, state_lru, state_conv, state_shift, P,
                   zero_state=False, tm=256, lru_bb=8, lru_tt=x_sample.shape[1], rwkv_nb=1)
    return (p_out[0], s_out[0]) + tuple(p_out[1:]) + tuple(s_out[1:])
```

```python
import functools

import numpy as np
import jax
import jax.numpy as jnp
from jax import lax
from jax.experimental import pallas as pl
from jax.experimental.pallas import tpu as pltpu

F32 = jnp.float32
BF16 = jnp.bfloat16

D_MODEL = 1024
DEPTH = 4
LRU_WIDTH = D_MODEL
LRU_HEADS = 16
LRU_BLOCK = LRU_WIDTH // LRU_HEADS
CONV_WIDTH = 4
LRU_C = 8.0
RWKV_HEAD = 64
RWKV_WIDTH = D_MODEL
RWKV_HEADS = RWKV_WIDTH // RWKV_HEAD
LORA_W = 64
LORA_A = 64
LORA_G = 128
SHIFT_WIDTH = 3 * RWKV_WIDTH + LORA_W + LORA_A + LORA_G
PROJ_WIDTH = 2 * LRU_WIDTH + SHIFT_WIDTH + 2 * D_MODEL
D_FF = 2816
N_EXPERTS = 8
NORM_EPS = 1e-6
GN_EPS = 64e-5

LANES = 128
SUBLANES = 8
MXU_DIM = 256
VMEM_LIMIT = 56 * 1024 * 1024

CHUNK = 64
GROUP_HEADS = MXU_DIM // CHUNK
GROUP_LANES = GROUP_HEADS * RWKV_HEAD
N_GROUPS = RWKV_HEADS // GROUP_HEADS
FF_CHUNK = 256
TOP_K = 2
ROUTE_TILE = 512
ROUTE_ROWS = 512
ROUTE_ROWS_SMALL = 128
SCAN_ROWS = 16
assert CHUNK == RWKV_HEAD


def _cparams(sem):
    return pltpu.CompilerParams(dimension_semantics=sem, vmem_limit_bytes=VMEM_LIMIT)


def _dot(a, b):
    return jnp.dot(a, b, preferred_element_type=F32)


def _dot_tb(a, b):
    return lax.dot_general(a, b, (((1,), (1,)), ((), ())), preferred_element_type=F32)


def _dot_ta(a, b):
    return lax.dot_general(a, b, (((0,), (0,)), ((), ())), preferred_element_type=F32)


def _sigmoid(x):
    return jax.nn.sigmoid(x)


def _softplus(z):
    return jnp.maximum(z, 0.0) + jnp.log(1.0 + jnp.exp(-jnp.abs(z)))


def _rms(x, g):
    return x * lax.rsqrt(jnp.mean(x * x, axis=-1, keepdims=True) + NORM_EPS) * g


_PROJ_SPLITS = ((0, 2 * LRU_WIDTH), (2 * LRU_WIDTH, SHIFT_WIDTH), (2 * LRU_WIDTH + SHIFT_WIDTH, 2 * D_MODEL))


def _proj_kernel(x_ref, g_ref, w_ref, xy_ref, ps_ref, gt_ref):
    u = _rms(x_ref[...], g_ref[...]).astype(BF16)
    for out_ref, (c0, width) in zip((xy_ref, ps_ref, gt_ref), _PROJ_SPLITS):
        j = 0
        while j < width:
            w = min(512, width - j)
            out_ref[:, j:j + w] = _dot(u, w_ref[:, c0 + j:c0 + j + w])
            j += w


def _proj(x, g_all, w_all, layer, tm):
    n = x.shape[0]
    return pl.pallas_call(
        _proj_kernel,
        grid=(n // tm,),
        in_specs=[
            pl.BlockSpec((tm, D_MODEL), lambda i: (i, 0)),
            pl.BlockSpec((None, 1, D_MODEL), lambda i: (layer, 0, 0)),
            pl.BlockSpec((None, D_MODEL, PROJ_WIDTH), lambda i: (layer, 0, 0),
                         pipeline_mode=pl.Buffered(1)),
        ],
        out_specs=[
            pl.BlockSpec((tm, 2 * LRU_WIDTH), lambda i: (i, 0)),
            pl.BlockSpec((tm, SHIFT_WIDTH), lambda i: (i, 0)),
            pl.BlockSpec((tm, 2 * D_MODEL), lambda i: (i, 0)),
        ],
        out_shape=[
            jax.ShapeDtypeStruct((n, 2 * LRU_WIDTH), F32),
            jax.ShapeDtypeStruct((n, SHIFT_WIDTH), F32),
            jax.ShapeDtypeStruct((n, 2 * D_MODEL), F32),
        ],
        compiler_params=_cparams(("arbitrary",)),
        name="proj",
    )(x, g_all, w_all)


def _gelu_tanh(x):
    return 0.5 * x * (1.0 + jnp.tanh(np.sqrt(2.0 / np.pi).astype(np.float32) * (x + 0.044715 * (x * x * x))))


def _lru_kernel(xa_ref, ya_ref, cs_ref, h0_ref, cw_ref, cb_ref, gaw_ref, gab_ref, gxw_ref, gxb_ref, lam_ref,
                y_ref, hl_ref, nc_ref, xp_s, h_s, a_s, b_s, *, bb, tt):
    ti = pl.program_id(1)
    neg_c = -LRU_C * _softplus(-lam_ref[...])
    cw = cw_ref[...]
    @pl.when(ti == 0)
    def _():
        for b in range(bb):
            xp_s[b, 5:8, :] = cs_ref[b]
            h_s[b] = h0_ref[b]

    for b in range(bb):
        xa = xa_ref[b]
        xp_s[b, 8:8 + tt, :] = xa
        xc = cb_ref[...] + xp_s[b, 5:5 + tt, :] * cw[0:1]
        xc = xc + xp_s[b, 6:6 + tt, :] * cw[1:2]
        xc = xc + xp_s[b, 7:7 + tt, :] * cw[2:3]
        xc = xc + xa * cw[3:4]
        tail = xp_s[b, 5 + tt:8 + tt, :]
        xp_s[b, 5:8, :] = tail
        nc_ref[b] = tail

        for c in range(LRU_WIDTH // MXU_DIM):
            sl = slice(c * MXU_DIM, (c + 1) * MXU_DIM)
            xcc = xc[:, sl]
            xcb = xcc.astype(BF16)
            r = _sigmoid(_dot(xcb, gaw_ref[c]) + gab_ref[:, sl])
            i = _sigmoid(_dot(xcb, gxw_ref[c]) + gxb_ref[:, sl])
            log_a = neg_c[:, sl] * r
            a = jnp.exp(log_a)
            a_s[:, sl] = a
            b_s[:, sl] = jnp.sqrt(-jnp.tanh(log_a) * (a * a + 1.0)) * (i * xcc)

        if tt % SCAN_ROWS == 0:
            row = lax.broadcasted_iota(jnp.int32, (SUBLANES, LRU_WIDTH), 0)

            def blk(j, h):
                r0 = pl.multiple_of(j * SCAN_ROWS, SCAN_ROWS)
                hbs = []
                for q in range(SCAN_ROWS // SUBLANES):
                    av = a_s[pl.ds(r0 + q * SUBLANES, SUBLANES), :]
                    bv = b_s[pl.ds(r0 + q * SUBLANES, SUBLANES), :]
                    for d in (1, 2, 4):
                        m = row >= d
                        a_sh = pltpu.roll(av, d, axis=0)
                        b_sh = pltpu.roll(bv, d, axis=0)
                        bv = jnp.where(m, av * b_sh + bv, bv)
                        av = jnp.where(m, av * a_sh, av)
                    hb = av * h + bv
                    h = hb[SUBLANES - 1:SUBLANES, :]
                    hbs.append(hb)
                hs = jnp.concatenate(hbs, axis=0)
                y_ref[b, pl.ds(r0, SCAN_ROWS), :] = (hs * _gelu_tanh(ya_ref[b, pl.ds(r0, SCAN_ROWS), :])).astype(BF16)
                return h

            h = lax.fori_loop(0, tt // SCAN_ROWS, blk, h_s[b])
        else:
            h = h_s[b]
            rows = []
            for t in range(tt):
                h = a_s[t:t + 1, :] * h + b_s[t:t + 1, :]
                rows.append(h)
            hs = jnp.concatenate(rows, axis=0)
            y_ref[b] = (hs * _gelu_tanh(ya_ref[b])).astype(BF16)
        h_s[b] = h
        hl_ref[b] = h


def _lru(xy3, st_conv, st_lru, P, layer, bb, tt):
    B, T, _ = xy3.shape
    W = LRU_WIDTH
    vec = lambda: pl.BlockSpec((None, 1, W), lambda b, t: (layer, 0, 0))
    kern = functools.partial(_lru_kernel, bb=bb, tt=tt)
    return pl.pallas_call(
        kern,
        grid=(B // bb, T // tt),
        in_specs=[
            pl.BlockSpec((bb, tt, W), lambda b, t: (b, t, 0)),
            pl.BlockSpec((bb, tt, W), lambda b, t: (b, t, 1)),
            pl.BlockSpec((bb, CONV_WIDTH - 1, W), lambda b, t: (b, 0, 0)),
            pl.BlockSpec((bb, 1, W), lambda b, t: (b, 0, 0)),
            pl.BlockSpec((None, CONV_WIDTH, W), lambda b, t: (layer, 0, 0)),
            vec(),
            pl.BlockSpec((None, W // MXU_DIM, MXU_DIM, MXU_DIM), lambda b, t: (layer, 0, 0, 0)),
            vec(),
            pl.BlockSpec((None, W // MXU_DIM, MXU_DIM, MXU_DIM), lambda b, t: (layer, 0, 0, 0)),
            vec(),
            vec(),
        ],
        out_specs=[
            pl.BlockSpec((bb, tt, W), lambda b, t: (b, t, 0)),
            pl.BlockSpec((bb, 1, W), lambda b, t: (b, 0, 0)),
            pl.BlockSpec((bb, CONV_WIDTH - 1, W), lambda b, t: (b, 0, 0)),
        ],
        out_shape=[
            jax.ShapeDtypeStruct((B, T, W), BF16),
            jax.ShapeDtypeStruct((B, 1, W), F32),
            jax.ShapeDtypeStruct((B, CONV_WIDTH - 1, W), F32),
        ],
        scratch_shapes=[
            pltpu.VMEM((bb, tt + 8, W), F32),
            pltpu.VMEM((bb, 1, W), F32),
            pltpu.VMEM((tt, W), F32),
            pltpu.VMEM((tt, W), F32),
        ],
        compiler_params=_cparams(("arbitrary", "arbitrary")),
        name="lru",
    )(xy3, xy3, st_conv, st_lru, P['conv_w'], P['conv_b'], P['gaw_bd'], P['gate_a_b'], P['gxw_bd'],
      P['gate_x_b'], P['lru_lambda'])


def _block_diag(x_bf16, mask_ref):
    return jnp.concatenate([x_bf16] * GROUP_HEADS, axis=0) * mask_ref[...]


def _rows(x, b, n):
    return x[b * n:(b + 1) * n]


def _dot_split3(sel_bf16, x):
    hi = x.astype(BF16)
    r1 = x - hi.astype(F32)
    mid = r1.astype(BF16)
    lo = (r1 - mid.astype(F32)).astype(BF16)
    return _dot(sel_bf16, hi) + _dot(sel_bf16, mid) + _dot(sel_bf16, lo)


def _wkv_kernel(ps_ref, sh_ref, s0_ref, mu_ref, w0_ref, lora_ref, gup_ref, a0_ref, kk_ref, ka_ref, rk_ref,
                lnw_ref, lnb_ref, tri_ref, endm_ref, ones_ref, mrow_ref, msq_ref, new_all_ref,
                y_ref, nsh_ref, ns_ref, car_s, st_s, *, nb, seq, zero_state):
    del new_all_ref
    C = CHUNK
    Q = C // seq
    R = nb * C
    o = RWKV_WIDTH
    ci = pl.program_id(1)
    last = ci == pl.num_programs(1) - 1
    t_idx = lax.broadcasted_iota(jnp.int32, (C, MXU_DIM), 0)
    i_idx = lax.broadcasted_iota(jnp.int32, (C, MXU_DIM), 1) % C
    same = (t_idx // seq) == (i_idx // seq)
    m_strict = same & (i_idx < t_idx)
    m_incl = same & (i_idx <= t_idx)
    eye_cat = (i_idx == t_idx).astype(F32)
    ones_bd = ones_ref[...]
    msq_f = msq_ref[...].astype(F32)
    n_lane_tiles = o // MXU_DIM

    def head_sum(x):
        xs = jnp.concatenate([x[:, g * MXU_DIM:(g + 1) * MXU_DIM] for g in range(n_lane_tiles)], axis=0)
        hi = xs.astype(BF16)
        lo = (xs - hi.astype(F32)).astype(BF16)
        s = _dot(jnp.concatenate([hi, lo], axis=0), ones_bd)
        s = s[:n_lane_tiles * R] + s[n_lane_tiles * R:]
        return jnp.concatenate([_rows(s, g, R) for g in range(n_lane_tiles)], axis=1)

    @pl.when(ci == 0)
    def _():
        for b in range(nb):
            car_s[b] = sh_ref[b]
            for q in range(Q):
                if zero_state:
                    st_s[b, q] = jnp.zeros(st_s.shape[2:], F32)
                else:
                    for g in range(N_GROUPS):
                        heads = [s0_ref[b, q * RWKV_HEADS + g * GROUP_HEADS + hh] for hh in range(GROUP_HEADS)]
                        stacked = jnp.concatenate(heads, axis=0)
                        st_s[b, q, g] = jnp.concatenate([stacked] * GROUP_HEADS, axis=1) * msq_f

    row_c = lax.broadcasted_iota(jnp.int32, (C, 1), 0)
    prevs = []
    for b in range(nb):
        ps_b = ps_ref[b]
        if Q == 1:
            first = car_s[b]
            new_carry = ps_b[C - 1:C, :]
        else:
            put = (lax.broadcasted_iota(jnp.int32, (C, Q), 0)
                   == seq * lax.broadcasted_iota(jnp.int32, (C, Q), 1)).astype(BF16)
            take = (lax.broadcasted_iota(jnp.int32, (Q, C), 1)
                    == seq * lax.broadcasted_iota(jnp.int32, (Q, C), 0) + (seq - 1)).astype(BF16)
            first = _dot_split3(put, car_s[b])
            new_carry = _dot_split3(take, ps_b)
        prevs.append(jnp.where(row_c % seq == 0, first, pltpu.roll(ps_b, 1, axis=0)))
        car_s[b] = new_carry
        nsh_ref[b] = new_carry
    ps = jnp.concatenate([ps_ref[b] for b in range(nb)], axis=0)
    prev = jnp.concatenate(prevs, axis=0)

    s = ps + (prev - ps) * mu_ref[...]
    r = s[:, :o]
    k = s[:, o:2 * o]
    v = s[:, 2 * o:3 * o]
    dwa = s[:, 3 * o:3 * o + LORA_W + LORA_A]
    dg = s[:, 3 * o + LORA_W + LORA_A:]
    lane = lax.broadcasted_iota(jnp.int32, dwa.shape, 1)
    lora_in = jnp.where(lane < LORA_W, jnp.tanh(dwa), dwa).astype(BF16)
    lora = _dot(lora_in, lora_ref[...])
    w_log = -_softplus(-(w0_ref[...] + lora[:, :o])) - 0.5
    lw = -jnp.exp(w_log)
    a = _sigmoid(a0_ref[...] + lora[:, o:])
    g = _dot(_sigmoid(dg).astype(BF16), gup_ref[...])
    kk = k * kk_ref[...]
    kk = kk / jnp.maximum(jnp.sqrt(head_sum(kk * kk)), 1e-12)
    kmod = k * (1.0 + (a - 1.0) * ka_ref[...])
    beta = kk * a
    lw_hi = lw.astype(BF16)
    lw_lo = (lw - lw_hi.astype(F32)).astype(BF16)
    L = _dot(tri_ref[...], lw_hi) + _dot(tri_ref[...], lw_lo)
    l_end = _dot(endm_ref[...], lw_hi) + _dot(endm_ref[...], lw_lo)
    e_neg = jnp.exp(-L)
    e_end = jnp.exp(l_end - L)
    p_end = jnp.exp(l_end)
    at = (-kk) * jnp.exp(L - lw)
    rt = r * jnp.exp(L)
    bt = (beta * e_neg).astype(BF16)
    kt = (kmod * e_neg).astype(BF16)
    bend = beta * e_end
    kend = kmod * e_end

    chains = []
    for b in range(nb):
        for gi in range(N_GROUPS):
            sl = slice(gi * GROUP_LANES, (gi + 1) * GROUP_LANES)
            chains.append(dict(
                b=b, gi=gi, vg=_rows(v, b, C)[:, sl].astype(BF16), p_end=_rows(p_end, b, C)[:, sl],
                x2=jnp.concatenate([_rows(at, b, C)[:, sl], _rows(rt, b, C)[:, sl]], axis=0).astype(BF16),
                wbd=jnp.concatenate([_block_diag(_rows(bt, b, C)[:, sl], mrow_ref),
                                     _block_diag(_rows(kt, b, C)[:, sl], mrow_ref)], axis=0),
                bk=jnp.concatenate([_rows(bend, b, C)[:, sl], _rows(kend, b, C)[:, sl]], axis=0).astype(BF16)))

    cw = GROUP_HEADS * C
    rowseq = (lax.broadcasted_iota(jnp.int32, (2 * C, 1), 0) % C) // seq
    for c in chains:
        res = _dot_tb(c['x2'], c['wbd'])
        n_cat = jnp.where(m_strict, res[:C, :cw], 0.0)
        c['a_ak'] = jnp.where(m_strict, res[:C, cw:], 0.0).astype(BF16)
        c['a_r'] = jnp.concatenate([jnp.where(m_incl, res[C:, :cw], 0.0),
                                    jnp.where(m_incl, res[C:, cw:], 0.0)], axis=1).astype(BF16)
        c['x_c'] = n_cat.astype(BF16)
        c['p_c'] = eye_cat + n_cat
        c['x_bd'] = _block_diag(c['x_c'], msq_ref)
    for c in chains:
        xs = None
        for q in range(Q):
            xq = _dot_tb(c['x2'], st_s[c['b'], q, c['gi']].astype(BF16))
            xs = xq if xs is None else jnp.where(rowseq == q, xq, xs)
        c['xs'] = xs
        c['v_bd'] = _block_diag(c['vg'], msq_ref)
    for c in chains:
        c['rhs'] = c['xs'][:C] + _dot(c['a_ak'], c['v_bd'])
    lvl = 2
    while lvl < seq:
        for c in chains:
            c['x_c'] = _dot(c['x_c'], c['x_bd']).astype(BF16)
            c['x_bd'] = _block_diag(c['x_c'], msq_ref)
        for c in chains:
            c['p_c'] = c['p_c'] + _dot(c['p_c'].astype(BF16), c['x_bd'])
        lvl *= 2
    for c in chains:
        c['u'] = _dot(c['p_c'].astype(BF16), _block_diag(c['rhs'].astype(BF16), msq_ref)).astype(BF16)
    ys = {}
    for c in chains:
        b, gi = c['b'], c['gi']
        ys[(b, gi)] = c['xs'][C:] + _dot(c['a_r'], jnp.concatenate([_block_diag(c['u'], msq_ref), c['v_bd']], axis=0))
        uv = jnp.concatenate([c['u'], c['vg']], axis=0)
        if Q > 1:
            uv = jnp.concatenate([jnp.where(rowseq == q, uv, jnp.zeros_like(uv)) for q in range(Q)], axis=1)
        ds = _dot_ta(uv, c['bk'])
        for q in range(Q):
            st_s[b, q, gi] = (st_s[b, q, gi] * c['p_end'][q * seq:q * seq + 1, :]
                              + _rows(ds, q, GROUP_LANES) * msq_f)

    yc = jnp.concatenate([jnp.concatenate([ys[(b, gi)] for gi in range(N_GROUPS)], axis=1) for b in range(nb)],
                         axis=0)
    inv_n = 1.0 / RWKV_HEAD
    mean = head_sum(yc) * inv_n
    dlt = yc - mean
    var = head_sum(dlt * dlt) * inv_n
    yn = dlt * lax.rsqrt(var + GN_EPS) * lnw_ref[...] + lnb_ref[...]
    bonus = head_sum(r * kmod * rk_ref[...]) * v
    out = ((yn + bonus) * g).astype(BF16)
    for b in range(nb):
        y_ref[b] = _rows(out, b, C)

    @pl.when(last)
    def _():
        for b in range(nb):
            for q in range(Q):
                for gi in range(N_GROUPS):
                    for hh in range(GROUP_HEADS):
                        ns_ref[b, q * RWKV_HEADS + gi * GROUP_HEADS + hh] = st_s[
                            b, q, gi, hh * RWKV_HEAD:(hh + 1) * RWKV_HEAD, hh * RWKV_HEAD:(hh + 1) * RWKV_HEAD]


def _wkv_consts(nb, seq):
    C = CHUNK
    gh = GROUP_HEADS
    rows = np.arange(nb * C)
    same = (rows[:, None] // seq) == (rows[None, :] // seq)
    tri = (same & (rows[None, :] <= rows[:, None])).astype(np.float32)
    endm = same.astype(np.float32)
    hl = np.arange(MXU_DIM) // RWKV_HEAD
    ones_bd = (hl[:, None] == hl[None, :]).astype(np.float32)
    rowh = np.arange(gh * C) // C
    colh = np.arange(GROUP_LANES) // RWKV_HEAD
    mrow = (rowh[:, None] == colh[None, :]).astype(np.float32)
    msq = (rowh[:, None] == rowh[None, :]).astype(np.float32)
    return tuple(jnp.asarray(m, BF16) for m in (tri, endm, ones_bd, mrow, msq))


def _wkv(ps3, st_shift, st_rwkv_all, new_rwkv_all, P, layer, nb, zero_state):
    B, T, _ = ps3.shape
    C = CHUNK
    seq = C if T % C == 0 else T
    assert C % seq == 0
    Q = C // seq
    G = B // Q
    tg = T * Q
    assert B % Q == 0 and G % nb == 0 and tg % C == 0
    o = RWKV_WIDTH
    consts = _wkv_consts(nb, seq)
    vec = lambda w: pl.BlockSpec((None, 1, w), lambda b, c: (layer, 0, 0))
    const = lambda arr: pl.BlockSpec(arr.shape, lambda b, c: (0,) * arr.ndim)
    kern = functools.partial(_wkv_kernel, nb=nb, seq=seq, zero_state=zero_state)
    shift_spec = pl.BlockSpec((nb, Q, SHIFT_WIDTH), lambda b, c: (b, 0, 0))
    state_shape = (DEPTH, G, Q * RWKV_HEADS, RWKV_HEAD, RWKV_HEAD)
    state_block = (None, nb, Q * RWKV_HEADS, RWKV_HEAD, RWKV_HEAD)
    state_index = lambda b, c: (layer, b, 0, 0, 0)
    in_specs = [
        pl.BlockSpec((nb, C, SHIFT_WIDTH), lambda b, c: (b, c, 0)),
        shift_spec,
        pl.BlockSpec(state_block, state_index, pipeline_mode=pl.Buffered(1)),
        vec(SHIFT_WIDTH),
        vec(o),
        pl.BlockSpec((None, LORA_W + LORA_A, 2 * o), lambda b, c: (layer, 0, 0)),
        pl.BlockSpec((None, LORA_G, o), lambda b, c: (layer, 0, 0)),
        vec(o), vec(o), vec(o), vec(o), vec(o), vec(o),
    ] + [const(m) for m in consts] + [pl.BlockSpec(memory_space=pl.ANY)]
    args = [ps3.reshape(G, tg, SHIFT_WIDTH), st_shift.reshape(G, Q, SHIFT_WIDTH), st_rwkv_all.reshape(state_shape),
            P['shift_mu'], P['w0'], P['lora_wa'], P['g_lora_up'], P['a0'], P['k_k'], P['k_a'],
            P['r_k'], P['ln_x_w'], P['ln_x_b'], *consts, new_rwkv_all.reshape(state_shape)]
    y, nsh, ns = pl.pallas_call(
        kern,
        grid=(G // nb, tg // C),
        in_specs=in_specs,
        out_specs=[
            pl.BlockSpec((nb, C, o), lambda b, c: (b, c, 0)),
            shift_spec,
            pl.BlockSpec(state_block, state_index),
        ],
        out_shape=[
            jax.ShapeDtypeStruct((G, tg, o), BF16),
            jax.ShapeDtypeStruct((G, Q, SHIFT_WIDTH), F32),
            jax.ShapeDtypeStruct(state_shape, F32),
        ],
        scratch_shapes=[
            pltpu.VMEM((nb, Q, SHIFT_WIDTH), F32),
            pltpu.VMEM((nb, Q, N_GROUPS, MXU_DIM, MXU_DIM), F32),
        ],
        input_output_aliases={len(args) - 1: 2},
        compiler_params=_cparams(("arbitrary", "arbitrary")),
        name="rwkv",
    )(*args)
    return (y.reshape(B, T, o), nsh.reshape(B, SHIFT_WIDTH),
            ns.reshape(DEPTH, B, RWKV_HEADS, RWKV_HEAD, RWKV_HEAD))


def _merge_kernel(x_ref, ya_ref, yb_ref, gt_ref, wa_ref, wb_ref, wo_ref, nf_ref, *rest, moe):
    if moe:
        wr_ref, xo_ref, gates_ref, cnt_ref = rest
    else:
        xo_ref, h_ref = rest
    ga = gt_ref[:, :D_MODEL]
    gb = gt_ref[:, D_MODEL:]
    m = _sigmoid(ga) * _dot(ya_ref[...], wa_ref[...]) + _sigmoid(gb) * _dot(yb_ref[...], wb_ref[...])
    x = x_ref[...] + _dot(m.astype(BF16), wo_ref[...])
    xo_ref[...] = x
    h = _rms(x, nf_ref[...])
    h_hi = h.astype(BF16)
    if not moe:
        h_ref[...] = h_hi
    else:
        w = wr_ref[...]
        w_hi = w.astype(BF16)
        w_lo = (w - w_hi.astype(F32)).astype(BF16)
        h_lo = (h - h_hi.astype(F32)).astype(BF16)
        logits = _dot(h_hi, w_hi) + _dot(h_lo, w_hi) + _dot(h_hi, w_lo)
        lane = lax.broadcasted_iota(jnp.int32, logits.shape, 1)
        real = lane < N_EXPERTS
        logits = jnp.where(real, logits, -jnp.inf)
        e = jnp.exp(logits - jnp.max(logits, axis=-1, keepdims=True))
        p = jnp.where(real, e / jnp.sum(e, axis=-1, keepdims=True), -1.0)
        m1 = jnp.max(p, axis=-1, keepdims=True)
        i1 = jnp.min(jnp.where(p == m1, lane, LANES), axis=-1, keepdims=True)
        oh1 = lane == i1
        p2 = jnp.where(oh1, -1.0, p)
        m2 = jnp.max(p2, axis=-1, keepdims=True)
        i2 = jnp.min(jnp.where(p2 == m2, lane, LANES), axis=-1, keepdims=True)
        oh2 = lane == i2
        tot = m1 + m2
        gates = jnp.where(oh1, m1 / tot, 0.0) + jnp.where(oh2, m2 / tot, 0.0)
        gates_ref[...] = gates
        cnt = jnp.sum((gates > 0.0).astype(F32), axis=0, keepdims=True)
        cnt_ref[...] = jnp.broadcast_to(cnt, (SUBLANES, LANES))


def _merge(x, ya, yb, gt, P, layer, tm, moe):
    n = x.shape[0]
    row = lambda w: pl.BlockSpec((tm, w), lambda i: (i, 0))
    wsq = lambda: pl.BlockSpec((None, D_MODEL, D_MODEL), lambda i: (layer, 0, 0))
    in_specs = [row(D_MODEL), row(D_MODEL), row(D_MODEL), row(2 * D_MODEL), wsq(), wsq(), wsq(),
                pl.BlockSpec((None, 1, D_MODEL), lambda i: (layer, 0, 0))]
    args = [x, ya, yb, gt, P['w_out_a'], P['w_out_b'], P['w_out'], P['norm_ffn']]
    out_specs = [row(D_MODEL)]
    out_shape = [jax.ShapeDtypeStruct((n, D_MODEL), F32)]
    if moe:
        in_specs.append(pl.BlockSpec((None, D_MODEL, LANES), lambda i: (layer // 2, 0, 0)))
        args.append(P['w_router_pad'])
        out_specs.append(row(LANES))
        out_shape.append(jax.ShapeDtypeStruct((n, LANES), F32))
        out_specs.append(pl.BlockSpec((None, SUBLANES, LANES), lambda i: (i, 0, 0)))
        out_shape.append(jax.ShapeDtypeStruct((n // tm, SUBLANES, LANES), F32))
    else:
        out_specs.append(row(D_MODEL))
        out_shape.append(jax.ShapeDtypeStruct((n, D_MODEL), BF16))
    return pl.pallas_call(
        functools.partial(_merge_kernel, moe=moe),
        grid=(n // tm,),
        in_specs=in_specs,
        out_specs=out_specs,
        out_shape=out_shape,
        compiler_params=_cparams(("arbitrary",)),
        name="merge",
    )(*args)


def _swiglu_acc(h, wg_ref, wu_ref, wd_ref):
    acc = None
    for c0 in range(0, D_FF, FF_CHUNK):
        sl = slice(c0, c0 + FF_CHUNK)
        gate = _dot(h, wg_ref[:, sl])
        up = _dot(h, wu_ref[:, sl])
        act = (gate * _sigmoid(gate) * up).astype(BF16)
        part = _dot(act, wd_ref[sl, :])
        acc = part if acc is None else acc + part
    return acc


def _ffn_kernel(x_ref, h_ref, wg_ref, wu_ref, wd_ref, o_ref):
    o_ref[...] = x_ref[...] + _swiglu_acc(h_ref[...], wg_ref, wu_ref, wd_ref)


def _ffn(x, h, P, j, tm):
    n = x.shape[0]
    row = lambda: pl.BlockSpec((tm, D_MODEL), lambda i: (i, 0))
    return pl.pallas_call(
        _ffn_kernel,
        grid=(n // tm,),
        in_specs=[row(), row(),
                  pl.BlockSpec((None, D_MODEL, D_FF), lambda i: (j, 0, 0), pipeline_mode=pl.Buffered(1)),
                  pl.BlockSpec((None, D_MODEL, D_FF), lambda i: (j, 0, 0), pipeline_mode=pl.Buffered(1)),
                  pl.BlockSpec((None, D_FF, D_MODEL), lambda i: (j, 0, 0), pipeline_mode=pl.Buffered(1))],
        out_specs=row(),
        out_shape=jax.ShapeDtypeStruct((n, D_MODEL), F32),
        compiler_params=_cparams(("arbitrary",)),
        name="ffn",
    )(x, h, P['w_ffn_gate'], P['w_ffn_up'], P['w_ffn_down'])


def _route_kernel(gates_ref, base_ref, pos_ref, *, tw, spare):
    mask = (gates_ref[...] > 0.0).astype(BF16)
    r_idx = lax.broadcasted_iota(jnp.int32, (tw, tw), 0)
    c_idx = lax.broadcasted_iota(jnp.int32, (tw, tw), 1)
    excl = _dot_ta(mask, (r_idx < c_idx).astype(BF16))
    sel = _dot_ta(mask, (r_idx == c_idx).astype(BF16)) > 0.0
    pos = excl + base_ref[:, 0:1]
    e_io = lax.broadcasted_iota(jnp.int32, (LANES, tw), 0)
    e_first = jnp.min(jnp.where(sel, e_io, LANES), axis=0, keepdims=True)
    e_last = jnp.max(jnp.where(sel, e_io, -1), axis=0, keepdims=True)
    p_first = jnp.sum(jnp.where(e_io == e_first, pos, 0.0), axis=0, keepdims=True)
    p_last = jnp.sum(jnp.where(e_io == e_last, pos, 0.0), axis=0, keepdims=True)
    p_last = jnp.where(e_last == e_first, float(spare), p_last)
    rows = jnp.concatenate([p_first, p_last, jnp.zeros((SUBLANES - 2, tw), F32)], axis=0)
    pos_ref[...] = rows.astype(jnp.int32)


def _route(gates, base_b, tw, spare):
    n = gates.shape[0]
    return pl.pallas_call(
        functools.partial(_route_kernel, tw=tw, spare=spare),
        grid=(n // tw,),
        in_specs=[pl.BlockSpec((tw, LANES), lambda i: (i, 0)),
                  pl.BlockSpec((None, LANES, LANES), lambda i: (i, 0, 0))],
        out_specs=pl.BlockSpec((None, SUBLANES, tw), lambda i: (i, 0, 0)),
        out_shape=jax.ShapeDtypeStruct((n // tw, SUBLANES, tw), jnp.int32),
        compiler_params=_cparams(("arbitrary",)),
        name="route",
    )(gates, base_b)


def _row_copy(src_hbm, src_row, dst_ref, dst_row, sem):
    return pltpu.make_async_copy(src_hbm.at[pl.ds(src_row, 1), :], dst_ref.at[pl.ds(dst_row, 1), :], sem)


def _disperse_kernel(pos_ref, x_ref, init_hbm, xs_hbm, sem, *, tw):
    del init_hbm

    def issue(n, carry):
        _row_copy(x_ref, n, xs_hbm, pos_ref[0, n], sem).start()
        _row_copy(x_ref, n, xs_hbm, pos_ref[1, n], sem).start()
        return carry

    def drain(n, carry):
        _row_copy(x_ref, 0, xs_hbm, 0, sem).wait()
        _row_copy(x_ref, 0, xs_hbm, 0, sem).wait()
        return carry

    lax.fori_loop(0, tw, issue, 0, unroll=8)
    lax.fori_loop(0, tw, drain, 0, unroll=8)


def _disperse(pos, x, n_rows, tw):
    n = x.shape[0]
    return pl.pallas_call(
        functools.partial(_disperse_kernel, tw=tw),
        grid=(n // tw,),
        in_specs=[pl.BlockSpec((None, SUBLANES, tw), lambda i: (i, 0, 0), memory_space=pltpu.SMEM),
                  pl.BlockSpec((tw, D_MODEL), lambda i: (i, 0)),
                  pl.BlockSpec(memory_space=pl.ANY)],
        out_specs=pl.BlockSpec(memory_space=pl.ANY),
        out_shape=jax.ShapeDtypeStruct((n_rows, D_MODEL), F32),
        scratch_shapes=[pltpu.SemaphoreType.DMA],
        input_output_aliases={2: 0},
        compiler_params=_cparams(("arbitrary",)),
        name="disperse",
    )(pos, x, jnp.zeros((n_rows, D_MODEL), F32))


def _experts_kernel(te_ref, used_ref, xs_ref, nf_ref, wg_ref, wu_ref, wd_ref, ys_ref):
    del te_ref
    j = pl.program_id(0)

    @pl.when(j < used_ref[0])
    def _():
        h = _rms(xs_ref[...], nf_ref[...]).astype(BF16)
        ys_ref[...] = _swiglu_acc(h, wg_ref, wu_ref, wd_ref)

    @pl.when(j >= used_ref[0])
    def _():
        ys_ref[...] = jnp.zeros(ys_ref.shape, F32)


def _experts(tile_expert, used, xs, P, layer, j):
    n_tiles = tile_expert.shape[0]
    tr = xs.shape[0] // n_tiles
    wspec = lambda a, b: pl.BlockSpec((None, None, a, b), lambda t, te, u: (j, te[t], 0, 0))
    grid_spec = pltpu.PrefetchScalarGridSpec(
        num_scalar_prefetch=2,
        grid=(n_tiles,),
        in_specs=[pl.BlockSpec((tr, D_MODEL), lambda t, te, u: (t, 0)),
                  pl.BlockSpec((None, 1, D_MODEL), lambda t, te, u: (layer, 0, 0)),
                  wspec(D_MODEL, D_FF), wspec(D_MODEL, D_FF), wspec(D_FF, D_MODEL)],
        out_specs=pl.BlockSpec((tr, D_MODEL), lambda t, te, u: (t, 0)),
    )
    return pl.pallas_call(
        _experts_kernel,
        grid_spec=grid_spec,
        out_shape=jax.ShapeDtypeStruct((n_tiles * tr, D_MODEL), F32),
        compiler_params=_cparams(("arbitrary",)),
        name="experts",
    )(tile_expert, used, xs, P['norm_ffn'], P['w_moe_gate'], P['w_moe_up'], P['w_moe_down'])


def _combine_kernel(pos_ref, x_ref, gates_ref, ys_hbm, *rest, tw, final):
    nf_ref = rest[0] if final else None
    o_ref, y1_s, y2_s, sem = rest[1:] if final else rest

    def issue(n, carry):
        _row_copy(ys_hbm, pos_ref[0, n], y1_s, n, sem).start()
        _row_copy(ys_hbm, pos_ref[1, n], y2_s, n, sem).start()
        return carry

    def drain(n, carry):
        _row_copy(ys_hbm, 0, y1_s, 0, sem).wait()
        _row_copy(ys_hbm, 0, y2_s, 0, sem).wait()
        return carry

    lax.fori_loop(0, tw, issue, 0, unroll=8)
    gates = gates_ref[...]
    lane = lax.broadcasted_iota(jnp.int32, gates.shape, 1)
    sel = gates > 0.0
    e_first = jnp.min(jnp.where(sel, lane, LANES), axis=-1, keepdims=True)
    e_last = jnp.max(jnp.where(sel, lane, -1), axis=-1, keepdims=True)
    g_first = jnp.sum(jnp.where(lane == e_first, gates, 0.0), axis=-1, keepdims=True)
    g_last = jnp.sum(jnp.where((lane == e_last) & (e_last != e_first), gates, 0.0), axis=-1, keepdims=True)
    lax.fori_loop(0, tw, drain, 0, unroll=8)
    x = x_ref[...] + g_first * y1_s[...] + g_last * y2_s[...]
    o_ref[...] = _rms(x, nf_ref[...]) if final else x


def _combine(pos, x, gates, ys, tw, norm_final=None):
    n = x.shape[0]
    final = norm_final is not None
    row = pl.BlockSpec((tw, D_MODEL), lambda i: (i, 0))
    in_specs = [pl.BlockSpec((None, SUBLANES, tw), lambda i: (i, 0, 0), memory_space=pltpu.SMEM),
                row, pl.BlockSpec((tw, LANES), lambda i: (i, 0)), pl.BlockSpec(memory_space=pl.ANY)]
    args = [pos, x, gates, ys]
    if final:
        in_specs.append(pl.BlockSpec((1, D_MODEL), lambda i: (0, 0)))
        args.append(norm_final)
    return pl.pallas_call(
        functools.partial(_combine_kernel, tw=tw, final=final),
        grid=(n // tw,),
        in_specs=in_specs,
        out_specs=row,
        out_shape=jax.ShapeDtypeStruct((n, D_MODEL), F32),
        scratch_shapes=[pltpu.VMEM((tw, D_MODEL), F32), pltpu.VMEM((tw, D_MODEL), F32), pltpu.SemaphoreType.DMA],
        compiler_params=_cparams(("arbitrary",)),
        name="combine",
    )(*args)


def _moe_routed(x, gates, cnt, P, layer, tm):
    n = x.shape[0]
    tw = min(ROUTE_TILE, n)
    tr = ROUTE_ROWS if TOP_K * n >= 4 * N_EXPERTS * ROUTE_ROWS else ROUTE_ROWS_SMALL
    n_tiles = (TOP_K * n) // tr + N_EXPERTS
    spare = n_tiles * tr
    c = cnt[:, 0, :].astype(jnp.int32).reshape(n // tw, tw // tm, LANES).sum(axis=1)
    seg_tiles = (c.sum(axis=0) + tr - 1) // tr
    seg_end = jnp.cumsum(seg_tiles)
    base = (seg_end - seg_tiles) * tr + jnp.cumsum(c, axis=0) - c
    base_b = jnp.broadcast_to(base.astype(F32)[:, :, None], (n // tw, LANES, LANES))
    tile_expert = jnp.minimum((jnp.arange(n_tiles + 1)[:, None] >= seg_end[None, :N_EXPERTS]).sum(axis=1),
                              N_EXPERTS - 1).astype(jnp.int32)
    used = seg_end[N_EXPERTS - 1:N_EXPERTS].astype(jnp.int32)
    pos = _route(gates, base_b, tw, spare)
    xs = _disperse(pos, x, spare + tr, tw)
    ys = _experts(tile_expert, used, xs, P, layer, layer // 2)
    return _combine(pos, x, gates, ys, tw, P['norm_final'] if layer == DEPTH - 1 else None)


def _block_diag_weights(w):
    d = w.shape[0]
    per = MXU_DIM // LRU_BLOCK
    w = w.reshape(d, LRU_HEADS // per, per, LRU_BLOCK, LRU_BLOCK)
    eye = jnp.eye(per, dtype=w.dtype)
    bd = jnp.einsum('dcpij,pq->dcpiqj', w, eye)
    return bd.reshape(d, LRU_HEADS // per, MXU_DIM, MXU_DIM).astype(BF16)


def _prep_params(p):
    P = dict(p)
    for name in ('w_in', 'w_out_a', 'w_out_b', 'w_out', 'w_ffn_gate', 'w_ffn_up', 'w_ffn_down',
                 'w_moe_gate', 'w_moe_up', 'w_moe_down', 'g_lora_up'):
        P[name] = p[name].astype(BF16)
    for name in ('norm_mix', 'conv_b', 'gate_a_b', 'gate_x_b', 'lru_lambda', 'shift_mu', 'w0', 'a0', 'k_k', 'k_a',
                 'ln_x_w', 'ln_x_b', 'norm_ffn'):
        P[name] = p[name][:, None, :]
    P['r_k'] = p['r_k'].reshape(DEPTH, 1, RWKV_WIDTH)
    P['norm_final'] = p['norm_final'][None, :]
    P['gaw_bd'] = _block_diag_weights(p['gate_a_w'])
    P['gxw_bd'] = _block_diag_weights(p['gate_x_w'])
    z = jnp.zeros((DEPTH, LORA_W, RWKV_WIDTH), F32)
    P['lora_wa'] = jnp.concatenate([jnp.concatenate([p['w_lora_up'], z], axis=2),
                                    jnp.concatenate([z, p['a_lora_up']], axis=2)], axis=1).astype(BF16)
    P['w_router_pad'] = jnp.pad(p['w_router'], ((0, 0), (0, 0), (0, LANES - N_EXPERTS)))
    return P


def _trunk(x3, st_rwkv, st_lru, st_conv, st_shift, P, *, zero_state, tm, lru_bb, lru_tt, rwkv_nb):
    B, T, _ = x3.shape
    n = B * T
    x = x3.reshape(n, D_MODEL)
    new_rwkv, n_lru, n_conv, n_shift = jnp.zeros(st_rwkv.shape, F32), [], [], []
    for l in range(DEPTH):
        xy, ps, gt = _proj(x, P['norm_mix'], P['w_in'], l, tm)
        ya, h_new, c_new = _lru(xy.reshape(B, T, 2 * LRU_WIDTH), st_conv[l], st_lru[l][:, None, :], P, l,
                                lru_bb, lru_tt)
        yb, sh_new, new_rwkv = _wkv(ps.reshape(B, T, SHIFT_WIDTH), st_shift[l], st_rwkv, new_rwkv, P, l, rwkv_nb,
                                    zero_state)
        moe = l % 2 == 1
        res = _merge(x, ya.reshape(n, D_MODEL), yb.reshape(n, D_MODEL), gt, P, l, tm, moe)
        if moe:
            x, gates, cnt = res
            x = _moe_routed(x, gates, cnt, P, l, tm)
        else:
            x, h = res
            x = _ffn(x, h, P, l // 2, tm)
        n_lru.append(h_new[:, 0, :])
        n_conv.append(c_new)
        n_shift.append(sh_new)
    assert DEPTH % 2 == 0
    return x.reshape(B, T, D_MODEL), new_rwkv, jnp.stack(n_lru), jnp.stack(n_conv), jnp.stack(n_shift)


def kernel(x_prompt, x_sample, state_rwkv, state_lru, state_conv, state_shift, norm_mix, w_in, conv_w, conv_b, gate_a_w, gate_a_b, gate_x_w, gate_x_b, lru_lambda, shift_mu, w0, w_lora_up, a0, a_lora_up, g_lora_up, k_k, k_a, r_k, ln_x_w, ln_x_b, w_out_a, w_out_b, w_out, norm_ffn, w_ffn_gate, w_ffn_up, w_ffn_down, w_router, w_moe_gate, w_moe_up, w_moe_down, norm_final):
    P = _prep_params(dict(
        norm_mix=norm_mix, w_in=w_in, conv_w=conv_w, conv_b=conv_b, gate_a_w=gate_a_w, gate_a_b=gate_a_b,
        gate_x_w=gate_x_w, gate_x_b=gate_x_b, lru_lambda=lru_lambda, shift_mu=shift_mu, w0=w0,
        w_lora_up=w_lora_up, a0=a0, a_lora_up=a_lora_up, g_lora_up=g_lora_up, k_k=k_k, k_a=k_a, r_k=r_k,
        ln_x_w=ln_x_w, ln_x_b=ln_x_b, w_out_a=w_out_a, w_out_b=w_out_b, w_out=w_out, norm_ffn=norm_ffn,
        w_ffn_gate=w_ffn_gate, w_ffn_up=w_ffn_up, w_ffn_down=w_ffn_down, w_router=w_router,
        w_moe_gate=w_moe_gate, w_moe_up=w_moe_up, w_moe_down=w_moe_down, norm_final=norm_final))
    bp = x_prompt.shape[0]
    bs = x_sample.shape[0]
    zeros = lambda *shape: jnp.zeros(shape, F32)
    p_out = _trunk(x_prompt,
                   zeros(DEPTH, bp, RWKV_HEADS, RWKV_HEAD, RWKV_HEAD), zeros(DEPTH, bp, LRU_WIDTH),
                   zeros(DEPTH, bp, CONV_WIDTH - 1, LRU_WIDTH), zeros(DEPTH, bp, SHIFT_WIDTH), P,
                   zero_state=True, tm=512, lru_bb=1, lru_tt=256, rwkv_nb=4)
    s_out = _trunk(x_sample, state_rwkv, state_lru, state_conv, state_shift, P,
                   zero_state=False, tm=256, lru_bb=8, lru_tt=x_sample.shape[1], rwkv_nb=1)
    return (p_out[0], s_out[0]) + tuple(p_out[1:]) + tuple(s_out[1:])
```

```python
import functools

import numpy as np
import jax
import jax.numpy as jnp
from jax import lax
from jax.experimental import pallas as pl
from jax.experimental.pallas import tpu as pltpu

F32 = jnp.float32
BF16 = jnp.bfloat16

D_MODEL = 1024
DEPTH = 4
LRU_WIDTH = D_MODEL
LRU_HEADS = 16
LRU_BLOCK = LRU_WIDTH // LRU_HEADS
CONV_WIDTH = 4
LRU_C = 8.0
RWKV_HEAD = 64
RWKV_WIDTH = D_MODEL
RWKV_HEADS = RWKV_WIDTH // RWKV_HEAD
LORA_W = 64
LORA_A = 64
LORA_G = 128
SHIFT_WIDTH = 3 * RWKV_WIDTH + LORA_W + LORA_A + LORA_G
PROJ_WIDTH = 2 * LRU_WIDTH + SHIFT_WIDTH + 2 * D_MODEL
D_FF = 2816
N_EXPERTS = 8
NORM_EPS = 1e-6
GN_EPS = 64e-5

LANES = 128
SUBLANES = 8
MXU_DIM = 256
VMEM_LIMIT = 56 * 1024 * 1024

CHUNK = 64
GROUP_HEADS = MXU_DIM // CHUNK
GROUP_LANES = GROUP_HEADS * RWKV_HEAD
N_GROUPS = RWKV_HEADS // GROUP_HEADS
FF_CHUNK = 256
TOP_K = 2
ROUTE_TILE = 512
ROUTE_ROWS = 512
ROUTE_ROWS_SMALL = 128
SCAN_ROWS = 16
assert CHUNK == RWKV_HEAD


def _cparams(sem):
    return pltpu.CompilerParams(dimension_semantics=sem, vmem_limit_bytes=VMEM_LIMIT)


def _dot(a, b):
    return jnp.dot(a, b, preferred_element_type=F32)


def _dot_tb(a, b):
    return lax.dot_general(a, b, (((1,), (1,)), ((), ())), preferred_element_type=F32)


def _dot_ta(a, b):
    return lax.dot_general(a, b, (((0,), (0,)), ((), ())), preferred_element_type=F32)


def _sigmoid(x):
    return jax.nn.sigmoid(x)


def _softplus(z):
    return jnp.maximum(z, 0.0) + jnp.log(1.0 + jnp.exp(-jnp.abs(z)))


def _rms(x, g):
    return x * lax.rsqrt(jnp.mean(x * x, axis=-1, keepdims=True) + NORM_EPS) * g


_PROJ_SPLITS = ((0, 2 * LRU_WIDTH), (2 * LRU_WIDTH, SHIFT_WIDTH), (2 * LRU_WIDTH + SHIFT_WIDTH, 2 * D_MODEL))


def _proj_kernel(x_ref, g_ref, w_ref, xy_ref, ps_ref, gt_ref):
    u = _rms(x_ref[...], g_ref[...]).astype(BF16)
    for out_ref, (c0, width) in zip((xy_ref, ps_ref, gt_ref), _PROJ_SPLITS):
        j = 0
        while j < width:
            w = min(512, width - j)
            out_ref[:, j:j + w] = _dot(u, w_ref[:, c0 + j:c0 + j + w])
            j += w


def _proj(x, g_all, w_all, layer, tm):
    n = x.shape[0]
    return pl.pallas_call(
        _proj_kernel,
        grid=(n // tm,),
        in_specs=[
            pl.BlockSpec((tm, D_MODEL), lambda i: (i, 0)),
            pl.BlockSpec((None, 1, D_MODEL), lambda i: (layer, 0, 0)),
            pl.BlockSpec((None, D_MODEL, PROJ_WIDTH), lambda i: (layer, 0, 0),
                         pipeline_mode=pl.Buffered(1)),
        ],
        out_specs=[
            pl.BlockSpec((tm, 2 * LRU_WIDTH), lambda i: (i, 0)),
            pl.BlockSpec((tm, SHIFT_WIDTH), lambda i: (i, 0)),
            pl.BlockSpec((tm, 2 * D_MODEL), lambda i: (i, 0)),
        ],
        out_shape=[
            jax.ShapeDtypeStruct((n, 2 * LRU_WIDTH), F32),
            jax.ShapeDtypeStruct((n, SHIFT_WIDTH), F32),
            jax.ShapeDtypeStruct((n, 2 * D_MODEL), F32),
        ],
        compiler_params=_cparams(("arbitrary",)),
        name="proj",
    )(x, g_all, w_all)


def _gelu_tanh(x):
    return 0.5 * x * (1.0 + jnp.tanh(np.sqrt(2.0 / np.pi).astype(np.float32) * (x + 0.044715 * (x * x * x))))


def _lru_kernel(xa_ref, ya_ref, cs_ref, h0_ref, cw_ref, cb_ref, gaw_ref, gab_ref, gxw_ref, gxb_ref, lam_ref,
                y_ref, hl_ref, nc_ref, xp_s, h_s, a_s, b_s, *, bb, tt):
    ti = pl.program_id(1)
    neg_c = -LRU_C * _softplus(-lam_ref[...])
    cw = cw_ref[...]
    @pl.when(ti == 0)
    def _():
        for b in range(bb):
            xp_s[b, 5:8, :] = cs_ref[b]
            h_s[b] = h0_ref[b]

    for b in range(bb):
        xa = xa_ref[b]
        xp_s[b, 8:8 + tt, :] = xa
        xc = cb_ref[...] + xp_s[b, 5:5 + tt, :] * cw[0:1]
        xc = xc + xp_s[b, 6:6 + tt, :] * cw[1:2]
        xc = xc + xp_s[b, 7:7 + tt, :] * cw[2:3]
        xc = xc + xa * cw[3:4]
        tail = xp_s[b, 5 + tt:8 + tt, :]
        xp_s[b, 5:8, :] = tail
        nc_ref[b] = tail

        for c in range(LRU_WIDTH // MXU_DIM):
            sl = slice(c * MXU_DIM, (c + 1) * MXU_DIM)
            xcc = xc[:, sl]
            xcb = xcc.astype(BF16)
            r = _sigmoid(_dot(xcb, gaw_ref[c]) + gab_ref[:, sl])
            i = _sigmoid(_dot(xcb, gxw_ref[c]) + gxb_ref[:, sl])
            log_a = neg_c[:, sl] * r
            a = jnp.exp(log_a)
            a_s[:, sl] = a
            b_s[:, sl] = jnp.sqrt(-jnp.tanh(log_a) * (a * a + 1.0)) * (i * xcc)

        if tt % SCAN_ROWS == 0:
            row = lax.broadcasted_iota(jnp.int32, (SUBLANES, LRU_WIDTH), 0)

            def blk(j, h):
                r0 = pl.multiple_of(j * SCAN_ROWS, SCAN_ROWS)
                hbs = []
                for q in range(SCAN_ROWS // SUBLANES):
                    av = a_s[pl.ds(r0 + q * SUBLANES, SUBLANES), :]
                    bv = b_s[pl.ds(r0 + q * SUBLANES, SUBLANES), :]
                    for d in (1, 2, 4):
                        m = row >= d
                        a_sh = pltpu.roll(av, d, axis=0)
                        b_sh = pltpu.roll(bv, d, axis=0)
                        bv = jnp.where(m, av * b_sh + bv, bv)
                        av = jnp.where(m, av * a_sh, av)
                    hb = av * h + bv
                    h = hb[SUBLANES - 1:SUBLANES, :]
                    hbs.append(hb)
                hs = jnp.concatenate(hbs, axis=0)
                y_ref[b, pl.ds(r0, SCAN_ROWS), :] = (hs * _gelu_tanh(ya_ref[b, pl.ds(r0, SCAN_ROWS), :])).astype(BF16)
                return h

            h = lax.fori_loop(0, tt // SCAN_ROWS, blk, h_s[b])
        else:
            h = h_s[b]
            rows = []
            for t in range(tt):
                h = a_s[t:t + 1, :] * h + b_s[t:t + 1, :]
                rows.append(h)
            hs = jnp.concatenate(rows, axis=0)
            y_ref[b] = (hs * _gelu_tanh(ya_ref[b])).astype(BF16)
        h_s[b] = h
        hl_ref[b] = h


def _lru(xy3, st_conv, st_lru, P, layer, bb, tt):
    B, T, _ = xy3.shape
    W = LRU_WIDTH
    vec = lambda: pl.BlockSpec((None, 1, W), lambda b, t: (layer, 0, 0))
    kern = functools.partial(_lru_kernel, bb=bb, tt=tt)
    return pl.pallas_call(
        kern,
        grid=(B // bb, T // tt),
        in_specs=[
            pl.BlockSpec((bb, tt, W), lambda b, t: (b, t, 0)),
            pl.BlockSpec((bb, tt, W), lambda b, t: (b, t, 1)),
            pl.BlockSpec((bb, CONV_WIDTH - 1, W), lambda b, t: (b, 0, 0)),
            pl.BlockSpec((bb, 1, W), lambda b, t: (b, 0, 0)),
            pl.BlockSpec((None, CONV_WIDTH, W), lambda b, t: (layer, 0, 0)),
            vec(),
            pl.BlockSpec((None, W // MXU_DIM, MXU_DIM, MXU_DIM), lambda b, t: (layer, 0, 0, 0)),
            vec(),
            pl.BlockSpec((None, W // MXU_DIM, MXU_DIM, MXU_DIM), lambda b, t: (layer, 0, 0, 0)),
            vec(),
            vec(),
        ],
        out_specs=[
            pl.BlockSpec((bb, tt, W), lambda b, t: (b, t, 0)),
            pl.BlockSpec((bb, 1, W), lambda b, t: (b, 0, 0)),
            pl.BlockSpec((bb, CONV_WIDTH - 1, W), lambda b, t: (b, 0, 0)),
        ],
        out_shape=[
            jax.ShapeDtypeStruct((B, T, W), BF16),
            jax.ShapeDtypeStruct((B, 1, W), F32),
            jax.ShapeDtypeStruct((B, CONV_WIDTH - 1, W), F32),
        ],
        scratch_shapes=[
            pltpu.VMEM((bb, tt + 8, W), F32),
            pltpu.VMEM((bb, 1, W), F32),
            pltpu.VMEM((tt, W), F32),
            pltpu.VMEM((tt, W), F32),
        ],
        compiler_params=_cparams(("arbitrary", "arbitrary")),
        name="lru",
    )(xy3, xy3, st_conv, st_lru, P['conv_w'], P['conv_b'], P['gaw_bd'], P['gate_a_b'], P['gxw_bd'],
      P['gate_x_b'], P['lru_lambda'])


def _block_diag(x_bf16, mask_ref):
    return jnp.concatenate([x_bf16] * GROUP_HEADS, axis=0) * mask_ref[...]


def _rows(x, b, n):
    return x[b * n:(b + 1) * n]


def _dot_split3(sel_bf16, x):
    hi = x.astype(BF16)
    r1 = x - hi.astype(F32)
    mid = r1.astype(BF16)
    lo = (r1 - mid.astype(F32)).astype(BF16)
    return _dot(sel_bf16, hi) + _dot(sel_bf16, mid) + _dot(sel_bf16, lo)


def _wkv_kernel(ps_ref, sh_ref, s0_ref, mu_ref, w0_ref, lora_ref, gup_ref, a0_ref, kk_ref, ka_ref, rk_ref,
                lnw_ref, lnb_ref, tri_ref, endm_ref, ones_ref, mrow_ref, msq_ref, new_all_ref,
                y_ref, nsh_ref, ns_ref, car_s, st_s, *, nb, seq, zero_state):
    del new_all_ref
    C = CHUNK
    Q = C // seq
    R = nb * C
    o = RWKV_WIDTH
    ci = pl.program_id(1)
    last = ci == pl.num_programs(1) - 1
    t_idx = lax.broadcasted_iota(jnp.int32, (C, MXU_DIM), 0)
    i_idx = lax.broadcasted_iota(jnp.int32, (C, MXU_DIM), 1) % C
    same = (t_idx // seq) == (i_idx // seq)
    m_strict = same & (i_idx < t_idx)
    m_incl = same & (i_idx <= t_idx)
    eye_cat = (i_idx == t_idx).astype(F32)
    ones_bd = ones_ref[...]
    msq_f = msq_ref[...].astype(F32)
    n_lane_tiles = o // MXU_DIM

    def head_sum(x):
        xs = jnp.concatenate([x[:, g * MXU_DIM:(g + 1) * MXU_DIM] for g in range(n_lane_tiles)], axis=0)
        hi = xs.astype(BF16)
        lo = (xs - hi.astype(F32)).astype(BF16)
        s = _dot(jnp.concatenate([hi, lo], axis=0), ones_bd)
        s = s[:n_lane_tiles * R] + s[n_lane_tiles * R:]
        return jnp.concatenate([_rows(s, g, R) for g in range(n_lane_tiles)], axis=1)

    @pl.when(ci == 0)
    def _():
        for b in range(nb):
            car_s[b] = sh_ref[b]
            for q in range(Q):
                if zero_state:
                    st_s[b, q] = jnp.zeros(st_s.shape[2:], F32)
                else:
                    for g in range(N_GROUPS):
                        heads = [s0_ref[b, q * RWKV_HEADS + g * GROUP_HEADS + hh] for hh in range(GROUP_HEADS)]
                        stacked = jnp.concatenate(heads, axis=0)
                        st_s[b, q, g] = jnp.concatenate([stacked] * GROUP_HEADS, axis=1) * msq_f

    row_c = lax.broadcasted_iota(jnp.int32, (C, 1), 0)
    prevs = []
    for b in range(nb):
        ps_b = ps_ref[b]
        if Q == 1:
            first = car_s[b]
            new_carry = ps_b[C - 1:C, :]
        else:
            put = (lax.broadcasted_iota(jnp.int32, (C, Q), 0)
                   == seq * lax.broadcasted_iota(jnp.int32, (C, Q), 1)).astype(BF16)
            take = (lax.broadcasted_iota(jnp.int32, (Q, C), 1)
                    == seq * lax.broadcasted_iota(jnp.int32, (Q, C), 0) + (seq - 1)).astype(BF16)
            first = _dot_split3(put, car_s[b])
            new_carry = _dot_split3(take, ps_b)
        prevs.append(jnp.where(row_c % seq == 0, first, pltpu.roll(ps_b, 1, axis=0)))
        car_s[b] = new_carry
        nsh_ref[b] = new_carry
    ps = jnp.concatenate([ps_ref[b] for b in range(nb)], axis=0)
    prev = jnp.concatenate(prevs, axis=0)

    s = ps + (prev - ps) * mu_ref[...]
    r = s[:, :o]
    k = s[:, o:2 * o]
    v = s[:, 2 * o:3 * o]
    dwa = s[:, 3 * o:3 * o + LORA_W + LORA_A]
    dg = s[:, 3 * o + LORA_W + LORA_A:]
    lane = lax.broadcasted_iota(jnp.int32, dwa.shape, 1)
    lora_in = jnp.where(lane < LORA_W, jnp.tanh(dwa), dwa).astype(BF16)
    lora = _dot(lora_in, lora_ref[...])
    w_log = -_softplus(-(w0_ref[...] + lora[:, :o])) - 0.5
    lw = -jnp.exp(w_log)
    a = _sigmoid(a0_ref[...] + lora[:, o:])
    g = _dot(_sigmoid(dg).astype(BF16), gup_ref[...])
    kk = k * kk_ref[...]
    kk = kk / jnp.maximum(jnp.sqrt(head_sum(kk * kk)), 1e-12)
    kmod = k * (1.0 + (a - 1.0) * ka_ref[...])
    beta = kk * a
    lw_hi = lw.astype(BF16)
    lw_lo = (lw - lw_hi.astype(F32)).astype(BF16)
    L = _dot(tri_ref[...], lw_hi) + _dot(tri_ref[...], lw_lo)
    l_end = _dot(endm_ref[...], lw_hi) + _dot(endm_ref[...], lw_lo)
    e_neg = jnp.exp(-L)
    e_end = jnp.exp(l_end - L)
    p_end = jnp.exp(l_end)
    at = (-kk) * jnp.exp(L - lw)
    rt = r * jnp.exp(L)
    bt = (beta * e_neg).astype(BF16)
    kt = (kmod * e_neg).astype(BF16)
    bend = beta * e_end
    kend = kmod * e_end

    chains = []
    for b in range(nb):
        for gi in range(N_GROUPS):
            sl = slice(gi * GROUP_LANES, (gi + 1) * GROUP_LANES)
            chains.append(dict(
                b=b, gi=gi, vg=_rows(v, b, C)[:, sl].astype(BF16), p_end=_rows(p_end, b, C)[:, sl],
                x2=jnp.concatenate([_rows(at, b, C)[:, sl], _rows(rt, b, C)[:, sl]], axis=0).astype(BF16),
                wbd=jnp.concatenate([_block_diag(_rows(bt, b, C)[:, sl], mrow_ref),
                                     _block_diag(_rows(kt, b, C)[:, sl], mrow_ref)], axis=0),
                bk=jnp.concatenate([_rows(bend, b, C)[:, sl], _rows(kend, b, C)[:, sl]], axis=0).astype(BF16)))

    cw = GROUP_HEADS * C
    rowseq = (lax.broadcasted_iota(jnp.int32, (2 * C, 1), 0) % C) // seq
    for c in chains:
        res = _dot_tb(c['x2'], c['wbd'])
        n_cat = jnp.where(m_strict, res[:C, :cw], 0.0)
        c['a_v'] = jnp.concatenate([jnp.where(m_strict, res[:C, cw:], 0.0),
                                    jnp.where(m_incl, res[C:, cw:], 0.0)], axis=0).astype(BF16)
        c['a_rb'] = jnp.where(m_incl, res[C:, :cw], 0.0).astype(BF16)
        c['x_c'] = n_cat.astype(BF16)
        c['p_c'] = eye_cat + n_cat
        c['x_bd'] = _block_diag(c['x_c'], msq_ref)
    for c in chains:
        xs = None
        for q in range(Q):
            xq = _dot_tb(c['x2'], st_s[c['b'], q, c['gi']].astype(BF16))
            xs = xq if xs is None else jnp.where(rowseq == q, xq, xs)
        c['xs'] = xs
        c['v_bd'] = _block_diag(c['vg'], msq_ref)
    for c in chains:
        sv = c['xs'] + _dot(c['a_v'], c['v_bd'])
        c['rhs'] = sv[:C]
        c['y_sv'] = sv[C:]
    lvl = 2
    while lvl < seq:
        for c in chains:
            c['x_c'] = _dot(c['x_c'], c['x_bd']).astype(BF16)
            c['x_bd'] = _block_diag(c['x_c'], msq_ref)
        for c in chains:
            c['p_c'] = c['p_c'] + _dot(c['p_c'].astype(BF16), c['x_bd'])
        lvl *= 2
    for c in chains:
        c['u'] = _dot(c['p_c'].astype(BF16), _block_diag(c['rhs'].astype(BF16), msq_ref)).astype(BF16)
    ys = {}
    for c in chains:
        b, gi = c['b'], c['gi']
        ys[(b, gi)] = c['y_sv'] + _dot(c['a_rb'], _block_diag(c['u'], msq_ref))
        uv = jnp.concatenate([c['u'], c['vg']], axis=0)
        if Q > 1:
            uv = jnp.concatenate([jnp.where(rowseq == q, uv, jnp.zeros_like(uv)) for q in range(Q)], axis=1)
        ds = _dot_ta(uv, c['bk'])
        for q in range(Q):
            st_s[b, q, gi] = (st_s[b, q, gi] * c['p_end'][q * seq:q * seq + 1, :]
                              + _rows(ds, q, GROUP_LANES) * msq_f)

    yc = jnp.concatenate([jnp.concatenate([ys[(b, gi)] for gi in range(N_GROUPS)], axis=1) for b in range(nb)],
                         axis=0)
    inv_n = 1.0 / RWKV_HEAD
    mean = head_sum(yc) * inv_n
    dlt = yc - mean
    var = head_sum(dlt * dlt) * inv_n
    yn = dlt * lax.rsqrt(var + GN_EPS) * lnw_ref[...] + lnb_ref[...]
    bonus = head_sum(r * kmod * rk_ref[...]) * v
    out = ((yn + bonus) * g).astype(BF16)
    for b in range(nb):
        y_ref[b] = _rows(out, b, C)

    @pl.when(last)
    def _():
        for b in range(nb):
            for q in range(Q):
                for gi in range(N_GROUPS):
                    for hh in range(GROUP_HEADS):
                        ns_ref[b, q * RWKV_HEADS + gi * GROUP_HEADS + hh] = st_s[
                            b, q, gi, hh * RWKV_HEAD:(hh + 1) * RWKV_HEAD, hh * RWKV_HEAD:(hh + 1) * RWKV_HEAD]


def _wkv_consts(nb, seq):
    C = CHUNK
    gh = GROUP_HEADS
    rows = np.arange(nb * C)
    same = (rows[:, None] // seq) == (rows[None, :] // seq)
    tri = (same & (rows[None, :] <= rows[:, None])).astype(np.float32)
    endm = same.astype(np.float32)
    hl = np.arange(MXU_DIM) // RWKV_HEAD
    ones_bd = (hl[:, None] == hl[None, :]).astype(np.float32)
    rowh = np.arange(gh * C) // C
    colh = np.arange(GROUP_LANES) // RWKV_HEAD
    mrow = (rowh[:, None] == colh[None, :]).astype(np.float32)
    msq = (rowh[:, None] == rowh[None, :]).astype(np.float32)
    return tuple(jnp.asarray(m, BF16) for m in (tri, endm, ones_bd, mrow, msq))


def _wkv(ps3, st_shift, st_rwkv_all, new_rwkv_all, P, layer, nb, zero_state):
    B, T, _ = ps3.shape
    C = CHUNK
    seq = C if T % C == 0 else T
    assert C % seq == 0
    Q = C // seq
    G = B // Q
    tg = T * Q
    assert B % Q == 0 and G % nb == 0 and tg % C == 0
    o = RWKV_WIDTH
    consts = _wkv_consts(nb, seq)
    vec = lambda w: pl.BlockSpec((None, 1, w), lambda b, c: (layer, 0, 0))
    const = lambda arr: pl.BlockSpec(arr.shape, lambda b, c: (0,) * arr.ndim)
    kern = functools.partial(_wkv_kernel, nb=nb, seq=seq, zero_state=zero_state)
    shift_spec = pl.BlockSpec((nb, Q, SHIFT_WIDTH), lambda b, c: (b, 0, 0))
    state_shape = (DEPTH, G, Q * RWKV_HEADS, RWKV_HEAD, RWKV_HEAD)
    state_block = (None, nb, Q * RWKV_HEADS, RWKV_HEAD, RWKV_HEAD)
    state_index = lambda b, c: (layer, b, 0, 0, 0)
    in_specs = [
        pl.BlockSpec((nb, C, SHIFT_WIDTH), lambda b, c: (b, c, 0)),
        shift_spec,
        pl.BlockSpec(state_block, state_index, pipeline_mode=pl.Buffered(1)),
        vec(SHIFT_WIDTH),
        vec(o),
        pl.BlockSpec((None, LORA_W + LORA_A, 2 * o), lambda b, c: (layer, 0, 0)),
        pl.BlockSpec((None, LORA_G, o), lambda b, c: (layer, 0, 0)),
        vec(o), vec(o), vec(o), vec(o), vec(o), vec(o),
    ] + [const(m) for m in consts] + [pl.BlockSpec(memory_space=pl.ANY)]
    args = [ps3.reshape(G, tg, SHIFT_WIDTH), st_shift.reshape(G, Q, SHIFT_WIDTH), st_rwkv_all.reshape(state_shape),
            P['shift_mu'], P['w0'], P['lora_wa'], P['g_lora_up'], P['a0'], P['k_k'], P['k_a'],
            P['r_k'], P['ln_x_w'], P['ln_x_b'], *consts, new_rwkv_all.reshape(state_shape)]
    y, nsh, ns = pl.pallas_call(
        kern,
        grid=(G // nb, tg // C),
        in_specs=in_specs,
        out_specs=[
            pl.BlockSpec((nb, C, o), lambda b, c: (b, c, 0)),
            shift_spec,
            pl.BlockSpec(state_block, state_index),
        ],
        out_shape=[
            jax.ShapeDtypeStruct((G, tg, o), BF16),
            jax.ShapeDtypeStruct((G, Q, SHIFT_WIDTH), F32),
            jax.ShapeDtypeStruct(state_shape, F32),
        ],
        scratch_shapes=[
            pltpu.VMEM((nb, Q, SHIFT_WIDTH), F32),
            pltpu.VMEM((nb, Q, N_GROUPS, MXU_DIM, MXU_DIM), F32),
        ],
        input_output_aliases={len(args) - 1: 2},
        compiler_params=_cparams(("arbitrary", "arbitrary")),
        name="rwkv",
    )(*args)
    return (y.reshape(B, T, o), nsh.reshape(B, SHIFT_WIDTH),
            ns.reshape(DEPTH, B, RWKV_HEADS, RWKV_HEAD, RWKV_HEAD))


def _merge_kernel(x_ref, ya_ref, yb_ref, gt_ref, wa_ref, wb_ref, wo_ref, nf_ref, *rest, moe):
    if moe:
        wr_ref, xo_ref, gates_ref, cnt_ref = rest
    else:
        xo_ref, h_ref = rest
    ga = gt_ref[:, :D_MODEL]
    gb = gt_ref[:, D_MODEL:]
    m = _sigmoid(ga) * _dot(ya_ref[...], wa_ref[...]) + _sigmoid(gb) * _dot(yb_ref[...], wb_ref[...])
    x = x_ref[...] + _dot(m.astype(BF16), wo_ref[...])
    xo_ref[...] = x
    h = _rms(x, nf_ref[...])
    h_hi = h.astype(BF16)
    if not moe:
        h_ref[...] = h_hi
    else:
        w = wr_ref[...]
        w_hi = w.astype(BF16)
        w_lo = (w - w_hi.astype(F32)).astype(BF16)
        h_lo = (h - h_hi.astype(F32)).astype(BF16)
        logits = _dot(h_hi, w_hi) + _dot(h_lo, w_hi) + _dot(h_hi, w_lo)
        lane = lax.broadcasted_iota(jnp.int32, logits.shape, 1)
        real = lane < N_EXPERTS
        logits = jnp.where(real, logits, -jnp.inf)
        e = jnp.exp(logits - jnp.max(logits, axis=-1, keepdims=True))
        p = jnp.where(real, e / jnp.sum(e, axis=-1, keepdims=True), -1.0)
        m1 = jnp.max(p, axis=-1, keepdims=True)
        i1 = jnp.min(jnp.where(p == m1, lane, LANES), axis=-1, keepdims=True)
        oh1 = lane == i1
        p2 = jnp.where(oh1, -1.0, p)
        m2 = jnp.max(p2, axis=-1, keepdims=True)
        i2 = jnp.min(jnp.where(p2 == m2, lane, LANES), axis=-1, keepdims=True)
        oh2 = lane == i2
        tot = m1 + m2
        gates = jnp.where(oh1, m1 / tot, 0.0) + jnp.where(oh2, m2 / tot, 0.0)
        gates_ref[...] = gates
        cnt = jnp.sum((gates > 0.0).astype(F32), axis=0, keepdims=True)
        cnt_ref[...] = jnp.broadcast_to(cnt, (SUBLANES, LANES))


def _merge(x, ya, yb, gt, P, layer, tm, moe):
    n = x.shape[0]
    row = lambda w: pl.BlockSpec((tm, w), lambda i: (i, 0))
    wsq = lambda: pl.BlockSpec((None, D_MODEL, D_MODEL), lambda i: (layer, 0, 0))
    in_specs = [row(D_MODEL), row(D_MODEL), row(D_MODEL), row(2 * D_MODEL), wsq(), wsq(), wsq(),
                pl.BlockSpec((None, 1, D_MODEL), lambda i: (layer, 0, 0))]
    args = [x, ya, yb, gt, P['w_out_a'], P['w_out_b'], P['w_out'], P['norm_ffn']]
    out_specs = [row(D_MODEL)]
    out_shape = [jax.ShapeDtypeStruct((n, D_MODEL), F32)]
    if moe:
        in_specs.append(pl.BlockSpec((None, D_MODEL, LANES), lambda i: (layer // 2, 0, 0)))
        args.append(P['w_router_pad'])
        out_specs.append(row(LANES))
        out_shape.append(jax.ShapeDtypeStruct((n, LANES), F32))
        out_specs.append(pl.BlockSpec((None, SUBLANES, LANES), lambda i: (i, 0, 0)))
        out_shape.append(jax.ShapeDtypeStruct((n // tm, SUBLANES, LANES), F32))
    else:
        out_specs.append(row(D_MODEL))
        out_shape.append(jax.ShapeDtypeStruct((n, D_MODEL), BF16))
    return pl.pallas_call(
        functools.partial(_merge_kernel, moe=moe),
        grid=(n // tm,),
        in_specs=in_specs,
        out_specs=out_specs,
        out_shape=out_shape,
        compiler_params=_cparams(("arbitrary",)),
        name="merge",
    )(*args)


def _swiglu_acc(h, wg_ref, wu_ref, wd_ref):
    acc = None
    for c0 in range(0, D_FF, FF_CHUNK):
        sl = slice(c0, c0 + FF_CHUNK)
        gate = _dot(h, wg_ref[:, sl])
        up = _dot(h, wu_ref[:, sl])
        act = (gate * _sigmoid(gate) * up).astype(BF16)
        part = _dot(act, wd_ref[sl, :])
        acc = part if acc is None else acc + part
    return acc


def _ffn_kernel(x_ref, h_ref, wg_ref, wu_ref, wd_ref, o_ref):
    o_ref[...] = x_ref[...] + _swiglu_acc(h_ref[...], wg_ref, wu_ref, wd_ref)


def _ffn(x, h, P, j, tm):
    n = x.shape[0]
    row = lambda: pl.BlockSpec((tm, D_MODEL), lambda i: (i, 0))
    return pl.pallas_call(
        _ffn_kernel,
        grid=(n // tm,),
        in_specs=[row(), row(),
                  pl.BlockSpec((None, D_MODEL, D_FF), lambda i: (j, 0, 0), pipeline_mode=pl.Buffered(1)),
                  pl.BlockSpec((None, D_MODEL, D_FF), lambda i: (j, 0, 0), pipeline_mode=pl.Buffered(1)),
                  pl.BlockSpec((None, D_FF, D_MODEL), lambda i: (j, 0, 0), pipeline_mode=pl.Buffered(1))],
        out_specs=row(),
        out_shape=jax.ShapeDtypeStruct((n, D_MODEL), F32),
        compiler_params=_cparams(("arbitrary",)),
        name="ffn",
    )(x, h, P['w_ffn_gate'], P['w_ffn_up'], P['w_ffn_down'])


def _route_kernel(gates_ref, base_ref, pos_ref, *, tw, spare):
    mask = (gates_ref[...] > 0.0).astype(BF16)
    r_idx = lax.broadcasted_iota(jnp.int32, (tw, tw), 0)
    c_idx = lax.broadcasted_iota(jnp.int32, (tw, tw), 1)
    excl = _dot_ta(mask, (r_idx < c_idx).astype(BF16))
    sel = _dot_ta(mask, (r_idx == c_idx).astype(BF16)) > 0.0
    pos = excl + base_ref[:, 0:1]
    e_io = lax.broadcasted_iota(jnp.int32, (LANES, tw), 0)
    e_first = jnp.min(jnp.where(sel, e_io, LANES), axis=0, keepdims=True)
    e_last = jnp.max(jnp.where(sel, e_io, -1), axis=0, keepdims=True)
    p_first = jnp.sum(jnp.where(e_io == e_first, pos, 0.0), axis=0, keepdims=True)
    p_last = jnp.sum(jnp.where(e_io == e_last, pos, 0.0), axis=0, keepdims=True)
    p_last = jnp.where(e_last == e_first, float(spare), p_last)
    rows = jnp.concatenate([p_first, p_last, jnp.zeros((SUBLANES - 2, tw), F32)], axis=0)
    pos_ref[...] = rows.astype(jnp.int32)


def _route(gates, base_b, tw, spare):
    n = gates.shape[0]
    return pl.pallas_call(
        functools.partial(_route_kernel, tw=tw, spare=spare),
        grid=(n // tw,),
        in_specs=[pl.BlockSpec((tw, LANES), lambda i: (i, 0)),
                  pl.BlockSpec((None, LANES, LANES), lambda i: (i, 0, 0))],
        out_specs=pl.BlockSpec((None, SUBLANES, tw), lambda i: (i, 0, 0)),
        out_shape=jax.ShapeDtypeStruct((n // tw, SUBLANES, tw), jnp.int32),
        compiler_params=_cparams(("arbitrary",)),
        name="route",
    )(gates, base_b)


def _row_copy(src_hbm, src_row, dst_ref, dst_row, sem):
    return pltpu.make_async_copy(src_hbm.at[pl.ds(src_row, 1), :], dst_ref.at[pl.ds(dst_row, 1), :], sem)


def _disperse_kernel(pos_ref, x_ref, init_hbm, xs_hbm, sem, *, tw):
    del init_hbm

    def issue(n, carry):
        _row_copy(x_ref, n, xs_hbm, pos_ref[0, n], sem).start()
        _row_copy(x_ref, n, xs_hbm, pos_ref[1, n], sem).start()
        return carry

    def drain(n, carry):
        _row_copy(x_ref, 0, xs_hbm, 0, sem).wait()
        _row_copy(x_ref, 0, xs_hbm, 0, sem).wait()
        return carry

    lax.fori_loop(0, tw, issue, 0, unroll=8)
    lax.fori_loop(0, tw, drain, 0, unroll=8)


def _disperse(pos, x, n_rows, tw):
    n = x.shape[0]
    return pl.pallas_call(
        functools.partial(_disperse_kernel, tw=tw),
        grid=(n // tw,),
        in_specs=[pl.BlockSpec((None, SUBLANES, tw), lambda i: (i, 0, 0), memory_space=pltpu.SMEM),
                  pl.BlockSpec((tw, D_MODEL), lambda i: (i, 0)),
                  pl.BlockSpec(memory_space=pl.ANY)],
        out_specs=pl.BlockSpec(memory_space=pl.ANY),
        out_shape=jax.ShapeDtypeStruct((n_rows, D_MODEL), F32),
        scratch_shapes=[pltpu.SemaphoreType.DMA],
        input_output_aliases={2: 0},
        compiler_params=_cparams(("arbitrary",)),
        name="disperse",
    )(pos, x, jnp.zeros((n_rows, D_MODEL), F32))


def _experts_kernel(te_ref, used_ref, xs_ref, nf_ref, wg_ref, wu_ref, wd_ref, ys_ref):
    del te_ref
    j = pl.program_id(0)

    @pl.when(j < used_ref[0])
    def _():
        h = _rms(xs_ref[...], nf_ref[...]).astype(BF16)
        ys_ref[...] = _swiglu_acc(h, wg_ref, wu_ref, wd_ref)

    @pl.when(j >= used_ref[0])
    def _():
        ys_ref[...] = jnp.zeros(ys_ref.shape, F32)


def _experts(tile_expert, used, xs, P, layer, j):
    n_tiles = tile_expert.shape[0]
    tr = xs.shape[0] // n_tiles
    wspec = lambda a, b: pl.BlockSpec((None, None, a, b), lambda t, te, u: (j, te[t], 0, 0))
    grid_spec = pltpu.PrefetchScalarGridSpec(
        num_scalar_prefetch=2,
        grid=(n_tiles,),
        in_specs=[pl.BlockSpec((tr, D_MODEL), lambda t, te, u: (t, 0)),
                  pl.BlockSpec((None, 1, D_MODEL), lambda t, te, u: (layer, 0, 0)),
                  wspec(D_MODEL, D_FF), wspec(D_MODEL, D_FF), wspec(D_FF, D_MODEL)],
        out_specs=pl.BlockSpec((tr, D_MODEL), lambda t, te, u: (t, 0)),
    )
    return pl.pallas_call(
        _experts_kernel,
        grid_spec=grid_spec,
        out_shape=jax.ShapeDtypeStruct((n_tiles * tr, D_MODEL), F32),
        compiler_params=_cparams(("arbitrary",)),
        name="experts",
    )(tile_expert, used, xs, P['norm_ffn'], P['w_moe_gate'], P['w_moe_up'], P['w_moe_down'])


def _combine_kernel(pos_ref, x_ref, gates_ref, ys_hbm, *rest, tw, final):
    nf_ref = rest[0] if final else None
    o_ref, y1_s, y2_s, sem = rest[1:] if final else rest

    def issue(n, carry):
        _row_copy(ys_hbm, pos_ref[0, n], y1_s, n, sem).start()
        _row_copy(ys_hbm, pos_ref[1, n], y2_s, n, sem).start()
        return carry

    def drain(n, carry):
        _row_copy(ys_hbm, 0, y1_s, 0, sem).wait()
        _row_copy(ys_hbm, 0, y2_s, 0, sem).wait()
        return carry

    lax.fori_loop(0, tw, issue, 0, unroll=8)
    gates = gates_ref[...]
    lane = lax.broadcasted_iota(jnp.int32, gates.shape, 1)
    sel = gates > 0.0
    e_first = jnp.min(jnp.where(sel, lane, LANES), axis=-1, keepdims=True)
    e_last = jnp.max(jnp.where(sel, lane, -1), axis=-1, keepdims=True)
    g_first = jnp.sum(jnp.where(lane == e_first, gates, 0.0), axis=-1, keepdims=True)
    g_last = jnp.sum(jnp.where((lane == e_last) & (e_last != e_first), gates, 0.0), axis=-1, keepdims=True)
    lax.fori_loop(0, tw, drain, 0, unroll=8)
    x = x_ref[...] + g_first * y1_s[...] + g_last * y2_s[...]
    o_ref[...] = _rms(x, nf_ref[...]) if final else x


def _combine(pos, x, gates, ys, tw, norm_final=None):
    n = x.shape[0]
    final = norm_final is not None
    row = pl.BlockSpec((tw, D_MODEL), lambda i: (i, 0))
    in_specs = [pl.BlockSpec((None, SUBLANES, tw), lambda i: (i, 0, 0), memory_space=pltpu.SMEM),
                row, pl.BlockSpec((tw, LANES), lambda i: (i, 0)), pl.BlockSpec(memory_space=pl.ANY)]
    args = [pos, x, gates, ys]
    if final:
        in_specs.append(pl.BlockSpec((1, D_MODEL), lambda i: (0, 0)))
        args.append(norm_final)
    return pl.pallas_call(
        functools.partial(_combine_kernel, tw=tw, final=final),
        grid=(n // tw,),
        in_specs=in_specs,
        out_specs=row,
        out_shape=jax.ShapeDtypeStruct((n, D_MODEL), F32),
        scratch_shapes=[pltpu.VMEM((tw, D_MODEL), F32), pltpu.VMEM((tw, D_MODEL), F32), pltpu.SemaphoreType.DMA],
        compiler_params=_cparams(("arbitrary",)),
        name="combine",
    )(*args)


def _moe_routed(x, gates, cnt, P, layer, tm):
    n = x.shape[0]
    tw = min(ROUTE_TILE, n)
    tr = ROUTE_ROWS if TOP_K * n >= 4 * N_EXPERTS * ROUTE_ROWS else ROUTE_ROWS_SMALL
    n_tiles = (TOP_K * n) // tr + N_EXPERTS
    spare = n_tiles * tr
    c = cnt[:, 0, :].astype(jnp.int32).reshape(n // tw, tw // tm, LANES).sum(axis=1)
    seg_tiles = (c.sum(axis=0) + tr - 1) // tr
    seg_end = jnp.cumsum(seg_tiles)
    base = (seg_end - seg_tiles) * tr + jnp.cumsum(c, axis=0) - c
    base_b = jnp.broadcast_to(base.astype(F32)[:, :, None], (n // tw, LANES, LANES))
    tile_expert = jnp.minimum((jnp.arange(n_tiles + 1)[:, None] >= seg_end[None, :N_EXPERTS]).sum(axis=1),
                              N_EXPERTS - 1).astype(jnp.int32)
    used = seg_end[N_EXPERTS - 1:N_EXPERTS].astype(jnp.int32)
    pos = _route(gates, base_b, tw, spare)
    xs = _disperse(pos, x, spare + tr, tw)
    ys = _experts(tile_expert, used, xs, P, layer, layer // 2)
    return _combine(pos, x, gates, ys, tw, P['norm_final'] if layer == DEPTH - 1 else None)


def _block_diag_weights(w):
    d = w.shape[0]
    per = MXU_DIM // LRU_BLOCK
    w = w.reshape(d, LRU_HEADS // per, per, LRU_BLOCK, LRU_BLOCK)
    eye = jnp.eye(per, dtype=w.dtype)
    bd = jnp.einsum('dcpij,pq->dcpiqj', w, eye)
    return bd.reshape(d, LRU_HEADS // per, MXU_DIM, MXU_DIM).astype(BF16)


def _prep_params(p):
    P = dict(p)
    for name in ('w_in', 'w_out_a', 'w_out_b', 'w_out', 'w_ffn_gate', 'w_ffn_up', 'w_ffn_down',
                 'w_moe_gate', 'w_moe_up', 'w_moe_down', 'g_lora_up'):
        P[name] = p[name].astype(BF16)
    for name in ('norm_mix', 'conv_b', 'gate_a_b', 'gate_x_b', 'lru_lambda', 'shift_mu', 'w0', 'a0', 'k_k', 'k_a',
                 'ln_x_w', 'ln_x_b', 'norm_ffn'):
        P[name] = p[name][:, None, :]
    P['r_k'] = p['r_k'].reshape(DEPTH, 1, RWKV_WIDTH)
    P['norm_final'] = p['norm_final'][None, :]
    P['gaw_bd'] = _block_diag_weights(p['gate_a_w'])
    P['gxw_bd'] = _block_diag_weights(p['gate_x_w'])
    z = jnp.zeros((DEPTH, LORA_W, RWKV_WIDTH), F32)
    P['lora_wa'] = jnp.concatenate([jnp.concatenate([p['w_lora_up'], z], axis=2),
                                    jnp.concatenate([z, p['a_lora_up']], axis=2)], axis=1).astype(BF16)
    P['w_router_pad'] = jnp.pad(p['w_router'], ((0, 0), (0, 0), (0, LANES - N_EXPERTS)))
    return P


def _trunk(x3, st_rwkv, st_lru, st_conv, st_shift, P, *, zero_state, tm, lru_bb, lru_tt, rwkv_nb):
    B, T, _ = x3.shape
    n = B * T
    x = x3.reshape(n, D_MODEL)
    new_rwkv, n_lru, n_conv, n_shift = jnp.zeros(st_rwkv.shape, F32), [], [], []
    for l in range(DEPTH):
        xy, ps, gt = _proj(x, P['norm_mix'], P['w_in'], l, tm)
        ya, h_new, c_new = _lru(xy.reshape(B, T, 2 * LRU_WIDTH), st_conv[l], st_lru[l][:, None, :], P, l,
                                lru_bb, lru_tt)
        yb, sh_new, new_rwkv = _wkv(ps.reshape(B, T, SHIFT_WIDTH), st_shift[l], st_rwkv, new_rwkv, P, l, rwkv_nb,
                                    zero_state)
        moe = l % 2 == 1
        res = _merge(x, ya.reshape(n, D_MODEL), yb.reshape(n, D_MODEL), gt, P, l, tm, moe)
        if moe:
            x, gates, cnt = res
            x = _moe_routed(x, gates, cnt, P, l, tm)
        else:
            x, h = res
            x = _ffn(x, h, P, l // 2, tm)
        n_lru.append(h_new[:, 0, :])
        n_conv.append(c_new)
        n_shift.append(sh_new)
    assert DEPTH % 2 == 0
    return x.reshape(B, T, D_MODEL), new_rwkv, jnp.stack(n_lru), jnp.stack(n_conv), jnp.stack(n_shift)


def kernel(x_prompt, x_sample, state_rwkv, state_lru, state_conv, state_shift, norm_mix, w_in, conv_w, conv_b, gate_a_w, gate_a_b, gate_x_w, gate_x_b, lru_lambda, shift_mu, w0, w_lora_up, a0, a_lora_up, g_lora_up, k_k, k_a, r_k, ln_x_w, ln_x_b, w_out_a, w_out_b, w_out, norm_ffn, w_ffn_gate, w_ffn_up, w_ffn_down, w_router, w_moe_gate, w_moe_up, w_moe_down, norm_final):
    P = _prep_params(dict(
        norm_mix=norm_mix, w_in=w_in, conv_w=conv_w, conv_b=conv_b, gate_a_w=gate_a_w, gate_a_b=gate_a_b,
        gate_x_w=gate_x_w, gate_x_b=gate_x_b, lru_lambda=lru_lambda, shift_mu=shift_mu, w0=w0,
        w_lora_up=w_lora_up, a0=a0, a_lora_up=a_lora_up, g_lora_up=g_lora_up, k_k=k_k, k_a=k_a, r_k=r_k,
        ln_x_w=ln_x_w, ln_x_b=ln_x_b, w_out_a=w_out_a, w_out_b=w_out_b, w_out=w_out, norm_ffn=norm_ffn,
        w_ffn_gate=w_ffn_gate, w_ffn_up=w_ffn_up, w_ffn_down=w_ffn_down, w_router=w_router,
        w_moe_gate=w_moe_gate, w_moe_up=w_moe_up, w_moe_down=w_moe_down, norm_final=norm_final))
    bp = x_prompt.shape[0]
    bs = x_sample.shape[0]
    zeros = lambda *shape: jnp.zeros(shape, F32)
    p_out = _trunk(x_prompt,
                   zeros(DEPTH, bp, RWKV_HEADS, RWKV_HEAD, RWKV_HEAD), zeros(DEPTH, bp, LRU_WIDTH),
                   zeros(DEPTH, bp, CONV_WIDTH - 1, LRU_WIDTH), zeros(DEPTH, bp, SHIFT_WIDTH), P,
                   zero_state=True, tm=512, lru_bb=1, lru_tt=512, rwkv_nb=4)
    s_out = _trunk(x_sample, state_rwkv, state_lru, state_conv, state_shift, P,
                   zero_state=False, tm=256, lru_bb=8, lru_tt=x_sample.shape[1], rwkv_nb=1)
    return (p_out[0], s_out[0]) + tuple(p_out[1:]) + tuple(s_out[1:])
```

```python
import functools

import numpy as np
import jax
import jax.numpy as jnp
from jax import lax
from jax.experimental import pallas as pl
from jax.experimental.pallas import tpu as pltpu

F32 = jnp.float32
BF16 = jnp.bfloat16

D_MODEL = 1024
DEPTH = 4
LRU_WIDTH = D_MODEL
LRU_HEADS = 16
LRU_BLOCK = LRU_WIDTH // LRU_HEADS
CONV_WIDTH = 4
LRU_C = 8.0
RWKV_HEAD = 64
RWKV_WIDTH = D_MODEL
RWKV_HEADS = RWKV_WIDTH // RWKV_HEAD
LORA_W = 64
LORA_A = 64
LORA_G = 128
SHIFT_WIDTH = 3 * RWKV_WIDTH + LORA_W + LORA_A + LORA_G
PROJ_WIDTH = 2 * LRU_WIDTH + SHIFT_WIDTH + 2 * D_MODEL
D_FF = 2816
N_EXPERTS = 8
NORM_EPS = 1e-6
GN_EPS = 64e-5

LANES = 128
SUBLANES = 8
MXU_DIM = 256
VMEM_LIMIT = 56 * 1024 * 1024

CHUNK = 64
GROUP_HEADS = MXU_DIM // CHUNK
GROUP_LANES = GROUP_HEADS * RWKV_HEAD
N_GROUPS = RWKV_HEADS // GROUP_HEADS
FF_CHUNK = 256
TOP_K = 2
ROUTE_TILE = 512
ROUTE_ROWS = 512
ROUTE_ROWS_SMALL = 128
SCAN_ROWS = 16
assert CHUNK == RWKV_HEAD


def _cparams(sem):
    return pltpu.CompilerParams(dimension_semantics=sem, vmem_limit_bytes=VMEM_LIMIT)


def _dot(a, b):
    return jnp.dot(a, b, preferred_element_type=F32)


def _dot_tb(a, b):
    return lax.dot_general(a, b, (((1,), (1,)), ((), ())), preferred_element_type=F32)


def _dot_ta(a, b):
    return lax.dot_general(a, b, (((0,), (0,)), ((), ())), preferred_element_type=F32)


def _sigmoid(x):
    return jax.nn.sigmoid(x)


def _softplus(z):
    return jnp.maximum(z, 0.0) + jnp.log(1.0 + jnp.exp(-jnp.abs(z)))


def _rms(x, g):
    return x * lax.rsqrt(jnp.mean(x * x, axis=-1, keepdims=True) + NORM_EPS) * g


_PROJ_SPLITS = ((0, 2 * LRU_WIDTH), (2 * LRU_WIDTH, SHIFT_WIDTH), (2 * LRU_WIDTH + SHIFT_WIDTH, 2 * D_MODEL))


def _proj_kernel(x_ref, g_ref, w_ref, xy_ref, ps_ref, gt_ref):
    u = _rms(x_ref[...], g_ref[...]).astype(BF16)
    for out_ref, (c0, width) in zip((xy_ref, ps_ref, gt_ref), _PROJ_SPLITS):
        j = 0
        while j < width:
            w = min(512, width - j)
            out_ref[:, j:j + w] = _dot(u, w_ref[:, c0 + j:c0 + j + w])
            j += w


def _proj(x, g_all, w_all, layer, tm):
    n = x.shape[0]
    return pl.pallas_call(
        _proj_kernel,
        grid=(n // tm,),
        in_specs=[
            pl.BlockSpec((tm, D_MODEL), lambda i: (i, 0)),
            pl.BlockSpec((None, 1, D_MODEL), lambda i: (layer, 0, 0)),
            pl.BlockSpec((None, D_MODEL, PROJ_WIDTH), lambda i: (layer, 0, 0),
                         pipeline_mode=pl.Buffered(1)),
        ],
        out_specs=[
            pl.BlockSpec((tm, 2 * LRU_WIDTH), lambda i: (i, 0)),
            pl.BlockSpec((tm, SHIFT_WIDTH), lambda i: (i, 0)),
            pl.BlockSpec((tm, 2 * D_MODEL), lambda i: (i, 0)),
        ],
        out_shape=[
            jax.ShapeDtypeStruct((n, 2 * LRU_WIDTH), F32),
            jax.ShapeDtypeStruct((n, SHIFT_WIDTH), F32),
            jax.ShapeDtypeStruct((n, 2 * D_MODEL), F32),
        ],
        compiler_params=_cparams(("arbitrary",)),
        name="proj",
    )(x, g_all, w_all)


def _gelu_tanh(x):
    return 0.5 * x * (1.0 + jnp.tanh(np.sqrt(2.0 / np.pi).astype(np.float32) * (x + 0.044715 * (x * x * x))))


def _lru_kernel(xa_ref, ya_ref, cs_ref, h0_ref, cw_ref, cb_ref, gaw_ref, gab_ref, gxw_ref, gxb_ref, lam_ref,
                y_ref, hl_ref, nc_ref, xp_s, h_s, a_s, b_s, *, bb, tt):
    ti = pl.program_id(1)
    neg_c = -LRU_C * _softplus(-lam_ref[...])
    cw = cw_ref[...]
    @pl.when(ti == 0)
    def _():
        for b in range(bb):
            xp_s[b, 5:8, :] = cs_ref[b]
            h_s[b] = h0_ref[b]

    for b in range(bb):
        xa = xa_ref[b]
        xp_s[b, 8:8 + tt, :] = xa
        xc = cb_ref[...] + xp_s[b, 5:5 + tt, :] * cw[0:1]
        xc = xc + xp_s[b, 6:6 + tt, :] * cw[1:2]
        xc = xc + xp_s[b, 7:7 + tt, :] * cw[2:3]
        xc = xc + xa * cw[3:4]
        tail = xp_s[b, 5 + tt:8 + tt, :]
        xp_s[b, 5:8, :] = tail
        nc_ref[b] = tail

        for c in range(LRU_WIDTH // MXU_DIM):
            sl = slice(c * MXU_DIM, (c + 1) * MXU_DIM)
            xcc = xc[:, sl]
            xcb = xcc.astype(BF16)
            r = _sigmoid(_dot(xcb, gaw_ref[c]) + gab_ref[:, sl])
            i = _sigmoid(_dot(xcb, gxw_ref[c]) + gxb_ref[:, sl])
            log_a = neg_c[:, sl] * r
            a = jnp.exp(log_a)
            a_s[:, sl] = a
            b_s[:, sl] = jnp.sqrt(-jnp.tanh(log_a) * (a * a + 1.0)) * (i * xcc)

        if tt % SCAN_ROWS == 0:
            row = lax.broadcasted_iota(jnp.int32, (SUBLANES, LRU_WIDTH), 0)

            def blk(j, h):
                r0 = pl.multiple_of(j * SCAN_ROWS, SCAN_ROWS)
                hbs = []
                for q in range(SCAN_ROWS // SUBLANES):
                    av = a_s[pl.ds(r0 + q * SUBLANES, SUBLANES), :]
                    bv = b_s[pl.ds(r0 + q * SUBLANES, SUBLANES), :]
                    for d in (1, 2, 4):
                        m = row >= d
                        a_sh = pltpu.roll(av, d, axis=0)
                        b_sh = pltpu.roll(bv, d, axis=0)
                        bv = jnp.where(m, av * b_sh + bv, bv)
                        av = jnp.where(m, av * a_sh, av)
                    hb = av * h + bv
                    h = hb[SUBLANES - 1:SUBLANES, :]
                    hbs.append(hb)
                hs = jnp.concatenate(hbs, axis=0)
                y_ref[b, pl.ds(r0, SCAN_ROWS), :] = (hs * _gelu_tanh(ya_ref[b, pl.ds(r0, SCAN_ROWS), :])).astype(BF16)
                return h

            h = lax.fori_loop(0, tt // SCAN_ROWS, blk, h_s[b])
        else:
            h = h_s[b]
            rows = []
            for t in range(tt):
                h = a_s[t:t + 1, :] * h + b_s[t:t + 1, :]
                rows.append(h)
            hs = jnp.concatenate(rows, axis=0)
            y_ref[b] = (hs * _gelu_tanh(ya_ref[b])).astype(BF16)
        h_s[b] = h
        hl_ref[b] = h


def _lru(xy3, st_conv, st_lru, P, layer, bb, tt):
    B, T, _ = xy3.shape
    W = LRU_WIDTH
    vec = lambda: pl.BlockSpec((None, 1, W), lambda b, t: (layer, 0, 0))
    kern = functools.partial(_lru_kernel, bb=bb, tt=tt)
    return pl.pallas_call(
        kern,
        grid=(B // bb, T // tt),
        in_specs=[
            pl.BlockSpec((bb, tt, W), lambda b, t: (b, t, 0)),
            pl.BlockSpec((bb, tt, W), lambda b, t: (b, t, 1)),
            pl.BlockSpec((bb, CONV_WIDTH - 1, W), lambda b, t: (b, 0, 0)),
            pl.BlockSpec((bb, 1, W), lambda b, t: (b, 0, 0)),
            pl.BlockSpec((None, CONV_WIDTH, W), lambda b, t: (layer, 0, 0)),
            vec(),
            pl.BlockSpec((None, W // MXU_DIM, MXU_DIM, MXU_DIM), lambda b, t: (layer, 0, 0, 0)),
            vec(),
            pl.BlockSpec((None, W // MXU_DIM, MXU_DIM, MXU_DIM), lambda b, t: (layer, 0, 0, 0)),
            vec(),
            vec(),
        ],
        out_specs=[
            pl.BlockSpec((bb, tt, W), lambda b, t: (b, t, 0)),
            pl.BlockSpec((bb, 1, W), lambda b, t: (b, 0, 0)),
            pl.BlockSpec((bb, CONV_WIDTH - 1, W), lambda b, t: (b, 0, 0)),
        ],
        out_shape=[
            jax.ShapeDtypeStruct((B, T, W), BF16),
            jax.ShapeDtypeStruct((B, 1, W), F32),
            jax.ShapeDtypeStruct((B, CONV_WIDTH - 1, W), F32),
        ],
        scratch_shapes=[
            pltpu.VMEM((bb, tt + 8, W), F32),
            pltpu.VMEM((bb, 1, W), F32),
            pltpu.VMEM((tt, W), F32),
            pltpu.VMEM((tt, W), F32),
        ],
        compiler_params=_cparams(("arbitrary", "arbitrary")),
        name="lru",
    )(xy3, xy3, st_conv, st_lru, P['conv_w'], P['conv_b'], P['gaw_bd'], P['gate_a_b'], P['gxw_bd'],
      P['gate_x_b'], P['lru_lambda'])


def _block_diag(x_bf16, mask_ref):
    return jnp.concatenate([x_bf16] * GROUP_HEADS, axis=0) * mask_ref[...]


def _rows(x, b, n):
    return x[b * n:(b + 1) * n]


def _dot_split3(sel_bf16, x):
    hi = x.astype(BF16)
    r1 = x - hi.astype(F32)
    mid = r1.astype(BF16)
    lo = (r1 - mid.astype(F32)).astype(BF16)
    return _dot(sel_bf16, hi) + _dot(sel_bf16, mid) + _dot(sel_bf16, lo)


def _wkv_kernel(ps_ref, sh_ref, s0_ref, mu_ref, w0_ref, lora_ref, gup_ref, a0_ref, kk_ref, ka_ref, rk_ref,
                lnw_ref, lnb_ref, tri_ref, endm_ref, ones_ref, mrow_ref, msq_ref, new_all_ref,
                y_ref, nsh_ref, ns_ref, car_s, st_s, *, nb, seq, zero_state):
    del new_all_ref
    C = CHUNK
    Q = C // seq
    R = nb * C
    o = RWKV_WIDTH
    ci = pl.program_id(1)
    last = ci == pl.num_programs(1) - 1
    t_idx = lax.broadcasted_iota(jnp.int32, (C, MXU_DIM), 0)
    i_idx = lax.broadcasted_iota(jnp.int32, (C, MXU_DIM), 1) % C
    same = (t_idx // seq) == (i_idx // seq)
    m_strict = same & (i_idx < t_idx)
    m_incl = same & (i_idx <= t_idx)
    eye_cat = (i_idx == t_idx).astype(F32)
    ones_bd = ones_ref[...]
    msq_f = msq_ref[...].astype(F32)
    n_lane_tiles = o // MXU_DIM

    def head_sum(x):
        xs = jnp.concatenate([x[:, g * MXU_DIM:(g + 1) * MXU_DIM] for g in range(n_lane_tiles)], axis=0)
        s = _dot(xs.astype(BF16), ones_bd)
        return jnp.concatenate([_rows(s, g, R) for g in range(n_lane_tiles)], axis=1)

    @pl.when(ci == 0)
    def _():
        for b in range(nb):
            car_s[b] = sh_ref[b]
            for q in range(Q):
                if zero_state:
                    st_s[b, q] = jnp.zeros(st_s.shape[2:], F32)
                else:
                    for g in range(N_GROUPS):
                        heads = [s0_ref[b, q * RWKV_HEADS + g * GROUP_HEADS + hh] for hh in range(GROUP_HEADS)]
                        stacked = jnp.concatenate(heads, axis=0)
                        st_s[b, q, g] = jnp.concatenate([stacked] * GROUP_HEADS, axis=1) * msq_f

    row_c = lax.broadcasted_iota(jnp.int32, (C, 1), 0)
    prevs = []
    for b in range(nb):
        ps_b = ps_ref[b]
        if Q == 1:
            first = car_s[b]
            new_carry = ps_b[C - 1:C, :]
        else:
            put = (lax.broadcasted_iota(jnp.int32, (C, Q), 0)
                   == seq * lax.broadcasted_iota(jnp.int32, (C, Q), 1)).astype(BF16)
            take = (lax.broadcasted_iota(jnp.int32, (Q, C), 1)
                    == seq * lax.broadcasted_iota(jnp.int32, (Q, C), 0) + (seq - 1)).astype(BF16)
            first = _dot_split3(put, car_s[b])
            new_carry = _dot_split3(take, ps_b)
        prevs.append(jnp.where(row_c % seq == 0, first, pltpu.roll(ps_b, 1, axis=0)))
        car_s[b] = new_carry
        nsh_ref[b] = new_carry
    ps = jnp.concatenate([ps_ref[b] for b in range(nb)], axis=0)
    prev = jnp.concatenate(prevs, axis=0)

    s = ps + (prev - ps) * mu_ref[...]
    r = s[:, :o]
    k = s[:, o:2 * o]
    v = s[:, 2 * o:3 * o]
    dwa = s[:, 3 * o:3 * o + LORA_W + LORA_A]
    dg = s[:, 3 * o + LORA_W + LORA_A:]
    lane = lax.broadcasted_iota(jnp.int32, dwa.shape, 1)
    lora_in = jnp.where(lane < LORA_W, jnp.tanh(dwa), dwa).astype(BF16)
    lora = _dot(lora_in, lora_ref[...])
    w_log = -_softplus(-(w0_ref[...] + lora[:, :o])) - 0.5
    lw = -jnp.exp(w_log)
    a = _sigmoid(a0_ref[...] + lora[:, o:])
    g = _dot(_sigmoid(dg).astype(BF16), gup_ref[...])
    kk = k * kk_ref[...]
    kk = kk * lax.rsqrt(jnp.maximum(head_sum(kk * kk), 1e-24))
    kmod = k * (1.0 + (a - 1.0) * ka_ref[...])
    beta = kk * a
    lw_hi = lw.astype(BF16)
    lw_lo = (lw - lw_hi.astype(F32)).astype(BF16)
    L = _dot(tri_ref[...], lw_hi) + _dot(tri_ref[...], lw_lo)
    l_end = _dot(endm_ref[...], lw_hi) + _dot(endm_ref[...], lw_lo)
    e_neg = jnp.exp(-L)
    p_end = jnp.exp(l_end)
    e_end = p_end * e_neg
    at = (-kk) * jnp.exp(L - lw)
    rt = r * jnp.exp(L)
    bt = (beta * e_neg).astype(BF16)
    kt = (kmod * e_neg).astype(BF16)
    bend = beta * e_end
    kend = kmod * e_end

    chains = []
    for b in range(nb):
        for gi in range(N_GROUPS):
            sl = slice(gi * GROUP_LANES, (gi + 1) * GROUP_LANES)
            chains.append(dict(
                b=b, gi=gi, vg=_rows(v, b, C)[:, sl].astype(BF16), p_end=_rows(p_end, b, C)[:, sl],
                x2=jnp.concatenate([_rows(at, b, C)[:, sl], _rows(rt, b, C)[:, sl]], axis=0).astype(BF16),
                wbd=jnp.concatenate([_block_diag(_rows(bt, b, C)[:, sl], mrow_ref),
                                     _block_diag(_rows(kt, b, C)[:, sl], mrow_ref)], axis=0),
                bk=jnp.concatenate([_rows(bend, b, C)[:, sl], _rows(kend, b, C)[:, sl]], axis=0).astype(BF16)))

    cw = GROUP_HEADS * C
    rowseq = (lax.broadcasted_iota(jnp.int32, (2 * C, 1), 0) % C) // seq
    for c in chains:
        res = _dot_tb(c['x2'], c['wbd'])
        n_cat = jnp.where(m_strict, res[:C, :cw], 0.0)
        c['a_v'] = jnp.concatenate([jnp.where(m_strict, res[:C, cw:], 0.0),
                                    jnp.where(m_incl, res[C:, cw:], 0.0)], axis=0).astype(BF16)
        c['a_rb'] = jnp.where(m_incl, res[C:, :cw], 0.0).astype(BF16)
        c['x_c'] = n_cat.astype(BF16)
        c['p_c'] = eye_cat + n_cat
        c['x_bd'] = _block_diag(c['x_c'], msq_ref)
    for c in chains:
        xs = None
        for q in range(Q):
            xq = _dot_tb(c['x2'], st_s[c['b'], q, c['gi']].astype(BF16))
            xs = xq if xs is None else jnp.where(rowseq == q, xq, xs)
        c['xs'] = xs
        c['v_bd'] = _block_diag(c['vg'], msq_ref)
    for c in chains:
        sv = c['xs'] + _dot(c['a_v'], c['v_bd'])
        c['rhs'] = sv[:C]
        c['y_sv'] = sv[C:]
    lvl = 2
    while lvl < seq:
        for c in chains:
            c['x_c'] = _dot(c['x_c'], c['x_bd']).astype(BF16)
            c['x_bd'] = _block_diag(c['x_c'], msq_ref)
        for c in chains:
            c['p_c'] = c['p_c'] + _dot(c['p_c'].astype(BF16), c['x_bd'])
        lvl *= 2
    for c in chains:
        c['u'] = _dot(c['p_c'].astype(BF16), _block_diag(c['rhs'].astype(BF16), msq_ref)).astype(BF16)
    ys = {}
    for c in chains:
        b, gi = c['b'], c['gi']
        ys[(b, gi)] = c['y_sv'] + _dot(c['a_rb'], _block_diag(c['u'], msq_ref))
        uv = jnp.concatenate([c['u'], c['vg']], axis=0)
        if Q > 1:
            uv = jnp.concatenate([jnp.where(rowseq == q, uv, jnp.zeros_like(uv)) for q in range(Q)], axis=1)
        ds = _dot_ta(uv, c['bk'])
        for q in range(Q):
            st_s[b, q, gi] = (st_s[b, q, gi] * c['p_end'][q * seq:q * seq + 1, :]
                              + _rows(ds, q, GROUP_LANES) * msq_f)

    yc = jnp.concatenate([jnp.concatenate([ys[(b, gi)] for gi in range(N_GROUPS)], axis=1) for b in range(nb)],
                         axis=0)
    inv_n = 1.0 / RWKV_HEAD
    mean = head_sum(yc) * inv_n
    dlt = yc - mean
    var = head_sum(dlt * dlt) * inv_n
    yn = dlt * lax.rsqrt(var + GN_EPS) * lnw_ref[...] + lnb_ref[...]
    bonus = head_sum(r * kmod * rk_ref[...]) * v
    out = ((yn + bonus) * g).astype(BF16)
    for b in range(nb):
        y_ref[b] = _rows(out, b, C)

    @pl.when(last)
    def _():
        for b in range(nb):
            for q in range(Q):
                for gi in range(N_GROUPS):
                    for hh in range(GROUP_HEADS):
                        ns_ref[b, q * RWKV_HEADS + gi * GROUP_HEADS + hh] = st_s[
                            b, q, gi, hh * RWKV_HEAD:(hh + 1) * RWKV_HEAD, hh * RWKV_HEAD:(hh + 1) * RWKV_HEAD]


def _wkv_consts(nb, seq):
    C = CHUNK
    gh = GROUP_HEADS
    rows = np.arange(nb * C)
    same = (rows[:, None] // seq) == (rows[None, :] // seq)
    tri = (same & (rows[None, :] <= rows[:, None])).astype(np.float32)
    endm = same.astype(np.float32)
    hl = np.arange(MXU_DIM) // RWKV_HEAD
    ones_bd = (hl[:, None] == hl[None, :]).astype(np.float32)
    rowh = np.arange(gh * C) // C
    colh = np.arange(GROUP_LANES) // RWKV_HEAD
    mrow = (rowh[:, None] == colh[None, :]).astype(np.float32)
    msq = (rowh[:, None] == rowh[None, :]).astype(np.float32)
    return tuple(jnp.asarray(m, BF16) for m in (tri, endm, ones_bd, mrow, msq))


def _wkv(ps3, st_shift, st_rwkv_all, new_rwkv_all, P, layer, nb, zero_state):
    B, T, _ = ps3.shape
    C = CHUNK
    seq = C if T % C == 0 else T
    assert C % seq == 0
    Q = C // seq
    G = B // Q
    tg = T * Q
    assert B % Q == 0 and G % nb == 0 and tg % C == 0
    o = RWKV_WIDTH
    consts = _wkv_consts(nb, seq)
    vec = lambda w: pl.BlockSpec((None, 1, w), lambda b, c: (layer, 0, 0))
    const = lambda arr: pl.BlockSpec(arr.shape, lambda b, c: (0,) * arr.ndim)
    kern = functools.partial(_wkv_kernel, nb=nb, seq=seq, zero_state=zero_state)
    shift_spec = pl.BlockSpec((nb, Q, SHIFT_WIDTH), lambda b, c: (b, 0, 0))
    state_shape = (DEPTH, G, Q * RWKV_HEADS, RWKV_HEAD, RWKV_HEAD)
    state_block = (None, nb, Q * RWKV_HEADS, RWKV_HEAD, RWKV_HEAD)
    state_index = lambda b, c: (layer, b, 0, 0, 0)
    in_specs = [
        pl.BlockSpec((nb, C, SHIFT_WIDTH), lambda b, c: (b, c, 0)),
        shift_spec,
        pl.BlockSpec(state_block, state_index, pipeline_mode=pl.Buffered(1)),
        vec(SHIFT_WIDTH),
        vec(o),
        pl.BlockSpec((None, LORA_W + LORA_A, 2 * o), lambda b, c: (layer, 0, 0)),
        pl.BlockSpec((None, LORA_G, o), lambda b, c: (layer, 0, 0)),
        vec(o), vec(o), vec(o), vec(o), vec(o), vec(o),
    ] + [const(m) for m in consts] + [pl.BlockSpec(memory_space=pl.ANY)]
    args = [ps3.reshape(G, tg, SHIFT_WIDTH), st_shift.reshape(G, Q, SHIFT_WIDTH), st_rwkv_all.reshape(state_shape),
            P['shift_mu'], P['w0'], P['lora_wa'], P['g_lora_up'], P['a0'], P['k_k'], P['k_a'],
            P['r_k'], P['ln_x_w'], P['ln_x_b'], *consts, new_rwkv_all.reshape(state_shape)]
    y, nsh, ns = pl.pallas_call(
        kern,
        grid=(G // nb, tg // C),
        in_specs=in_specs,
        out_specs=[
            pl.BlockSpec((nb, C, o), lambda b, c: (b, c, 0)),
            shift_spec,
            pl.BlockSpec(state_block, state_index),
        ],
        out_shape=[
            jax.ShapeDtypeStruct((G, tg, o), BF16),
            jax.ShapeDtypeStruct((G, Q, SHIFT_WIDTH), F32),
            jax.ShapeDtypeStruct(state_shape, F32),
        ],
        scratch_shapes=[
            pltpu.VMEM((nb, Q, SHIFT_WIDTH), F32),
            pltpu.VMEM((nb, Q, N_GROUPS, MXU_DIM, MXU_DIM), F32),
        ],
        input_output_aliases={len(args) - 1: 2},
        compiler_params=_cparams(("arbitrary", "arbitrary")),
        name="rwkv",
    )(*args)
    return (y.reshape(B, T, o), nsh.reshape(B, SHIFT_WIDTH),
            ns.reshape(DEPTH, B, RWKV_HEADS, RWKV_HEAD, RWKV_HEAD))


def _merge_kernel(x_ref, ya_ref, yb_ref, gt_ref, wa_ref, wb_ref, wo_ref, nf_ref, *rest, moe):
    if moe:
        wr_ref, xo_ref, gates_ref, cnt_ref = rest
    else:
        xo_ref, h_ref = rest
    ga = gt_ref[:, :D_MODEL]
    gb = gt_ref[:, D_MODEL:]
    m = _sigmoid(ga) * _dot(ya_ref[...], wa_ref[...]) + _sigmoid(gb) * _dot(yb_ref[...], wb_ref[...])
    x = x_ref[...] + _dot(m.astype(BF16), wo_ref[...])
    xo_ref[...] = x
    h = _rms(x, nf_ref[...])
    h_hi = h.astype(BF16)
    if not moe:
        h_ref[...] = h_hi
    else:
        w = wr_ref[...]
        w_hi = w.astype(BF16)
        w_lo = (w - w_hi.astype(F32)).astype(BF16)
        h_lo = (h - h_hi.astype(F32)).astype(BF16)
        logits = _dot(h_hi, w_hi) + _dot(h_lo, w_hi) + _dot(h_hi, w_lo)
        lane = lax.broadcasted_iota(jnp.int32, logits.shape, 1)
        real = lane < N_EXPERTS
        logits = jnp.where(real, logits, -jnp.inf)
        e = jnp.exp(logits - jnp.max(logits, axis=-1, keepdims=True))
        p = jnp.where(real, e / jnp.sum(e, axis=-1, keepdims=True), -1.0)
        m1 = jnp.max(p, axis=-1, keepdims=True)
        i1 = jnp.min(jnp.where(p == m1, lane, LANES), axis=-1, keepdims=True)
        oh1 = lane == i1
        p2 = jnp.where(oh1, -1.0, p)
        m2 = jnp.max(p2, axis=-1, keepdims=True)
        i2 = jnp.min(jnp.where(p2 == m2, lane, LANES), axis=-1, keepdims=True)
        oh2 = lane == i2
        tot = m1 + m2
        gates = jnp.where(oh1, m1 / tot, 0.0) + jnp.where(oh2, m2 / tot, 0.0)
        gates_ref[...] = gates
        cnt = jnp.sum((gates > 0.0).astype(F32), axis=0, keepdims=True)
        cnt_ref[...] = jnp.broadcast_to(cnt, (SUBLANES, LANES))


def _merge(x, ya, yb, gt, P, layer, tm, moe):
    n = x.shape[0]
    row = lambda w: pl.BlockSpec((tm, w), lambda i: (i, 0))
    wsq = lambda: pl.BlockSpec((None, D_MODEL, D_MODEL), lambda i: (layer, 0, 0))
    in_specs = [row(D_MODEL), row(D_MODEL), row(D_MODEL), row(2 * D_MODEL), wsq(), wsq(), wsq(),
                pl.BlockSpec((None, 1, D_MODEL), lambda i: (layer, 0, 0))]
    args = [x, ya, yb, gt, P['w_out_a'], P['w_out_b'], P['w_out'], P['norm_ffn']]
    out_specs = [row(D_MODEL)]
    out_shape = [jax.ShapeDtypeStruct((n, D_MODEL), F32)]
    if moe:
        in_specs.append(pl.BlockSpec((None, D_MODEL, LANES), lambda i: (layer // 2, 0, 0)))
        args.append(P['w_router_pad'])
        out_specs.append(row(LANES))
        out_shape.append(jax.ShapeDtypeStruct((n, LANES), F32))
        out_specs.append(pl.BlockSpec((None, SUBLANES, LANES), lambda i: (i, 0, 0)))
        out_shape.append(jax.ShapeDtypeStruct((n // tm, SUBLANES, LANES), F32))
    else:
        out_specs.append(row(D_MODEL))
        out_shape.append(jax.ShapeDtypeStruct((n, D_MODEL), BF16))
    return pl.pallas_call(
        functools.partial(_merge_kernel, moe=moe),
        grid=(n // tm,),
        in_specs=in_specs,
        out_specs=out_specs,
        out_shape=out_shape,
        compiler_params=_cparams(("arbitrary",)),
        name="merge",
    )(*args)


def _swiglu_acc(h, wg_ref, wu_ref, wd_ref):
    acc = None
    for c0 in range(0, D_FF, FF_CHUNK):
        sl = slice(c0, c0 + FF_CHUNK)
        gate = _dot(h, wg_ref[:, sl])
        up = _dot(h, wu_ref[:, sl])
        act = (gate * _sigmoid(gate) * up).astype(BF16)
        part = _dot(act, wd_ref[sl, :])
        acc = part if acc is None else acc + part
    return acc


def _ffn_kernel(x_ref, h_ref, wg_ref, wu_ref, wd_ref, o_ref):
    o_ref[...] = x_ref[...] + _swiglu_acc(h_ref[...], wg_ref, wu_ref, wd_ref)


def _ffn(x, h, P, j, tm):
    n = x.shape[0]
    row = lambda: pl.BlockSpec((tm, D_MODEL), lambda i: (i, 0))
    return pl.pallas_call(
        _ffn_kernel,
        grid=(n // tm,),
        in_specs=[row(), row(),
                  pl.BlockSpec((None, D_MODEL, D_FF), lambda i: (j, 0, 0), pipeline_mode=pl.Buffered(1)),
                  pl.BlockSpec((None, D_MODEL, D_FF), lambda i: (j, 0, 0), pipeline_mode=pl.Buffered(1)),
                  pl.BlockSpec((None, D_FF, D_MODEL), lambda i: (j, 0, 0), pipeline_mode=pl.Buffered(1))],
        out_specs=row(),
        out_shape=jax.ShapeDtypeStruct((n, D_MODEL), F32),
        compiler_params=_cparams(("arbitrary",)),
        name="ffn",
    )(x, h, P['w_ffn_gate'], P['w_ffn_up'], P['w_ffn_down'])


def _route_kernel(gates_ref, base_ref, pos_ref, *, tw, spare):
    mask = (gates_ref[...] > 0.0).astype(BF16)
    r_idx = lax.broadcasted_iota(jnp.int32, (tw, tw), 0)
    c_idx = lax.broadcasted_iota(jnp.int32, (tw, tw), 1)
    excl = _dot_ta(mask, (r_idx < c_idx).astype(BF16))
    sel = _dot_ta(mask, (r_idx == c_idx).astype(BF16)) > 0.0
    pos = excl + base_ref[:, 0:1]
    e_io = lax.broadcasted_iota(jnp.int32, (LANES, tw), 0)
    e_first = jnp.min(jnp.where(sel, e_io, LANES), axis=0, keepdims=True)
    e_last = jnp.max(jnp.where(sel, e_io, -1), axis=0, keepdims=True)
    p_first = jnp.sum(jnp.where(e_io == e_first, pos, 0.0), axis=0, keepdims=True)
    p_last = jnp.sum(jnp.where(e_io == e_last, pos, 0.0), axis=0, keepdims=True)
    p_last = jnp.where(e_last == e_first, float(spare), p_last)
    rows = jnp.concatenate([p_first, p_last, jnp.zeros((SUBLANES - 2, tw), F32)], axis=0)
    pos_ref[...] = rows.astype(jnp.int32)


def _route(gates, base_b, tw, spare):
    n = gates.shape[0]
    return pl.pallas_call(
        functools.partial(_route_kernel, tw=tw, spare=spare),
        grid=(n // tw,),
        in_specs=[pl.BlockSpec((tw, LANES), lambda i: (i, 0)),
                  pl.BlockSpec((None, LANES, LANES), lambda i: (i, 0, 0))],
        out_specs=pl.BlockSpec((None, SUBLANES, tw), lambda i: (i, 0, 0)),
        out_shape=jax.ShapeDtypeStruct((n // tw, SUBLANES, tw), jnp.int32),
        compiler_params=_cparams(("arbitrary",)),
        name="route",
    )(gates, base_b)


def _row_copy(src_hbm, src_row, dst_ref, dst_row, sem):
    return pltpu.make_async_copy(src_hbm.at[pl.ds(src_row, 1), :], dst_ref.at[pl.ds(dst_row, 1), :], sem)


def _disperse_kernel(pos_ref, x_ref, init_hbm, xs_hbm, sem, *, tw):
    del init_hbm

    def issue(n, carry):
        _row_copy(x_ref, n, xs_hbm, pos_ref[0, n], sem).start()
        _row_copy(x_ref, n, xs_hbm, pos_ref[1, n], sem).start()
        return carry

    def drain(n, carry):
        _row_copy(x_ref, 0, xs_hbm, 0, sem).wait()
        _row_copy(x_ref, 0, xs_hbm, 0, sem).wait()
        return carry

    lax.fori_loop(0, tw, issue, 0, unroll=8)
    lax.fori_loop(0, tw, drain, 0, unroll=8)


def _disperse(pos, x, n_rows, tw):
    n = x.shape[0]
    return pl.pallas_call(
        functools.partial(_disperse_kernel, tw=tw),
        grid=(n // tw,),
        in_specs=[pl.BlockSpec((None, SUBLANES, tw), lambda i: (i, 0, 0), memory_space=pltpu.SMEM),
                  pl.BlockSpec((tw, D_MODEL), lambda i: (i, 0)),
                  pl.BlockSpec(memory_space=pl.ANY)],
        out_specs=pl.BlockSpec(memory_space=pl.ANY),
        out_shape=jax.ShapeDtypeStruct((n_rows, D_MODEL), F32),
        scratch_shapes=[pltpu.SemaphoreType.DMA],
        input_output_aliases={2: 0},
        compiler_params=_cparams(("arbitrary",)),
        name="disperse",
    )(pos, x, jnp.zeros((n_rows, D_MODEL), F32))


def _experts_kernel(te_ref, used_ref, xs_ref, nf_ref, wg_ref, wu_ref, wd_ref, ys_ref):
    del te_ref
    j = pl.program_id(0)

    @pl.when(j < used_ref[0])
    def _():
        h = _rms(xs_ref[...], nf_ref[...]).astype(BF16)
        ys_ref[...] = _swiglu_acc(h, wg_ref, wu_ref, wd_ref)

    @pl.when(j >= used_ref[0])
    def _():
        ys_ref[...] = jnp.zeros(ys_ref.shape, F32)


def _experts(tile_expert, used, xs, P, layer, j):
    n_tiles = tile_expert.shape[0]
    tr = xs.shape[0] // n_tiles
    wspec = lambda a, b: pl.BlockSpec((None, None, a, b), lambda t, te, u: (j, te[t], 0, 0))
    grid_spec = pltpu.PrefetchScalarGridSpec(
        num_scalar_prefetch=2,
        grid=(n_tiles,),
        in_specs=[pl.BlockSpec((tr, D_MODEL), lambda t, te, u: (t, 0)),
                  pl.BlockSpec((None, 1, D_MODEL), lambda t, te, u: (layer, 0, 0)),
                  wspec(D_MODEL, D_FF), wspec(D_MODEL, D_FF), wspec(D_FF, D_MODEL)],
        out_specs=pl.BlockSpec((tr, D_MODEL), lambda t, te, u: (t, 0)),
    )
    return pl.pallas_call(
        _experts_kernel,
        grid_spec=grid_spec,
        out_shape=jax.ShapeDtypeStruct((n_tiles * tr, D_MODEL), F32),
        compiler_params=_cparams(("arbitrary",)),
        name="experts",
    )(tile_expert, used, xs, P['norm_ffn'], P['w_moe_gate'], P['w_moe_up'], P['w_moe_down'])


def _combine_kernel(pos_ref, x_ref, gates_ref, ys_hbm, *rest, tw, final):
    nf_ref = rest[0] if final else None
    o_ref, y1_s, y2_s, sem = rest[1:] if final else rest

    def issue(n, carry):
        _row_copy(ys_hbm, pos_ref[0, n], y1_s, n, sem).start()
        _row_copy(ys_hbm, pos_ref[1, n], y2_s, n, sem).start()
        return carry

    def drain(n, carry):
        _row_copy(ys_hbm, 0, y1_s, 0, sem).wait()
        _row_copy(ys_hbm, 0, y2_s, 0, sem).wait()
        return carry

    lax.fori_loop(0, tw, issue, 0, unroll=8)
    gates = gates_ref[...]
    lane = lax.broadcasted_iota(jnp.int32, gates.shape, 1)
    sel = gates > 0.0
    e_first = jnp.min(jnp.where(sel, lane, LANES), axis=-1, keepdims=True)
    e_last = jnp.max(jnp.where(sel, lane, -1), axis=-1, keepdims=True)
    g_first = jnp.sum(jnp.where(lane == e_first, gates, 0.0), axis=-1, keepdims=True)
    g_last = jnp.sum(jnp.where((lane == e_last) & (e_last != e_first), gates, 0.0), axis=-1, keepdims=True)
    lax.fori_loop(0, tw, drain, 0, unroll=8)
    x = x_ref[...] + g_first * y1_s[...] + g_last * y2_s[...]
    o_ref[...] = _rms(x, nf_ref[...]) if final else x


def _combine(pos, x, gates, ys, tw, norm_final=None):
    n = x.shape[0]
    final = norm_final is not None
    row = pl.BlockSpec((tw, D_MODEL), lambda i: (i, 0))
    in_specs = [pl.BlockSpec((None, SUBLANES, tw), lambda i: (i, 0, 0), memory_space=pltpu.SMEM),
                row, pl.BlockSpec((tw, LANES), lambda i: (i, 0)), pl.BlockSpec(memory_space=pl.ANY)]
    args = [pos, x, gates, ys]
    if final:
        in_specs.append(pl.BlockSpec((1, D_MODEL), lambda i: (0, 0)))
        args.append(norm_final)
    return pl.pallas_call(
        functools.partial(_combine_kernel, tw=tw, final=final),
        grid=(n // tw,),
        in_specs=in_specs,
        out_specs=row,
        out_shape=jax.ShapeDtypeStruct((n, D_MODEL), F32),
        scratch_shapes=[pltpu.VMEM((tw, D_MODEL), F32), pltpu.VMEM((tw, D_MODEL), F32), pltpu.SemaphoreType.DMA],
        compiler_params=_cparams(("arbitrary",)),
        name="combine",
    )(*args)


def _moe_routed(x, gates, cnt, P, layer, tm):
    n = x.shape[0]
    tw = min(ROUTE_TILE, n)
    tr = ROUTE_ROWS if TOP_K * n >= 4 * N_EXPERTS * ROUTE_ROWS else ROUTE_ROWS_SMALL
    n_tiles = (TOP_K * n) // tr + N_EXPERTS
    spare = n_tiles * tr
    c = cnt[:, 0, :].astype(jnp.int32).reshape(n // tw, tw // tm, LANES).sum(axis=1)
    seg_tiles = (c.sum(axis=0) + tr - 1) // tr
    seg_end = jnp.cumsum(seg_tiles)
    base = (seg_end - seg_tiles) * tr + jnp.cumsum(c, axis=0) - c
    base_b = jnp.broadcast_to(base.astype(F32)[:, :, None], (n // tw, LANES, LANES))
    tile_expert = jnp.minimum((jnp.arange(n_tiles + 1)[:, None] >= seg_end[None, :N_EXPERTS]).sum(axis=1),
                              N_EXPERTS - 1).astype(jnp.int32)
    used = seg_end[N_EXPERTS - 1:N_EXPERTS].astype(jnp.int32)
    pos = _route(gates, base_b, tw, spare)
    xs = _disperse(pos, x, spare + tr, tw)
    ys = _experts(tile_expert, used, xs, P, layer, layer // 2)
    return _combine(pos, x, gates, ys, tw, P['norm_final'] if layer == DEPTH - 1 else None)


def _block_diag_weights(w):
    d = w.shape[0]
    per = MXU_DIM // LRU_BLOCK
    w = w.reshape(d, LRU_HEADS // per, per, LRU_BLOCK, LRU_BLOCK)
    eye = jnp.eye(per, dtype=w.dtype)
    bd = jnp.einsum('dcpij,pq->dcpiqj', w, eye)
    return bd.reshape(d, LRU_HEADS // per, MXU_DIM, MXU_DIM).astype(BF16)


def _prep_params(p):
    P = dict(p)
    for name in ('w_in', 'w_out_a', 'w_out_b', 'w_out', 'w_ffn_gate', 'w_ffn_up', 'w_ffn_down',
                 'w_moe_gate', 'w_moe_up', 'w_moe_down', 'g_lora_up'):
        P[name] = p[name].astype(BF16)
    for name in ('norm_mix', 'conv_b', 'gate_a_b', 'gate_x_b', 'lru_lambda', 'shift_mu', 'w0', 'a0', 'k_k', 'k_a',
                 'ln_x_w', 'ln_x_b', 'norm_ffn'):
        P[name] = p[name][:, None, :]
    P['r_k'] = p['r_k'].reshape(DEPTH, 1, RWKV_WIDTH)
    P['norm_final'] = p['norm_final'][None, :]
    P['gaw_bd'] = _block_diag_weights(p['gate_a_w'])
    P['gxw_bd'] = _block_diag_weights(p['gate_x_w'])
    z = jnp.zeros((DEPTH, LORA_W, RWKV_WIDTH), F32)
    P['lora_wa'] = jnp.concatenate([jnp.concatenate([p['w_lora_up'], z], axis=2),
                                    jnp.concatenate([z, p['a_lora_up']], axis=2)], axis=1).astype(BF16)
    P['w_router_pad'] = jnp.pad(p['w_router'], ((0, 0), (0, 0), (0, LANES - N_EXPERTS)))
    return P


def _trunk(x3, st_rwkv, st_lru, st_conv, st_shift, P, *, zero_state, tm, lru_bb, lru_tt, rwkv_nb):
    B, T, _ = x3.shape
    n = B * T
    x = x3.reshape(n, D_MODEL)
    new_rwkv, n_lru, n_conv, n_shift = jnp.zeros(st_rwkv.shape, F32), [], [], []
    for l in range(DEPTH):
        xy, ps, gt = _proj(x, P['norm_mix'], P['w_in'], l, tm)
        ya, h_new, c_new = _lru(xy.reshape(B, T, 2 * LRU_WIDTH), st_conv[l], st_lru[l][:, None, :], P, l,
                                lru_bb, lru_tt)
        yb, sh_new, new_rwkv = _wkv(ps.reshape(B, T, SHIFT_WIDTH), st_shift[l], st_rwkv, new_rwkv, P, l, rwkv_nb,
                                    zero_state)
        moe = l % 2 == 1
        res = _merge(x, ya.reshape(n, D_MODEL), yb.reshape(n, D_MODEL), gt, P, l, tm, moe)
        if moe:
            x, gates, cnt = res
            x = _moe_routed(x, gates, cnt, P, l, tm)
        else:
            x, h = res
            x = _ffn(x, h, P, l // 2, tm)
        n_lru.append(h_new[:, 0, :])
        n_conv.append(c_new)
        n_shift.append(sh_new)
    assert DEPTH % 2 == 0
    return x.reshape(B, T, D_MODEL), new_rwkv, jnp.stack(n_lru), jnp.stack(n_conv), jnp.stack(n_shift)


def kernel(x_prompt, x_sample, state_rwkv, state_lru, state_conv, state_shift, norm_mix, w_in, conv_w, conv_b, gate_a_w, gate_a_b, gate_x_w, gate_x_b, lru_lambda, shift_mu, w0, w_lora_up, a0, a_lora_up, g_lora_up, k_k, k_a, r_k, ln_x_w, ln_x_b, w_out_a, w_out_b, w_out, norm_ffn, w_ffn_gate, w_ffn_up, w_ffn_down, w_router, w_moe_gate, w_moe_up, w_moe_down, norm_final):
    P = _prep_params(dict(
        norm_mix=norm_mix, w_in=w_in, conv_w=conv_w, conv_b=conv_b, gate_a_w=gate_a_w, gate_a_b=gate_a_b,
        gate_x_w=gate_x_w, gate_x_b=gate_x_b, lru_lambda=lru_lambda, shift_mu=shift_mu, w0=w0,
        w_lora_up=w_lora_up, a0=a0, a_lora_up=a_lora_up, g_lora_up=g_lora_up, k_k=k_k, k_a=k_a, r_k=r_k,
        ln_x_w=ln_x_w, ln_x_b=ln_x_b, w_out_a=w_out_a, w_out_b=w_out_b, w_out=w_out, norm_ffn=norm_ffn,
        w_ffn_gate=w_ffn_gate, w_ffn_up=w_ffn_up, w_ffn_down=w_ffn_down, w_router=w_router,
        w_moe_gate=w_moe_gate, w_moe_up=w_moe_up, w_moe_down=w_moe_down, norm_final=norm_final))
    bp = x_prompt.shape[0]
    bs = x_sample.shape[0]
    zeros = lambda *shape: jnp.zeros(shape, F32)
    p_out = _trunk(x_prompt,
                   zeros(DEPTH, bp, RWKV_HEADS, RWKV_HEAD, RWKV_HEAD), zeros(DEPTH, bp, LRU_WIDTH),
                   zeros(DEPTH, bp, CONV_WIDTH - 1, LRU_WIDTH), zeros(DEPTH, bp, SHIFT_WIDTH), P,
                   zero_state=True, tm=512, lru_bb=1, lru_tt=512, rwkv_nb=4)
    s_out = _trunk(x_sample, state_rwkv, state_lru, state_conv, state_shift, P,
                   zero_state=False, tm=256, lru_bb=8, lru_tt=x_sample.shape[1], rwkv_nb=1)
    return (p_out[0], s_out[0]) + tuple(p_out[1:]) + tuple(s_out[1:])
```

```python
import functools

import numpy as np
import jax
import jax.numpy as jnp
from jax import lax
from jax.experimental import pallas as pl
from jax.experimental.pallas import tpu as pltpu

F32 = jnp.float32
BF16 = jnp.bfloat16

D_MODEL = 1024
DEPTH = 4
LRU_WIDTH = D_MODEL
LRU_HEADS = 16
LRU_BLOCK = LRU_WIDTH // LRU_HEADS
CONV_WIDTH = 4
LRU_C = 8.0
RWKV_HEAD = 64
RWKV_WIDTH = D_MODEL
RWKV_HEADS = RWKV_WIDTH // RWKV_HEAD
LORA_W = 64
LORA_A = 64
LORA_G = 128
SHIFT_WIDTH = 3 * RWKV_WIDTH + LORA_W + LORA_A + LORA_G
PROJ_WIDTH = 2 * LRU_WIDTH + SHIFT_WIDTH + 2 * D_MODEL
D_FF = 2816
N_EXPERTS = 8
NORM_EPS = 1e-6
GN_EPS = 64e-5

LANES = 128
SUBLANES = 8
MXU_DIM = 256
VMEM_LIMIT = 56 * 1024 * 1024

CHUNK = 64
GROUP_HEADS = MXU_DIM // CHUNK
GROUP_LANES = GROUP_HEADS * RWKV_HEAD
N_GROUPS = RWKV_HEADS // GROUP_HEADS
FF_CHUNK = 256
TOP_K = 2
ROUTE_TILE = 512
ROUTE_ROWS = 512
ROUTE_ROWS_SMALL = 128
SCAN_ROWS = 16
assert CHUNK == RWKV_HEAD


def _cparams(sem):
    return pltpu.CompilerParams(dimension_semantics=sem, vmem_limit_bytes=VMEM_LIMIT)


def _dot(a, b):
    return jnp.dot(a, b, preferred_element_type=F32)


def _dot_tb(a, b):
    return lax.dot_general(a, b, (((1,), (1,)), ((), ())), preferred_element_type=F32)


def _dot_ta(a, b):
    return lax.dot_general(a, b, (((0,), (0,)), ((), ())), preferred_element_type=F32)


def _sigmoid(x):
    return jax.nn.sigmoid(x)


def _softplus(z):
    return jnp.maximum(z, 0.0) + jnp.log(1.0 + jnp.exp(-jnp.abs(z)))


def _rms(x, g):
    return x * lax.rsqrt(jnp.mean(x * x, axis=-1, keepdims=True) + NORM_EPS) * g


_PROJ_SPLITS = ((0, 2 * LRU_WIDTH), (2 * LRU_WIDTH, SHIFT_WIDTH), (2 * LRU_WIDTH + SHIFT_WIDTH, 2 * D_MODEL))


def _proj_kernel(x_ref, g_ref, w_ref, xy_ref, ps_ref, gt_ref):
    u = _rms(x_ref[...], g_ref[...]).astype(BF16)
    for out_ref, (c0, width) in zip((xy_ref, ps_ref, gt_ref), _PROJ_SPLITS):
        j = 0
        while j < width:
            w = min(512, width - j)
            out_ref[:, j:j + w] = _dot(u, w_ref[:, c0 + j:c0 + j + w])
            j += w


def _proj(x, g_all, w_all, layer, tm):
    n = x.shape[0]
    return pl.pallas_call(
        _proj_kernel,
        grid=(n // tm,),
        in_specs=[
            pl.BlockSpec((tm, D_MODEL), lambda i: (i, 0)),
            pl.BlockSpec((None, 1, D_MODEL), lambda i: (layer, 0, 0)),
            pl.BlockSpec((None, D_MODEL, PROJ_WIDTH), lambda i: (layer, 0, 0),
                         pipeline_mode=pl.Buffered(1)),
        ],
        out_specs=[
            pl.BlockSpec((tm, 2 * LRU_WIDTH), lambda i: (i, 0)),
            pl.BlockSpec((tm, SHIFT_WIDTH), lambda i: (i, 0)),
            pl.BlockSpec((tm, 2 * D_MODEL), lambda i: (i, 0)),
        ],
        out_shape=[
            jax.ShapeDtypeStruct((n, 2 * LRU_WIDTH), F32),
            jax.ShapeDtypeStruct((n, SHIFT_WIDTH), F32),
            jax.ShapeDtypeStruct((n, 2 * D_MODEL), F32),
        ],
        compiler_params=_cparams(("arbitrary",)),
        name="proj",
    )(x, g_all, w_all)


def _gelu_tanh(x):
    return 0.5 * x * (1.0 + jnp.tanh(np.sqrt(2.0 / np.pi).astype(np.float32) * (x + 0.044715 * (x * x * x))))


def _lru_kernel(xa_ref, ya_ref, cs_ref, h0_ref, cw_ref, cb_ref, gaw_ref, gab_ref, gxw_ref, gxb_ref, lam_ref,
                y_ref, hl_ref, nc_ref, xp_s, h_s, a_s, b_s, *, bb, tt):
    ti = pl.program_id(1)
    neg_c = -LRU_C * _softplus(-lam_ref[...])
    cw = cw_ref[...]
    @pl.when(ti == 0)
    def _():
        for b in range(bb):
            xp_s[b, 5:8, :] = cs_ref[b]
            h_s[b] = h0_ref[b]

    for b in range(bb):
        xa = xa_ref[b]
        xp_s[b, 8:8 + tt, :] = xa
        xc = cb_ref[...] + xp_s[b, 5:5 + tt, :] * cw[0:1]
        xc = xc + xp_s[b, 6:6 + tt, :] * cw[1:2]
        xc = xc + xp_s[b, 7:7 + tt, :] * cw[2:3]
        xc = xc + xa * cw[3:4]
        tail = xp_s[b, 5 + tt:8 + tt, :]
        xp_s[b, 5:8, :] = tail
        nc_ref[b] = tail

        for c in range(LRU_WIDTH // MXU_DIM):
            sl = slice(c * MXU_DIM, (c + 1) * MXU_DIM)
            xcc = xc[:, sl]
            xcb = xcc.astype(BF16)
            r = _sigmoid(_dot(xcb, gaw_ref[c]) + gab_ref[:, sl])
            i = _sigmoid(_dot(xcb, gxw_ref[c]) + gxb_ref[:, sl])
            log_a = neg_c[:, sl] * r
            a = jnp.exp(log_a)
            a_s[:, sl] = a
            b_s[:, sl] = jnp.sqrt(-jnp.tanh(log_a) * (a * a + 1.0)) * (i * xcc)

        if tt % SCAN_ROWS == 0:
            row = lax.broadcasted_iota(jnp.int32, (SUBLANES, LRU_WIDTH), 0)

            def blk(j, h):
                r0 = pl.multiple_of(j * SCAN_ROWS, SCAN_ROWS)
                hbs = []
                for q in range(SCAN_ROWS // SUBLANES):
                    av = a_s[pl.ds(r0 + q * SUBLANES, SUBLANES), :]
                    bv = b_s[pl.ds(r0 + q * SUBLANES, SUBLANES), :]
                    for d in (1, 2, 4):
                        m = row >= d
                        a_sh = pltpu.roll(av, d, axis=0)
                        b_sh = pltpu.roll(bv, d, axis=0)
                        bv = jnp.where(m, av * b_sh + bv, bv)
                        av = jnp.where(m, av * a_sh, av)
                    hb = av * h + bv
                    h = hb[SUBLANES - 1:SUBLANES, :]
                    hbs.append(hb)
                hs = jnp.concatenate(hbs, axis=0)
                y_ref[b, pl.ds(r0, SCAN_ROWS), :] = (hs * _gelu_tanh(ya_ref[b, pl.ds(r0, SCAN_ROWS), :])).astype(BF16)
                return h

            h = lax.fori_loop(0, tt // SCAN_ROWS, blk, h_s[b])
        else:
            h = h_s[b]
            rows = []
            for t in range(tt):
                h = a_s[t:t + 1, :] * h + b_s[t:t + 1, :]
                rows.append(h)
            hs = jnp.concatenate(rows, axis=0)
            y_ref[b] = (hs * _gelu_tanh(ya_ref[b])).astype(BF16)
        h_s[b] = h
        hl_ref[b] = h


def _lru(xy3, st_conv, st_lru, P, layer, bb, tt):
    B, T, _ = xy3.shape
    W = LRU_WIDTH
    vec = lambda: pl.BlockSpec((None, 1, W), lambda b, t: (layer, 0, 0))
    kern = functools.partial(_lru_kernel, bb=bb, tt=tt)
    return pl.pallas_call(
        kern,
        grid=(B // bb, T // tt),
        in_specs=[
            pl.BlockSpec((bb, tt, W), lambda b, t: (b, t, 0)),
            pl.BlockSpec((bb, tt, W), lambda b, t: (b, t, 1)),
            pl.BlockSpec((bb, CONV_WIDTH - 1, W), lambda b, t: (b, 0, 0)),
            pl.BlockSpec((bb, 1, W), lambda b, t: (b, 0, 0)),
            pl.BlockSpec((None, CONV_WIDTH, W), lambda b, t: (layer, 0, 0)),
            vec(),
            pl.BlockSpec((None, W // MXU_DIM, MXU_DIM, MXU_DIM), lambda b, t: (layer, 0, 0, 0)),
            vec(),
            pl.BlockSpec((None, W // MXU_DIM, MXU_DIM, MXU_DIM), lambda b, t: (layer, 0, 0, 0)),
            vec(),
            vec(),
        ],
        out_specs=[
            pl.BlockSpec((bb, tt, W), lambda b, t: (b, t, 0)),
            pl.BlockSpec((bb, 1, W), lambda b, t: (b, 0, 0)),
            pl.BlockSpec((bb, CONV_WIDTH - 1, W), lambda b, t: (b, 0, 0)),
        ],
        out_shape=[
            jax.ShapeDtypeStruct((B, T, W), BF16),
            jax.ShapeDtypeStruct((B, 1, W), F32),
            jax.ShapeDtypeStruct((B, CONV_WIDTH - 1, W), F32),
        ],
        scratch_shapes=[
            pltpu.VMEM((bb, tt + 8, W), F32),
            pltpu.VMEM((bb, 1, W), F32),
            pltpu.VMEM((tt, W), F32),
            pltpu.VMEM((tt, W), F32),
        ],
        compiler_params=_cparams(("arbitrary", "arbitrary")),
        name="lru",
    )(xy3, xy3, st_conv, st_lru, P['conv_w'], P['conv_b'], P['gaw_bd'], P['gate_a_b'], P['gxw_bd'],
      P['gate_x_b'], P['lru_lambda'])


def _block_diag(x_bf16, mask_ref):
    return jnp.concatenate([x_bf16] * GROUP_HEADS, axis=0) * mask_ref[...]


def _rows(x, b, n):
    return x[b * n:(b + 1) * n]


def _dot_split3(sel_bf16, x):
    hi = x.astype(BF16)
    r1 = x - hi.astype(F32)
    mid = r1.astype(BF16)
    lo = (r1 - mid.astype(F32)).astype(BF16)
    return _dot(sel_bf16, hi) + _dot(sel_bf16, mid) + _dot(sel_bf16, lo)


def _wkv_kernel(ps_ref, sh_ref, s0_ref, mu_ref, w0_ref, lora_ref, gup_ref, a0_ref, kk_ref, ka_ref, rk_ref,
                lnw_ref, lnb_ref, tri_ref, endm_ref, ones_ref, mrow_ref, msq_ref, new_all_ref,
                y_ref, nsh_ref, ns_ref, car_s, st_s, *, nb, seq, zero_state):
    del new_all_ref
    C = CHUNK
    Q = C // seq
    R = nb * C
    o = RWKV_WIDTH
    ci = pl.program_id(1)
    last = ci == pl.num_programs(1) - 1
    t_idx = lax.broadcasted_iota(jnp.int32, (C, MXU_DIM), 0)
    i_idx = lax.broadcasted_iota(jnp.int32, (C, MXU_DIM), 1) % C
    same = (t_idx // seq) == (i_idx // seq)
    m_strict = same & (i_idx < t_idx)
    m_incl = same & (i_idx <= t_idx)
    eye_cat = (i_idx == t_idx).astype(F32)
    ones_bd = ones_ref[...]
    msq_f = msq_ref[...].astype(F32)
    n_lane_tiles = o // MXU_DIM

    def head_sum(x):
        xs = jnp.concatenate([x[:, g * MXU_DIM:(g + 1) * MXU_DIM] for g in range(n_lane_tiles)], axis=0)
        s = _dot(xs.astype(BF16), ones_bd)
        return jnp.concatenate([_rows(s, g, R) for g in range(n_lane_tiles)], axis=1)

    @pl.when(ci == 0)
    def _():
        for b in range(nb):
            car_s[b] = sh_ref[b]
            for q in range(Q):
                if zero_state:
                    st_s[b, q] = jnp.zeros(st_s.shape[2:], F32)
                else:
                    for g in range(N_GROUPS):
                        heads = [s0_ref[b, q * RWKV_HEADS + g * GROUP_HEADS + hh] for hh in range(GROUP_HEADS)]
                        stacked = jnp.concatenate(heads, axis=0)
                        st_s[b, q, g] = jnp.concatenate([stacked] * GROUP_HEADS, axis=1) * msq_f

    row_c = lax.broadcasted_iota(jnp.int32, (C, 1), 0)
    prevs = []
    for b in range(nb):
        ps_b = ps_ref[b]
        if Q == 1:
            first = car_s[b]
            new_carry = ps_b[C - 1:C, :]
        else:
            put = (lax.broadcasted_iota(jnp.int32, (C, Q), 0)
                   == seq * lax.broadcasted_iota(jnp.int32, (C, Q), 1)).astype(BF16)
            take = (lax.broadcasted_iota(jnp.int32, (Q, C), 1)
                    == seq * lax.broadcasted_iota(jnp.int32, (Q, C), 0) + (seq - 1)).astype(BF16)
            first = _dot_split3(put, car_s[b])
            new_carry = _dot_split3(take, ps_b)
        prevs.append(jnp.where(row_c % seq == 0, first, pltpu.roll(ps_b, 1, axis=0)))
        car_s[b] = new_carry
        nsh_ref[b] = new_carry
    ps = jnp.concatenate([ps_ref[b] for b in range(nb)], axis=0)
    prev = jnp.concatenate(prevs, axis=0)

    s = ps + (prev - ps) * mu_ref[...]
    r = s[:, :o]
    k = s[:, o:2 * o]
    v = s[:, 2 * o:3 * o]
    dwa = s[:, 3 * o:3 * o + LORA_W + LORA_A]
    dg = s[:, 3 * o + LORA_W + LORA_A:]
    lane = lax.broadcasted_iota(jnp.int32, dwa.shape, 1)
    lora_in = jnp.where(lane < LORA_W, jnp.tanh(dwa), dwa).astype(BF16)
    lora = _dot(lora_in, lora_ref[...])
    w_log = -_softplus(-(w0_ref[...] + lora[:, :o])) - 0.5
    lw = -jnp.exp(w_log)
    a = _sigmoid(a0_ref[...] + lora[:, o:])
    g = _dot(_sigmoid(dg).astype(BF16), gup_ref[...])
    kk = k * kk_ref[...]
    kk = kk * lax.rsqrt(jnp.maximum(head_sum(kk * kk), 1e-24))
    kmod = k * (1.0 + (a - 1.0) * ka_ref[...])
    beta = kk * a
    lw_hi = lw.astype(BF16)
    lw_lo = (lw - lw_hi.astype(F32)).astype(BF16)
    L = _dot(tri_ref[...], lw_hi) + _dot(tri_ref[...], lw_lo)
    if Q == 1:
        l_end = jnp.concatenate([jnp.broadcast_to(_rows(L, b, C)[C - 1:C], (C, o)) for b in range(nb)], axis=0)
    else:
        l_end = _dot(endm_ref[...], lw_hi) + _dot(endm_ref[...], lw_lo)
    e_neg = jnp.exp(-L)
    p_end = jnp.exp(l_end)
    e_end = p_end * e_neg
    at = (-kk) * jnp.exp(L - lw)
    rt = r * jnp.exp(L)
    bt = (beta * e_neg).astype(BF16)
    kt = (kmod * e_neg).astype(BF16)
    bend = beta * e_end
    kend = kmod * e_end

    chains = []
    for b in range(nb):
        for gi in range(N_GROUPS):
            sl = slice(gi * GROUP_LANES, (gi + 1) * GROUP_LANES)
            chains.append(dict(
                b=b, gi=gi, vg=_rows(v, b, C)[:, sl].astype(BF16), p_end=_rows(p_end, b, C)[:, sl],
                x2=jnp.concatenate([_rows(at, b, C)[:, sl], _rows(rt, b, C)[:, sl]], axis=0).astype(BF16),
                wbd=jnp.concatenate([_block_diag(_rows(bt, b, C)[:, sl], mrow_ref),
                                     _block_diag(_rows(kt, b, C)[:, sl], mrow_ref)], axis=0),
                bk=jnp.concatenate([_rows(bend, b, C)[:, sl], _rows(kend, b, C)[:, sl]], axis=0).astype(BF16)))

    cw = GROUP_HEADS * C
    rowseq = (lax.broadcasted_iota(jnp.int32, (2 * C, 1), 0) % C) // seq
    for c in chains:
        res = _dot_tb(c['x2'], c['wbd'])
        n_cat = jnp.where(m_strict, res[:C, :cw], 0.0)
        c['a_v'] = jnp.concatenate([jnp.where(m_strict, res[:C, cw:], 0.0),
                                    jnp.where(m_incl, res[C:, cw:], 0.0)], axis=0).astype(BF16)
        c['a_rb'] = jnp.where(m_incl, res[C:, :cw], 0.0).astype(BF16)
        c['x_c'] = n_cat.astype(BF16)
        c['p_c'] = eye_cat + n_cat
        c['x_bd'] = _block_diag(c['x_c'], msq_ref)
    for c in chains:
        xs = None
        for q in range(Q):
            xq = _dot_tb(c['x2'], st_s[c['b'], q, c['gi']].astype(BF16))
            xs = xq if xs is None else jnp.where(rowseq == q, xq, xs)
        c['xs'] = xs
        c['v_bd'] = _block_diag(c['vg'], msq_ref)
    for c in chains:
        sv = c['xs'] + _dot(c['a_v'], c['v_bd'])
        c['rhs'] = sv[:C]
        c['y_sv'] = sv[C:]
    lvl = 2
    while lvl < seq:
        for c in chains:
            c['x_c'] = _dot(c['x_c'], c['x_bd']).astype(BF16)
            c['x_bd'] = _block_diag(c['x_c'], msq_ref)
        for c in chains:
            c['p_c'] = c['p_c'] + _dot(c['p_c'].astype(BF16), c['x_bd'])
        lvl *= 2
    for c in chains:
        c['u'] = _dot(c['p_c'].astype(BF16), _block_diag(c['rhs'].astype(BF16), msq_ref)).astype(BF16)
    ys = {}
    for c in chains:
        b, gi = c['b'], c['gi']
        ys[(b, gi)] = c['y_sv'] + _dot(c['a_rb'], _block_diag(c['u'], msq_ref))
        uv = jnp.concatenate([c['u'], c['vg']], axis=0)
        if Q > 1:
            uv = jnp.concatenate([jnp.where(rowseq == q, uv, jnp.zeros_like(uv)) for q in range(Q)], axis=1)
        ds = _dot_ta(uv, c['bk'])
        for q in range(Q):
            st_s[b, q, gi] = (st_s[b, q, gi] * c['p_end'][q * seq:q * seq + 1, :]
                              + _rows(ds, q, GROUP_LANES) * msq_f)

    yc = jnp.concatenate([jnp.concatenate([ys[(b, gi)] for gi in range(N_GROUPS)], axis=1) for b in range(nb)],
                         axis=0)
    inv_n = 1.0 / RWKV_HEAD
    mean = head_sum(yc) * inv_n
    dlt = yc - mean
    var = head_sum(dlt * dlt) * inv_n
    yn = dlt * lax.rsqrt(var + GN_EPS) * lnw_ref[...] + lnb_ref[...]
    bonus = head_sum(r * kmod * rk_ref[...]) * v
    out = ((yn + bonus) * g).astype(BF16)
    for b in range(nb):
        y_ref[b] = _rows(out, b, C)

    @pl.when(last)
    def _():
        for b in range(nb):
            for q in range(Q):
                for gi in range(N_GROUPS):
                    for hh in range(GROUP_HEADS):
                        ns_ref[b, q * RWKV_HEADS + gi * GROUP_HEADS + hh] = st_s[
                            b, q, gi, hh * RWKV_HEAD:(hh + 1) * RWKV_HEAD, hh * RWKV_HEAD:(hh + 1) * RWKV_HEAD]


def _wkv_consts(nb, seq):
    C = CHUNK
    gh = GROUP_HEADS
    rows = np.arange(nb * C)
    same = (rows[:, None] // seq) == (rows[None, :] // seq)
    tri = (same & (rows[None, :] <= rows[:, None])).astype(np.float32)
    endm = same.astype(np.float32)
    hl = np.arange(MXU_DIM) // RWKV_HEAD
    ones_bd = (hl[:, None] == hl[None, :]).astype(np.float32)
    rowh = np.arange(gh * C) // C
    colh = np.arange(GROUP_LANES) // RWKV_HEAD
    mrow = (rowh[:, None] == colh[None, :]).astype(np.float32)
    msq = (rowh[:, None] == rowh[None, :]).astype(np.float32)
    return tuple(jnp.asarray(m, BF16) for m in (tri, endm, ones_bd, mrow, msq))


def _wkv(ps3, st_shift, st_rwkv_all, new_rwkv_all, P, layer, nb, zero_state):
    B, T, _ = ps3.shape
    C = CHUNK
    seq = C if T % C == 0 else T
    assert C % seq == 0
    Q = C // seq
    G = B // Q
    tg = T * Q
    assert B % Q == 0 and G % nb == 0 and tg % C == 0
    o = RWKV_WIDTH
    consts = _wkv_consts(nb, seq)
    vec = lambda w: pl.BlockSpec((None, 1, w), lambda b, c: (layer, 0, 0))
    const = lambda arr: pl.BlockSpec(arr.shape, lambda b, c: (0,) * arr.ndim)
    kern = functools.partial(_wkv_kernel, nb=nb, seq=seq, zero_state=zero_state)
    shift_spec = pl.BlockSpec((nb, Q, SHIFT_WIDTH), lambda b, c: (b, 0, 0))
    state_shape = (DEPTH, G, Q * RWKV_HEADS, RWKV_HEAD, RWKV_HEAD)
    state_block = (None, nb, Q * RWKV_HEADS, RWKV_HEAD, RWKV_HEAD)
    state_index = lambda b, c: (layer, b, 0, 0, 0)
    in_specs = [
        pl.BlockSpec((nb, C, SHIFT_WIDTH), lambda b, c: (b, c, 0)),
        shift_spec,
        pl.BlockSpec(state_block, state_index, pipeline_mode=pl.Buffered(1)),
        vec(SHIFT_WIDTH),
        vec(o),
        pl.BlockSpec((None, LORA_W + LORA_A, 2 * o), lambda b, c: (layer, 0, 0)),
        pl.BlockSpec((None, LORA_G, o), lambda b, c: (layer, 0, 0)),
        vec(o), vec(o), vec(o), vec(o), vec(o), vec(o),
    ] + [const(m) for m in consts] + [pl.BlockSpec(memory_space=pl.ANY)]
    args = [ps3.reshape(G, tg, SHIFT_WIDTH), st_shift.reshape(G, Q, SHIFT_WIDTH), st_rwkv_all.reshape(state_shape),
            P['shift_mu'], P['w0'], P['lora_wa'], P['g_lora_up'], P['a0'], P['k_k'], P['k_a'],
            P['r_k'], P['ln_x_w'], P['ln_x_b'], *consts, new_rwkv_all.reshape(state_shape)]
    y, nsh, ns = pl.pallas_call(
        kern,
        grid=(G // nb, tg // C),
        in_specs=in_specs,
        out_specs=[
            pl.BlockSpec((nb, C, o), lambda b, c: (b, c, 0)),
            shift_spec,
            pl.BlockSpec(state_block, state_index),
        ],
        out_shape=[
            jax.ShapeDtypeStruct((G, tg, o), BF16),
            jax.ShapeDtypeStruct((G, Q, SHIFT_WIDTH), F32),
            jax.ShapeDtypeStruct(state_shape, F32),
        ],
        scratch_shapes=[
            pltpu.VMEM((nb, Q, SHIFT_WIDTH), F32),
            pltpu.VMEM((nb, Q, N_GROUPS, MXU_DIM, MXU_DIM), F32),
        ],
        input_output_aliases={len(args) - 1: 2},
        compiler_params=_cparams(("arbitrary", "arbitrary")),
        name="rwkv",
    )(*args)
    return (y.reshape(B, T, o), nsh.reshape(B, SHIFT_WIDTH),
            ns.reshape(DEPTH, B, RWKV_HEADS, RWKV_HEAD, RWKV_HEAD))


def _merge_kernel(x_ref, ya_ref, yb_ref, gt_ref, wa_ref, wb_ref, wo_ref, nf_ref, *rest, moe):
    if moe:
        wr_ref, xo_ref, gates_ref, cnt_ref = rest
    else:
        xo_ref, h_ref = rest
    ga = gt_ref[:, :D_MODEL]
    gb = gt_ref[:, D_MODEL:]
    m = _sigmoid(ga) * _dot(ya_ref[...], wa_ref[...]) + _sigmoid(gb) * _dot(yb_ref[...], wb_ref[...])
    x = x_ref[...] + _dot(m.astype(BF16), wo_ref[...])
    xo_ref[...] = x
    h = _rms(x, nf_ref[...])
    h_hi = h.astype(BF16)
    if not moe:
        h_ref[...] = h_hi
    else:
        w = wr_ref[...]
        w_hi = w.astype(BF16)
        w_lo = (w - w_hi.astype(F32)).astype(BF16)
        h_lo = (h - h_hi.astype(F32)).astype(BF16)
        logits = _dot(h_hi, w_hi) + _dot(h_lo, w_hi) + _dot(h_hi, w_lo)
        lane = lax.broadcasted_iota(jnp.int32, logits.shape, 1)
        real = lane < N_EXPERTS
        logits = jnp.where(real, logits, -jnp.inf)
        e = jnp.exp(logits - jnp.max(logits, axis=-1, keepdims=True))
        p = jnp.where(real, e / jnp.sum(e, axis=-1, keepdims=True), -1.0)
        m1 = jnp.max(p, axis=-1, keepdims=True)
        i1 = jnp.min(jnp.where(p == m1, lane, LANES), axis=-1, keepdims=True)
        oh1 = lane == i1
        p2 = jnp.where(oh1, -1.0, p)
        m2 = jnp.max(p2, axis=-1, keepdims=True)
        i2 = jnp.min(jnp.where(p2 == m2, lane, LANES), axis=-1, keepdims=True)
        oh2 = lane == i2
        tot = m1 + m2
        gates = jnp.where(oh1, m1 / tot, 0.0) + jnp.where(oh2, m2 / tot, 0.0)
        gates_ref[...] = gates
        cnt = jnp.sum((gates > 0.0).astype(F32), axis=0, keepdims=True)
        cnt_ref[...] = jnp.broadcast_to(cnt, (SUBLANES, LANES))


def _merge(x, ya, yb, gt, P, layer, tm, moe):
    n = x.shape[0]
    row = lambda w: pl.BlockSpec((tm, w), lambda i: (i, 0))
    wsq = lambda: pl.BlockSpec((None, D_MODEL, D_MODEL), lambda i: (layer, 0, 0))
    in_specs = [row(D_MODEL), row(D_MODEL), row(D_MODEL), row(2 * D_MODEL), wsq(), wsq(), wsq(),
                pl.BlockSpec((None, 1, D_MODEL), lambda i: (layer, 0, 0))]
    args = [x, ya, yb, gt, P['w_out_a'], P['w_out_b'], P['w_out'], P['norm_ffn']]
    out_specs = [row(D_MODEL)]
    out_shape = [jax.ShapeDtypeStruct((n, D_MODEL), F32)]
    if moe:
        in_specs.append(pl.BlockSpec((None, D_MODEL, LANES), lambda i: (layer // 2, 0, 0)))
        args.append(P['w_router_pad'])
        out_specs.append(row(LANES))
        out_shape.append(jax.ShapeDtypeStruct((n, LANES), F32))
        out_specs.append(pl.BlockSpec((None, SUBLANES, LANES), lambda i: (i, 0, 0)))
        out_shape.append(jax.ShapeDtypeStruct((n // tm, SUBLANES, LANES), F32))
    else:
        out_specs.append(row(D_MODEL))
        out_shape.append(jax.ShapeDtypeStruct((n, D_MODEL), BF16))
    return pl.pallas_call(
        functools.partial(_merge_kernel, moe=moe),
        grid=(n // tm,),
        in_specs=in_specs,
        out_specs=out_specs,
        out_shape=out_shape,
        compiler_params=_cparams(("arbitrary",)),
        name="merge",
    )(*args)


def _swiglu_acc(h, wg_ref, wu_ref, wd_ref):
    acc = None
    for c0 in range(0, D_FF, FF_CHUNK):
        sl = slice(c0, c0 + FF_CHUNK)
        gate = _dot(h, wg_ref[:, sl])
        up = _dot(h, wu_ref[:, sl])
        act = (gate * _sigmoid(gate) * up).astype(BF16)
        part = _dot(act, wd_ref[sl, :])
        acc = part if acc is None else acc + part
    return acc


def _ffn_kernel(x_ref, h_ref, wg_ref, wu_ref, wd_ref, o_ref):
    o_ref[...] = x_ref[...] + _swiglu_acc(h_ref[...], wg_ref, wu_ref, wd_ref)


def _ffn(x, h, P, j, tm):
    n = x.shape[0]
    row = lambda: pl.BlockSpec((tm, D_MODEL), lambda i: (i, 0))
    return pl.pallas_call(
        _ffn_kernel,
        grid=(n // tm,),
        in_specs=[row(), row(),
                  pl.BlockSpec((None, D_MODEL, D_FF), lambda i: (j, 0, 0), pipeline_mode=pl.Buffered(1)),
                  pl.BlockSpec((None, D_MODEL, D_FF), lambda i: (j, 0, 0), pipeline_mode=pl.Buffered(1)),
                  pl.BlockSpec((None, D_FF, D_MODEL), lambda i: (j, 0, 0), pipeline_mode=pl.Buffered(1))],
        out_specs=row(),
        out_shape=jax.ShapeDtypeStruct((n, D_MODEL), F32),
        compiler_params=_cparams(("arbitrary",)),
        name="ffn",
    )(x, h, P['w_ffn_gate'], P['w_ffn_up'], P['w_ffn_down'])


def _route_kernel(gates_ref, base_ref, pos_ref, *, tw, spare):
    mask = (gates_ref[...] > 0.0).astype(BF16)
    r_idx = lax.broadcasted_iota(jnp.int32, (tw, tw), 0)
    c_idx = lax.broadcasted_iota(jnp.int32, (tw, tw), 1)
    excl = _dot_ta(mask, (r_idx < c_idx).astype(BF16))
    sel = _dot_ta(mask, (r_idx == c_idx).astype(BF16)) > 0.0
    pos = excl + base_ref[:, 0:1]
    e_io = lax.broadcasted_iota(jnp.int32, (LANES, tw), 0)
    e_first = jnp.min(jnp.where(sel, e_io, LANES), axis=0, keepdims=True)
    e_last = jnp.max(jnp.where(sel, e_io, -1), axis=0, keepdims=True)
    p_first = jnp.sum(jnp.where(e_io == e_first, pos, 0.0), axis=0, keepdims=True)
    p_last = jnp.sum(jnp.where(e_io == e_last, pos, 0.0), axis=0, keepdims=True)
    p_last = jnp.where(e_last == e_first, float(spare), p_last)
    rows = jnp.concatenate([p_first, p_last, jnp.zeros((SUBLANES - 2, tw), F32)], axis=0)
    pos_ref[...] = rows.astype(jnp.int32)


def _route(gates, base_b, tw, spare):
    n = gates.shape[0]
    return pl.pallas_call(
        functools.partial(_route_kernel, tw=tw, spare=spare),
        grid=(n // tw,),
        in_specs=[pl.BlockSpec((tw, LANES), lambda i: (i, 0)),
                  pl.BlockSpec((None, LANES, LANES), lambda i: (i, 0, 0))],
        out_specs=pl.BlockSpec((None, SUBLANES, tw), lambda i: (i, 0, 0)),
        out_shape=jax.ShapeDtypeStruct((n // tw, SUBLANES, tw), jnp.int32),
        compiler_params=_cparams(("arbitrary",)),
        name="route",
    )(gates, base_b)


def _row_copy(src_hbm, src_row, dst_ref, dst_row, sem):
    return pltpu.make_async_copy(src_hbm.at[pl.ds(src_row, 1), :], dst_ref.at[pl.ds(dst_row, 1), :], sem)


def _disperse_kernel(pos_ref, x_ref, init_hbm, xs_hbm, sem, *, tw):
    del init_hbm

    def issue(n, carry):
        _row_copy(x_ref, n, xs_hbm, pos_ref[0, n], sem).start()
        _row_copy(x_ref, n, xs_hbm, pos_ref[1, n], sem).start()
        return carry

    def drain(n, carry):
        _row_copy(x_ref, 0, xs_hbm, 0, sem).wait()
        _row_copy(x_ref, 0, xs_hbm, 0, sem).wait()
        return carry

    lax.fori_loop(0, tw, issue, 0, unroll=8)
    lax.fori_loop(0, tw, drain, 0, unroll=8)


def _disperse(pos, x, n_rows, tw):
    n = x.shape[0]
    return pl.pallas_call(
        functools.partial(_disperse_kernel, tw=tw),
        grid=(n // tw,),
        in_specs=[pl.BlockSpec((None, SUBLANES, tw), lambda i: (i, 0, 0), memory_space=pltpu.SMEM),
                  pl.BlockSpec((tw, D_MODEL), lambda i: (i, 0)),
                  pl.BlockSpec(memory_space=pl.ANY)],
        out_specs=pl.BlockSpec(memory_space=pl.ANY),
        out_shape=jax.ShapeDtypeStruct((n_rows, D_MODEL), F32),
        scratch_shapes=[pltpu.SemaphoreType.DMA],
        input_output_aliases={2: 0},
        compiler_params=_cparams(("arbitrary",)),
        name="disperse",
    )(pos, x, jnp.zeros((n_rows, D_MODEL), F32))


def _experts_kernel(te_ref, used_ref, xs_ref, nf_ref, wg_ref, wu_ref, wd_ref, ys_ref):
    del te_ref
    j = pl.program_id(0)

    @pl.when(j < used_ref[0])
    def _():
        h = _rms(xs_ref[...], nf_ref[...]).astype(BF16)
        ys_ref[...] = _swiglu_acc(h, wg_ref, wu_ref, wd_ref)

    @pl.when(j >= used_ref[0])
    def _():
        ys_ref[...] = jnp.zeros(ys_ref.shape, F32)


def _experts(tile_expert, used, xs, P, layer, j):
    n_tiles = tile_expert.shape[0]
    tr = xs.shape[0] // n_tiles
    wspec = lambda a, b: pl.BlockSpec((None, None, a, b), lambda t, te, u: (j, te[t], 0, 0))
    grid_spec = pltpu.PrefetchScalarGridSpec(
        num_scalar_prefetch=2,
        grid=(n_tiles,),
        in_specs=[pl.BlockSpec((tr, D_MODEL), lambda t, te, u: (t, 0)),
                  pl.BlockSpec((None, 1, D_MODEL), lambda t, te, u: (layer, 0, 0)),
                  wspec(D_MODEL, D_FF), wspec(D_MODEL, D_FF), wspec(D_FF, D_MODEL)],
        out_specs=pl.BlockSpec((tr, D_MODEL), lambda t, te, u: (t, 0)),
    )
    return pl.pallas_call(
        _experts_kernel,
        grid_spec=grid_spec,
        out_shape=jax.ShapeDtypeStruct((n_tiles * tr, D_MODEL), F32),
        compiler_params=_cparams(("arbitrary",)),
        name="experts",
    )(tile_expert, used, xs, P['norm_ffn'], P['w_moe_gate'], P['w_moe_up'], P['w_moe_down'])


def _combine_kernel(pos_ref, x_ref, gates_ref, ys_hbm, *rest, tw, final):
    nf_ref = rest[0] if final else None
    o_ref, y1_s, y2_s, sem = rest[1:] if final else rest

    def issue(n, carry):
        _row_copy(ys_hbm, pos_ref[0, n], y1_s, n, sem).start()
        _row_copy(ys_hbm, pos_ref[1, n], y2_s, n, sem).start()
        return carry

    def drain(n, carry):
        _row_copy(ys_hbm, 0, y1_s, 0, sem).wait()
        _row_copy(ys_hbm, 0, y2_s, 0, sem).wait()
        return carry

    lax.fori_loop(0, tw, issue, 0, unroll=8)
    gates = gates_ref[...]
    lane = lax.broadcasted_iota(jnp.int32, gates.shape, 1)
    sel = gates > 0.0
    e_first = jnp.min(jnp.where(sel, lane, LANES), axis=-1, keepdims=True)
    e_last = jnp.max(jnp.where(sel, lane, -1), axis=-1, keepdims=True)
    g_first = jnp.sum(jnp.where(lane == e_first, gates, 0.0), axis=-1, keepdims=True)
    g_last = jnp.sum(jnp.where((lane == e_last) & (e_last != e_first), gates, 0.0), axis=-1, keepdims=True)
    lax.fori_loop(0, tw, drain, 0, unroll=8)
    x = x_ref[...] + g_first * y1_s[...] + g_last * y2_s[...]
    o_ref[...] = _rms(x, nf_ref[...]) if final else x


def _combine(pos, x, gates, ys, tw, norm_final=None):
    n = x.shape[0]
    final = norm_final is not None
    row = pl.BlockSpec((tw, D_MODEL), lambda i: (i, 0))
    in_specs = [pl.BlockSpec((None, SUBLANES, tw), lambda i: (i, 0, 0), memory_space=pltpu.SMEM),
                row, pl.BlockSpec((tw, LANES), lambda i: (i, 0)), pl.BlockSpec(memory_space=pl.ANY)]
    args = [pos, x, gates, ys]
    if final:
        in_specs.append(pl.BlockSpec((1, D_MODEL), lambda i: (0, 0)))
        args.append(norm_final)
    return pl.pallas_call(
        functools.partial(_combine_kernel, tw=tw, final=final),
        grid=(n // tw,),
        in_specs=in_specs,
        out_specs=row,
        out_shape=jax.ShapeDtypeStruct((n, D_MODEL), F32),
        scratch_shapes=[pltpu.VMEM((tw, D_MODEL), F32), pltpu.VMEM((tw, D_MODEL), F32), pltpu.SemaphoreType.DMA],
        compiler_params=_cparams(("arbitrary",)),
        name="combine",
    )(*args)


def _moe_routed(x, gates, cnt, P, layer, tm):
    n = x.shape[0]
    tw = min(ROUTE_TILE, n)
    tr = ROUTE_ROWS if TOP_K * n >= 4 * N_EXPERTS * ROUTE_ROWS else ROUTE_ROWS_SMALL
    n_tiles = (TOP_K * n) // tr + N_EXPERTS
    spare = n_tiles * tr
    c = cnt[:, 0, :].astype(jnp.int32).reshape(n // tw, tw // tm, LANES).sum(axis=1)
    seg_tiles = (c.sum(axis=0) + tr - 1) // tr
    seg_end = jnp.cumsum(seg_tiles)
    base = (seg_end - seg_tiles) * tr + jnp.cumsum(c, axis=0) - c
    base_b = jnp.broadcast_to(base.astype(F32)[:, :, None], (n // tw, LANES, LANES))
    tile_expert = jnp.minimum((jnp.arange(n_tiles + 1)[:, None] >= seg_end[None, :N_EXPERTS]).sum(axis=1),
                              N_EXPERTS - 1).astype(jnp.int32)
    used = seg_end[N_EXPERTS - 1:N_EXPERTS].astype(jnp.int32)
    pos = _route(gates, base_b, tw, spare)
    xs = _disperse(pos, x, spare + tr, tw)
    ys = _experts(tile_expert, used, xs, P, layer, layer // 2)
    return _combine(pos, x, gates, ys, tw, P['norm_final'] if layer == DEPTH - 1 else None)


def _block_diag_weights(w):
    d = w.shape[0]
    per = MXU_DIM // LRU_BLOCK
    w = w.reshape(d, LRU_HEADS // per, per, LRU_BLOCK, LRU_BLOCK)
    eye = jnp.eye(per, dtype=w.dtype)
    bd = jnp.einsum('dcpij,pq->dcpiqj', w, eye)
    return bd.reshape(d, LRU_HEADS // per, MXU_DIM, MXU_DIM).astype(BF16)


def _prep_params(p):
    P = dict(p)
    for name in ('w_in', 'w_out_a', 'w_out_b', 'w_out', 'w_ffn_gate', 'w_ffn_up', 'w_ffn_down',
                 'w_moe_gate', 'w_moe_up', 'w_moe_down', 'g_lora_up'):
        P[name] = p[name].astype(BF16)
    for name in ('norm_mix', 'conv_b', 'gate_a_b', 'gate_x_b', 'lru_lambda', 'shift_mu', 'w0', 'a0', 'k_k', 'k_a',
                 'ln_x_w', 'ln_x_b', 'norm_ffn'):
        P[name] = p[name][:, None, :]
    P['r_k'] = p['r_k'].reshape(DEPTH, 1, RWKV_WIDTH)
    P['norm_final'] = p['norm_final'][None, :]
    P['gaw_bd'] = _block_diag_weights(p['gate_a_w'])
    P['gxw_bd'] = _block_diag_weights(p['gate_x_w'])
    z = jnp.zeros((DEPTH, LORA_W, RWKV_WIDTH), F32)
    P['lora_wa'] = jnp.concatenate([jnp.concatenate([p['w_lora_up'], z], axis=2),
                                    jnp.concatenate([z, p['a_lora_up']], axis=2)], axis=1).astype(BF16)
    P['w_router_pad'] = jnp.pad(p['w_router'], ((0, 0), (0, 0), (0, LANES - N_EXPERTS)))
    return P


def _trunk(x3, st_rwkv, st_lru, st_conv, st_shift, P, *, zero_state, tm, lru_bb, lru_tt, rwkv_nb):
    B, T, _ = x3.shape
    n = B * T
    x = x3.reshape(n, D_MODEL)
    new_rwkv, n_lru, n_conv, n_shift = jnp.zeros(st_rwkv.shape, F32), [], [], []
    for l in range(DEPTH):
        xy, ps, gt = _proj(x, P['norm_mix'], P['w_in'], l, tm)
        ya, h_new, c_new = _lru(xy.reshape(B, T, 2 * LRU_WIDTH), st_conv[l], st_lru[l][:, None, :], P, l,
                                lru_bb, lru_tt)
        yb, sh_new, new_rwkv = _wkv(ps.reshape(B, T, SHIFT_WIDTH), st_shift[l], st_rwkv, new_rwkv, P, l, rwkv_nb,
                                    zero_state)
        moe = l % 2 == 1
        res = _merge(x, ya.reshape(n, D_MODEL), yb.reshape(n, D_MODEL), gt, P, l, tm, moe)
        if moe:
            x, gates, cnt = res
            x = _moe_routed(x, gates, cnt, P, l, tm)
        else:
            x, h = res
            x = _ffn(x, h, P, l // 2, tm)
        n_lru.append(h_new[:, 0, :])
        n_conv.append(c_new)
        n_shift.append(sh_new)
    assert DEPTH % 2 == 0
    return x.reshape(B, T, D_MODEL), new_rwkv, jnp.stack(n_lru), jnp.stack(n_conv), jnp.stack(n_shift)


def kernel(x_prompt, x_sample, state_rwkv, state_lru, state_conv, state_shift, norm_mix, w_in, conv_w, conv_b, gate_a_w, gate_a_b, gate_x_w, gate_x_b, lru_lambda, shift_mu, w0, w_lora_up, a0, a_lora_up, g_lora_up, k_k, k_a, r_k, ln_x_w, ln_x_b, w_out_a, w_out_b, w_out, norm_ffn, w_ffn_gate, w_ffn_up, w_ffn_down, w_router, w_moe_gate, w_moe_up, w_moe_down, norm_final):
    P = _prep_params(dict(
        norm_mix=norm_mix, w_in=w_in, conv_w=conv_w, conv_b=conv_b, gate_a_w=gate_a_w, gate_a_b=gate_a_b,
        gate_x_w=gate_x_w, gate_x_b=gate_x_b, lru_lambda=lru_lambda, shift_mu=shift_mu, w0=w0,
        w_lora_up=w_lora_up, a0=a0, a_lora_up=a_lora_up, g_lora_up=g_lora_up, k_k=k_k, k_a=k_a, r_k=r_k,
        ln_x_w=ln_x_w, ln_x_b=ln_x_b, w_out_a=w_out_a, w_out_b=w_out_b, w_out=w_out, norm_ffn=norm_ffn,
        w_ffn_gate=w_ffn_gate, w_ffn_up=w_ffn_up, w_ffn_down=w_ffn_down, w_router=w_router,
        w_moe_gate=w_moe_gate, w_moe_up=w_moe_up, w_moe_down=w_moe_down, norm_final=norm_final))
    bp = x_prompt.shape[0]
    bs = x_sample.shape[0]
    zeros = lambda *shape: jnp.zeros(shape, F32)
    p_out = _trunk(x_prompt,
                   zeros(DEPTH, bp, RWKV_HEADS, RWKV_HEAD, RWKV_HEAD), zeros(DEPTH, bp, LRU_WIDTH),
                   zeros(DEPTH, bp, CONV_WIDTH - 1, LRU_WIDTH), zeros(DEPTH, bp, SHIFT_WIDTH), P,
                   zero_state=True, tm=512, lru_bb=1, lru_tt=512, rwkv_nb=4)
    s_out = _trunk(x_sample, state_rwkv, state_lru, state_conv, state_shift, P,
                   zero_state=False, tm=256, lru_bb=8, lru_tt=x_sample.shape[1], rwkv_nb=1)
    return (p_out[0], s_out[0]) + tuple(p_out[1:]) + tuple(s_out[1:])
```

```python
import functools

import numpy as np
import jax
import jax.numpy as jnp
from jax import lax
from jax.experimental import pallas as pl
from jax.experimental.pallas import tpu as pltpu

F32 = jnp.float32
BF16 = jnp.bfloat16

D_MODEL = 1024
DEPTH = 4
LRU_WIDTH = D_MODEL
LRU_HEADS = 16
LRU_BLOCK = LRU_WIDTH // LRU_HEADS
CONV_WIDTH = 4
LRU_C = 8.0
RWKV_HEAD = 64
RWKV_WIDTH = D_MODEL
RWKV_HEADS = RWKV_WIDTH // RWKV_HEAD
LORA_W = 64
LORA_A = 64
LORA_G = 128
SHIFT_WIDTH = 3 * RWKV_WIDTH + LORA_W + LORA_A + LORA_G
PROJ_WIDTH = 2 * LRU_WIDTH + SHIFT_WIDTH + 2 * D_MODEL
D_FF = 2816
N_EXPERTS = 8
NORM_EPS = 1e-6
GN_EPS = 64e-5

LANES = 128
SUBLANES = 8
MXU_DIM = 256
VMEM_LIMIT = 56 * 1024 * 1024

CHUNK = 64
GROUP_HEADS = MXU_DIM // CHUNK
GROUP_LANES = GROUP_HEADS * RWKV_HEAD
N_GROUPS = RWKV_HEADS // GROUP_HEADS
FF_CHUNK = 256
TOP_K = 2
ROUTE_TILE = 512
ROUTE_ROWS = 512
ROUTE_ROWS_SMALL = 128
SCAN_ROWS = 16
assert CHUNK == RWKV_HEAD


def _cparams(sem):
    return pltpu.CompilerParams(dimension_semantics=sem, vmem_limit_bytes=VMEM_LIMIT)


def _dot(a, b):
    return jnp.dot(a, b, preferred_element_type=F32)


def _dot_tb(a, b):
    return lax.dot_general(a, b, (((1,), (1,)), ((), ())), preferred_element_type=F32)


def _dot_ta(a, b):
    return lax.dot_general(a, b, (((0,), (0,)), ((), ())), preferred_element_type=F32)


def _sigmoid(x):
    return jax.nn.sigmoid(x)


def _softplus(z):
    return jnp.maximum(z, 0.0) + jnp.log(1.0 + jnp.exp(-jnp.abs(z)))


def _rms(x, g):
    return x * lax.rsqrt(jnp.mean(x * x, axis=-1, keepdims=True) + NORM_EPS) * g


_PROJ_SPLITS = ((0, 2 * LRU_WIDTH), (2 * LRU_WIDTH, SHIFT_WIDTH), (2 * LRU_WIDTH + SHIFT_WIDTH, 2 * D_MODEL))


def _proj_kernel(x_ref, g_ref, w_ref, xy_ref, ps_ref, gt_ref):
    u = _rms(x_ref[...], g_ref[...]).astype(BF16)
    for out_ref, (c0, width) in zip((xy_ref, ps_ref, gt_ref), _PROJ_SPLITS):
        j = 0
        while j < width:
            w = min(512, width - j)
            out_ref[:, j:j + w] = _dot(u, w_ref[:, c0 + j:c0 + j + w])
            j += w


def _proj(x, g_all, w_all, layer, tm):
    n = x.shape[0]
    return pl.pallas_call(
        _proj_kernel,
        grid=(n // tm,),
        in_specs=[
            pl.BlockSpec((tm, D_MODEL), lambda i: (i, 0)),
            pl.BlockSpec((None, 1, D_MODEL), lambda i: (layer, 0, 0)),
            pl.BlockSpec((None, D_MODEL, PROJ_WIDTH), lambda i: (layer, 0, 0),
                         pipeline_mode=pl.Buffered(1)),
        ],
        out_specs=[
            pl.BlockSpec((tm, 2 * LRU_WIDTH), lambda i: (i, 0)),
            pl.BlockSpec((tm, SHIFT_WIDTH), lambda i: (i, 0)),
            pl.BlockSpec((tm, 2 * D_MODEL), lambda i: (i, 0)),
        ],
        out_shape=[
            jax.ShapeDtypeStruct((n, 2 * LRU_WIDTH), F32),
            jax.ShapeDtypeStruct((n, SHIFT_WIDTH), F32),
            jax.ShapeDtypeStruct((n, 2 * D_MODEL), F32),
        ],
        compiler_params=_cparams(("arbitrary",)),
        name="proj",
    )(x, g_all, w_all)


def _gelu_tanh(x):
    return 0.5 * x * (1.0 + jnp.tanh(np.sqrt(2.0 / np.pi).astype(np.float32) * (x + 0.044715 * (x * x * x))))


def _lru_kernel(xa_ref, ya_ref, cs_ref, h0_ref, cw_ref, cb_ref, gaw_ref, gab_ref, gxw_ref, gxb_ref, lam_ref,
                y_ref, hl_ref, nc_ref, xp_s, h_s, a_s, b_s, *, bb, tt):
    ti = pl.program_id(1)
    neg_c = -LRU_C * _softplus(-lam_ref[...])
    cw = cw_ref[...]
    @pl.when(ti == 0)
    def _():
        for b in range(bb):
            xp_s[b, 5:8, :] = cs_ref[b]
            h_s[b] = h0_ref[b]

    for b in range(bb):
        xa = xa_ref[b]
        xp_s[b, 8:8 + tt, :] = xa
        xc = cb_ref[...] + xp_s[b, 5:5 + tt, :] * cw[0:1]
        xc = xc + xp_s[b, 6:6 + tt, :] * cw[1:2]
        xc = xc + xp_s[b, 7:7 + tt, :] * cw[2:3]
        xc = xc + xa * cw[3:4]
        tail = xp_s[b, 5 + tt:8 + tt, :]
        xp_s[b, 5:8, :] = tail
        nc_ref[b] = tail

        for c in range(LRU_WIDTH // MXU_DIM):
            sl = slice(c * MXU_DIM, (c + 1) * MXU_DIM)
            xcc = xc[:, sl]
            xcb = xcc.astype(BF16)
            r = _sigmoid(_dot(xcb, gaw_ref[c]) + gab_ref[:, sl])
            i = _sigmoid(_dot(xcb, gxw_ref[c]) + gxb_ref[:, sl])
            log_a = neg_c[:, sl] * r
            a = jnp.exp(log_a)
            a_s[:, sl] = a
            b_s[:, sl] = jnp.sqrt(-jnp.tanh(log_a) * (a * a + 1.0)) * (i * xcc)

        if tt % SCAN_ROWS == 0:
            row = lax.broadcasted_iota(jnp.int32, (SUBLANES, LRU_WIDTH), 0)

            def blk(j, h):
                r0 = pl.multiple_of(j * SCAN_ROWS, SCAN_ROWS)
                hbs = []
                for q in range(SCAN_ROWS // SUBLANES):
                    av = a_s[pl.ds(r0 + q * SUBLANES, SUBLANES), :]
                    bv = b_s[pl.ds(r0 + q * SUBLANES, SUBLANES), :]
                    for d in (1, 2, 4):
                        m = row >= d
                        a_sh = pltpu.roll(av, d, axis=0)
                        b_sh = pltpu.roll(bv, d, axis=0)
                        bv = jnp.where(m, av * b_sh + bv, bv)
                        av = jnp.where(m, av * a_sh, av)
                    hb = av * h + bv
                    h = hb[SUBLANES - 1:SUBLANES, :]
                    hbs.append(hb)
                hs = jnp.concatenate(hbs, axis=0)
                y_ref[b, pl.ds(r0, SCAN_ROWS), :] = (hs * _gelu_tanh(ya_ref[b, pl.ds(r0, SCAN_ROWS), :])).astype(BF16)
                return h

            h = lax.fori_loop(0, tt // SCAN_ROWS, blk, h_s[b])
        else:
            h = h_s[b]
            rows = []
            for t in range(tt):
                h = a_s[t:t + 1, :] * h + b_s[t:t + 1, :]
                rows.append(h)
            hs = jnp.concatenate(rows, axis=0)
            y_ref[b] = (hs * _gelu_tanh(ya_ref[b])).astype(BF16)
        h_s[b] = h
        hl_ref[b] = h


def _lru(xy3, st_conv, st_lru, P, layer, bb, tt):
    B, T, _ = xy3.shape
    W = LRU_WIDTH
    vec = lambda: pl.BlockSpec((None, 1, W), lambda b, t: (layer, 0, 0))
    kern = functools.partial(_lru_kernel, bb=bb, tt=tt)
    return pl.pallas_call(
        kern,
        grid=(B // bb, T // tt),
        in_specs=[
            pl.BlockSpec((bb, tt, W), lambda b, t: (b, t, 0)),
            pl.BlockSpec((bb, tt, W), lambda b, t: (b, t, 1)),
            pl.BlockSpec((bb, CONV_WIDTH - 1, W), lambda b, t: (b, 0, 0)),
            pl.BlockSpec((bb, 1, W), lambda b, t: (b, 0, 0)),
            pl.BlockSpec((None, CONV_WIDTH, W), lambda b, t: (layer, 0, 0)),
            vec(),
            pl.BlockSpec((None, W // MXU_DIM, MXU_DIM, MXU_DIM), lambda b, t: (layer, 0, 0, 0)),
            vec(),
            pl.BlockSpec((None, W // MXU_DIM, MXU_DIM, MXU_DIM), lambda b, t: (layer, 0, 0, 0)),
            vec(),
            vec(),
        ],
        out_specs=[
            pl.BlockSpec((bb, tt, W), lambda b, t: (b, t, 0)),
            pl.BlockSpec((bb, 1, W), lambda b, t: (b, 0, 0)),
            pl.BlockSpec((bb, CONV_WIDTH - 1, W), lambda b, t: (b, 0, 0)),
        ],
        out_shape=[
            jax.ShapeDtypeStruct((B, T, W), BF16),
            jax.ShapeDtypeStruct((B, 1, W), F32),
            jax.ShapeDtypeStruct((B, CONV_WIDTH - 1, W), F32),
        ],
        scratch_shapes=[
            pltpu.VMEM((bb, tt + 8, W), F32),
            pltpu.VMEM((bb, 1, W), F32),
            pltpu.VMEM((tt, W), F32),
            pltpu.VMEM((tt, W), F32),
        ],
        compiler_params=_cparams(("arbitrary", "arbitrary")),
        name="lru",
    )(xy3, xy3, st_conv, st_lru, P['conv_w'], P['conv_b'], P['gaw_bd'], P['gate_a_b'], P['gxw_bd'],
      P['gate_x_b'], P['lru_lambda'])


def _block_diag(x_bf16, mask_ref):
    return jnp.concatenate([x_bf16] * GROUP_HEADS, axis=0) * mask_ref[...]


def _rows(x, b, n):
    return x[b * n:(b + 1) * n]


def _dot_split3(sel_bf16, x):
    hi = x.astype(BF16)
    r1 = x - hi.astype(F32)
    mid = r1.astype(BF16)
    lo = (r1 - mid.astype(F32)).astype(BF16)
    return _dot(sel_bf16, hi) + _dot(sel_bf16, mid) + _dot(sel_bf16, lo)


def _wkv_kernel(ps_ref, sh_ref, s0_ref, mu_ref, w0_ref, lora_ref, gup_ref, a0_ref, kk_ref, ka_ref, rk_ref,
                lnw_ref, lnb_ref, tri_ref, endm_ref, ones_ref, mrow_ref, msq_ref, new_all_ref,
                y_ref, nsh_ref, ns_ref, car_s, st_s, *, nb, seq, zero_state):
    del new_all_ref
    C = CHUNK
    Q = C // seq
    R = nb * C
    o = RWKV_WIDTH
    ci = pl.program_id(1)
    last = ci == pl.num_programs(1) - 1
    t_idx = lax.broadcasted_iota(jnp.int32, (C, MXU_DIM), 0)
    i_idx = lax.broadcasted_iota(jnp.int32, (C, MXU_DIM), 1) % C
    same = (t_idx // seq) == (i_idx // seq)
    m_strict = same & (i_idx < t_idx)
    m_incl = same & (i_idx <= t_idx)
    eye_cat = (i_idx == t_idx).astype(F32)
    ones_bd = ones_ref[...]
    msq_f = msq_ref[...].astype(F32)
    n_lane_tiles = o // MXU_DIM

    def head_sum(x):
        xs = jnp.concatenate([x[:, g * MXU_DIM:(g + 1) * MXU_DIM] for g in range(n_lane_tiles)], axis=0)
        s = _dot(xs.astype(BF16), ones_bd)
        return jnp.concatenate([_rows(s, g, R) for g in range(n_lane_tiles)], axis=1)

    @pl.when(ci == 0)
    def _():
        for b in range(nb):
            car_s[b] = sh_ref[b]
            for q in range(Q):
                if zero_state:
                    st_s[b, q] = jnp.zeros(st_s.shape[2:], F32)
                else:
                    for g in range(N_GROUPS):
                        heads = [s0_ref[b, q * RWKV_HEADS + g * GROUP_HEADS + hh] for hh in range(GROUP_HEADS)]
                        stacked = jnp.concatenate(heads, axis=0)
                        st_s[b, q, g] = jnp.concatenate([stacked] * GROUP_HEADS, axis=1) * msq_f

    row_c = lax.broadcasted_iota(jnp.int32, (C, 1), 0)
    prevs = []
    for b in range(nb):
        ps_b = ps_ref[b]
        if Q == 1:
            first = car_s[b]
            new_carry = ps_b[C - 1:C, :]
        else:
            put = (lax.broadcasted_iota(jnp.int32, (C, Q), 0)
                   == seq * lax.broadcasted_iota(jnp.int32, (C, Q), 1)).astype(BF16)
            take = (lax.broadcasted_iota(jnp.int32, (Q, C), 1)
                    == seq * lax.broadcasted_iota(jnp.int32, (Q, C), 0) + (seq - 1)).astype(BF16)
            first = _dot_split3(put, car_s[b])
            new_carry = _dot_split3(take, ps_b)
        prevs.append(jnp.where(row_c % seq == 0, first, pltpu.roll(ps_b, 1, axis=0)))
        car_s[b] = new_carry
        nsh_ref[b] = new_carry
    ps = jnp.concatenate([ps_ref[b] for b in range(nb)], axis=0)
    prev = jnp.concatenate(prevs, axis=0)

    s = ps + (prev - ps) * mu_ref[...]
    r = s[:, :o]
    k = s[:, o:2 * o]
    v = s[:, 2 * o:3 * o]
    dwa = s[:, 3 * o:3 * o + LORA_W + LORA_A]
    dg = s[:, 3 * o + LORA_W + LORA_A:]
    lane = lax.broadcasted_iota(jnp.int32, dwa.shape, 1)
    lora_in = jnp.where(lane < LORA_W, jnp.tanh(dwa), dwa).astype(BF16)
    lora = _dot(lora_in, lora_ref[...])
    w_log = -_softplus(-(w0_ref[...] + lora[:, :o])) - 0.5
    lw = -jnp.exp(w_log)
    a = _sigmoid(a0_ref[...] + lora[:, o:])
    g = _dot(_sigmoid(dg).astype(BF16), gup_ref[...])
    kk = k * kk_ref[...]
    kk = kk * lax.rsqrt(jnp.maximum(head_sum(kk * kk), 1e-24))
    kmod = k * (1.0 + (a - 1.0) * ka_ref[...])
    beta = kk * a
    lw_hi = lw.astype(BF16)
    lw_lo = (lw - lw_hi.astype(F32)).astype(BF16)
    L = _dot(tri_ref[...], lw_hi) + _dot(tri_ref[...], lw_lo)
    l_end = _dot(endm_ref[...], lw_hi) + _dot(endm_ref[...], lw_lo)
    e_neg = jnp.exp(-L)
    p_end = jnp.exp(l_end)
    e_end = p_end * e_neg
    at = (-kk) * jnp.exp(L - lw)
    rt = r * jnp.exp(L)
    bt = (beta * e_neg).astype(BF16)
    kt = (kmod * e_neg).astype(BF16)
    bend = beta * e_end
    kend = kmod * e_end

    chains = []
    for b in range(nb):
        for gi in range(N_GROUPS):
            sl = slice(gi * GROUP_LANES, (gi + 1) * GROUP_LANES)
            chains.append(dict(
                b=b, gi=gi, vg=_rows(v, b, C)[:, sl].astype(BF16), p_end=_rows(p_end, b, C)[:, sl],
                x2=jnp.concatenate([_rows(at, b, C)[:, sl], _rows(rt, b, C)[:, sl]], axis=0).astype(BF16),
                wbd=jnp.concatenate([_block_diag(_rows(bt, b, C)[:, sl], mrow_ref),
                                     _block_diag(_rows(kt, b, C)[:, sl], mrow_ref)], axis=0),
                bk=jnp.concatenate([_rows(bend, b, C)[:, sl], _rows(kend, b, C)[:, sl]], axis=0).astype(BF16)))

    cw = GROUP_HEADS * C
    rowseq = (lax.broadcasted_iota(jnp.int32, (2 * C, 1), 0) % C) // seq
    for c in chains:
        res = _dot_tb(c['x2'], c['wbd'])
        n_cat = jnp.where(m_strict, res[:C, :cw], 0.0)
        c['a_v'] = jnp.concatenate([jnp.where(m_strict, res[:C, cw:], 0.0),
                                    jnp.where(m_incl, res[C:, cw:], 0.0)], axis=0).astype(BF16)
        c['a_rb'] = jnp.where(m_incl, res[C:, :cw], 0.0).astype(BF16)
        c['x_c'] = n_cat.astype(BF16)
        c['p_c'] = eye_cat + n_cat
        c['x_bd'] = _block_diag(c['x_c'], msq_ref)
    for c in chains:
        xs = None
        for q in range(Q):
            xq = _dot_tb(c['x2'], st_s[c['b'], q, c['gi']].astype(BF16))
            xs = xq if xs is None else jnp.where(rowseq == q, xq, xs)
        c['xs'] = xs
        c['v_bd'] = _block_diag(c['vg'], msq_ref)
    for c in chains:
        sv = c['xs'] + _dot(c['a_v'], c['v_bd'])
        c['rhs'] = sv[:C]
        c['y_sv'] = sv[C:]
    lvl = 2
    while lvl < seq:
        for c in chains:
            c['x_c'] = _dot(c['x_c'], c['x_bd']).astype(BF16)
            c['x_bd'] = _block_diag(c['x_c'], msq_ref)
        for c in chains:
            c['p_c'] = c['p_c'] + _dot(c['p_c'].astype(BF16), c['x_bd'])
        lvl *= 2
    for c in chains:
        c['u'] = _dot(c['p_c'].astype(BF16), _block_diag(c['rhs'].astype(BF16), msq_ref)).astype(BF16)
    ys = {}
    for c in chains:
        b, gi = c['b'], c['gi']
        ys[(b, gi)] = c['y_sv'] + _dot(c['a_rb'], _block_diag(c['u'], msq_ref))
        uv = jnp.concatenate([c['u'], c['vg']], axis=0)
        if Q > 1:
            uv = jnp.concatenate([jnp.where(rowseq == q, uv, jnp.zeros_like(uv)) for q in range(Q)], axis=1)
        ds = _dot_ta(uv, c['bk'])
        for q in range(Q):
            st_s[b, q, gi] = (st_s[b, q, gi] * c['p_end'][q * seq:q * seq + 1, :]
                              + _rows(ds, q, GROUP_LANES) * msq_f)

    yc = jnp.concatenate([jnp.concatenate([ys[(b, gi)] for gi in range(N_GROUPS)], axis=1) for b in range(nb)],
                         axis=0)
    inv_n = 1.0 / RWKV_HEAD
    mean = head_sum(yc) * inv_n
    dlt = yc - mean
    var = head_sum(dlt * dlt) * inv_n
    yn = dlt * lax.rsqrt(var + GN_EPS) * lnw_ref[...] + lnb_ref[...]
    bonus = head_sum(r * kmod * rk_ref[...]) * v
    out = ((yn + bonus) * g).astype(BF16)
    for b in range(nb):
        y_ref[b] = _rows(out, b, C)

    @pl.when(last)
    def _():
        for b in range(nb):
            for q in range(Q):
                for gi in range(N_GROUPS):
                    for hh in range(GROUP_HEADS):
                        ns_ref[b, q * RWKV_HEADS + gi * GROUP_HEADS + hh] = st_s[
                            b, q, gi, hh * RWKV_HEAD:(hh + 1) * RWKV_HEAD, hh * RWKV_HEAD:(hh + 1) * RWKV_HEAD]


def _wkv_consts(nb, seq):
    C = CHUNK
    gh = GROUP_HEADS
    rows = np.arange(nb * C)
    same = (rows[:, None] // seq) == (rows[None, :] // seq)
    tri = (same & (rows[None, :] <= rows[:, None])).astype(np.float32)
    endm = same.astype(np.float32)
    hl = np.arange(MXU_DIM) // RWKV_HEAD
    ones_bd = (hl[:, None] == hl[None, :]).astype(np.float32)
    rowh = np.arange(gh * C) // C
    colh = np.arange(GROUP_LANES) // RWKV_HEAD
    mrow = (rowh[:, None] == colh[None, :]).astype(np.float32)
    msq = (rowh[:, None] == rowh[None, :]).astype(np.float32)
    return tuple(jnp.asarray(m, BF16) for m in (tri, endm, ones_bd, mrow, msq))


def _wkv(ps3, st_shift, st_rwkv_all, new_rwkv_all, P, layer, nb, zero_state):
    B, T, _ = ps3.shape
    C = CHUNK
    seq = C if T % C == 0 else T
    assert C % seq == 0
    Q = C // seq
    G = B // Q
    tg = T * Q
    assert B % Q == 0 and G % nb == 0 and tg % C == 0
    o = RWKV_WIDTH
    consts = _wkv_consts(nb, seq)
    vec = lambda w: pl.BlockSpec((None, 1, w), lambda b, c: (layer, 0, 0))
    const = lambda arr: pl.BlockSpec(arr.shape, lambda b, c: (0,) * arr.ndim)
    kern = functools.partial(_wkv_kernel, nb=nb, seq=seq, zero_state=zero_state)
    shift_spec = pl.BlockSpec((nb, Q, SHIFT_WIDTH), lambda b, c: (b, 0, 0))
    state_shape = (DEPTH, G, Q * RWKV_HEADS, RWKV_HEAD, RWKV_HEAD)
    state_block = (None, nb, Q * RWKV_HEADS, RWKV_HEAD, RWKV_HEAD)
    state_index = lambda b, c: (layer, b, 0, 0, 0)
    in_specs = [
        pl.BlockSpec((nb, C, SHIFT_WIDTH), lambda b, c: (b, c, 0)),
        shift_spec,
        pl.BlockSpec(state_block, state_index, pipeline_mode=pl.Buffered(1)),
        vec(SHIFT_WIDTH),
        vec(o),
        pl.BlockSpec((None, LORA_W + LORA_A, 2 * o), lambda b, c: (layer, 0, 0)),
        pl.BlockSpec((None, LORA_G, o), lambda b, c: (layer, 0, 0)),
        vec(o), vec(o), vec(o), vec(o), vec(o), vec(o),
    ] + [const(m) for m in consts] + [pl.BlockSpec(memory_space=pl.ANY)]
    args = [ps3.reshape(G, tg, SHIFT_WIDTH), st_shift.reshape(G, Q, SHIFT_WIDTH), st_rwkv_all.reshape(state_shape),
            P['shift_mu'], P['w0'], P['lora_wa'], P['g_lora_up'], P['a0'], P['k_k'], P['k_a'],
            P['r_k'], P['ln_x_w'], P['ln_x_b'], *consts, new_rwkv_all.reshape(state_shape)]
    y, nsh, ns = pl.pallas_call(
        kern,
        grid=(G // nb, tg // C),
        in_specs=in_specs,
        out_specs=[
            pl.BlockSpec((nb, C, o), lambda b, c: (b, c, 0)),
            shift_spec,
            pl.BlockSpec(state_block, state_index),
        ],
        out_shape=[
            jax.ShapeDtypeStruct((G, tg, o), BF16),
            jax.ShapeDtypeStruct((G, Q, SHIFT_WIDTH), F32),
            jax.ShapeDtypeStruct(state_shape, F32),
        ],
        scratch_shapes=[
            pltpu.VMEM((nb, Q, SHIFT_WIDTH), F32),
            pltpu.VMEM((nb, Q, N_GROUPS, MXU_DIM, MXU_DIM), F32),
        ],
        input_output_aliases={len(args) - 1: 2},
        compiler_params=_cparams(("arbitrary", "arbitrary")),
        name="rwkv",
    )(*args)
    return (y.reshape(B, T, o), nsh.reshape(B, SHIFT_WIDTH),
            ns.reshape(DEPTH, B, RWKV_HEADS, RWKV_HEAD, RWKV_HEAD))


def _merge_kernel(x_ref, ya_ref, yb_ref, gt_ref, wa_ref, wb_ref, wo_ref, nf_ref, *rest, moe):
    if moe:
        wr_ref, xo_ref, gates_ref, cnt_ref = rest
    else:
        xo_ref, h_ref = rest
    ga = gt_ref[:, :D_MODEL]
    gb = gt_ref[:, D_MODEL:]
    m = _sigmoid(ga) * _dot(ya_ref[...], wa_ref[...]) + _sigmoid(gb) * _dot(yb_ref[...], wb_ref[...])
    x = x_ref[...] + _dot(m.astype(BF16), wo_ref[...])
    xo_ref[...] = x
    h = _rms(x, nf_ref[...])
    h_hi = h.astype(BF16)
    if not moe:
        h_ref[...] = h_hi
    else:
        w = wr_ref[...]
        w_hi = w.astype(BF16)
        w_lo = (w - w_hi.astype(F32)).astype(BF16)
        h_lo = (h - h_hi.astype(F32)).astype(BF16)
        logits = _dot(h_hi, w_hi) + _dot(h_lo, w_hi) + _dot(h_hi, w_lo)
        lane = lax.broadcasted_iota(jnp.int32, logits.shape, 1)
        real = lane < N_EXPERTS
        logits = jnp.where(real, logits, -jnp.inf)
        e = jnp.exp(logits - jnp.max(logits, axis=-1, keepdims=True))
        p = jnp.where(real, e / jnp.sum(e, axis=-1, keepdims=True), -1.0)
        m1 = jnp.max(p, axis=-1, keepdims=True)
        i1 = jnp.min(jnp.where(p == m1, lane, LANES), axis=-1, keepdims=True)
        oh1 = lane == i1
        p2 = jnp.where(oh1, -1.0, p)
        m2 = jnp.max(p2, axis=-1, keepdims=True)
        i2 = jnp.min(jnp.where(p2 == m2, lane, LANES), axis=-1, keepdims=True)
        oh2 = lane == i2
        tot = m1 + m2
        gates = jnp.where(oh1, m1 / tot, 0.0) + jnp.where(oh2, m2 / tot, 0.0)
        gates_ref[...] = gates
        cnt = jnp.sum((gates > 0.0).astype(F32), axis=0, keepdims=True)
        cnt_ref[...] = jnp.broadcast_to(cnt, (SUBLANES, LANES))


def _merge(x, ya, yb, gt, P, layer, tm, moe):
    n = x.shape[0]
    row = lambda w: pl.BlockSpec((tm, w), lambda i: (i, 0))
    wsq = lambda: pl.BlockSpec((None, D_MODEL, D_MODEL), lambda i: (layer, 0, 0))
    in_specs = [row(D_MODEL), row(D_MODEL), row(D_MODEL), row(2 * D_MODEL), wsq(), wsq(), wsq(),
                pl.BlockSpec((None, 1, D_MODEL), lambda i: (layer, 0, 0))]
    args = [x, ya, yb, gt, P['w_out_a'], P['w_out_b'], P['w_out'], P['norm_ffn']]
    out_specs = [row(D_MODEL)]
    out_shape = [jax.ShapeDtypeStruct((n, D_MODEL), F32)]
    if moe:
        in_specs.append(pl.BlockSpec((None, D_MODEL, LANES), lambda i: (layer // 2, 0, 0)))
        args.append(P['w_router_pad'])
        out_specs.append(row(LANES))
        out_shape.append(jax.ShapeDtypeStruct((n, LANES), F32))
        out_specs.append(pl.BlockSpec((None, SUBLANES, LANES), lambda i: (i, 0, 0)))
        out_shape.append(jax.ShapeDtypeStruct((n // tm, SUBLANES, LANES), F32))
    else:
        out_specs.append(row(D_MODEL))
        out_shape.append(jax.ShapeDtypeStruct((n, D_MODEL), BF16))
    return pl.pallas_call(
        functools.partial(_merge_kernel, moe=moe),
        grid=(n // tm,),
        in_specs=in_specs,
        out_specs=out_specs,
        out_shape=out_shape,
        compiler_params=_cparams(("arbitrary",)),
        name="merge",
    )(*args)


def _swiglu_acc(h, wg_ref, wu_ref, wd_ref):
    acc = None
    for c0 in range(0, D_FF, FF_CHUNK):
        sl = slice(c0, c0 + FF_CHUNK)
        gate = _dot(h, wg_ref[:, sl])
        up = _dot(h, wu_ref[:, sl])
        act = (gate * _sigmoid(gate) * up).astype(BF16)
        part = _dot(act, wd_ref[sl, :])
        acc = part if acc is None else acc + part
    return acc


def _ffn_kernel(x_ref, h_ref, wg_ref, wu_ref, wd_ref, o_ref):
    o_ref[...] = x_ref[...] + _swiglu_acc(h_ref[...], wg_ref, wu_ref, wd_ref)


def _ffn(x, h, P, j, tm):
    n = x.shape[0]
    row = lambda: pl.BlockSpec((tm, D_MODEL), lambda i: (i, 0))
    return pl.pallas_call(
        _ffn_kernel,
        grid=(n // tm,),
        in_specs=[row(), row(),
                  pl.BlockSpec((None, D_MODEL, D_FF), lambda i: (j, 0, 0), pipeline_mode=pl.Buffered(1)),
                  pl.BlockSpec((None, D_MODEL, D_FF), lambda i: (j, 0, 0), pipeline_mode=pl.Buffered(1)),
                  pl.BlockSpec((None, D_FF, D_MODEL), lambda i: (j, 0, 0), pipeline_mode=pl.Buffered(1))],
        out_specs=row(),
        out_shape=jax.ShapeDtypeStruct((n, D_MODEL), F32),
        compiler_params=_cparams(("arbitrary",)),
        name="ffn",
    )(x, h, P['w_ffn_gate'], P['w_ffn_up'], P['w_ffn_down'])


def _route_kernel(gates_ref, base_ref, pos_ref, *, tw, spare):
    mask = (gates_ref[...] > 0.0).astype(BF16)
    r_idx = lax.broadcasted_iota(jnp.int32, (tw, tw), 0)
    c_idx = lax.broadcasted_iota(jnp.int32, (tw, tw), 1)
    excl = _dot_ta(mask, (r_idx < c_idx).astype(BF16))
    sel = _dot_ta(mask, (r_idx == c_idx).astype(BF16)) > 0.0
    pos = excl + base_ref[:, 0:1]
    e_io = lax.broadcasted_iota(jnp.int32, (LANES, tw), 0)
    e_first = jnp.min(jnp.where(sel, e_io, LANES), axis=0, keepdims=True)
    e_last = jnp.max(jnp.where(sel, e_io, -1), axis=0, keepdims=True)
    p_first = jnp.sum(jnp.where(e_io == e_first, pos, 0.0), axis=0, keepdims=True)
    p_last = jnp.sum(jnp.where(e_io == e_last, pos, 0.0), axis=0, keepdims=True)
    p_last = jnp.where(e_last == e_first, float(spare), p_last)
    rows = jnp.concatenate([p_first, p_last, jnp.zeros((SUBLANES - 2, tw), F32)], axis=0)
    pos_ref[...] = rows.astype(jnp.int32)


def _route(gates, base_b, tw, spare):
    n = gates.shape[0]
    return pl.pallas_call(
        functools.partial(_route_kernel, tw=tw, spare=spare),
        grid=(n // tw,),
        in_specs=[pl.BlockSpec((tw, LANES), lambda i: (i, 0)),
                  pl.BlockSpec((None, LANES, LANES), lambda i: (i, 0, 0))],
        out_specs=pl.BlockSpec((None, SUBLANES, tw), lambda i: (i, 0, 0)),
        out_shape=jax.ShapeDtypeStruct((n // tw, SUBLANES, tw), jnp.int32),
        compiler_params=_cparams(("arbitrary",)),
        name="route",
    )(gates, base_b)


def _row_copy(src_hbm, src_row, dst_ref, dst_row, sem):
    return pltpu.make_async_copy(src_hbm.at[pl.ds(src_row, 1), :], dst_ref.at[pl.ds(dst_row, 1), :], sem)


def _disperse_kernel(tail_ref, pos_ref, x_ref, xs_hbm, zero_s, sem, zero_sem, *, tw):
    @pl.when(pl.program_id(0) == 0)
    def _():
        tr = zero_s.shape[0]
        zero_s[...] = jnp.zeros(zero_s.shape, F32)

        def fill_tile(t):
            fill = pltpu.make_async_copy(zero_s, xs_hbm.at[pl.ds(pl.multiple_of(t * tr, tr), tr), :], zero_sem)
            fill.start()
            fill.wait()

        for e in range(N_EXPERTS):
            fill_tile(tail_ref[e])
        lax.fori_loop(tail_ref[N_EXPERTS], xs_hbm.shape[0] // tr, lambda t, c: (fill_tile(t), c)[1], 0)

    def issue(n, carry):
        _row_copy(x_ref, n, xs_hbm, pos_ref[0, n], sem).start()
        _row_copy(x_ref, n, xs_hbm, pos_ref[1, n], sem).start()
        return carry

    def drain(n, carry):
        _row_copy(x_ref, 0, xs_hbm, 0, sem).wait()
        _row_copy(x_ref, 0, xs_hbm, 0, sem).wait()
        return carry

    lax.fori_loop(0, tw, issue, 0, unroll=8)
    lax.fori_loop(0, tw, drain, 0, unroll=8)


def _disperse(tail_tiles, pos, x, n_rows, tw, tr):
    n = x.shape[0]
    grid_spec = pltpu.PrefetchScalarGridSpec(
        num_scalar_prefetch=1,
        grid=(n // tw,),
        in_specs=[pl.BlockSpec((None, SUBLANES, tw), lambda i, t: (i, 0, 0), memory_space=pltpu.SMEM),
                  pl.BlockSpec((tw, D_MODEL), lambda i, t: (i, 0))],
        out_specs=pl.BlockSpec(memory_space=pl.ANY),
        scratch_shapes=[pltpu.VMEM((tr, D_MODEL), F32), pltpu.SemaphoreType.DMA, pltpu.SemaphoreType.DMA],
    )
    return pl.pallas_call(
        functools.partial(_disperse_kernel, tw=tw),
        grid_spec=grid_spec,
        out_shape=jax.ShapeDtypeStruct((n_rows, D_MODEL), F32),
        compiler_params=_cparams(("arbitrary",)),
        name="disperse",
    )(tail_tiles, pos, x)


def _experts_kernel(te_ref, used_ref, xs_ref, nf_ref, wg_ref, wu_ref, wd_ref, ys_ref):
    del te_ref
    j = pl.program_id(0)

    @pl.when(j < used_ref[0])
    def _():
        h = _rms(xs_ref[...], nf_ref[...]).astype(BF16)
        ys_ref[...] = _swiglu_acc(h, wg_ref, wu_ref, wd_ref)

    @pl.when(j >= used_ref[0])
    def _():
        ys_ref[...] = jnp.zeros(ys_ref.shape, F32)


def _experts(tile_expert, used, xs, P, layer, j):
    n_tiles = tile_expert.shape[0]
    tr = xs.shape[0] // n_tiles
    wspec = lambda a, b: pl.BlockSpec((None, None, a, b), lambda t, te, u: (j, te[t], 0, 0))
    grid_spec = pltpu.PrefetchScalarGridSpec(
        num_scalar_prefetch=2,
        grid=(n_tiles,),
        in_specs=[pl.BlockSpec((tr, D_MODEL), lambda t, te, u: (jnp.minimum(t, u[0] - 1), 0)),
                  pl.BlockSpec((None, 1, D_MODEL), lambda t, te, u: (layer, 0, 0)),
                  wspec(D_MODEL, D_FF), wspec(D_MODEL, D_FF), wspec(D_FF, D_MODEL)],
        out_specs=pl.BlockSpec((tr, D_MODEL), lambda t, te, u: (t, 0)),
    )
    return pl.pallas_call(
        _experts_kernel,
        grid_spec=grid_spec,
        out_shape=jax.ShapeDtypeStruct((n_tiles * tr, D_MODEL), F32),
        compiler_params=_cparams(("arbitrary",)),
        name="experts",
    )(tile_expert, used, xs, P['norm_ffn'], P['w_moe_gate'], P['w_moe_up'], P['w_moe_down'])


def _combine_kernel(pos_ref, x_ref, gates_ref, ys_hbm, *rest, tw, final):
    nf_ref = rest[0] if final else None
    o_ref, y1_s, y2_s, sem = rest[1:] if final else rest

    def issue(n, carry):
        _row_copy(ys_hbm, pos_ref[0, n], y1_s, n, sem).start()
        _row_copy(ys_hbm, pos_ref[1, n], y2_s, n, sem).start()
        return carry

    def drain(n, carry):
        _row_copy(ys_hbm, 0, y1_s, 0, sem).wait()
        _row_copy(ys_hbm, 0, y2_s, 0, sem).wait()
        return carry

    lax.fori_loop(0, tw, issue, 0, unroll=8)
    gates = gates_ref[...]
    lane = lax.broadcasted_iota(jnp.int32, gates.shape, 1)
    sel = gates > 0.0
    e_first = jnp.min(jnp.where(sel, lane, LANES), axis=-1, keepdims=True)
    e_last = jnp.max(jnp.where(sel, lane, -1), axis=-1, keepdims=True)
    g_first = jnp.sum(jnp.where(lane == e_first, gates, 0.0), axis=-1, keepdims=True)
    g_last = jnp.sum(jnp.where((lane == e_last) & (e_last != e_first), gates, 0.0), axis=-1, keepdims=True)
    lax.fori_loop(0, tw, drain, 0, unroll=8)
    x = x_ref[...] + g_first * y1_s[...] + g_last * y2_s[...]
    o_ref[...] = _rms(x, nf_ref[...]) if final else x


def _combine(pos, x, gates, ys, tw, norm_final=None):
    n = x.shape[0]
    final = norm_final is not None
    row = pl.BlockSpec((tw, D_MODEL), lambda i: (i, 0))
    in_specs = [pl.BlockSpec((None, SUBLANES, tw), lambda i: (i, 0, 0), memory_space=pltpu.SMEM),
                row, pl.BlockSpec((tw, LANES), lambda i: (i, 0)), pl.BlockSpec(memory_space=pl.ANY)]
    args = [pos, x, gates, ys]
    if final:
        in_specs.append(pl.BlockSpec((1, D_MODEL), lambda i: (0, 0)))
        args.append(norm_final)
    return pl.pallas_call(
        functools.partial(_combine_kernel, tw=tw, final=final),
        grid=(n // tw,),
        in_specs=in_specs,
        out_specs=row,
        out_shape=jax.ShapeDtypeStruct((n, D_MODEL), F32),
        scratch_shapes=[pltpu.VMEM((tw, D_MODEL), F32), pltpu.VMEM((tw, D_MODEL), F32), pltpu.SemaphoreType.DMA],
        compiler_params=_cparams(("arbitrary",)),
        name="combine",
    )(*args)


def _moe_routed(x, gates, cnt, P, layer, tm):
    n = x.shape[0]
    tw = min(ROUTE_TILE, n)
    tr = ROUTE_ROWS if TOP_K * n >= 4 * N_EXPERTS * ROUTE_ROWS else ROUTE_ROWS_SMALL
    n_tiles = (TOP_K * n) // tr + N_EXPERTS
    spare = n_tiles * tr
    c = cnt[:, 0, :].astype(jnp.int32).reshape(n // tw, tw // tm, LANES).sum(axis=1)
    seg_tiles = (c.sum(axis=0) + tr - 1) // tr
    seg_end = jnp.cumsum(seg_tiles)
    base = (seg_end - seg_tiles) * tr + jnp.cumsum(c, axis=0) - c
    base_b = jnp.broadcast_to(base.astype(F32)[:, :, None], (n // tw, LANES, LANES))
    tile_expert = jnp.minimum((jnp.arange(n_tiles + 1)[:, None] >= seg_end[None, :N_EXPERTS]).sum(axis=1),
                              N_EXPERTS - 1).astype(jnp.int32)
    used = seg_end[N_EXPERTS - 1:N_EXPERTS].astype(jnp.int32)
    pos = _route(gates, base_b, tw, spare)
    tail_tiles = jnp.concatenate([jnp.where(seg_tiles > 0, seg_end - 1, n_tiles)[:N_EXPERTS],
                                  seg_end[N_EXPERTS - 1:N_EXPERTS]]).astype(jnp.int32)
    xs = _disperse(tail_tiles, pos, x, spare + tr, tw, tr)
    ys = _experts(tile_expert, used, xs, P, layer, layer // 2)
    return _combine(pos, x, gates, ys, tw, P['norm_final'] if layer == DEPTH - 1 else None)


def _block_diag_weights(w):
    d = w.shape[0]
    per = MXU_DIM // LRU_BLOCK
    w = w.reshape(d, LRU_HEADS // per, per, LRU_BLOCK, LRU_BLOCK)
    eye = jnp.eye(per, dtype=w.dtype)
    bd = jnp.einsum('dcpij,pq->dcpiqj', w, eye)
    return bd.reshape(d, LRU_HEADS // per, MXU_DIM, MXU_DIM).astype(BF16)


def _prep_params(p):
    P = dict(p)
    for name in ('w_in', 'w_out_a', 'w_out_b', 'w_out', 'w_ffn_gate', 'w_ffn_up', 'w_ffn_down',
                 'w_moe_gate', 'w_moe_up', 'w_moe_down', 'g_lora_up'):
        P[name] = p[name].astype(BF16)
    for name in ('norm_mix', 'conv_b', 'gate_a_b', 'gate_x_b', 'lru_lambda', 'shift_mu', 'w0', 'a0', 'k_k', 'k_a',
                 'ln_x_w', 'ln_x_b', 'norm_ffn'):
        P[name] = p[name][:, None, :]
    P['r_k'] = p['r_k'].reshape(DEPTH, 1, RWKV_WIDTH)
    P['norm_final'] = p['norm_final'][None, :]
    P['gaw_bd'] = _block_diag_weights(p['gate_a_w'])
    P['gxw_bd'] = _block_diag_weights(p['gate_x_w'])
    z = jnp.zeros((DEPTH, LORA_W, RWKV_WIDTH), F32)
    P['lora_wa'] = jnp.concatenate([jnp.concatenate([p['w_lora_up'], z], axis=2),
                                    jnp.concatenate([z, p['a_lora_up']], axis=2)], axis=1).astype(BF16)
    P['w_router_pad'] = jnp.pad(p['w_router'], ((0, 0), (0, 0), (0, LANES - N_EXPERTS)))
    return P


def _trunk(x3, st_rwkv, st_lru, st_conv, st_shift, P, *, zero_state, tm, lru_bb, lru_tt, rwkv_nb):
    B, T, _ = x3.shape
    n = B * T
    x = x3.reshape(n, D_MODEL)
    new_rwkv, n_lru, n_conv, n_shift = jnp.zeros(st_rwkv.shape, F32), [], [], []
    for l in range(DEPTH):
        xy, ps, gt = _proj(x, P['norm_mix'], P['w_in'], l, tm)
        ya, h_new, c_new = _lru(xy.reshape(B, T, 2 * LRU_WIDTH), st_conv[l], st_lru[l][:, None, :], P, l,
                                lru_bb, lru_tt)
        yb, sh_new, new_rwkv = _wkv(ps.reshape(B, T, SHIFT_WIDTH), st_shift[l], st_rwkv, new_rwkv, P, l, rwkv_nb,
                                    zero_state)
        moe = l % 2 == 1
        res = _merge(x, ya.reshape(n, D_MODEL), yb.reshape(n, D_MODEL), gt, P, l, tm, moe)
        if moe:
            x, gates, cnt = res
            x = _moe_routed(x, gates, cnt, P, l, tm)
        else:
            x, h = res
            x = _ffn(x, h, P, l // 2, tm)
        n_lru.append(h_new[:, 0, :])
        n_conv.append(c_new)
        n_shift.append(sh_new)
    assert DEPTH % 2 == 0
    return x.reshape(B, T, D_MODEL), new_rwkv, jnp.stack(n_lru), jnp.stack(n_conv), jnp.stack(n_shift)


def kernel(x_prompt, x_sample, state_rwkv, state_lru, state_conv, state_shift, norm_mix, w_in, conv_w, conv_b, gate_a_w, gate_a_b, gate_x_w, gate_x_b, lru_lambda, shift_mu, w0, w_lora_up, a0, a_lora_up, g_lora_up, k_k, k_a, r_k, ln_x_w, ln_x_b, w_out_a, w_out_b, w_out, norm_ffn, w_ffn_gate, w_ffn_up, w_ffn_down, w_router, w_moe_gate, w_moe_up, w_moe_down, norm_final):
    P = _prep_params(dict(
        norm_mix=norm_mix, w_in=w_in, conv_w=conv_w, conv_b=conv_b, gate_a_w=gate_a_w, gate_a_b=gate_a_b,
        gate_x_w=gate_x_w, gate_x_b=gate_x_b, lru_lambda=lru_lambda, shift_mu=shift_mu, w0=w0,
        w_lora_up=w_lora_up, a0=a0, a_lora_up=a_lora_up, g_lora_up=g_lora_up, k_k=k_k, k_a=k_a, r_k=r_k,
        ln_x_w=ln_x_w, ln_x_b=ln_x_b, w_out_a=w_out_a, w_out_b=w_out_b, w_out=w_out, norm_ffn=norm_ffn,
        w_ffn_gate=w_ffn_gate, w_ffn_up=w_ffn_up, w_ffn_down=w_ffn_down, w_router=w_router,
        w_moe_gate=w_moe_gate, w_moe_up=w_moe_up, w_moe_down=w_moe_down, norm_final=norm_final))
    bp = x_prompt.shape[0]
    bs = x_sample.shape[0]
    zeros = lambda *shape: jnp.zeros(shape, F32)
    p_out = _trunk(x_prompt,
                   zeros(DEPTH, bp, RWKV_HEADS, RWKV_HEAD, RWKV_HEAD), zeros(DEPTH, bp, LRU_WIDTH),
                   zeros(DEPTH, bp, CONV_WIDTH - 1, LRU_WIDTH), zeros(DEPTH, bp, SHIFT_WIDTH), P,
                   zero_state=True, tm=512, lru_bb=1, lru_tt=512, rwkv_nb=4)
    s_out = _trunk(x_sample, state_rwkv, state_lru, state_conv, state_shift, P,
                   zero_state=False, tm=256, lru_bb=8, lru_tt=x_sample.shape[1], rwkv_nb=1)
    return (p_out[0], s_out[0]) + tuple(p_out[1:]) + tuple(s_out[1:])
```
